```python
import jax
import jax.numpy as jnp
from jax import lax
import numpy as np

D_MODEL = 2048
BATCH = 2
SEQ = 4096
DEPTH = 4
DEC_BATCH = 8
DEC_SEQ = 1
PAST_LEN = 16384
PAGE_SIZE = 128

HEAD_DIM = 128
NSA_HEADS = 8
NSA_KV = 2
NSA_GROUP = NSA_HEADS // NSA_KV
CMP_LEN = 32
CMP_STRIDE = 16
CMP_HIDDEN = 256
SEL_LEN = 64
SEL_TOP = 16
WINDOW = 512
FOX_HEADS = 8
ROPE_THETA = 500000.0
ROPE_DIMS = HEAD_DIM // 4
RET_HEADS = 8
RET_DK = 256
RET_DV = 512
RET_CHUNK = 128
RET_THETA = 10000.0
D_FF = 5632
Q_BLOCK = 128
LN_EPS = 1e-5
GN_EPS = 1e-6
N_EVEN = (DEPTH + 1) // 2
N_ODD = DEPTH // 2
ALPHA = (2.0 * DEPTH) ** 0.25
BETA = (8.0 * DEPTH) ** -0.25

O_NQ = 0
O_NKV = O_NQ + NSA_HEADS * HEAD_DIM
O_NG = O_NKV + 6 * NSA_KV * HEAD_DIM
O_FQ = O_NG + 3 * NSA_HEADS
O_FK = O_FQ + FOX_HEADS * HEAD_DIM
O_FV = O_FK + FOX_HEADS * HEAD_DIM
O_FF = O_FV + FOX_HEADS * HEAD_DIM
E_EVEN = O_FF + FOX_HEADS
O_RK = RET_HEADS * RET_DK
O_RV = 2 * RET_HEADS * RET_DK
O_RG = O_RV + RET_HEADS * RET_DV
E_ODD = O_RG + RET_HEADS * RET_DV

kernel_name = 'hybrid_nsa_fox_retention_step'


def layer_norm(x, g, b):
    xf = x.astype(jnp.float32)
    mu = xf.mean(-1, keepdims=True)
    var = jnp.square(xf - mu).mean(-1, keepdims=True)
    return ((xf - mu) * lax.rsqrt(var + LN_EPS) * g.astype(jnp.float32) + b.astype(jnp.float32)).astype(x.dtype)


def post_norm(x, sub, g, b):
    return layer_norm(ALPHA * x + sub, g, b)


def swiglu(x, w_in, w_out):
    a, b = jnp.split(x @ w_in, 2, axis=-1)
    return (jax.nn.silu(a) * b) @ w_out


def rotary(x, pos, n_rot, theta):
    half = n_rot // 2
    inv = 1.0 / (theta ** (jnp.arange(half, dtype=jnp.float32) / half))
    ang = pos.astype(jnp.float32)[:, None] * inv[None, :]
    cos = jnp.cos(ang)[:, None, :]
    sin = jnp.sin(ang)[:, None, :]
    xr = x[..., :n_rot].astype(jnp.float32)
    x1, x2 = xr[..., :half], xr[..., half:]
    rot = jnp.concatenate([x1 * cos - x2 * sin, x2 * cos + x1 * sin], -1).astype(x.dtype)
    if n_rot == x.shape[-1]:
        return rot
    return jnp.concatenate([rot, x[..., n_rot:]], -1)


def masked_softmax(logits, mask):
    lg = jnp.where(mask, logits.astype(jnp.float32), -jnp.inf)
    m = jnp.max(lg, -1, keepdims=True)
    m = jnp.where(jnp.isfinite(m), m, 0.0)
    p = jnp.exp(lg - m)
    s = p.sum(-1, keepdims=True)
    return p / jnp.where(s > 0, s, 1.0)


def nsa_compress(k, pos_emb, w1, w2):
    b, l, g, dh = k.shape
    ratio = CMP_LEN // CMP_STRIDE
    n16 = l // CMP_STRIDE
    n_c = n16 - ratio + 1
    c = k[:, :n16 * CMP_STRIDE].reshape(b, n16, CMP_STRIDE, g, dh)
    blk = jnp.concatenate([c[:, j:j + n_c] for j in range(ratio)], axis=2)
    blk = blk + pos_emb[:, None, :].astype(k.dtype)
    flat = blk.transpose(0, 1, 3, 2, 4).reshape(b, n_c, g, CMP_LEN * dh)
    return jax.nn.gelu(flat @ w1) @ w2


def nsa_block(q_rot, q_raw, q_pos, kc, vc, gather_slc, n_sel, kw, vw, kw_pos, gate):
    b, tq, _, dh = q_rot.shape
    g, m = NSA_KV, NSA_GROUP
    scale = dh ** -0.5
    qg = q_rot.reshape(b, tq, g, m, dh)
    qc = q_raw.reshape(b, tq, g, m, dh)
    n_c = kc.shape[1]
    c_end = jnp.arange(n_c) * CMP_STRIDE + CMP_LEN
    p_c = masked_softmax(jnp.einsum('bqgmd,bcgd->bgmqc', qc, kc) * scale, c_end[None, :] <= q_pos[:, None] + 1)
    o_c = jnp.einsum('bgmqc,bcgd->bqgmd', p_c.astype(vc.dtype), vc)
    ratio = SEL_LEN // CMP_STRIDE
    imp = jnp.pad(p_c.sum(2), ((0, 0), (0, 0), (0, 0), (1, ratio * (n_sel + 1) - n_c - 1)))
    imp = imp.reshape(b, g, tq, n_sel + 1, ratio)
    s_sel = imp[..., :n_sel, :].sum(-1) + imp[..., 1:, 0]
    blk = jnp.arange(n_sel)[None, :]
    cur = (q_pos // SEL_LEN)[:, None]
    forced = (blk == 0) | (blk == cur) | (blk == cur - 1)
    score = jnp.where(blk <= cur, jnp.where(forced, jnp.inf, s_sel), -jnp.inf)
    top_v, top_i = lax.top_k(score, min(SEL_TOP, n_sel))
    k_pos = top_i[..., None] * SEL_LEN + jnp.arange(SEL_LEN)
    ks, vs = gather_slc(k_pos)
    n_k = k_pos.shape[3] * SEL_LEN
    s_mask = (top_v > -jnp.inf)[..., None] & (k_pos <= q_pos[:, None, None])
    l_s = jnp.einsum('bqgmd,bgqksd->bgmqks', qg, ks).reshape(b, g, m, tq, n_k) * scale
    p_s = masked_softmax(l_s, s_mask.reshape(b, g, 1, tq, n_k))
    o_s = jnp.einsum('bgmqn,bgqnd->bqgmd', p_s.astype(vs.dtype), vs.reshape(b, g, tq, n_k, dh))
    w_mask = (kw_pos[None, :] <= q_pos[:, None]) & (kw_pos[None, :] > q_pos[:, None] - WINDOW) & (kw_pos[None, :] >= 0)
    p_w = masked_softmax(jnp.einsum('bqgmd,bwgd->bgmqw', qg, kw) * scale, w_mask)
    o_w = jnp.einsum('bgmqw,bwgd->bqgmd', p_w.astype(vw.dtype), vw)
    gt = gate.reshape(b, tq, g, m, 3)
    o = gt[..., 0:1] * o_c + gt[..., 1:2] * o_s + gt[..., 2:3] * o_w
    return o.reshape(b, tq, NSA_HEADS * dh)


def fox_attend(q, q_pos, c_q, k, v, c_k, k_pos):
    b, tq, h, dh = q.shape
    lg = jnp.einsum('bqhd,bkhd->bhqk', q, k).astype(jnp.float32) * (dh ** -0.5)
    bias = jnp.moveaxis(c_q, 1, 2)[..., :, None] - jnp.moveaxis(c_k, 1, 2)[..., None, :]
    p = masked_softmax(lg + bias, k_pos[None, :] <= q_pos[:, None])
    return jnp.einsum('bhqk,bkhd->bqhd', p.astype(v.dtype), v).reshape(b, tq, h * dh)


def project_even(x, w_in, f_bias):
    b, t, _ = x.shape
    h = x @ w_in
    nq = h[..., O_NQ:O_NKV].reshape(b, t, NSA_HEADS, HEAD_DIM)
    nkv = h[..., O_NKV:O_NG].reshape(b, t, 6, NSA_KV, HEAD_DIM)
    ng = jax.nn.sigmoid(h[..., O_NG:O_FQ].reshape(b, t, NSA_HEADS, 3))
    fq = h[..., O_FQ:O_FK].reshape(b, t, FOX_HEADS, HEAD_DIM)
    fk = h[..., O_FK:O_FV].reshape(b, t, FOX_HEADS, HEAD_DIM)
    fv = h[..., O_FV:O_FF].reshape(b, t, FOX_HEADS, HEAD_DIM)
    logf = jax.nn.log_sigmoid((h[..., O_FF:E_EVEN] + f_bias).astype(jnp.float32))
    return nq, nkv, ng, fq, fk, fv, logf


def even_mixer_prompt(x, w_in, w_out, f_bias, cmp_pos, cmp_w1, cmp_w2):
    b, t, _ = x.shape
    pos = jnp.arange(t)
    nq, nkv, ng, fq, fk, fv, logf = project_even(x, w_in, f_bias)
    q_rot = rotary(nq, pos, ROPE_DIMS, ROPE_THETA)
    k_slc = rotary(nkv[:, :, 2], pos, ROPE_DIMS, ROPE_THETA)
    v_slc = nkv[:, :, 3]
    k_win = rotary(nkv[:, :, 4], pos, ROPE_DIMS, ROPE_THETA)
    v_win = nkv[:, :, 5]
    kc = nsa_compress(nkv[:, :, 0], cmp_pos[0], cmp_w1[0], cmp_w2[0])
    vc = nsa_compress(nkv[:, :, 1], cmp_pos[1], cmp_w1[1], cmp_w2[1])
    n_sel = t // SEL_LEN
    bi = jnp.arange(b)[:, None, None, None, None]
    gi = jnp.arange(NSA_KV)[None, :, None, None, None]

    def gather_slc(k_pos):
        return k_slc[bi, k_pos, gi], v_slc[bi, k_pos, gi]

    pad = ((0, 0), (WINDOW, 0), (0, 0), (0, 0))
    kw_pad = jnp.pad(k_win, pad)
    vw_pad = jnp.pad(v_win, pad)
    c = jnp.cumsum(logf, axis=1)

    def qblock(i):
        s = i * Q_BLOCK
        def sl(a):
            return lax.dynamic_slice_in_dim(a, s, Q_BLOCK, axis=1)
        qp = s + jnp.arange(Q_BLOCK)
        kwp = s - WINDOW + jnp.arange(WINDOW + Q_BLOCK)
        kw = lax.dynamic_slice_in_dim(kw_pad, s, WINDOW + Q_BLOCK, axis=1)
        vw = lax.dynamic_slice_in_dim(vw_pad, s, WINDOW + Q_BLOCK, axis=1)
        o_nsa = nsa_block(sl(q_rot), sl(nq), qp, kc, vc, gather_slc, n_sel, kw, vw, kwp, sl(ng))
        o_fox = fox_attend(sl(fq), qp, sl(c), fk, fv, c, pos)
        return jnp.concatenate([o_nsa, o_fox], -1)

    o = lax.map(qblock, jnp.arange(t // Q_BLOCK))
    out = jnp.moveaxis(o, 0, 1).reshape(b, t, -1) @ w_out
    n_w = min(WINDOW, t)
    new_win = jnp.stack([k_win, v_win], 2)[:, t - n_w:]
    return out, nkv[:, :, 0:2], jnp.stack([k_slc, v_slc], 2), new_win, jnp.stack([fk, fv], 2), logf


def even_mixer_sample(x, e, pool_cmp, pool_slc, win_buf, pool_fox, pool_logf, page_table,
                      w_in, w_out, f_bias, cmp_pos, cmp_w1, cmp_w2):
    b, t, _ = x.shape
    l_tot = PAST_LEN + t
    pos = PAST_LEN + jnp.arange(t)
    nq, nkv, ng, fq, fk, fv, logf = project_even(x, w_in, f_bias)
    q_rot = rotary(nq, pos, ROPE_DIMS, ROPE_THETA)
    k_slc = rotary(nkv[:, :, 2], pos, ROPE_DIMS, ROPE_THETA)
    v_slc = nkv[:, :, 3]
    k_win = rotary(nkv[:, :, 4], pos, ROPE_DIMS, ROPE_THETA)
    v_win = nkv[:, :, 5]

    def pages(pool):
        return pool[e, page_table].reshape((b, PAST_LEN) + pool.shape[3:])

    cmp_all = jnp.concatenate([pages(pool_cmp), nkv[:, :, 0:2]], axis=1)
    kc = nsa_compress(cmp_all[:, :, 0], cmp_pos[0], cmp_w1[0], cmp_w2[0])
    vc = nsa_compress(cmp_all[:, :, 1], cmp_pos[1], cmp_w1[1], cmp_w2[1])
    n_sel = -(-l_tot // SEL_LEN)
    bi = jnp.arange(b)[:, None, None, None, None]
    gi = jnp.arange(NSA_KV)[None, :, None, None, None]

    def gather_slc(k_pos):
        in_past = (k_pos < PAST_LEN)[..., None]
        pc = jnp.minimum(k_pos, PAST_LEN - 1)
        phys = page_table[bi, pc // PAGE_SIZE]
        off = pc % PAGE_SIZE
        ni = jnp.clip(k_pos - PAST_LEN, 0, t - 1)
        ks = jnp.where(in_past, pool_slc[e, phys, off, 0, gi], k_slc[bi, ni, gi])
        vs = jnp.where(in_past, pool_slc[e, phys, off, 1, gi], v_slc[bi, ni, gi])
        return ks, vs

    n_buf = win_buf.shape[1]
    kw = jnp.concatenate([win_buf[:, :, 0], k_win], axis=1)
    vw = jnp.concatenate([win_buf[:, :, 1], v_win], axis=1)
    kwp = PAST_LEN - n_buf + jnp.arange(n_buf + t)
    o_nsa = nsa_block(q_rot, nq, pos, kc, vc, gather_slc, n_sel, kw, vw, kwp, ng)
    fox_all = jnp.concatenate([pages(pool_fox), jnp.stack([fk, fv], 2)], axis=1)
    c = jnp.cumsum(jnp.concatenate([pages(pool_logf).astype(jnp.float32), logf], axis=1), axis=1)
    o_fox = fox_attend(fq, pos, c[:, PAST_LEN:], fox_all[:, :, 0], fox_all[:, :, 1], c, jnp.arange(l_tot))
    out = jnp.concatenate([o_nsa, o_fox], -1) @ w_out
    new_win = jnp.concatenate([win_buf, jnp.stack([k_win, v_win], 2).astype(win_buf.dtype)], axis=1)[:, t:]
    return out, nkv[:, :, 0:2], jnp.stack([k_slc, v_slc], 2), new_win, jnp.stack([fk, fv], 2), logf


def retention_chunk(q, k, v, s):
    c = q.shape[1]
    dt = q.dtype
    lg = jnp.log1p(-(2.0 ** (-5.0 - jnp.arange(RET_HEADS, dtype=jnp.float32))))
    i = jnp.arange(c, dtype=jnp.float32)
    diff = i[:, None] - i[None, :]
    dec = jnp.where(diff >= 0, jnp.exp(jnp.maximum(diff, 0.0)[None] * lg[:, None, None]), 0.0)
    xi = jnp.exp((i[:, None] + 1.0) * lg[None, :])
    zeta = jnp.exp((c - 1.0 - i)[:, None] * lg[None, :])
    g_c = jnp.exp(c * lg)
    a = jnp.einsum('bihd,bjhd->bhij', q, k) * dec.astype(dt)
    o = (jnp.einsum('bhij,bjhe->bihe', a, v)
         + jnp.einsum('bihd,bhde->bihe', q, s.astype(dt)) * xi.astype(dt)[None, :, :, None])
    s_new = (s * g_c.astype(s.dtype)[None, :, None, None]
             + jnp.einsum('bjhd,bjhe->bhde', k * zeta.astype(dt)[None, :, :, None], v).astype(s.dtype))
    return o, s_new


def retention_mixer(x, pos, s0, w_in, gn_g, w_out):
    b, t, _ = x.shape
    h = x @ w_in
    q = rotary(h[..., :O_RK].reshape(b, t, RET_HEADS, RET_DK), pos, RET_DK, RET_THETA)
    k = rotary(h[..., O_RK:O_RV].reshape(b, t, RET_HEADS, RET_DK), pos, RET_DK, RET_THETA) * (RET_DK ** -0.5)
    v = h[..., O_RV:O_RG].reshape(b, t, RET_HEADS, RET_DV)
    gate = h[..., O_RG:E_ODD]
    if t % RET_CHUNK == 0:
        n_ch = t // RET_CHUNK

        def chunks(a):
            return jnp.moveaxis(a.reshape((b, n_ch, RET_CHUNK) + a.shape[2:]), 1, 0)

        def step(s, qkv):
            o_c, s = retention_chunk(qkv[0], qkv[1], qkv[2], s)
            return s, o_c

        s_fin, o = lax.scan(step, s0, (chunks(q), chunks(k), chunks(v)))
        o = jnp.moveaxis(o, 0, 1).reshape(b, t, RET_HEADS, RET_DV)
    else:
        o, s_fin = retention_chunk(q, k, v, s0)
    of = o.astype(jnp.float32)
    mu = of.mean(-1, keepdims=True)
    var = jnp.square(of - mu).mean(-1, keepdims=True)
    y = ((of - mu) * lax.rsqrt(var + GN_EPS)).reshape(b, t, RET_HEADS * RET_DV) * gn_g.astype(jnp.float32)
    out = (jax.nn.silu(gate) * y.astype(x.dtype)) @ w_out
    return out, s_fin


def setup_inputs(seed: int = 0) -> dict:
    key = jax.random.key(seed)
    ks = jax.random.split(key, 24)
    n_pages = PAST_LEN // PAGE_SIZE
    n_used = DEC_BATCH * n_pages
    n_pool = n_used + n_used // 4
    win_buf = min(WINDOW, PAST_LEN)

    def nrm(k, shape, s=1.0):
        return s * jax.random.normal(k, shape, jnp.float32)

    page_table = jax.random.permutation(ks[0], n_pool)[:n_used].reshape(DEC_BATCH, n_pages).astype(jnp.int32)
    return {
        'x_prompt': nrm(ks[1], (BATCH, SEQ, D_MODEL)),
        'x_sample': nrm(ks[2], (DEC_BATCH, DEC_SEQ, D_MODEL)),
        'cache_nsa_cmp': nrm(ks[3], (N_EVEN, n_pool, PAGE_SIZE, 2, NSA_KV, HEAD_DIM)),
        'cache_nsa_slc': nrm(ks[4], (N_EVEN, n_pool, PAGE_SIZE, 2, NSA_KV, HEAD_DIM)),
        'cache_nsa_win': nrm(ks[5], (N_EVEN, DEC_BATCH, win_buf, 2, NSA_KV, HEAD_DIM)),
        'cache_fox_kv': nrm(ks[6], (N_EVEN, n_pool, PAGE_SIZE, 2, FOX_HEADS, HEAD_DIM)),
        'cache_fox_logf': jax.nn.log_sigmoid(3.0 + nrm(ks[7], (N_EVEN, n_pool, PAGE_SIZE, FOX_HEADS))),
        'state_ret': nrm(ks[8], (N_ODD, DEC_BATCH, RET_HEADS, RET_DK, RET_DV), 0.1),
        'page_table': page_table,
        'w_ffn_in': nrm(ks[9], (DEPTH, 2, D_MODEL, 2 * D_FF), D_MODEL ** -0.5),
        'w_ffn_out': nrm(ks[10], (DEPTH, 2, D_FF, D_MODEL), BETA * D_FF ** -0.5),
        'ln_g': 1.0 + nrm(ks[11], (DEPTH, 3, D_MODEL), 0.02),
        'ln_b': nrm(ks[12], (DEPTH, 3, D_MODEL), 0.02),
        'w_in_even': nrm(ks[13], (N_EVEN, D_MODEL, E_EVEN), D_MODEL ** -0.5),
        'w_out_even': nrm(ks[14], (N_EVEN, (NSA_HEADS + FOX_HEADS) * HEAD_DIM, D_MODEL), BETA * ((NSA_HEADS + FOX_HEADS) * HEAD_DIM) ** -0.5),
        'fox_f_bias': 3.0 + nrm(ks[15], (N_EVEN, FOX_HEADS), 0.5),
        'nsa_cmp_pos': nrm(ks[16], (N_EVEN, 2, CMP_LEN, HEAD_DIM), 0.02),
        'nsa_cmp_w1': nrm(ks[17], (N_EVEN, 2, CMP_LEN * HEAD_DIM, CMP_HIDDEN), (CMP_LEN * HEAD_DIM) ** -0.5),
        'nsa_cmp_w2': nrm(ks[18], (N_EVEN, 2, CMP_HIDDEN, HEAD_DIM), CMP_HIDDEN ** -0.5),
        'w_in_odd': nrm(ks[19], (N_ODD, D_MODEL, E_ODD), D_MODEL ** -0.5),
        'ret_gn_g': 1.0 + nrm(ks[20], (N_ODD, RET_HEADS * RET_DV), 0.02),
        'w_out_odd': nrm(ks[21], (N_ODD, RET_HEADS * RET_DV, D_MODEL), BETA * (RET_HEADS * RET_DV) ** -0.5),
    }


def reference(x_prompt, x_sample, cache_nsa_cmp, cache_nsa_slc, cache_nsa_win, cache_fox_kv, cache_fox_logf,
              state_ret, page_table, w_ffn_in, w_ffn_out, ln_g, ln_b, w_in_even, w_out_even, fox_f_bias,
              nsa_cmp_pos, nsa_cmp_w1, nsa_cmp_w2, w_in_odd, ret_gn_g, w_out_odd):
    xp, xs = x_prompt, x_sample
    pos_p = jnp.arange(xp.shape[1])
    pos_s = PAST_LEN + jnp.arange(xs.shape[1])
    new_p = [[] for _ in range(6)]
    new_s = [[] for _ in range(6)]
    for l in range(DEPTH):
        xp = post_norm(xp, 0.5 * swiglu(xp, w_ffn_in[l, 0], w_ffn_out[l, 0]), ln_g[l, 0], ln_b[l, 0])
        xs = post_norm(xs, 0.5 * swiglu(xs, w_ffn_in[l, 0], w_ffn_out[l, 0]), ln_g[l, 0], ln_b[l, 0])
        if l % 2 == 0:
            e = l // 2
            wts = (w_in_even[e], w_out_even[e], fox_f_bias[e], nsa_cmp_pos[e], nsa_cmp_w1[e], nsa_cmp_w2[e])
            mp, *st_p = even_mixer_prompt(xp, *wts)
            ms, *st_s = even_mixer_sample(xs, e, cache_nsa_cmp, cache_nsa_slc, cache_nsa_win[e], cache_fox_kv,
                                          cache_fox_logf, page_table, *wts)
            slots = (0, 1, 2, 3, 4)
        else:
            o = l // 2
            s0 = jnp.zeros((xp.shape[0], RET_HEADS, RET_DK, RET_DV), xp.dtype)
            mp, sp = retention_mixer(xp, pos_p, s0, w_in_odd[o], ret_gn_g[o], w_out_odd[o])
            ms, ss = retention_mixer(xs, pos_s, state_ret[o], w_in_odd[o], ret_gn_g[o], w_out_odd[o])
            st_p, st_s = [sp], [ss]
            slots = (5,)
        for i, a_p, a_s in zip(slots, st_p, st_s):
            new_p[i].append(a_p)
            new_s[i].append(a_s)
        xp = post_norm(xp, mp, ln_g[l, 1], ln_b[l, 1])
        xs = post_norm(xs, ms, ln_g[l, 1], ln_b[l, 1])
        xp = post_norm(xp, 0.5 * swiglu(xp, w_ffn_in[l, 1], w_ffn_out[l, 1]), ln_g[l, 2], ln_b[l, 2])
        xs = post_norm(xs, 0.5 * swiglu(xs, w_ffn_in[l, 1], w_ffn_out[l, 1]), ln_g[l, 2], ln_b[l, 2])
    return (xp, xs,
            jnp.stack(new_p[0]), jnp.stack(new_s[0]),
            jnp.stack(new_p[1]), jnp.stack(new_s[1]),
            jnp.stack(new_p[2]), jnp.stack(new_s[2]),
            jnp.stack(new_p[3]), jnp.stack(new_s[3]),
            jnp.stack(new_p[4]), jnp.stack(new_s[4]),
            jnp.stack(new_p[5]), jnp.stack(new_s[5]))
```

```python
import functools
import math

import numpy as np
import jax
import jax.numpy as jnp
from jax import lax
from jax.experimental import pallas as pl
from jax.experimental.pallas import tpu as pltpu

F32 = jnp.float32
MXU_DTYPE = jnp.bfloat16

HEAD_DIM = 128
NSA_HEADS = 8
NSA_KV = 2
NSA_GROUP = NSA_HEADS // NSA_KV
CMP_LEN = 32
CMP_STRIDE = 16
CMP_HIDDEN = 256
SEL_LEN = 64
SEL_TOP = 16
WINDOW = 512
FOX_HEADS = 8
ROPE_THETA = 500000.0
ROPE_DIMS = HEAD_DIM // 4
RET_HEADS = 8
RET_DK = 256
RET_DV = 512
RET_CHUNK = 128
RET_THETA = 10000.0
LN_EPS = 1e-5
GN_EPS = 1e-6

O_NQ = 0
O_NKV = O_NQ + NSA_HEADS * HEAD_DIM
O_NG = O_NKV + 6 * NSA_KV * HEAD_DIM
O_FQ = O_NG + 3 * NSA_HEADS
O_FK = O_FQ + FOX_HEADS * HEAD_DIM
O_FV = O_FK + FOX_HEADS * HEAD_DIM
O_FF = O_FV + FOX_HEADS * HEAD_DIM
E_EVEN = O_FF + FOX_HEADS
P_NKV = NSA_HEADS * HEAD_DIM
P_FQ = P_NKV + 6 * NSA_KV * HEAD_DIM
P_SMALL = P_FQ + 3 * FOX_HEADS * HEAD_DIM
P_EVEN = P_SMALL + 128
N_GATE = 3 * NSA_HEADS
O_RK = RET_HEADS * RET_DK
O_RV = 2 * RET_HEADS * RET_DK
O_RG = O_RV + RET_HEADS * RET_DV
E_ODD = O_RG + RET_HEADS * RET_DV

V7X_VMEM_LIMIT = 56 * 1024 * 1024
NEG_BIG = -1e30
NT_DIMS = (((1,), (1,)), ((), ()))
TN_DIMS = (((0,), (0,)), ((), ()))


def _params(*sem):
    return pltpu.CompilerParams(dimension_semantics=sem, vmem_limit_bytes=V7X_VMEM_LIMIT)


def _mx(x):
    return x.astype(MXU_DTYPE)


def _pick(n, prefs):
    for p in prefs:
        if n % p == 0:
            return p
    return n


def _swiglu_kernel(x_ref, wa_ref, wb_ref, o_ref, wa_s, wb_s):
    @pl.when(pl.program_id(1) == 0)
    def _():
        wa_s[...] = _mx(wa_ref[...])
        wb_s[...] = _mx(wb_ref[...])
    x = x_ref[...]
    a = jnp.dot(x, wa_s[...], preferred_element_type=F32)
    b = jnp.dot(x, wb_s[...], preferred_element_type=F32)
    o_ref[...] = (a * jax.nn.sigmoid(a) * b).astype(o_ref.dtype)


def swiglu_hidden(xb, w_in, l, s):
    m, d = xb.shape
    f = w_in.shape[-1] // 2
    tm = _pick(m, (1024, 512, 256, 128))
    tn = _pick(f, (512, 256, 128))
    nj = f // tn
    return pl.pallas_call(
        _swiglu_kernel,
        grid=(nj, m // tm),
        in_specs=[
            pl.BlockSpec((tm, d), lambda j, i: (i, 0)),
            pl.BlockSpec((None, None, d, tn), lambda j, i: (l, s, 0, j)),
            pl.BlockSpec((None, None, d, tn), lambda j, i: (l, s, 0, j + nj)),
        ],
        out_specs=pl.BlockSpec((tm, tn), lambda j, i: (i, j)),
        out_shape=jax.ShapeDtypeStruct((m, f), MXU_DTYPE),
        scratch_shapes=[pltpu.VMEM((d, tn), MXU_DTYPE), pltpu.VMEM((d, tn), MXU_DTYPE)],
        compiler_params=_params("arbitrary", "arbitrary"),
        name="swiglu_hidden",
    )(xb, w_in, w_in)


def _mm_ln_kernel(x_ref, w_ref, r_ref, g_ref, b_ref, o_ref, ob_ref, acc, *, alpha, scale, nk):
    k = pl.program_id(1)

    @pl.when(k == 0)
    def _():
        acc[...] = jnp.zeros_like(acc)

    acc[...] += jnp.dot(x_ref[...], _mx(w_ref[...]), preferred_element_type=F32)

    @pl.when(k == nk - 1)
    def _():
        z = alpha * r_ref[...] + scale * acc[...]
        mu = jnp.mean(z, axis=-1, keepdims=True)
        dz = z - mu
        var = jnp.mean(dz * dz, axis=-1, keepdims=True)
        y = dz * lax.rsqrt(var + LN_EPS) * g_ref[...] + b_ref[...]
        o_ref[...] = y
        ob_ref[...] = y.astype(ob_ref.dtype)


def matmul_postnorm(xb, w, widx, res, g, b, gidx, alpha, scale):
    m, kdim = xb.shape
    n = w.shape[-1]
    tm = _pick(m, (512, 256, 128))
    tk = _pick(kdim, (512, 256, 128))
    nk = kdim // tk
    nlead = len(widx)
    w_spec = pl.BlockSpec((None,) * nlead + (tk, n), lambda i, k: tuple(widx) + (k, 0))
    g_spec = pl.BlockSpec((None, None, 1, n), lambda i, k: tuple(gidx) + (0, 0))
    return pl.pallas_call(
        functools.partial(_mm_ln_kernel, alpha=alpha, scale=scale, nk=nk),
        grid=(m // tm, nk),
        in_specs=[
            pl.BlockSpec((tm, tk), lambda i, k: (i, k)),
            w_spec,
            pl.BlockSpec((tm, n), lambda i, k: (i, 0)),
            g_spec, g_spec,
        ],
        out_specs=[pl.BlockSpec((tm, n), lambda i, k: (i, 0)), pl.BlockSpec((tm, n), lambda i, k: (i, 0))],
        out_shape=[jax.ShapeDtypeStruct((m, n), F32), jax.ShapeDtypeStruct((m, n), MXU_DTYPE)],
        scratch_shapes=[pltpu.VMEM((tm, n), F32)],
        compiler_params=_params("arbitrary", "arbitrary"),
        name="matmul_postnorm",
    )(xb, w, res, g[:, :, None, :], b[:, :, None, :])


def _mm_kernel(x_ref, w_ref, o_ref, w_s):
    @pl.when(pl.program_id(1) == 0)
    def _():
        w_s[...] = _mx(w_ref[...])
    o_ref[...] = jnp.dot(x_ref[...], w_s[...], preferred_element_type=F32)


def matmul(xb, w, widx, tn_prefs):
    m, kdim = xb.shape
    n = w.shape[-1]
    tm = _pick(m, (1024, 512, 256, 128))
    tn = _pick(n, tn_prefs)
    nlead = len(widx)
    return pl.pallas_call(
        _mm_kernel,
        grid=(n // tn, m // tm),
        in_specs=[
            pl.BlockSpec((tm, kdim), lambda j, i: (i, 0)),
            pl.BlockSpec((None,) * nlead + (kdim, tn), lambda j, i: tuple(widx) + (0, j)),
        ],
        out_specs=pl.BlockSpec((tm, tn), lambda j, i: (i, j)),
        out_shape=jax.ShapeDtypeStruct((m, n), F32),
        scratch_shapes=[pltpu.VMEM((kdim, tn), MXU_DTYPE)],
        compiler_params=_params("arbitrary", "arbitrary"),
        name="matmul",
    )(xb, w)


def _rope_tables(pos):
    half = ROPE_DIMS // 2
    inv = 1.0 / (ROPE_THETA ** (jnp.arange(half, dtype=F32) / half))
    ang = pos.astype(F32)[:, None] * inv[None, :]
    cos, sin = jnp.cos(ang), jnp.sin(ang)
    n = pos.shape[0]
    ones = jnp.ones((n, HEAD_DIM - ROPE_DIMS), F32)
    zeros = jnp.zeros((n, HEAD_DIM - ROPE_DIMS), F32)
    zh = jnp.zeros((n, half), F32)
    c = jnp.concatenate([cos, cos, ones], -1)
    a = jnp.concatenate([-sin, zh, zeros], -1)
    b = jnp.concatenate([zh, sin, zeros], -1)
    return c, a, b


def _rope(x, c, a, b):
    half = ROPE_DIMS // 2
    return x * c + pltpu.roll(x, HEAD_DIM - half, 1) * a + pltpu.roll(x, half, 1) * b


def _even_post_kernel(q_ref, slc_ref, win_ref, sm_ref, c_ref, a_ref, b_ref, fb_ref,
                      qr_ref, slco_ref, wino_ref, gate_ref, logf_ref):
    c, a, b = c_ref[...], a_ref[...], b_ref[...]
    for h in range(NSA_HEADS):
        sl = slice(h * HEAD_DIM, (h + 1) * HEAD_DIM)
        qr_ref[:, sl] = _rope(q_ref[:, sl], c, a, b)
    for src, dst in ((slc_ref, slco_ref), (win_ref, wino_ref)):
        for g in range(NSA_KV):
            sl = slice(g * HEAD_DIM, (g + 1) * HEAD_DIM)
            dst[:, sl] = _rope(src[:, sl], c, a, b)
        vs = slice(NSA_KV * HEAD_DIM, 2 * NSA_KV * HEAD_DIM)
        dst[:, vs] = src[:, vs]
    sm = sm_ref[...]
    gate_ref[...] = jax.nn.sigmoid(sm)
    z = sm + fb_ref[...]
    logf_ref[...] = jnp.minimum(z, 0.0) - jnp.log1p(jnp.exp(-jnp.abs(z)))


def even_post(h, tabs, fb_row, npos_blocks):
    m = h.shape[0]
    tm = _pick(m, (512, 256, 128))
    kvw = 2 * NSA_KV * HEAD_DIM
    tab_spec = pl.BlockSpec((tm, HEAD_DIM), lambda i: (i % npos_blocks, 0))
    return pl.pallas_call(
        _even_post_kernel,
        grid=(m // tm,),
        in_specs=[
            pl.BlockSpec((tm, P_NKV), lambda i: (i, 0)),
            pl.BlockSpec((tm, kvw), lambda i: (i, (P_NKV + kvw) // kvw)),
            pl.BlockSpec((tm, kvw), lambda i: (i, (P_NKV + 2 * kvw) // kvw)),
            pl.BlockSpec((tm, 128), lambda i: (i, P_SMALL // 128)),
            tab_spec, tab_spec, tab_spec,
            pl.BlockSpec((1, 128), lambda i: (0, 0)),
        ],
        out_specs=[
            pl.BlockSpec((tm, P_NKV), lambda i: (i, 0)),
            pl.BlockSpec((tm, kvw), lambda i: (i, 0)),
            pl.BlockSpec((tm, kvw), lambda i: (i, 0)),
            pl.BlockSpec((tm, 128), lambda i: (i, 0)),
            pl.BlockSpec((tm, 128), lambda i: (i, 0)),
        ],
        out_shape=[
            jax.ShapeDtypeStruct((m, P_NKV), F32),
            jax.ShapeDtypeStruct((m, kvw), F32),
            jax.ShapeDtypeStruct((m, kvw), F32),
            jax.ShapeDtypeStruct((m, 128), F32),
            jax.ShapeDtypeStruct((m, 128), F32),
        ],
        compiler_params=_params("arbitrary"),
        name="even_post",
    )(h, h, h, h, *tabs, fb_row)


def _cumsum_kernel(x_ref, tri_ref, o_ref, carry):
    @pl.when(pl.program_id(1) == 0)
    def _():
        carry[...] = jnp.zeros_like(carry)
    c = jnp.dot(tri_ref[...], x_ref[...], preferred_element_type=F32,
                precision=lax.Precision.HIGHEST) + carry[...]
    o_ref[...] = c
    carry[...] = c[-1:, :]


def cumsum_rows(x, nb, t):
    tc = _pick(t, (512, 256, 128))
    tri = jnp.tril(jnp.ones((tc, tc), F32))
    nt = t // tc
    return pl.pallas_call(
        _cumsum_kernel,
        grid=(nb, nt),
        in_specs=[pl.BlockSpec((tc, 128), lambda b, i: (b * nt + i, 0)),
                  pl.BlockSpec((tc, tc), lambda b, i: (0, 0))],
        out_specs=pl.BlockSpec((tc, 128), lambda b, i: (b * nt + i, 0)),
        out_shape=jax.ShapeDtypeStruct(x.shape, F32),
        scratch_shapes=[pltpu.VMEM((1, 128), F32)],
        compiler_params=_params("arbitrary", "arbitrary"),
        name="cumsum_rows",
    )(x, tri)


def _compress_kernel(x_ref, w1_ref, pe_ref, w2_ref, o_ref, u0, u1, *, n16):
    u0[...] = jnp.zeros_like(u0)
    u1[...] = jnp.zeros_like(u1)
    for r in range(CMP_STRIDE):
        xr = x_ref[pl.ds(r, n16, stride=CMP_STRIDE), :]
        u0[...] += jnp.dot(_mx(xr + pe_ref[r:r + 1, :]), _mx(w1_ref[r]), preferred_element_type=F32)
        u1[...] += jnp.dot(_mx(xr + pe_ref[CMP_STRIDE + r:CMP_STRIDE + r + 1, :]), _mx(w1_ref[CMP_STRIDE + r]),
                           preferred_element_type=F32)
    pre = u0[...] + pltpu.roll(u1[...], n16 - 1, 0)
    out = jnp.dot(_mx(jax.nn.gelu(pre)), _mx(w2_ref[...]), preferred_element_type=F32)
    row = lax.broadcasted_iota(jnp.int32, out.shape, 0)
    o_ref[...] = jnp.where(row < n16 - 1, out, 0.0)


def nsa_compress(x, x_spec, nb, length, e, w1, pe, w2):
    n16 = length // CMP_STRIDE
    w1r = w1.reshape(w1.shape[0], 2, CMP_LEN, HEAD_DIM, CMP_HIDDEN)
    return pl.pallas_call(
        functools.partial(_compress_kernel, n16=n16),
        grid=(nb, 2 * NSA_KV),
        in_specs=[
            x_spec,
            pl.BlockSpec((None, None, CMP_LEN, HEAD_DIM, CMP_HIDDEN), lambda b, c: (e, c // NSA_KV, 0, 0, 0)),
            pl.BlockSpec((None, None, CMP_LEN, HEAD_DIM), lambda b, c: (e, c // NSA_KV, 0, 0)),
            pl.BlockSpec((None, None, CMP_HIDDEN, HEAD_DIM), lambda b, c: (e, c // NSA_KV, 0, 0)),
        ],
        out_specs=pl.BlockSpec((None, None, n16, HEAD_DIM), lambda b, c: (b, c, 0, 0)),
        out_shape=jax.ShapeDtypeStruct((nb, 2 * NSA_KV, n16, HEAD_DIM), F32),
        scratch_shapes=[pltpu.VMEM((n16, CMP_HIDDEN), F32), pltpu.VMEM((n16, CMP_HIDDEN), F32)],
        compiler_params=_params("arbitrary", "arbitrary"),
        name="nsa_compress",
    )(x, w1r, pe, w2)


def _sel_matrix(n_c_pad, n_c, n_sel, width):
    ratio = SEL_LEN // CMP_STRIDE
    i = np.arange(n_c_pad)[:, None]
    j = np.arange(width)[None, :]
    a = (i >= ratio * j - 1) & (i <= ratio * j + ratio - 1) & (i < n_c) & (j < n_sel)
    return jnp.asarray(a.astype(np.float32))


def _masked_softmax(lg, mask):
    lg = jnp.where(mask, lg, -jnp.inf)
    m = jnp.max(lg, axis=-1, keepdims=True)
    m = jnp.where(m == -jnp.inf, 0.0, m)
    p = jnp.exp(lg - m)
    s = jnp.sum(p, axis=-1, keepdims=True)
    return p / jnp.where(s > 0, s, 1.0)


def _cmp_select_kernel(q_ref, kc_ref, vc_ref, a_ref, oc_ref, sel_ref, *, tq, n_sel, n_top):
    i = pl.program_id(2)
    ncp = kc_ref.shape[0]
    qpos = i * tq + lax.broadcasted_iota(jnp.int32, (tq, 1), 0)
    cend = lax.broadcasted_iota(jnp.int32, (1, ncp), 1) * CMP_STRIDE + CMP_LEN
    cmask = cend <= qpos + 1
    kc = _mx(kc_ref[...])
    vc = _mx(vc_ref[...])
    scale = HEAD_DIM ** -0.5
    imp = jnp.zeros((tq, ncp), F32)
    for m in range(NSA_GROUP):
        sl = slice(m * HEAD_DIM, (m + 1) * HEAD_DIM)
        lg = lax.dot_general(_mx(q_ref[:, sl]), kc, NT_DIMS, preferred_element_type=F32) * scale
        p = _masked_softmax(lg, cmask)
        oc_ref[:, sl] = jnp.dot(_mx(p), vc, preferred_element_type=F32)
        imp = imp + p
    s_sel = jnp.dot(imp, a_ref[...], preferred_element_type=F32, precision=lax.Precision.HIGHEST)
    st = s_sel.T[:n_sel, :]
    blk = lax.broadcasted_iota(jnp.int32, (n_sel, tq), 0)
    cur = (i * tq + lax.broadcasted_iota(jnp.int32, (n_sel, tq), 1)) // SEL_LEN
    forced = (blk == 0) | (blk == cur) | (blk == cur - 1)
    allowed = blk <= cur
    v = jnp.where(allowed, jnp.where(forced, jnp.inf, st), -jnp.inf)
    rank = jnp.zeros((n_sel, tq), jnp.int32)
    for r in range(n_sel):
        vr = v[r:r + 1, :]
        before = (vr > v) | ((vr == v) & (blk > r))
        rank = rank + before.astype(jnp.int32)
    sel = ((rank < n_top) & allowed).astype(F32)
    if n_sel < tq:
        sel = jnp.concatenate([sel, jnp.zeros((tq - n_sel, tq), F32)], axis=0)
    sel_ref[...] = sel.T.astype(sel_ref.dtype)


def cmp_select(h, kvc, a_mat, nb, t):
    tq = 128
    nt = t // tq
    n_sel = t // SEL_LEN
    assert n_sel <= tq and n_sel % 8 == 0
    ncp = kvc.shape[2]
    gw = NSA_GROUP * HEAD_DIM
    return pl.pallas_call(
        functools.partial(_cmp_select_kernel, tq=tq, n_sel=n_sel, n_top=min(SEL_TOP, n_sel)),
        grid=(nb, NSA_KV, nt),
        in_specs=[
            pl.BlockSpec((tq, gw), lambda b, g, i: (b * nt + i, g)),
            pl.BlockSpec((None, None, ncp, HEAD_DIM), lambda b, g, i: (b, g, 0, 0)),
            pl.BlockSpec((None, None, ncp, HEAD_DIM), lambda b, g, i: (b, NSA_KV + g, 0, 0)),
            pl.BlockSpec((ncp, 128), lambda b, g, i: (0, 0)),
        ],
        out_specs=[
            pl.BlockSpec((tq, gw), lambda b, g, i: (b * nt + i, g)),
            pl.BlockSpec((None, None, tq, 128), lambda b, g, i: (b, g, i, 0)),
        ],
        out_shape=[
            jax.ShapeDtypeStruct((nb * t, NSA_HEADS * HEAD_DIM), F32),
            jax.ShapeDtypeStruct((nb, NSA_KV, t, 128), MXU_DTYPE),
        ],
        compiler_params=_params("arbitrary", "arbitrary", "arbitrary"),
        name="cmp_select",
    )(h, kvc, kvc, a_mat)


def _flash_kernel(*refs, mode, tile, nk):
    if mode == "fox":
        q_ref, k_ref, v_ref, cq_ref, ck_ref, o_ref, m_s, l_s, acc = refs
    elif mode == "slc":
        q_ref, k_ref, v_ref, sel_ref, o_ref, m_s, l_s, acc = refs
    else:
        q_ref, k_ref, v_ref, o_ref, m_s, l_s, acc = refs
    qi, kj = pl.program_id(2), pl.program_id(3)
    kt = qi - 1 + kj if mode == "win" else kj

    @pl.when(kj == 0)
    def _():
        m_s[...] = jnp.full_like(m_s, NEG_BIG)
        l_s[...] = jnp.zeros_like(l_s)
        acc[...] = jnp.zeros_like(acc)

    active = (kt >= 0) if mode == "win" else (kj <= qi)

    @pl.when(active)
    def _():
        s = lax.dot_general(_mx(q_ref[...]), _mx(k_ref[...]), NT_DIMS, preferred_element_type=F32)
        s = s * (HEAD_DIM ** -0.5)
        qpos = qi * tile + lax.broadcasted_iota(jnp.int32, (tile, tile), 0)
        kpos = kt * tile + lax.broadcasted_iota(jnp.int32, (tile, tile), 1)
        mask = kpos <= qpos
        if mode == "fox":
            s = s + (cq_ref[...] - ck_ref[...])
        elif mode == "slc":
            blk = lax.broadcasted_iota(jnp.int32, (128, tile), 0)
            kblk = (kt * tile + lax.broadcasted_iota(jnp.int32, (128, tile), 1)) // SEL_LEN
            expand = (blk == kblk).astype(MXU_DTYPE)
            selk = jnp.dot(sel_ref[...], expand, preferred_element_type=F32)
            mask = mask & (selk > 0.5)
        else:
            mask = mask & (kpos > qpos - WINDOW)
        sm = jnp.where(mask, s, NEG_BIG)
        m_old = m_s[...]
        m_new = jnp.maximum(m_old, jnp.max(sm, axis=-1, keepdims=True))
        p = jnp.where(mask, jnp.exp(sm - m_new), 0.0)
        alpha = jnp.exp(m_old - m_new)
        l_s[...] = alpha * l_s[...] + jnp.sum(p, axis=-1, keepdims=True)
        acc[...] = alpha * acc[...] + jnp.dot(_mx(p), _mx(v_ref[...]), preferred_element_type=F32)
        m_s[...] = m_new

    @pl.when(kj == nk - 1)
    def _():
        l = l_s[...]
        o_ref[...] = acc[...] / jnp.where(l > 0, l, 1.0)


def flash_attention(mode, nb, t, q, qcol, k, kcol, v, vcol, extra=()):
    tile = WINDOW
    assert t % tile == 0
    nt = t // tile
    nk = 2 if mode == "win" else nt
    if mode == "win":
        def ktile(qi, kj):
            return jnp.maximum(qi - 1 + kj, 0)
    else:
        def ktile(qi, kj):
            return jnp.minimum(kj, qi)
    in_specs = [
        pl.BlockSpec((tile, HEAD_DIM), lambda b, h, qi, kj: (b * nt + qi, qcol(h))),
        pl.BlockSpec((tile, HEAD_DIM), lambda b, h, qi, kj: (b * nt + ktile(qi, kj), kcol(h))),
        pl.BlockSpec((tile, HEAD_DIM), lambda b, h, qi, kj: (b * nt + ktile(qi, kj), vcol(h))),
    ]
    if mode == "fox":
        in_specs += [
            pl.BlockSpec((None, None, tile, 1), lambda b, h, qi, kj: (b, h, qi, 0)),
            pl.BlockSpec((None, None, 1, tile), lambda b, h, qi, kj: (b, h, 0, ktile(qi, kj))),
        ]
    elif mode == "slc":
        in_specs += [pl.BlockSpec((None, None, tile, 128), lambda b, h, qi, kj: (b, h // NSA_GROUP, qi, 0))]
    return pl.pallas_call(
        functools.partial(_flash_kernel, mode=mode, tile=tile, nk=nk),
        grid=(nb, NSA_HEADS, nt, nk),
        in_specs=in_specs,
        out_specs=pl.BlockSpec((tile, HEAD_DIM), lambda b, h, qi, kj: (b * nt + qi, h)),
        out_shape=jax.ShapeDtypeStruct((nb * t, NSA_HEADS * HEAD_DIM), F32),
        scratch_shapes=[pltpu.VMEM((tile, 1), F32), pltpu.VMEM((tile, 1), F32), pltpu.VMEM((tile, HEAD_DIM), F32)],
        compiler_params=_params("arbitrary", "arbitrary", "arbitrary", "arbitrary"),
        name="flash_" + mode,
    )(q, k, v, *extra)


def _combine_kernel(oc_ref, os_ref, ow_ref, of_ref, g_ref, o_ref):
    gates = g_ref[...]
    nw = NSA_HEADS * HEAD_DIM
    for h in range(NSA_HEADS):
        sl = slice(h * HEAD_DIM, (h + 1) * HEAD_DIM)
        o = (gates[:, 3 * h:3 * h + 1] * oc_ref[:, sl] + gates[:, 3 * h + 1:3 * h + 2] * os_ref[:, sl]
             + gates[:, 3 * h + 2:3 * h + 3] * ow_ref[:, sl])
        o_ref[:, sl] = o.astype(o_ref.dtype)
    o_ref[:, nw:] = of_ref[...].astype(o_ref.dtype)


def combine_heads(o_c, o_s, o_w, o_f, gates):
    m, nw = o_c.shape
    tm = _pick(m, (512, 256, 128))
    spec = pl.BlockSpec((tm, nw), lambda i: (i, 0))
    return pl.pallas_call(
        _combine_kernel,
        grid=(m // tm,),
        in_specs=[spec, spec, spec, spec, pl.BlockSpec((tm, 128), lambda i: (i, 0))],
        out_specs=pl.BlockSpec((tm, 2 * nw), lambda i: (i, 0)),
        out_shape=jax.ShapeDtypeStruct((m, 2 * nw), MXU_DTYPE),
        compiler_params=_params("arbitrary"),
        name="combine_heads",
    )(o_c, o_s, o_w, o_f, gates)


def _gather_pages_kernel(pt_ref, x_ref, o_ref):
    for c in range(2 * NSA_KV):
        o_ref[c] = x_ref[:, c * HEAD_DIM:(c + 1) * HEAD_DIM]


def gather_cmp_pages(pool, e, page_table):
    nb, n_pages = page_table.shape
    page = pool.shape[2]
    return pl.pallas_call(
        _gather_pages_kernel,
        grid_spec=pltpu.PrefetchScalarGridSpec(
            num_scalar_prefetch=1,
            grid=(nb, n_pages),
            in_specs=[pl.BlockSpec((None, None, page, 2 * NSA_KV * HEAD_DIM), lambda b, p, pt: (e, pt[b, p], 0, 0))],
            out_specs=pl.BlockSpec((None, 2 * NSA_KV, page, HEAD_DIM), lambda b, p, pt: (b, 0, p, 0)),
        ),
        out_shape=jax.ShapeDtypeStruct((nb, 2 * NSA_KV, n_pages * page, HEAD_DIM), F32),
        compiler_params=_params("arbitrary", "arbitrary"),
        name="gather_cmp_pages",
    )(page_table, pool)


def _sample_cmp_win_kernel(qraw_ref, qrot_ref, kc_ref, vc_ref, a_ref, kw_ref, vw_ref,
                           oc_ref, ow_ref, idx_ref, *, n_c, n_sel, n_top, q_pos):
    scale = HEAD_DIM ** -0.5
    ncp = kc_ref.shape[0]
    rows = qraw_ref.shape[0]
    cend = lax.broadcasted_iota(jnp.int32, (1, ncp), 1) * CMP_STRIDE + CMP_LEN
    cidx = lax.broadcasted_iota(jnp.int32, (1, ncp), 1)
    cmask = (cend <= q_pos + 1) & (cidx < n_c)
    lg = lax.dot_general(_mx(qraw_ref[...]), _mx(kc_ref[...]), NT_DIMS, preferred_element_type=F32) * scale
    p = _masked_softmax(lg, cmask)
    oc_ref[...] = jnp.dot(_mx(p), _mx(vc_ref[...]), preferred_element_type=F32)
    head = lax.broadcasted_iota(jnp.int32, p.shape, 0)
    imp = jnp.sum(jnp.where(head < NSA_GROUP, p, 0.0), axis=0, keepdims=True)
    imp = jnp.broadcast_to(imp, (rows, ncp))
    s_sel = jnp.dot(imp, a_ref[...], preferred_element_type=F32, precision=lax.Precision.HIGHEST)
    nsp = s_sel.shape[1]
    lane = lax.broadcasted_iota(jnp.int32, (rows, nsp), 1).astype(F32)
    cur = q_pos // SEL_LEN
    forced = (lane == 0.0) | (lane == float(cur)) | (lane == float(cur - 1))
    v = jnp.where(lane <= float(cur), jnp.where(forced, jnp.inf, s_sel), -jnp.inf)
    out_lane = lax.broadcasted_iota(jnp.int32, (rows, 128), 1)
    picked = jnp.full((rows, 128), -1.0, F32)
    for r in range(n_top):
        mx = jnp.max(v, axis=1, keepdims=True)
        ix = jnp.min(jnp.where(v == mx, lane, float(nsp)), axis=1, keepdims=True)
        ix = jnp.where(mx > -jnp.inf, ix, -1.0)
        picked = jnp.where(out_lane == r, ix, picked)
        v = jnp.where(lane == ix, -jnp.inf, v)
    idx_ref[...] = picked.astype(jnp.int32)
    lw = lax.dot_general(_mx(qrot_ref[...]), _mx(kw_ref[...]), NT_DIMS, preferred_element_type=F32) * scale
    pw = _masked_softmax(lw, jnp.full(lw.shape, True))
    ow_ref[...] = jnp.dot(_mx(pw), _mx(vw_ref[...]), preferred_element_type=F32)


def sample_cmp_win(qraw, qrot, kvc, a_mat, win, n_c, n_sel, q_pos):
    nb, _, rows, _ = qraw.shape
    ncp = kvc.shape[2]
    nsp = a_mat.shape[1]
    wlen = win.shape[1]
    qspec = pl.BlockSpec((None, None, rows, HEAD_DIM), lambda b, g: (b, g, 0, 0))
    return pl.pallas_call(
        functools.partial(_sample_cmp_win_kernel, n_c=n_c, n_sel=n_sel, n_top=min(SEL_TOP, n_sel), q_pos=q_pos),
        grid=(nb, NSA_KV),
        in_specs=[
            qspec, qspec,
            pl.BlockSpec((None, None, ncp, HEAD_DIM), lambda b, g: (b, g, 0, 0)),
            pl.BlockSpec((None, None, ncp, HEAD_DIM), lambda b, g: (b, NSA_KV + g, 0, 0)),
            pl.BlockSpec((ncp, nsp), lambda b, g: (0, 0)),
            pl.BlockSpec((None, wlen, HEAD_DIM), lambda b, g: (b, 0, g)),
            pl.BlockSpec((None, wlen, HEAD_DIM), lambda b, g: (b, 0, NSA_KV + g)),
        ],
        out_specs=[qspec, qspec, pl.BlockSpec((None, None, rows, 128), lambda b, g: (b, g, 0, 0))],
        out_shape=[
            jax.ShapeDtypeStruct(qraw.shape, F32),
            jax.ShapeDtypeStruct(qraw.shape, F32),
            jax.ShapeDtypeStruct((nb, NSA_KV, rows, 128), jnp.int32),
        ],
        compiler_params=_params("arbitrary", "arbitrary"),
        name="sample_cmp_win",
    )(qraw, qrot, kvc, kvc, a_mat, win, win)


def _softmax_update(m_s, l_s, acc, s, v):
    m_old = m_s[...]
    m_new = jnp.maximum(m_old, jnp.max(s, axis=-1, keepdims=True))
    p = jnp.exp(s - m_new)
    alpha = jnp.exp(m_old - m_new)
    l_s[...] = alpha * l_s[...] + jnp.sum(p, axis=-1, keepdims=True)
    acc[...] = alpha * acc[...] + jnp.dot(_mx(p), _mx(v), preferred_element_type=F32)
    m_s[...] = m_new


def _sample_slc_kernel(pt_ref, idx_ref, q_ref, k_ref, v_ref, kn_ref, vn_ref, o_ref, m_s, l_s, acc,
                       *, n_top, n_past_blk):
    b, g, r = pl.program_id(0), pl.program_id(1), pl.program_id(2)
    row = b * NSA_KV + g
    scale = HEAD_DIM ** -0.5

    @pl.when(r == 0)
    def _():
        m_s[...] = jnp.full_like(m_s, NEG_BIG)
        l_s[...] = jnp.zeros_like(l_s)
        acc[...] = jnp.zeros_like(acc)

    blk = idx_ref[row, r]

    @pl.when((blk >= 0) & (blk < n_past_blk))
    def _():
        s = lax.dot_general(_mx(q_ref[...]), _mx(k_ref[...]), NT_DIMS, preferred_element_type=F32) * scale
        _softmax_update(m_s, l_s, acc, s, v_ref[...])

    @pl.when(r == n_top - 1)
    def _():
        has_new = idx_ref[row, 0] == n_past_blk
        for t in range(1, n_top):
            has_new = has_new | (idx_ref[row, t] == n_past_blk)
        q = q_ref[...]
        s_new = jnp.sum(q * kn_ref[...], axis=-1, keepdims=True) * scale
        s_new = jnp.where(has_new, s_new, NEG_BIG)
        m_old = m_s[...]
        m_new = jnp.maximum(m_old, s_new)
        p = jnp.where(has_new, jnp.exp(s_new - m_new), 0.0)
        alpha = jnp.exp(m_old - m_new)
        l = alpha * l_s[...] + p
        o = alpha * acc[...] + p * vn_ref[...]
        o_ref[...] = o / jnp.where(l > 0, l, 1.0)


def sample_slc(qrot, pool, e, page_table, idx, slc_new, n_past_blk):
    nb, _, rows, _ = qrot.shape
    n_top = idx.shape[1]
    page = pool.shape[2]
    per_page = page // SEL_LEN

    def kv_map(col):
        def f(b, g, r, pt, ix):
            blk = jnp.clip(ix[b * NSA_KV + g, r], 0, n_past_blk - 1)
            return (e, pt[b, blk // per_page], blk % per_page, col(g))
        return f

    return pl.pallas_call(
        functools.partial(_sample_slc_kernel, n_top=n_top, n_past_blk=n_past_blk),
        grid_spec=pltpu.PrefetchScalarGridSpec(
            num_scalar_prefetch=2,
            grid=(nb, NSA_KV, n_top),
            in_specs=[
                pl.BlockSpec((None, None, rows, HEAD_DIM), lambda b, g, r, pt, ix: (b, g, 0, 0)),
                pl.BlockSpec((None, None, SEL_LEN, HEAD_DIM), kv_map(lambda g: g)),
                pl.BlockSpec((None, None, SEL_LEN, HEAD_DIM), kv_map(lambda g: NSA_KV + g)),
                pl.BlockSpec((None, 1, HEAD_DIM), lambda b, g, r, pt, ix: (b, 0, g)),
                pl.BlockSpec((None, 1, HEAD_DIM), lambda b, g, r, pt, ix: (b, 0, NSA_KV + g)),
            ],
            out_specs=pl.BlockSpec((None, None, rows, HEAD_DIM), lambda b, g, r, pt, ix: (b, g, 0, 0)),
            scratch_shapes=[pltpu.VMEM((rows, 1), F32), pltpu.VMEM((rows, 1), F32), pltpu.VMEM((rows, HEAD_DIM), F32)],
        ),
        out_shape=jax.ShapeDtypeStruct(qrot.shape, F32),
        compiler_params=_params("arbitrary", "arbitrary", "arbitrary"),
        name="sample_slc",
    )(page_table, idx, qrot, pool, pool, slc_new, slc_new)


def _sample_fox_kernel(pt_ref, q_ref, k_ref, v_ref, lf_ref, tri_ref, kn_ref, vn_ref, lfn_ref, o_ref,
                       qbd, m_s, l_s, acc, crun, *, n_pages):
    p_id = pl.program_id(1)
    scale = HEAD_DIM ** -0.5
    nh = FOX_HEADS
    width = nh * HEAD_DIM

    @pl.when(p_id == 0)
    def _():
        q = q_ref[...]
        qt = jnp.concatenate([q] * nh, axis=1)
        r_id = lax.broadcasted_iota(jnp.int32, (nh, width), 0)
        c_id = lax.broadcasted_iota(jnp.int32, (nh, width), 1) // HEAD_DIM
        qbd[...] = jnp.where(r_id == c_id, qt, 0.0)
        m_s[...] = jnp.full_like(m_s, NEG_BIG)
        l_s[...] = jnp.zeros_like(l_s)
        acc[...] = jnp.zeros_like(acc)
        crun[...] = jnp.zeros_like(crun)

    ck = jnp.dot(lf_ref[...], tri_ref[...], preferred_element_type=F32,
                 precision=lax.Precision.HIGHEST) + crun[...]
    crun[...] = ck[:, -1:]
    s = lax.dot_general(_mx(qbd[...]), _mx(k_ref[...]), NT_DIMS, preferred_element_type=F32) * scale - ck
    _softmax_update(m_s, l_s, acc, s, v_ref[...])

    @pl.when(p_id == n_pages - 1)
    def _():
        cq = crun[...] + lfn_ref[...]
        s_new = jnp.sum(q_ref[...] * kn_ref[...], axis=-1, keepdims=True) * scale - cq
        m_old = m_s[...]
        m_new = jnp.maximum(m_old, s_new)
        p = jnp.exp(s_new - m_new)
        alpha = jnp.exp(m_old - m_new)
        l = alpha * l_s[...] + p
        o = (alpha * acc[...] + p * vn_ref[...]) / l
        for h in range(nh):
            o_ref[h:h + 1, :] = o[h:h + 1, h * HEAD_DIM:(h + 1) * HEAD_DIM]


def sample_fox(fq, pool, logf_t, e, page_table, k_new, v_new, logf_new):
    nb, n_pages = page_table.shape
    page = pool.shape[2]
    width = FOX_HEADS * HEAD_DIM
    tri = jnp.triu(jnp.ones((page, page), F32))
    return pl.pallas_call(
        functools.partial(_sample_fox_kernel, n_pages=n_pages),
        grid_spec=pltpu.PrefetchScalarGridSpec(
            num_scalar_prefetch=1,
            grid=(nb, n_pages),
            in_specs=[
                pl.BlockSpec((None, FOX_HEADS, HEAD_DIM), lambda b, p, pt: (b, 0, 0)),
                pl.BlockSpec((None, None, page, width), lambda b, p, pt: (e, pt[b, p], 0, 0)),
                pl.BlockSpec((None, None, page, width), lambda b, p, pt: (e, pt[b, p], 0, 1)),
                pl.BlockSpec((None, None, FOX_HEADS, page), lambda b, p, pt: (e, pt[b, p], 0, 0)),
                pl.BlockSpec((page, page), lambda b, p, pt: (0, 0)),
                pl.BlockSpec((None, FOX_HEADS, HEAD_DIM), lambda b, p, pt: (b, 0, 0)),
                pl.BlockSpec((None, 1, width), lambda b, p, pt: (b, 0, 0)),
                pl.BlockSpec((None, FOX_HEADS, 1), lambda b, p, pt: (b, 0, 0)),
            ],
            out_specs=pl.BlockSpec((None, FOX_HEADS, HEAD_DIM), lambda b, p, pt: (b, 0, 0)),
            scratch_shapes=[
                pltpu.VMEM((FOX_HEADS, width), F32),
                pltpu.VMEM((FOX_HEADS, 1), F32), pltpu.VMEM((FOX_HEADS, 1), F32),
                pltpu.VMEM((FOX_HEADS, width), F32), pltpu.VMEM((FOX_HEADS, 1), F32),
            ],
        ),
        out_shape=jax.ShapeDtypeStruct((nb, FOX_HEADS, HEAD_DIM), F32),
        compiler_params=_params("arbitrary", "arbitrary"),
        name="sample_fox",
    )(page_table, fq, pool, pool, logf_t, tri, k_new, v_new, logf_new)


def _ret_tables(chunk):
    lg = jnp.log1p(-(2.0 ** (-5.0 - jnp.arange(RET_HEADS, dtype=F32))))
    i = jnp.arange(chunk, dtype=F32)
    diff = i[:, None] - i[None, :]
    dec = jnp.where(diff >= 0, jnp.exp(jnp.maximum(diff, 0.0)[None] * lg[:, None, None]), 0.0)
    xi = jnp.exp((i[None, :] + 1.0) * lg[:, None])
    zeta = jnp.exp((chunk - 1.0 - i)[None, :] * lg[:, None])
    g_c = jnp.exp(chunk * lg)
    return dec, xi, zeta, g_c


def _ret_rope_tables(pos):
    half = RET_DK // 2
    inv = 1.0 / (RET_THETA ** (jnp.arange(half, dtype=F32) / half))
    ang = pos.astype(F32)[:, None] * inv[None, :]
    return jnp.cos(ang), jnp.sin(ang)


def _group_norm_gate(o, gate, gn):
    mu = jnp.mean(o, axis=-1, keepdims=True)
    d = o - mu
    var = jnp.mean(d * d, axis=-1, keepdims=True)
    y = d * lax.rsqrt(var + GN_EPS) * gn
    return gate * jax.nn.sigmoid(gate) * y


def _retention_kernel(q_ref, k_ref, v_ref, gate_ref, cos_ref, sin_ref, dec_ref, coef_ref, gn_ref,
                      y_ref, s_ref, state, *, n_chunks):
    c = pl.program_id(2)
    half = RET_DK // 2

    @pl.when(c == 0)
    def _():
        state[...] = jnp.zeros_like(state)

    cos, sin = cos_ref[...], sin_ref[...]

    def rot(x_ref):
        x1, x2 = x_ref[:, :half], x_ref[:, half:]
        return jnp.concatenate([x1 * cos - x2 * sin, x2 * cos + x1 * sin], axis=-1)

    coef = coef_ref[...]
    xi, zeta, g_c = coef[:, 0:1], coef[:, 1:2], coef[0:1, 2:3]
    q = _mx(rot(q_ref))
    kf = rot(k_ref) * (RET_DK ** -0.5)
    v = _mx(v_ref[...])
    s_old = state[...]
    a = lax.dot_general(q, _mx(kf), NT_DIMS, preferred_element_type=F32) * dec_ref[...]
    o = (jnp.dot(_mx(a), v, preferred_element_type=F32)
         + jnp.dot(q, _mx(s_old), preferred_element_type=F32) * xi)
    s_new = s_old * g_c + lax.dot_general(_mx(kf * zeta), v, TN_DIMS, preferred_element_type=F32)
    state[...] = s_new
    y_ref[...] = _group_norm_gate(o, gate_ref[...], gn_ref[...]).astype(y_ref.dtype)

    @pl.when(c == n_chunks - 1)
    def _():
        s_ref[...] = s_new


def retention_prompt(h, nb, t, gn_g, o_idx):
    ch = RET_CHUNK
    n_chunks = t // ch
    dec, xi, zeta, g_c = _ret_tables(ch)
    coef = jnp.stack([xi, zeta, jnp.broadcast_to(g_c[:, None], xi.shape)], axis=-1)
    coef = jnp.pad(coef, ((0, 0), (0, 0), (0, 128 - 3)))
    cos, sin = _ret_rope_tables(jnp.arange(t))
    kb, vb = O_RK // RET_DK, O_RV // RET_DV
    gb = O_RG // RET_DV
    return pl.pallas_call(
        functools.partial(_retention_kernel, n_chunks=n_chunks),
        grid=(nb, RET_HEADS, n_chunks),
        in_specs=[
            pl.BlockSpec((ch, RET_DK), lambda b, hh, c: (b * n_chunks + c, hh)),
            pl.BlockSpec((ch, RET_DK), lambda b, hh, c: (b * n_chunks + c, kb + hh)),
            pl.BlockSpec((ch, RET_DV), lambda b, hh, c: (b * n_chunks + c, vb + hh)),
            pl.BlockSpec((ch, RET_DV), lambda b, hh, c: (b * n_chunks + c, gb + hh)),
            pl.BlockSpec((ch, RET_DK // 2), lambda b, hh, c: (c, 0)),
            pl.BlockSpec((ch, RET_DK // 2), lambda b, hh, c: (c, 0)),
            pl.BlockSpec((None, ch, ch), lambda b, hh, c: (hh, 0, 0)),
            pl.BlockSpec((None, ch, 128), lambda b, hh, c: (hh, 0, 0)),
            pl.BlockSpec((None, 1, RET_DV), lambda b, hh, c: (o_idx, 0, hh)),
        ],
        out_specs=[
            pl.BlockSpec((ch, RET_DV), lambda b, hh, c: (b * n_chunks + c, hh)),
            pl.BlockSpec((None, None, RET_DK, RET_DV), lambda b, hh, c: (b, hh, 0, 0)),
        ],
        out_shape=[
            jax.ShapeDtypeStruct((nb * t, RET_HEADS * RET_DV), MXU_DTYPE),
            jax.ShapeDtypeStruct((nb, RET_HEADS, RET_DK, RET_DV), F32),
        ],
        scratch_shapes=[pltpu.VMEM((RET_DK, RET_DV), F32)],
        compiler_params=_params("arbitrary", "arbitrary", "arbitrary"),
        name="retention_prompt",
    )(h, h, h, h, cos, sin, dec, coef, gn_g[:, None, :])


def _retention_step_kernel(q_ref, k_ref, v_ref, gate_ref, cos_ref, sin_ref, coef_ref, gn_ref, s0_ref,
                           y_ref, s_ref):
    half = RET_DK // 2
    hh = pl.program_id(1)
    cos, sin = cos_ref[...], sin_ref[...]

    def rot(x_ref):
        x1, x2 = x_ref[:half, :], x_ref[half:, :]
        return jnp.concatenate([x1 * cos - x2 * sin, x2 * cos + x1 * sin], axis=0)

    head = lax.broadcasted_iota(jnp.int32, coef_ref.shape, 0)
    coef = jnp.sum(jnp.where(head == hh, coef_ref[...], 0.0), axis=0, keepdims=True)
    dec, xi, zeta, g_c = coef[:, 0:1], coef[:, 1:2], coef[:, 2:3], coef[:, 3:4]
    q = rot(q_ref)
    k = rot(k_ref) * (RET_DK ** -0.5)
    v = v_ref[...]
    s_old = s0_ref[...]
    a = jnp.sum(q * k, axis=0, keepdims=True) * dec
    o = a * v + jnp.sum(q * s_old, axis=0, keepdims=True) * xi
    s_ref[...] = s_old * g_c + (k * zeta) * v
    y_ref[...] = _group_norm_gate(o, gate_ref[...], gn_ref[...])


def retention_step(q_col, k_col, v_row, gate_row, pos, s0, o_idx, gn_g):
    nb = q_col.shape[0]
    dec, xi, zeta, g_c = _ret_tables(1)
    coef = jnp.stack([dec[:, 0, 0], xi[:, 0], zeta[:, 0], g_c], axis=-1)
    coef = jnp.pad(coef, ((0, 0), (0, 128 - 4)))
    cos, sin = _ret_rope_tables(pos)
    col = pl.BlockSpec((None, None, RET_DK, 1), lambda b, hh: (b, hh, 0, 0))
    row = pl.BlockSpec((None, None, 1, RET_DV), lambda b, hh: (b, hh, 0, 0))
    tab = pl.BlockSpec((RET_DK // 2, 1), lambda b, hh: (0, 0))
    return pl.pallas_call(
        _retention_step_kernel,
        grid=(nb, RET_HEADS),
        in_specs=[
            col, col, row, row, tab, tab,
            pl.BlockSpec((RET_HEADS, 128), lambda b, hh: (0, 0)),
            pl.BlockSpec((None, 1, RET_DV), lambda b, hh: (o_idx, 0, hh)),
            pl.BlockSpec((None, None, None, RET_DK, RET_DV), lambda b, hh: (o_idx, b, hh, 0, 0)),
        ],
        out_specs=[row, pl.BlockSpec((None, None, RET_DK, RET_DV), lambda b, hh: (b, hh, 0, 0))],
        out_shape=[
            jax.ShapeDtypeStruct((nb, RET_HEADS, 1, RET_DV), F32),
            jax.ShapeDtypeStruct((nb, RET_HEADS, RET_DK, RET_DV), F32),
        ],
        compiler_params=_params("arbitrary", "arbitrary"),
        name="retention_step",
    )(q_col, k_col, v_row, gate_row, cos.reshape(-1, 1), sin.reshape(-1, 1), coef, gn_g[:, None, :], s0)


def _pad_rows(x, rows):
    return jnp.pad(x, ((0, rows - x.shape[0]),) + ((0, 0),) * (x.ndim - 1))


def _repack_even(w):
    small = jnp.concatenate([w[:, O_NG:O_FQ], w[:, O_FF:E_EVEN]], axis=1)
    small = jnp.pad(small, ((0, 0), (0, 128 - small.shape[1])))
    return jnp.concatenate([w[:, O_NQ:O_NG], w[:, O_FQ:O_FF], small], axis=1)


def _even_layer(e, xb_p, xb_s, nb, t, ns, past, caches, page_table, wts):
    cache_cmp, cache_slc, cache_win, cache_fox, logf_pool_t = caches
    w_in, fox_f_bias, cmp_pos, cmp_w1, cmp_w2 = wts
    w_rep = _repack_even(w_in[e])
    fb_row = jnp.zeros((1, 128), F32).at[0, N_GATE:N_GATE + FOX_HEADS].set(fox_f_bias[e])
    hq = P_FQ // HEAD_DIM
    kvw = 2 * NSA_KV * HEAD_DIM
    nw = NSA_HEADS * HEAD_DIM

    h_p = matmul(xb_p, w_rep, (), (1152, 640, 384, 128))
    tm_post = _pick(nb * t, (512, 256, 128))
    q_rot, slc_p, win_p, gates_p, logf_p = even_post(
        h_p, _rope_tables(jnp.arange(t)), fb_row, t // tm_post)
    c_p = cumsum_rows(logf_p, nb, t)
    c_heads = jnp.swapaxes(c_p[:, N_GATE:N_GATE + FOX_HEADS].reshape(nb, t, FOX_HEADS), 1, 2)
    n16 = t // CMP_STRIDE
    kvc_p = nsa_compress(
        h_p, pl.BlockSpec((t, HEAD_DIM), lambda b, c: (b, P_NKV // HEAD_DIM + c)), nb, t, e, cmp_w1, cmp_pos, cmp_w2)
    a_p = _sel_matrix(n16, n16 - 1, t // SEL_LEN, 128)
    o_c, sel = cmp_select(h_p, kvc_p, a_p, nb, t)
    o_s = flash_attention("slc", nb, t, q_rot, lambda h: h, slc_p, lambda h: h // NSA_GROUP,
                          slc_p, lambda h: NSA_KV + h // NSA_GROUP, (sel,))
    o_w = flash_attention("win", nb, t, q_rot, lambda h: h, win_p, lambda h: h // NSA_GROUP,
                          win_p, lambda h: NSA_KV + h // NSA_GROUP)
    o_f = flash_attention("fox", nb, t, h_p, lambda h: hq + h, h_p, lambda h: hq + FOX_HEADS + h,
                          h_p, lambda h: hq + 2 * FOX_HEADS + h, (c_heads[..., None], c_heads[:, :, None, :]))
    xcat_p = combine_heads(o_c, o_s, o_w, o_f, gates_p)
    st_p = (
        h_p[:, P_NKV:P_NKV + kvw].reshape(nb, t, 2, NSA_KV, HEAD_DIM),
        slc_p.reshape(nb, t, 2, NSA_KV, HEAD_DIM),
        win_p.reshape(nb, t, 2, NSA_KV, HEAD_DIM)[:, t - min(WINDOW, t):],
        h_p[:, P_FQ + nw:P_FQ + 3 * nw].reshape(nb, t, 2, FOX_HEADS, HEAD_DIM),
        logf_p[:, N_GATE:N_GATE + FOX_HEADS].reshape(nb, t, FOX_HEADS),
    )

    rows = xb_s.shape[0]
    h_s = matmul(xb_s, w_rep, (), (1152, 640, 384, 128))
    q_rot_s, slc_s, win_s, gates_s, logf_s = even_post(
        h_s, _rope_tables(jnp.full((rows,), past)), fb_row, 1)
    new_win = jnp.concatenate(
        [cache_win[e].reshape(ns, -1, kvw), win_s[:ns, None, :]], axis=1)[:, 1:]
    cmp_rows = gather_cmp_pages(cache_cmp, e, page_table)
    l_tot = past + 1
    n16_s = l_tot // CMP_STRIDE
    assert n16_s * CMP_STRIDE == past
    kvc_s = nsa_compress(
        cmp_rows, pl.BlockSpec((None, None, past, HEAD_DIM), lambda b, c: (b, c, 0, 0)), ns, past, e,
        cmp_w1, cmp_pos, cmp_w2)
    n_sel_s = -(-l_tot // SEL_LEN)
    nsp = -(-n_sel_s // 128) * 128
    a_s = _sel_matrix(n16_s, n16_s - 1, n_sel_s, nsp)

    def q_rows(x):
        x = x[:ns].reshape(ns, NSA_KV, NSA_GROUP, HEAD_DIM)
        return jnp.pad(x, ((0, 0), (0, 0), (0, 8 - NSA_GROUP), (0, 0)))

    qraw_g, qrot_g = q_rows(h_s[:, :nw]), q_rows(q_rot_s)
    o_c_s, o_w_s, idx = sample_cmp_win(qraw_g, qrot_g, kvc_s, a_s, new_win, n16_s - 1, n_sel_s, past)
    idx2 = idx[:, :, 0, :min(SEL_TOP, n_sel_s)].reshape(ns * NSA_KV, -1)
    o_s_s = sample_slc(qrot_g, cache_slc, e, page_table, idx2, slc_s[:ns, None, :], past // SEL_LEN)
    fq = h_s[:ns, P_FQ:P_FQ + nw].reshape(ns, FOX_HEADS, HEAD_DIM)
    fk = h_s[:ns, P_FQ + nw:P_FQ + 2 * nw]
    fv = h_s[:ns, P_FQ + 2 * nw:P_FQ + 3 * nw]
    logf_new = logf_s[:ns, N_GATE:N_GATE + FOX_HEADS]
    o_f_s = sample_fox(fq, cache_fox, logf_pool_t, e, page_table, fk.reshape(ns, FOX_HEADS, HEAD_DIM),
                       fv[:, None, :], logf_new[:, :, None])

    def heads_flat(x):
        return _pad_rows(x[:, :, :NSA_GROUP].reshape(ns, nw), rows)

    xcat_s = combine_heads(heads_flat(o_c_s), heads_flat(o_s_s), heads_flat(o_w_s),
                           _pad_rows(o_f_s.reshape(ns, nw), rows), gates_s)
    st_s = (
        h_s[:ns, P_NKV:P_NKV + kvw].reshape(ns, 1, 2, NSA_KV, HEAD_DIM),
        slc_s[:ns].reshape(ns, 1, 2, NSA_KV, HEAD_DIM),
        new_win.reshape(ns, -1, 2, NSA_KV, HEAD_DIM),
        h_s[:ns, P_FQ + nw:P_FQ + 3 * nw].reshape(ns, 1, 2, FOX_HEADS, HEAD_DIM),
        logf_new.reshape(ns, 1, FOX_HEADS),
    )
    return xcat_p, xcat_s, st_p, st_s


def _odd_layer(o, xb_p, xb_s, nb, t, ns, past, state_ret, w_in_odd, ret_gn_g):
    h_p = matmul(xb_p, w_in_odd, (o,), (1024, 512, 256, 128))
    y_p, s_p = retention_prompt(h_p, nb, t, ret_gn_g, o)
    rows = xb_s.shape[0]
    h_s = matmul(xb_s, w_in_odd, (o,), (1024, 512, 256, 128))[:ns]
    q_col = h_s[:, :O_RK].reshape(ns, RET_HEADS, RET_DK, 1)
    k_col = h_s[:, O_RK:O_RV].reshape(ns, RET_HEADS, RET_DK, 1)
    v_row = h_s[:, O_RV:O_RG].reshape(ns, RET_HEADS, 1, RET_DV)
    g_row = h_s[:, O_RG:E_ODD].reshape(ns, RET_HEADS, 1, RET_DV)
    y_s, s_s = retention_step(q_col, k_col, v_row, g_row, jnp.full((1,), past), state_ret, o, ret_gn_g)
    y_s = _pad_rows(y_s.reshape(ns, RET_HEADS * RET_DV), rows).astype(MXU_DTYPE)
    return y_p, y_s, s_p, s_s


def kernel(x_prompt, x_sample, cache_nsa_cmp, cache_nsa_slc, cache_nsa_win, cache_fox_kv, cache_fox_logf,
           state_ret, page_table, w_ffn_in, w_ffn_out, ln_g, ln_b, w_in_even, w_out_even, fox_f_bias,
           nsa_cmp_pos, nsa_cmp_w1, nsa_cmp_w2, w_in_odd, ret_gn_g, w_out_odd):
    nb, t, d = x_prompt.shape
    ns, ts, _ = x_sample.shape
    depth = w_ffn_in.shape[0]
    n_pages = page_table.shape[1]
    page = cache_nsa_cmp.shape[2]
    past = n_pages * page
    assert ts == 1 and past >= WINDOW and past % SEL_LEN == 0 and t % RET_CHUNK == 0
    alpha = (2.0 * depth) ** 0.25
    rows = max(16, -(-ns // 16) * 16)

    xp = x_prompt.reshape(nb * t, d)
    xs = _pad_rows(x_sample.reshape(ns * ts, d), rows)
    xp_b, xs_b = xp.astype(MXU_DTYPE), xs.astype(MXU_DTYPE)

    n_pool = cache_nsa_cmp.shape[1]
    kvw = 2 * NSA_KV * HEAD_DIM
    caches = (
        cache_nsa_cmp.reshape(-1, n_pool, page, kvw),
        cache_nsa_slc.reshape(-1, n_pool, page, kvw),
        cache_nsa_win,
        cache_fox_kv.reshape(-1, n_pool, page, 2 * FOX_HEADS * HEAD_DIM),
        jnp.swapaxes(cache_fox_logf, 2, 3),
    )

    def ffn(x, xb, l, s):
        hid = swiglu_hidden(xb, w_ffn_in, l, s)
        return matmul_postnorm(hid, w_ffn_out, (l, s), x, ln_g, ln_b, (l, 2 * s), alpha, 0.5)

    new_p = [[] for _ in range(6)]
    new_s = [[] for _ in range(6)]
    for l in range(depth):
        xp, xp_b = ffn(xp, xp_b, l, 0)
        xs, xs_b = ffn(xs, xs_b, l, 0)
        if l % 2 == 0:
            e = l // 2
            mp, ms, st_p, st_s = _even_layer(
                e, xp_b, xs_b, nb, t, ns, past, caches, page_table,
                (w_in_even, fox_f_bias, nsa_cmp_pos, nsa_cmp_w1, nsa_cmp_w2))
            w_out, widx = w_out_even, (e,)
            slots = (0, 1, 2, 3, 4)
        else:
            o = l // 2
            mp, ms, sp, ss = _odd_layer(o, xp_b, xs_b, nb, t, ns, past, state_ret, w_in_odd, ret_gn_g)
            st_p, st_s = (sp,), (ss,)
            w_out, widx = w_out_odd, (o,)
            slots = (5,)
        for i, a_p, a_s in zip(slots, st_p, st_s):
            new_p[i].append(a_p)
            new_s[i].append(a_s)
        xp, xp_b = matmul_postnorm(mp, w_out, widx, xp, ln_g, ln_b, (l, 1), alpha, 1.0)
        xs, xs_b = matmul_postnorm(ms, w_out, widx, xs, ln_g, ln_b, (l, 1), alpha, 1.0)
        xp, xp_b = ffn(xp, xp_b, l, 1)
        xs, xs_b = ffn(xs, xs_b, l, 1)

    outs = [xp.reshape(nb, t, d), xs[:ns].reshape(ns, ts, d)]
    for i in range(6):
        outs.append(jnp.stack(new_p[i]))
        outs.append(jnp.stack(new_s[i]))
    return tuple(outs)
```

```python
import functools
import math

import numpy as np
import jax
import jax.numpy as jnp
from jax import lax
from jax.experimental import pallas as pl
from jax.experimental.pallas import tpu as pltpu

F32 = jnp.float32
MXU_DTYPE = jnp.bfloat16

HEAD_DIM = 128
NSA_HEADS = 8
NSA_KV = 2
NSA_GROUP = NSA_HEADS // NSA_KV
CMP_LEN = 32
CMP_STRIDE = 16
CMP_HIDDEN = 256
SEL_LEN = 64
SEL_TOP = 16
WINDOW = 512
FOX_HEADS = 8
ROPE_THETA = 500000.0
ROPE_DIMS = HEAD_DIM // 4
RET_HEADS = 8
RET_DK = 256
RET_DV = 512
RET_CHUNK = 128
RET_THETA = 10000.0
LN_EPS = 1e-5
GN_EPS = 1e-6

O_NQ = 0
O_NKV = O_NQ + NSA_HEADS * HEAD_DIM
O_NG = O_NKV + 6 * NSA_KV * HEAD_DIM
O_FQ = O_NG + 3 * NSA_HEADS
O_FK = O_FQ + FOX_HEADS * HEAD_DIM
O_FV = O_FK + FOX_HEADS * HEAD_DIM
O_FF = O_FV + FOX_HEADS * HEAD_DIM
E_EVEN = O_FF + FOX_HEADS
P_NKV = NSA_HEADS * HEAD_DIM
P_FQ = P_NKV + 6 * NSA_KV * HEAD_DIM
P_SMALL = P_FQ + 3 * FOX_HEADS * HEAD_DIM
P_EVEN = P_SMALL + 128
N_GATE = 3 * NSA_HEADS
O_RK = RET_HEADS * RET_DK
O_RV = 2 * RET_HEADS * RET_DK
O_RG = O_RV + RET_HEADS * RET_DV
E_ODD = O_RG + RET_HEADS * RET_DV

V7X_VMEM_LIMIT = 56 * 1024 * 1024
LN_ROWS = 128
NEG_BIG = -1e30
NT_DIMS = (((1,), (1,)), ((), ()))
TN_DIMS = (((0,), (0,)), ((), ()))


def _params(*sem):
    return pltpu.CompilerParams(dimension_semantics=sem, vmem_limit_bytes=V7X_VMEM_LIMIT)


def _mx(x):
    return x.astype(MXU_DTYPE)


def _pick(n, prefs):
    for p in prefs:
        if n % p == 0:
            return p
    return n


def _swiglu_kernel(x_ref, wa_ref, wb_ref, o_ref, wa_s, wb_s):
    @pl.when(pl.program_id(1) == 0)
    def _():
        wa_s[...] = _mx(wa_ref[...])
        wb_s[...] = _mx(wb_ref[...])
    x = x_ref[...]
    a = jnp.dot(x, wa_s[...], preferred_element_type=F32)
    b = jnp.dot(x, wb_s[...], preferred_element_type=F32)
    o_ref[...] = (a * jax.nn.sigmoid(a) * b).astype(o_ref.dtype)


def swiglu_hidden(xb, w_in, l, s):
    m, d = xb.shape
    f = w_in.shape[-1] // 2
    tm = _pick(m, (1024, 512, 256, 128))
    tn = _pick(f, (512, 256, 128))
    nj = f // tn
    return pl.pallas_call(
        _swiglu_kernel,
        grid=(nj, m // tm),
        in_specs=[
            pl.BlockSpec((tm, d), lambda j, i: (i, 0)),
            pl.BlockSpec((None, None, d, tn), lambda j, i: (l, s, 0, j)),
            pl.BlockSpec((None, None, d, tn), lambda j, i: (l, s, 0, j + nj)),
        ],
        out_specs=pl.BlockSpec((tm, tn), lambda j, i: (i, j)),
        out_shape=jax.ShapeDtypeStruct((m, f), MXU_DTYPE),
        scratch_shapes=[pltpu.VMEM((d, tn), MXU_DTYPE), pltpu.VMEM((d, tn), MXU_DTYPE)],
        compiler_params=_params("arbitrary", "arbitrary"),
        name="swiglu_hidden",
    )(xb, w_in, w_in)


def _mm_ln_kernel(x_ref, w_ref, r_ref, g_ref, b_ref, o_ref, ob_ref, *, alpha, scale, nk):
    k = pl.program_id(1)
    def part():
        return jnp.dot(x_ref[...], _mx(w_ref[...]), preferred_element_type=F32)

    @pl.when(k == 0)
    def _():
        o_ref[...] = part()

    @pl.when(k > 0)
    def _():
        o_ref[...] += part()

    @pl.when(k == nk - 1)
    def _():
        rows = min(LN_ROWS, o_ref.shape[0])

        def norm_rows(c, carry):
            sl = pl.ds(pl.multiple_of(c * rows, rows), rows)
            z = alpha * r_ref[sl, :] + scale * o_ref[sl, :]
            mu = jnp.mean(z, axis=-1, keepdims=True)
            dz = z - mu
            var = jnp.mean(dz * dz, axis=-1, keepdims=True)
            y = dz * lax.rsqrt(var + LN_EPS) * g_ref[...] + b_ref[...]
            o_ref[sl, :] = y
            ob_ref[sl, :] = y.astype(ob_ref.dtype)
            return carry

        lax.fori_loop(0, o_ref.shape[0] // rows, norm_rows, 0)


def matmul_postnorm(xb, w, widx, res, g, b, gidx, alpha, scale):
    m, kdim = xb.shape
    n = w.shape[-1]
    tm = _pick(m, (1024, 512, 256, 128))
    tk = _pick(kdim, (512, 256, 128))
    nk = kdim // tk
    nlead = len(widx)
    w_spec = pl.BlockSpec((None,) * nlead + (tk, n), lambda i, k: tuple(widx) + (k, 0))
    g_spec = pl.BlockSpec((None, None, 1, n), lambda i, k: tuple(gidx) + (0, 0))
    return pl.pallas_call(
        functools.partial(_mm_ln_kernel, alpha=alpha, scale=scale, nk=nk),
        grid=(m // tm, nk),
        in_specs=[
            pl.BlockSpec((tm, tk), lambda i, k: (i, k)),
            w_spec,
            pl.BlockSpec((tm, n), lambda i, k: (i, 0)),
            g_spec, g_spec,
        ],
        out_specs=[pl.BlockSpec((tm, n), lambda i, k: (i, 0)), pl.BlockSpec((tm, n), lambda i, k: (i, 0))],
        out_shape=[jax.ShapeDtypeStruct((m, n), F32), jax.ShapeDtypeStruct((m, n), MXU_DTYPE)],
        compiler_params=_params("arbitrary", "arbitrary"),
        name="matmul_postnorm",
    )(xb, w, res, g[:, :, None, :], b[:, :, None, :])


def _mm_kernel(x_ref, w_ref, o_ref, w_s):
    @pl.when(pl.program_id(1) == 0)
    def _():
        w_s[...] = _mx(w_ref[...])
    o_ref[...] = jnp.dot(x_ref[...], w_s[...], preferred_element_type=F32)


def matmul(xb, w, widx, tn_prefs):
    m, kdim = xb.shape
    n = w.shape[-1]
    tm = _pick(m, (1024, 512, 256, 128))
    tn = _pick(n, tn_prefs)
    nlead = len(widx)
    return pl.pallas_call(
        _mm_kernel,
        grid=(n // tn, m // tm),
        in_specs=[
            pl.BlockSpec((tm, kdim), lambda j, i: (i, 0)),
            pl.BlockSpec((None,) * nlead + (kdim, tn), lambda j, i: tuple(widx) + (0, j)),
        ],
        out_specs=pl.BlockSpec((tm, tn), lambda j, i: (i, j)),
        out_shape=jax.ShapeDtypeStruct((m, n), F32),
        scratch_shapes=[pltpu.VMEM((kdim, tn), MXU_DTYPE)],
        compiler_params=_params("arbitrary", "arbitrary"),
        name="matmul",
    )(xb, w)


def _rope_tables(pos):
    half = ROPE_DIMS // 2
    inv = 1.0 / (ROPE_THETA ** (jnp.arange(half, dtype=F32) / half))
    ang = pos.astype(F32)[:, None] * inv[None, :]
    cos, sin = jnp.cos(ang), jnp.sin(ang)
    n = pos.shape[0]
    ones = jnp.ones((n, HEAD_DIM - ROPE_DIMS), F32)
    zeros = jnp.zeros((n, HEAD_DIM - ROPE_DIMS), F32)
    zh = jnp.zeros((n, half), F32)
    c = jnp.concatenate([cos, cos, ones], -1)
    a = jnp.concatenate([-sin, zh, zeros], -1)
    b = jnp.concatenate([zh, sin, zeros], -1)
    return c, a, b


def _rope(x, c, a, b):
    half = ROPE_DIMS // 2
    return x * c + pltpu.roll(x, HEAD_DIM - half, 1) * a + pltpu.roll(x, half, 1) * b


def _even_post_kernel(q_ref, slc_ref, win_ref, sm_ref, c_ref, a_ref, b_ref, fb_ref,
                      qr_ref, slco_ref, wino_ref, gate_ref, logf_ref):
    c, a, b = c_ref[...], a_ref[...], b_ref[...]
    for h in range(NSA_HEADS):
        sl = slice(h * HEAD_DIM, (h + 1) * HEAD_DIM)
        qr_ref[:, sl] = _rope(q_ref[:, sl], c, a, b)
    for src, dst in ((slc_ref, slco_ref), (win_ref, wino_ref)):
        for g in range(NSA_KV):
            sl = slice(g * HEAD_DIM, (g + 1) * HEAD_DIM)
            dst[:, sl] = _rope(src[:, sl], c, a, b)
        vs = slice(NSA_KV * HEAD_DIM, 2 * NSA_KV * HEAD_DIM)
        dst[:, vs] = src[:, vs]
    sm = sm_ref[...]
    gate_ref[...] = jax.nn.sigmoid(sm)
    z = sm + fb_ref[...]
    logf_ref[...] = jnp.minimum(z, 0.0) - jnp.log1p(jnp.exp(-jnp.abs(z)))


def even_post(h, tabs, fb_row, npos_blocks):
    m = h.shape[0]
    tm = _pick(m, (512, 256, 128))
    kvw = 2 * NSA_KV * HEAD_DIM
    tab_spec = pl.BlockSpec((tm, HEAD_DIM), lambda i: (i % npos_blocks, 0))
    return pl.pallas_call(
        _even_post_kernel,
        grid=(m // tm,),
        in_specs=[
            pl.BlockSpec((tm, P_NKV), lambda i: (i, 0)),
            pl.BlockSpec((tm, kvw), lambda i: (i, (P_NKV + kvw) // kvw)),
            pl.BlockSpec((tm, kvw), lambda i: (i, (P_NKV + 2 * kvw) // kvw)),
            pl.BlockSpec((tm, 128), lambda i: (i, P_SMALL // 128)),
            tab_spec, tab_spec, tab_spec,
            pl.BlockSpec((1, 128), lambda i: (0, 0)),
        ],
        out_specs=[
            pl.BlockSpec((tm, P_NKV), lambda i: (i, 0)),
            pl.BlockSpec((tm, kvw), lambda i: (i, 0)),
            pl.BlockSpec((tm, kvw), lambda i: (i, 0)),
            pl.BlockSpec((tm, 128), lambda i: (i, 0)),
            pl.BlockSpec((tm, 128), lambda i: (i, 0)),
        ],
        out_shape=[
            jax.ShapeDtypeStruct((m, P_NKV), F32),
            jax.ShapeDtypeStruct((m, kvw), F32),
            jax.ShapeDtypeStruct((m, kvw), F32),
            jax.ShapeDtypeStruct((m, 128), F32),
            jax.ShapeDtypeStruct((m, 128), F32),
        ],
        compiler_params=_params("arbitrary"),
        name="even_post",
    )(h, h, h, h, *tabs, fb_row)


def _cumsum_kernel(x_ref, tri_ref, o_ref, carry):
    @pl.when(pl.program_id(1) == 0)
    def _():
        carry[...] = jnp.zeros_like(carry)
    c = jnp.dot(tri_ref[...], x_ref[...], preferred_element_type=F32,
                precision=lax.Precision.HIGHEST) + carry[...]
    o_ref[...] = c
    carry[...] = c[-1:, :]


def cumsum_rows(x, nb, t):
    tc = _pick(t, (512, 256, 128))
    tri = jnp.tril(jnp.ones((tc, tc), F32))
    nt = t // tc
    return pl.pallas_call(
        _cumsum_kernel,
        grid=(nb, nt),
        in_specs=[pl.BlockSpec((tc, 128), lambda b, i: (b * nt + i, 0)),
                  pl.BlockSpec((tc, tc), lambda b, i: (0, 0))],
        out_specs=pl.BlockSpec((tc, 128), lambda b, i: (b * nt + i, 0)),
        out_shape=jax.ShapeDtypeStruct(x.shape, F32),
        scratch_shapes=[pltpu.VMEM((1, 128), F32)],
        compiler_params=_params("arbitrary", "arbitrary"),
        name="cumsum_rows",
    )(x, tri)


def _compress_kernel(x_ref, w1_ref, pe_ref, w2_ref, o_ref, u0, u1, *, n16):
    u0[...] = jnp.zeros_like(u0)
    u1[...] = jnp.zeros_like(u1)
    for r in range(CMP_STRIDE):
        xr = x_ref[pl.ds(r, n16, stride=CMP_STRIDE), :]
        u0[...] += jnp.dot(_mx(xr + pe_ref[r:r + 1, :]), _mx(w1_ref[r]), preferred_element_type=F32)
        u1[...] += jnp.dot(_mx(xr + pe_ref[CMP_STRIDE + r:CMP_STRIDE + r + 1, :]), _mx(w1_ref[CMP_STRIDE + r]),
                           preferred_element_type=F32)
    pre = u0[...] + pltpu.roll(u1[...], n16 - 1, 0)
    out = jnp.dot(_mx(jax.nn.gelu(pre)), _mx(w2_ref[...]), preferred_element_type=F32)
    row = lax.broadcasted_iota(jnp.int32, out.shape, 0)
    o_ref[...] = jnp.where(row < n16 - 1, out, 0.0)


def nsa_compress(x, x_spec, nb, length, e, w1, pe, w2):
    n16 = length // CMP_STRIDE
    w1r = w1.reshape(w1.shape[0], 2, CMP_LEN, HEAD_DIM, CMP_HIDDEN)
    return pl.pallas_call(
        functools.partial(_compress_kernel, n16=n16),
        grid=(nb, 2 * NSA_KV),
        in_specs=[
            x_spec,
            pl.BlockSpec((None, None, CMP_LEN, HEAD_DIM, CMP_HIDDEN), lambda b, c: (e, c // NSA_KV, 0, 0, 0)),
            pl.BlockSpec((None, None, CMP_LEN, HEAD_DIM), lambda b, c: (e, c // NSA_KV, 0, 0)),
            pl.BlockSpec((None, None, CMP_HIDDEN, HEAD_DIM), lambda b, c: (e, c // NSA_KV, 0, 0)),
        ],
        out_specs=pl.BlockSpec((None, None, n16, HEAD_DIM), lambda b, c: (b, c, 0, 0)),
        out_shape=jax.ShapeDtypeStruct((nb, 2 * NSA_KV, n16, HEAD_DIM), F32),
        scratch_shapes=[pltpu.VMEM((n16, CMP_HIDDEN), F32), pltpu.VMEM((n16, CMP_HIDDEN), F32)],
        compiler_params=_params("arbitrary", "arbitrary"),
        name="nsa_compress",
    )(x, w1r, pe, w2)


def _sel_matrix(n_c_pad, n_c, n_sel, width):
    ratio = SEL_LEN // CMP_STRIDE
    i = np.arange(n_c_pad)[:, None]
    j = np.arange(width)[None, :]
    a = (i >= ratio * j - 1) & (i <= ratio * j + ratio - 1) & (i < n_c) & (j < n_sel)
    return jnp.asarray(a.astype(np.float32))


def _masked_softmax(lg, mask):
    lg = jnp.where(mask, lg, -jnp.inf)
    m = jnp.max(lg, axis=-1, keepdims=True)
    m = jnp.where(m == -jnp.inf, 0.0, m)
    p = jnp.exp(lg - m)
    s = jnp.sum(p, axis=-1, keepdims=True)
    return p / jnp.where(s > 0, s, 1.0)


def _cmp_select_kernel(q_ref, kc_ref, vc_ref, a_ref, oc_ref, sel_ref, *, tq, n_sel, n_top):
    i = pl.program_id(2)
    ncp = kc_ref.shape[0]
    qpos = i * tq + lax.broadcasted_iota(jnp.int32, (tq, 1), 0)
    cend = lax.broadcasted_iota(jnp.int32, (1, ncp), 1) * CMP_STRIDE + CMP_LEN
    cmask = cend <= qpos + 1
    kc = _mx(kc_ref[...])
    vc = _mx(vc_ref[...])
    scale = HEAD_DIM ** -0.5
    imp = jnp.zeros((tq, ncp), F32)
    for m in range(NSA_GROUP):
        sl = slice(m * HEAD_DIM, (m + 1) * HEAD_DIM)
        lg = lax.dot_general(_mx(q_ref[:, sl]), kc, NT_DIMS, preferred_element_type=F32) * scale
        p = _masked_softmax(lg, cmask)
        oc_ref[:, sl] = jnp.dot(_mx(p), vc, preferred_element_type=F32)
        imp = imp + p
    s_sel = jnp.dot(imp, a_ref[...], preferred_element_type=F32, precision=lax.Precision.HIGHEST)
    st = s_sel.T[:n_sel, :]
    blk = lax.broadcasted_iota(jnp.int32, (n_sel, tq), 0)
    cur = (i * tq + lax.broadcasted_iota(jnp.int32, (n_sel, tq), 1)) // SEL_LEN
    forced = (blk == 0) | (blk == cur) | (blk == cur - 1)
    allowed = blk <= cur
    v = jnp.where(allowed, jnp.where(forced, jnp.inf, st), -jnp.inf)
    rank = jnp.zeros((n_sel, tq), jnp.int32)
    for r in range(n_sel):
        vr = v[r:r + 1, :]
        before = (vr > v) | ((vr == v) & (blk > r))
        rank = rank + before.astype(jnp.int32)
    sel = ((rank < n_top) & allowed).astype(F32)
    if n_sel < tq:
        sel = jnp.concatenate([sel, jnp.zeros((tq - n_sel, tq), F32)], axis=0)
    sel_ref[...] = sel.T.astype(sel_ref.dtype)


def cmp_select(h, kvc, a_mat, nb, t):
    tq = 128
    nt = t // tq
    n_sel = t // SEL_LEN
    assert n_sel <= tq and n_sel % 8 == 0
    ncp = kvc.shape[2]
    gw = NSA_GROUP * HEAD_DIM
    return pl.pallas_call(
        functools.partial(_cmp_select_kernel, tq=tq, n_sel=n_sel, n_top=min(SEL_TOP, n_sel)),
        grid=(nb, NSA_KV, nt),
        in_specs=[
            pl.BlockSpec((tq, gw), lambda b, g, i: (b * nt + i, g)),
            pl.BlockSpec((None, None, ncp, HEAD_DIM), lambda b, g, i: (b, g, 0, 0)),
            pl.BlockSpec((None, None, ncp, HEAD_DIM), lambda b, g, i: (b, NSA_KV + g, 0, 0)),
            pl.BlockSpec((ncp, 128), lambda b, g, i: (0, 0)),
        ],
        out_specs=[
            pl.BlockSpec((tq, gw), lambda b, g, i: (b * nt + i, g)),
            pl.BlockSpec((None, None, tq, 128), lambda b, g, i: (b, g, i, 0)),
        ],
        out_shape=[
            jax.ShapeDtypeStruct((nb * t, NSA_HEADS * HEAD_DIM), F32),
            jax.ShapeDtypeStruct((nb, NSA_KV, t, 128), MXU_DTYPE),
        ],
        compiler_params=_params("arbitrary", "arbitrary", "arbitrary"),
        name="cmp_select",
    )(h, kvc, kvc, a_mat)


def _flash_steps(mode, t, tq, tk):
    rows = []
    for qi in range(t // tq):
        q_lo, q_hi = qi * tq, qi * tq + tq - 1
        k_hi = q_hi // tk
        k_lo = max((q_lo - WINDOW + 1) // tk, 0) if mode == "win" else 0
        for kt in range(k_lo, k_hi + 1):
            every_key_visible = kt * tk + tk - 1 <= q_lo
            rows.append((qi, kt, int(kt == k_lo), int(kt == k_hi), int(not every_key_visible)))
    return jnp.asarray(np.array(rows, np.int32).T)


def _flash_kernel(*refs, mode, tq, tk):
    if mode == "fox":
        tab, q_ref, k_ref, v_ref, cq_ref, ck_ref, o_ref, m_s, l_s, acc = refs
    elif mode == "slc":
        tab, q_ref, k_ref, v_ref, sel_ref, o_ref, m_s, l_s, acc = refs
    else:
        tab, q_ref, k_ref, v_ref, o_ref, m_s, l_s, acc = refs
    step = pl.program_id(2)
    qi, kt = tab[0, step], tab[1, step]

    @pl.when(tab[2, step] == 1)
    def _():
        m_s[...] = jnp.full_like(m_s, NEG_BIG)
        l_s[...] = jnp.zeros_like(l_s)
        acc[...] = jnp.zeros_like(acc)

    def update(causal):
        s = lax.dot_general(_mx(q_ref[...]), _mx(k_ref[...]), NT_DIMS, preferred_element_type=F32)
        s = s * (HEAD_DIM ** -0.5)
        mask = None
        if causal or mode == "win":
            qpos = qi * tq + lax.broadcasted_iota(jnp.int32, (tq, tk), 0)
            kpos = kt * tk + lax.broadcasted_iota(jnp.int32, (tq, tk), 1)
            mask = kpos <= qpos
            if mode == "win":
                mask = mask & (kpos > qpos - WINDOW)
        if mode == "fox":
            s = s + (cq_ref[...] - ck_ref[...])
        elif mode == "slc":
            blk = lax.broadcasted_iota(jnp.int32, (128, tk), 0)
            kblk = (kt * tk + lax.broadcasted_iota(jnp.int32, (128, tk), 1)) // SEL_LEN
            expand = (blk == kblk).astype(MXU_DTYPE)
            chosen = jnp.dot(sel_ref[...], expand, preferred_element_type=F32) > 0.5
            mask = chosen if mask is None else mask & chosen
        m_old = m_s[...]
        if mask is None:
            m_new = jnp.maximum(m_old, jnp.max(s, axis=-1, keepdims=True))
            p = jnp.exp(s - m_new)
        else:
            sm = jnp.where(mask, s, NEG_BIG)
            m_new = jnp.maximum(m_old, jnp.max(sm, axis=-1, keepdims=True))
            p = jnp.where(mask, jnp.exp(sm - m_new), 0.0)
        alpha = jnp.exp(m_old - m_new)
        l_s[...] = alpha * l_s[...] + jnp.sum(p, axis=-1, keepdims=True)
        acc[...] = alpha * acc[...] + jnp.dot(_mx(p), _mx(v_ref[...]), preferred_element_type=F32)
        m_s[...] = m_new

    if mode == "win":
        update(True)
    else:
        pl.when(tab[4, step] == 1)(functools.partial(update, True))
        pl.when(tab[4, step] == 0)(functools.partial(update, False))

    @pl.when(tab[3, step] == 1)
    def _():
        l = l_s[...]
        o_ref[...] = acc[...] / jnp.where(l > 0, l, 1.0)


def flash_attention(mode, nb, t, q, qcol, k, kcol, v, vcol, extra=()):
    tk = _pick(t, (512, 256, 128))
    tq = _pick(t, (1024, 512, 256, 128))
    ntq, ntk = t // tq, t // tk
    steps = _flash_steps(mode, t, tq, tk)
    in_specs = [
        pl.BlockSpec((tq, HEAD_DIM), lambda b, h, s, tab: (b * ntq + tab[0, s], qcol(h))),
        pl.BlockSpec((tk, HEAD_DIM), lambda b, h, s, tab: (b * ntk + tab[1, s], kcol(h))),
        pl.BlockSpec((tk, HEAD_DIM), lambda b, h, s, tab: (b * ntk + tab[1, s], vcol(h))),
    ]
    if mode == "fox":
        in_specs += [
            pl.BlockSpec((None, None, tq, 1), lambda b, h, s, tab: (b, h, tab[0, s], 0)),
            pl.BlockSpec((None, None, 1, tk), lambda b, h, s, tab: (b, h, 0, tab[1, s])),
        ]
    elif mode == "slc":
        in_specs += [pl.BlockSpec((None, None, tq, 128), lambda b, h, s, tab: (b, h // NSA_GROUP, tab[0, s], 0))]
    return pl.pallas_call(
        functools.partial(_flash_kernel, mode=mode, tq=tq, tk=tk),
        grid_spec=pltpu.PrefetchScalarGridSpec(
            num_scalar_prefetch=1,
            grid=(nb, NSA_HEADS, steps.shape[1]),
            in_specs=in_specs,
            out_specs=pl.BlockSpec((tq, HEAD_DIM), lambda b, h, s, tab: (b * ntq + tab[0, s], h)),
            scratch_shapes=[pltpu.VMEM((tq, 1), F32), pltpu.VMEM((tq, 1), F32), pltpu.VMEM((tq, HEAD_DIM), F32)],
        ),
        out_shape=jax.ShapeDtypeStruct((nb * t, NSA_HEADS * HEAD_DIM), F32),
        compiler_params=_params("arbitrary", "arbitrary", "arbitrary"),
        name="flash_" + mode,
    )(steps, q, k, v, *extra)


def _combine_kernel(oc_ref, os_ref, ow_ref, of_ref, g_ref, o_ref):
    gates = g_ref[...]
    nw = NSA_HEADS * HEAD_DIM
    for h in range(NSA_HEADS):
        sl = slice(h * HEAD_DIM, (h + 1) * HEAD_DIM)
        o = (gates[:, 3 * h:3 * h + 1] * oc_ref[:, sl] + gates[:, 3 * h + 1:3 * h + 2] * os_ref[:, sl]
             + gates[:, 3 * h + 2:3 * h + 3] * ow_ref[:, sl])
        o_ref[:, sl] = o.astype(o_ref.dtype)
    o_ref[:, nw:] = of_ref[...].astype(o_ref.dtype)


def combine_heads(o_c, o_s, o_w, o_f, gates):
    m, nw = o_c.shape
    tm = _pick(m, (512, 256, 128))
    spec = pl.BlockSpec((tm, nw), lambda i: (i, 0))
    return pl.pallas_call(
        _combine_kernel,
        grid=(m // tm,),
        in_specs=[spec, spec, spec, spec, pl.BlockSpec((tm, 128), lambda i: (i, 0))],
        out_specs=pl.BlockSpec((tm, 2 * nw), lambda i: (i, 0)),
        out_shape=jax.ShapeDtypeStruct((m, 2 * nw), MXU_DTYPE),
        compiler_params=_params("arbitrary"),
        name="combine_heads",
    )(o_c, o_s, o_w, o_f, gates)


def _gather_pages_kernel(pt_ref, *refs, page):
    *x_refs, o_ref = refs
    for j, x_ref in enumerate(x_refs):
        for c in range(2 * NSA_KV):
            o_ref[c, j * page:(j + 1) * page, :] = x_ref[:, c // NSA_KV, c % NSA_KV, :]


def gather_cmp_pages(pool, e, page_table):
    nb, n_pages = page_table.shape
    page = pool.shape[2]
    per_step = _pick(n_pages, (4, 2, 1))

    def page_spec(j):
        return pl.BlockSpec((None, None, page, 2, NSA_KV, HEAD_DIM),
                            lambda b, p, pt: (e, pt[b, p * per_step + j], 0, 0, 0, 0))

    return pl.pallas_call(
        functools.partial(_gather_pages_kernel, page=page),
        grid_spec=pltpu.PrefetchScalarGridSpec(
            num_scalar_prefetch=1,
            grid=(nb, n_pages // per_step),
            in_specs=[page_spec(j) for j in range(per_step)],
            out_specs=pl.BlockSpec((None, 2 * NSA_KV, per_step * page, HEAD_DIM), lambda b, p, pt: (b, 0, p, 0)),
        ),
        out_shape=jax.ShapeDtypeStruct((nb, 2 * NSA_KV, n_pages * page, HEAD_DIM), F32),
        compiler_params=_params("arbitrary", "arbitrary"),
        name="gather_cmp_pages",
    )(page_table, *([pool] * per_step))


def _sample_cmp_win_kernel(qraw_ref, qrot_ref, kc_ref, vc_ref, a_ref, kw_ref, vw_ref,
                           oc_ref, ow_ref, idx_ref, *, n_c, n_sel, n_top, q_pos):
    scale = HEAD_DIM ** -0.5
    ncp = kc_ref.shape[0]
    rows = qraw_ref.shape[0]
    cend = lax.broadcasted_iota(jnp.int32, (1, ncp), 1) * CMP_STRIDE + CMP_LEN
    cidx = lax.broadcasted_iota(jnp.int32, (1, ncp), 1)
    cmask = (cend <= q_pos + 1) & (cidx < n_c)
    lg = lax.dot_general(_mx(qraw_ref[...]), _mx(kc_ref[...]), NT_DIMS, preferred_element_type=F32) * scale
    p = _masked_softmax(lg, cmask)
    oc_ref[...] = jnp.dot(_mx(p), _mx(vc_ref[...]), preferred_element_type=F32)
    head = lax.broadcasted_iota(jnp.int32, p.shape, 0)
    imp = jnp.sum(jnp.where(head < NSA_GROUP, p, 0.0), axis=0, keepdims=True)
    imp = jnp.broadcast_to(imp, (rows, ncp))
    s_sel = jnp.dot(imp, a_ref[...], preferred_element_type=F32, precision=lax.Precision.HIGHEST)
    nsp = s_sel.shape[1]
    lane = lax.broadcasted_iota(jnp.int32, (rows, nsp), 1).astype(F32)
    cur = q_pos // SEL_LEN
    forced = (lane == 0.0) | (lane == float(cur)) | (lane == float(cur - 1))
    v = jnp.where(lane <= float(cur), jnp.where(forced, jnp.inf, s_sel), -jnp.inf)
    out_lane = lax.broadcasted_iota(jnp.int32, (rows, 128), 1)
    picked = jnp.full((rows, 128), -1.0, F32)
    for r in range(n_top):
        mx = jnp.max(v, axis=1, keepdims=True)
        ix = jnp.min(jnp.where(v == mx, lane, float(nsp)), axis=1, keepdims=True)
        ix = jnp.where(mx > -jnp.inf, ix, -1.0)
        picked = jnp.where(out_lane == r, ix, picked)
        v = jnp.where(lane == ix, -jnp.inf, v)
    idx_ref[...] = picked.astype(jnp.int32)
    lw = lax.dot_general(_mx(qrot_ref[...]), _mx(kw_ref[...]), NT_DIMS, preferred_element_type=F32) * scale
    pw = _masked_softmax(lw, jnp.full(lw.shape, True))
    ow_ref[...] = jnp.dot(_mx(pw), _mx(vw_ref[...]), preferred_element_type=F32)


def sample_cmp_win(qraw, qrot, kvc, a_mat, win, n_c, n_sel, q_pos):
    nb, _, rows, _ = qraw.shape
    ncp = kvc.shape[2]
    nsp = a_mat.shape[1]
    wlen = win.shape[1]
    qspec = pl.BlockSpec((None, None, rows, HEAD_DIM), lambda b, g: (b, g, 0, 0))
    return pl.pallas_call(
        functools.partial(_sample_cmp_win_kernel, n_c=n_c, n_sel=n_sel, n_top=min(SEL_TOP, n_sel), q_pos=q_pos),
        grid=(nb, NSA_KV),
        in_specs=[
            qspec, qspec,
            pl.BlockSpec((None, None, ncp, HEAD_DIM), lambda b, g: (b, g, 0, 0)),
            pl.BlockSpec((None, None, ncp, HEAD_DIM), lambda b, g: (b, NSA_KV + g, 0, 0)),
            pl.BlockSpec((ncp, nsp), lambda b, g: (0, 0)),
            pl.BlockSpec((None, wlen, HEAD_DIM), lambda b, g: (b, 0, g)),
            pl.BlockSpec((None, wlen, HEAD_DIM), lambda b, g: (b, 0, NSA_KV + g)),
        ],
        out_specs=[qspec, qspec, pl.BlockSpec((None, None, rows, 128), lambda b, g: (b, g, 0, 0))],
        out_shape=[
            jax.ShapeDtypeStruct(qraw.shape, F32),
            jax.ShapeDtypeStruct(qraw.shape, F32),
            jax.ShapeDtypeStruct((nb, NSA_KV, rows, 128), jnp.int32),
        ],
        compiler_params=_params("arbitrary", "arbitrary"),
        name="sample_cmp_win",
    )(qraw, qrot, kvc, kvc, a_mat, win, win)


def _softmax_update(m_s, l_s, acc, s, v):
    m_old = m_s[...]
    m_new = jnp.maximum(m_old, jnp.max(s, axis=-1, keepdims=True))
    p = jnp.exp(s - m_new)
    alpha = jnp.exp(m_old - m_new)
    l_s[...] = alpha * l_s[...] + jnp.sum(p, axis=-1, keepdims=True)
    acc[...] = alpha * acc[...] + jnp.dot(_mx(p), _mx(v), preferred_element_type=F32)
    m_s[...] = m_new


def _sample_slc_kernel(pt_ref, idx_ref, q_ref, kv_ref, kn_ref, vn_ref, o_ref, m_s, l_s, acc,
                       *, n_top, n_past_blk):
    b, g, r = pl.program_id(0), pl.program_id(1), pl.program_id(2)
    row = b * NSA_KV + g
    scale = HEAD_DIM ** -0.5

    @pl.when(r == 0)
    def _():
        m_s[...] = jnp.full_like(m_s, NEG_BIG)
        l_s[...] = jnp.zeros_like(l_s)
        acc[...] = jnp.zeros_like(acc)

    blk = idx_ref[row, r]
    in_past = (blk >= 0) & (blk < n_past_blk)

    for gg in range(NSA_KV):
        @pl.when(in_past & (g == gg))
        def _(gg=gg):
            s = lax.dot_general(_mx(q_ref[...]), _mx(kv_ref[:, 0, gg, :]), NT_DIMS,
                                preferred_element_type=F32) * scale
            _softmax_update(m_s, l_s, acc, s, kv_ref[:, 1, gg, :])

    @pl.when(r == n_top - 1)
    def _():
        has_new = idx_ref[row, 0] == n_past_blk
        for t in range(1, n_top):
            has_new = has_new | (idx_ref[row, t] == n_past_blk)
        q = q_ref[...]
        s_new = jnp.sum(q * kn_ref[...], axis=-1, keepdims=True) * scale
        s_new = jnp.where(has_new, s_new, NEG_BIG)
        m_old = m_s[...]
        m_new = jnp.maximum(m_old, s_new)
        p = jnp.where(has_new, jnp.exp(s_new - m_new), 0.0)
        alpha = jnp.exp(m_old - m_new)
        l = alpha * l_s[...] + p
        o = alpha * acc[...] + p * vn_ref[...]
        o_ref[...] = o / jnp.where(l > 0, l, 1.0)


def sample_slc(qrot, pool, e, page_table, idx, slc_new, n_past_blk):
    nb, _, rows, _ = qrot.shape
    n_top = idx.shape[1]
    page = pool.shape[2]
    per_page = page // SEL_LEN

    def kv_map(b, g, r, pt, ix):
        blk = jnp.clip(ix[b * NSA_KV + g, r], 0, n_past_blk - 1)
        return (e, pt[b, blk // per_page], blk % per_page, 0, 0, 0)

    return pl.pallas_call(
        functools.partial(_sample_slc_kernel, n_top=n_top, n_past_blk=n_past_blk),
        grid_spec=pltpu.PrefetchScalarGridSpec(
            num_scalar_prefetch=2,
            grid=(nb, NSA_KV, n_top),
            in_specs=[
                pl.BlockSpec((None, None, rows, HEAD_DIM), lambda b, g, r, pt, ix: (b, g, 0, 0)),
                pl.BlockSpec((None, None, SEL_LEN, 2, NSA_KV, HEAD_DIM), kv_map),
                pl.BlockSpec((None, 1, HEAD_DIM), lambda b, g, r, pt, ix: (b, 0, g)),
                pl.BlockSpec((None, 1, HEAD_DIM), lambda b, g, r, pt, ix: (b, 0, NSA_KV + g)),
            ],
            out_specs=pl.BlockSpec((None, None, rows, HEAD_DIM), lambda b, g, r, pt, ix: (b, g, 0, 0)),
            scratch_shapes=[pltpu.VMEM((rows, 1), F32), pltpu.VMEM((rows, 1), F32), pltpu.VMEM((rows, HEAD_DIM), F32)],
        ),
        out_shape=jax.ShapeDtypeStruct(qrot.shape, F32),
        compiler_params=_params("arbitrary", "arbitrary", "arbitrary"),
        name="sample_slc",
    )(page_table, idx, qrot, pool, slc_new, slc_new)


def _sample_fox_kernel(pt_ref, q_ref, k_ref, v_ref, lf_ref, tri_ref, kn_ref, vn_ref, lfn_ref, o_ref,
                       qbd, m_s, l_s, acc, crun, *, n_pages):
    p_id = pl.program_id(1)
    scale = HEAD_DIM ** -0.5
    nh = FOX_HEADS
    width = nh * HEAD_DIM

    @pl.when(p_id == 0)
    def _():
        q = q_ref[...]
        qt = jnp.concatenate([q] * nh, axis=1)
        r_id = lax.broadcasted_iota(jnp.int32, (nh, width), 0)
        c_id = lax.broadcasted_iota(jnp.int32, (nh, width), 1) // HEAD_DIM
        qbd[...] = jnp.where(r_id == c_id, qt, 0.0)
        m_s[...] = jnp.full_like(m_s, NEG_BIG)
        l_s[...] = jnp.zeros_like(l_s)
        acc[...] = jnp.zeros_like(acc)
        crun[...] = jnp.zeros_like(crun)

    ck = jnp.dot(lf_ref[...], tri_ref[...], preferred_element_type=F32,
                 precision=lax.Precision.HIGHEST) + crun[...]
    crun[...] = ck[:, -1:]
    k_all = jnp.concatenate([k_ref[:, h, :] for h in range(nh)], axis=1)
    v_all = jnp.concatenate([v_ref[:, h, :] for h in range(nh)], axis=1)
    s = lax.dot_general(_mx(qbd[...]), _mx(k_all), NT_DIMS, preferred_element_type=F32) * scale - ck
    _softmax_update(m_s, l_s, acc, s, v_all)

    @pl.when(p_id == n_pages - 1)
    def _():
        cq = crun[...] + lfn_ref[...]
        s_new = jnp.sum(q_ref[...] * kn_ref[...], axis=-1, keepdims=True) * scale - cq
        m_old = m_s[...]
        m_new = jnp.maximum(m_old, s_new)
        p = jnp.exp(s_new - m_new)
        alpha = jnp.exp(m_old - m_new)
        l = alpha * l_s[...] + p
        o = (alpha * acc[...] + p * vn_ref[...]) / l
        for h in range(nh):
            o_ref[h:h + 1, :] = o[h:h + 1, h * HEAD_DIM:(h + 1) * HEAD_DIM]


def sample_fox(fq, pool, logf_t, e, page_table, k_new, v_new, logf_new):
    nb, n_pages = page_table.shape
    page = pool.shape[2]
    width = FOX_HEADS * HEAD_DIM
    tri = jnp.triu(jnp.ones((page, page), F32))
    return pl.pallas_call(
        functools.partial(_sample_fox_kernel, n_pages=n_pages),
        grid_spec=pltpu.PrefetchScalarGridSpec(
            num_scalar_prefetch=1,
            grid=(nb, n_pages),
            in_specs=[
                pl.BlockSpec((None, FOX_HEADS, HEAD_DIM), lambda b, p, pt: (b, 0, 0)),
                pl.BlockSpec((None, None, page, None, FOX_HEADS, HEAD_DIM), lambda b, p, pt: (e, pt[b, p], 0, 0, 0, 0)),
                pl.BlockSpec((None, None, page, None, FOX_HEADS, HEAD_DIM), lambda b, p, pt: (e, pt[b, p], 0, 1, 0, 0)),
                pl.BlockSpec((None, None, FOX_HEADS, page), lambda b, p, pt: (e, pt[b, p], 0, 0)),
                pl.BlockSpec((page, page), lambda b, p, pt: (0, 0)),
                pl.BlockSpec((None, FOX_HEADS, HEAD_DIM), lambda b, p, pt: (b, 0, 0)),
                pl.BlockSpec((None, 1, width), lambda b, p, pt: (b, 0, 0)),
                pl.BlockSpec((None, FOX_HEADS, 1), lambda b, p, pt: (b, 0, 0)),
            ],
            out_specs=pl.BlockSpec((None, FOX_HEADS, HEAD_DIM), lambda b, p, pt: (b, 0, 0)),
            scratch_shapes=[
                pltpu.VMEM((FOX_HEADS, width), F32),
                pltpu.VMEM((FOX_HEADS, 1), F32), pltpu.VMEM((FOX_HEADS, 1), F32),
                pltpu.VMEM((FOX_HEADS, width), F32), pltpu.VMEM((FOX_HEADS, 1), F32),
            ],
        ),
        out_shape=jax.ShapeDtypeStruct((nb, FOX_HEADS, HEAD_DIM), F32),
        compiler_params=_params("arbitrary", "arbitrary"),
        name="sample_fox",
    )(page_table, fq, pool, pool, logf_t, tri, k_new, v_new, logf_new)


def _ret_tables(chunk):
    lg = jnp.log1p(-(2.0 ** (-5.0 - jnp.arange(RET_HEADS, dtype=F32))))
    i = jnp.arange(chunk, dtype=F32)
    diff = i[:, None] - i[None, :]
    dec = jnp.where(diff >= 0, jnp.exp(jnp.maximum(diff, 0.0)[None] * lg[:, None, None]), 0.0)
    xi = jnp.exp((i[None, :] + 1.0) * lg[:, None])
    zeta = jnp.exp((chunk - 1.0 - i)[None, :] * lg[:, None])
    g_c = jnp.exp(chunk * lg)
    return dec, xi, zeta, g_c


def _ret_rope_tables(pos):
    half = RET_DK // 2
    inv = 1.0 / (RET_THETA ** (jnp.arange(half, dtype=F32) / half))
    ang = pos.astype(F32)[:, None] * inv[None, :]
    return jnp.cos(ang), jnp.sin(ang)


def _group_norm_gate(o, gate, gn):
    mu = jnp.mean(o, axis=-1, keepdims=True)
    d = o - mu
    var = jnp.mean(d * d, axis=-1, keepdims=True)
    y = d * lax.rsqrt(var + GN_EPS) * gn
    return gate * jax.nn.sigmoid(gate) * y


def _retention_kernel(q_ref, k_ref, v_ref, gate_ref, cos_ref, sin_ref, dec_ref, coef_ref, gn_ref,
                      y_ref, s_ref, state, *, n_chunks):
    c = pl.program_id(2)
    half = RET_DK // 2

    @pl.when(c == 0)
    def _():
        state[...] = jnp.zeros_like(state)

    cos, sin = cos_ref[...], sin_ref[...]

    def rot(x_ref):
        x1, x2 = x_ref[:, :half], x_ref[:, half:]
        return jnp.concatenate([x1 * cos - x2 * sin, x2 * cos + x1 * sin], axis=-1)

    coef = coef_ref[...]
    xi, zeta, g_c = coef[:, 0:1], coef[:, 1:2], coef[0:1, 2:3]
    q = _mx(rot(q_ref))
    kf = rot(k_ref) * (RET_DK ** -0.5)
    v = _mx(v_ref[...])
    s_old = state[...]
    a = lax.dot_general(q, _mx(kf), NT_DIMS, preferred_element_type=F32) * dec_ref[...]
    o = (jnp.dot(_mx(a), v, preferred_element_type=F32)
         + jnp.dot(q, _mx(s_old), preferred_element_type=F32) * xi)
    s_new = s_old * g_c + lax.dot_general(_mx(kf * zeta), v, TN_DIMS, preferred_element_type=F32)
    state[...] = s_new
    y_ref[...] = _group_norm_gate(o, gate_ref[...], gn_ref[...]).astype(y_ref.dtype)

    @pl.when(c == n_chunks - 1)
    def _():
        s_ref[...] = s_new


def retention_prompt(h, nb, t, gn_g, o_idx):
    ch = RET_CHUNK
    n_chunks = t // ch
    dec, xi, zeta, g_c = _ret_tables(ch)
    coef = jnp.stack([xi, zeta, jnp.broadcast_to(g_c[:, None], xi.shape)], axis=-1)
    coef = jnp.pad(coef, ((0, 0), (0, 0), (0, 128 - 3)))
    cos, sin = _ret_rope_tables(jnp.arange(t))
    kb, vb = O_RK // RET_DK, O_RV // RET_DV
    gb = O_RG // RET_DV
    return pl.pallas_call(
        functools.partial(_retention_kernel, n_chunks=n_chunks),
        grid=(nb, RET_HEADS, n_chunks),
        in_specs=[
            pl.BlockSpec((ch, RET_DK), lambda b, hh, c: (b * n_chunks + c, hh)),
            pl.BlockSpec((ch, RET_DK), lambda b, hh, c: (b * n_chunks + c, kb + hh)),
            pl.BlockSpec((ch, RET_DV), lambda b, hh, c: (b * n_chunks + c, vb + hh)),
            pl.BlockSpec((ch, RET_DV), lambda b, hh, c: (b * n_chunks + c, gb + hh)),
            pl.BlockSpec((ch, RET_DK // 2), lambda b, hh, c: (c, 0)),
            pl.BlockSpec((ch, RET_DK // 2), lambda b, hh, c: (c, 0)),
            pl.BlockSpec((None, ch, ch), lambda b, hh, c: (hh, 0, 0)),
            pl.BlockSpec((None, ch, 128), lambda b, hh, c: (hh, 0, 0)),
            pl.BlockSpec((None, 1, RET_DV), lambda b, hh, c: (o_idx, 0, hh)),
        ],
        out_specs=[
            pl.BlockSpec((ch, RET_DV), lambda b, hh, c: (b * n_chunks + c, hh)),
            pl.BlockSpec((None, None, RET_DK, RET_DV), lambda b, hh, c: (b, hh, 0, 0)),
        ],
        out_shape=[
            jax.ShapeDtypeStruct((nb * t, RET_HEADS * RET_DV), MXU_DTYPE),
            jax.ShapeDtypeStruct((nb, RET_HEADS, RET_DK, RET_DV), F32),
        ],
        scratch_shapes=[pltpu.VMEM((RET_DK, RET_DV), F32)],
        compiler_params=_params("arbitrary", "arbitrary", "arbitrary"),
        name="retention_prompt",
    )(h, h, h, h, cos, sin, dec, coef, gn_g[:, None, :])


def _retention_step_kernel(q_ref, k_ref, v_ref, gate_ref, cos_ref, sin_ref, coef_ref, gn_ref, s0_ref,
                           y_ref, s_ref):
    half = RET_DK // 2
    hh = pl.program_id(1)
    cos, sin = cos_ref[...], sin_ref[...]

    def rot(x_ref):
        x1, x2 = x_ref[:half, :], x_ref[half:, :]
        return jnp.concatenate([x1 * cos - x2 * sin, x2 * cos + x1 * sin], axis=0)

    head = lax.broadcasted_iota(jnp.int32, coef_ref.shape, 0)
    coef = jnp.sum(jnp.where(head == hh, coef_ref[...], 0.0), axis=0, keepdims=True)
    dec, xi, zeta, g_c = coef[:, 0:1], coef[:, 1:2], coef[:, 2:3], coef[:, 3:4]
    q = rot(q_ref)
    k = rot(k_ref) * (RET_DK ** -0.5)
    v = v_ref[...]
    s_old = s0_ref[...]
    a = jnp.sum(q * k, axis=0, keepdims=True) * dec
    o = a * v + jnp.sum(q * s_old, axis=0, keepdims=True) * xi
    s_ref[...] = s_old * g_c + (k * zeta) * v
    y_ref[...] = _group_norm_gate(o, gate_ref[...], gn_ref[...])


def retention_step(q_col, k_col, v_row, gate_row, pos, s0, o_idx, gn_g):
    nb = q_col.shape[0]
    dec, xi, zeta, g_c = _ret_tables(1)
    coef = jnp.stack([dec[:, 0, 0], xi[:, 0], zeta[:, 0], g_c], axis=-1)
    coef = jnp.pad(coef, ((0, 0), (0, 128 - 4)))
    cos, sin = _ret_rope_tables(pos)
    col = pl.BlockSpec((None, None, RET_DK, 1), lambda b, hh: (b, hh, 0, 0))
    row = pl.BlockSpec((None, None, 1, RET_DV), lambda b, hh: (b, hh, 0, 0))
    tab = pl.BlockSpec((RET_DK // 2, 1), lambda b, hh: (0, 0))
    return pl.pallas_call(
        _retention_step_kernel,
        grid=(nb, RET_HEADS),
        in_specs=[
            col, col, row, row, tab, tab,
            pl.BlockSpec((RET_HEADS, 128), lambda b, hh: (0, 0)),
            pl.BlockSpec((None, 1, RET_DV), lambda b, hh: (o_idx, 0, hh)),
            pl.BlockSpec((None, None, None, RET_DK, RET_DV), lambda b, hh: (o_idx, b, hh, 0, 0)),
        ],
        out_specs=[row, pl.BlockSpec((None, None, RET_DK, RET_DV), lambda b, hh: (b, hh, 0, 0))],
        out_shape=[
            jax.ShapeDtypeStruct((nb, RET_HEADS, 1, RET_DV), F32),
            jax.ShapeDtypeStruct((nb, RET_HEADS, RET_DK, RET_DV), F32),
        ],
        compiler_params=_params("arbitrary", "arbitrary"),
        name="retention_step",
    )(q_col, k_col, v_row, gate_row, cos.reshape(-1, 1), sin.reshape(-1, 1), coef, gn_g[:, None, :], s0)


def _pad_rows(x, rows):
    return jnp.pad(x, ((0, rows - x.shape[0]),) + ((0, 0),) * (x.ndim - 1))


def _repack_even(w):
    small = jnp.concatenate([w[:, O_NG:O_FQ], w[:, O_FF:E_EVEN]], axis=1)
    small = jnp.pad(small, ((0, 0), (0, 128 - small.shape[1])))
    return jnp.concatenate([w[:, O_NQ:O_NG], w[:, O_FQ:O_FF], small], axis=1)


def _even_layer(e, xb_p, xb_s, nb, t, ns, past, caches, page_table, wts):
    cache_cmp, cache_slc, cache_win, cache_fox, logf_pool_t = caches
    w_in, fox_f_bias, cmp_pos, cmp_w1, cmp_w2 = wts
    w_rep = _repack_even(w_in[e])
    fb_row = jnp.zeros((1, 128), F32).at[0, N_GATE:N_GATE + FOX_HEADS].set(fox_f_bias[e])
    hq = P_FQ // HEAD_DIM
    kvw = 2 * NSA_KV * HEAD_DIM
    nw = NSA_HEADS * HEAD_DIM

    h_p = matmul(xb_p, w_rep, (), (1152, 640, 384, 128))
    tm_post = _pick(nb * t, (512, 256, 128))
    q_rot, slc_p, win_p, gates_p, logf_p = even_post(
        h_p, _rope_tables(jnp.arange(t)), fb_row, t // tm_post)
    c_p = cumsum_rows(logf_p, nb, t)
    c_heads = jnp.swapaxes(c_p[:, N_GATE:N_GATE + FOX_HEADS].reshape(nb, t, FOX_HEADS), 1, 2)
    n16 = t // CMP_STRIDE
    kvc_p = nsa_compress(
        h_p, pl.BlockSpec((t, HEAD_DIM), lambda b, c: (b, P_NKV // HEAD_DIM + c)), nb, t, e, cmp_w1, cmp_pos, cmp_w2)
    a_p = _sel_matrix(n16, n16 - 1, t // SEL_LEN, 128)
    o_c, sel = cmp_select(h_p, kvc_p, a_p, nb, t)
    o_s = flash_attention("slc", nb, t, q_rot, lambda h: h, slc_p, lambda h: h // NSA_GROUP,
                          slc_p, lambda h: NSA_KV + h // NSA_GROUP, (sel,))
    o_w = flash_attention("win", nb, t, q_rot, lambda h: h, win_p, lambda h: h // NSA_GROUP,
                          win_p, lambda h: NSA_KV + h // NSA_GROUP)
    o_f = flash_attention("fox", nb, t, h_p, lambda h: hq + h, h_p, lambda h: hq + FOX_HEADS + h,
                          h_p, lambda h: hq + 2 * FOX_HEADS + h, (c_heads[..., None], c_heads[:, :, None, :]))
    xcat_p = combine_heads(o_c, o_s, o_w, o_f, gates_p)
    st_p = (
        h_p[:, P_NKV:P_NKV + kvw].reshape(nb, t, 2, NSA_KV, HEAD_DIM),
        slc_p.reshape(nb, t, 2, NSA_KV, HEAD_DIM),
        win_p.reshape(nb, t, 2, NSA_KV, HEAD_DIM)[:, t - min(WINDOW, t):],
        h_p[:, P_FQ + nw:P_FQ + 3 * nw].reshape(nb, t, 2, FOX_HEADS, HEAD_DIM),
        logf_p[:, N_GATE:N_GATE + FOX_HEADS].reshape(nb, t, FOX_HEADS),
    )

    rows = xb_s.shape[0]
    h_s = matmul(xb_s, w_rep, (), (1152, 640, 384, 128))
    q_rot_s, slc_s, win_s, gates_s, logf_s = even_post(
        h_s, _rope_tables(jnp.full((rows,), past)), fb_row, 1)
    new_win = jnp.concatenate(
        [cache_win[e].reshape(ns, -1, kvw), win_s[:ns, None, :]], axis=1)[:, 1:]
    cmp_rows = gather_cmp_pages(cache_cmp, e, page_table)
    l_tot = past + 1
    n16_s = l_tot // CMP_STRIDE
    assert n16_s * CMP_STRIDE == past
    kvc_s = nsa_compress(
        cmp_rows, pl.BlockSpec((None, None, past, HEAD_DIM), lambda b, c: (b, c, 0, 0)), ns, past, e,
        cmp_w1, cmp_pos, cmp_w2)
    n_sel_s = -(-l_tot // SEL_LEN)
    nsp = -(-n_sel_s // 128) * 128
    a_s = _sel_matrix(n16_s, n16_s - 1, n_sel_s, nsp)

    def q_rows(x):
        x = x[:ns].reshape(ns, NSA_KV, NSA_GROUP, HEAD_DIM)
        return jnp.pad(x, ((0, 0), (0, 0), (0, 8 - NSA_GROUP), (0, 0)))

    qraw_g, qrot_g = q_rows(h_s[:, :nw]), q_rows(q_rot_s)
    o_c_s, o_w_s, idx = sample_cmp_win(qraw_g, qrot_g, kvc_s, a_s, new_win, n16_s - 1, n_sel_s, past)
    idx2 = idx[:, :, 0, :min(SEL_TOP, n_sel_s)].reshape(ns * NSA_KV, -1)
    o_s_s = sample_slc(qrot_g, cache_slc, e, page_table, idx2, slc_s[:ns, None, :], past // SEL_LEN)
    fq = h_s[:ns, P_FQ:P_FQ + nw].reshape(ns, FOX_HEADS, HEAD_DIM)
    fk = h_s[:ns, P_FQ + nw:P_FQ + 2 * nw]
    fv = h_s[:ns, P_FQ + 2 * nw:P_FQ + 3 * nw]
    logf_new = logf_s[:ns, N_GATE:N_GATE + FOX_HEADS]
    o_f_s = sample_fox(fq, cache_fox, logf_pool_t, e, page_table, fk.reshape(ns, FOX_HEADS, HEAD_DIM),
                       fv[:, None, :], logf_new[:, :, None])

    def heads_flat(x):
        return _pad_rows(x[:, :, :NSA_GROUP].reshape(ns, nw), rows)

    xcat_s = combine_heads(heads_flat(o_c_s), heads_flat(o_s_s), heads_flat(o_w_s),
                           _pad_rows(o_f_s.reshape(ns, nw), rows), gates_s)
    st_s = (
        h_s[:ns, P_NKV:P_NKV + kvw].reshape(ns, 1, 2, NSA_KV, HEAD_DIM),
        slc_s[:ns].reshape(ns, 1, 2, NSA_KV, HEAD_DIM),
        new_win.reshape(ns, -1, 2, NSA_KV, HEAD_DIM),
        h_s[:ns, P_FQ + nw:P_FQ + 3 * nw].reshape(ns, 1, 2, FOX_HEADS, HEAD_DIM),
        logf_new.reshape(ns, 1, FOX_HEADS),
    )
    return xcat_p, xcat_s, st_p, st_s


def _odd_layer(o, xb_p, xb_s, nb, t, ns, past, state_ret, w_in_odd, ret_gn_g):
    h_p = matmul(xb_p, w_in_odd, (o,), (1024, 512, 256, 128))
    y_p, s_p = retention_prompt(h_p, nb, t, ret_gn_g, o)
    rows = xb_s.shape[0]
    h_s = matmul(xb_s, w_in_odd, (o,), (1024, 512, 256, 128))[:ns]
    q_col = h_s[:, :O_RK].reshape(ns, RET_HEADS, RET_DK, 1)
    k_col = h_s[:, O_RK:O_RV].reshape(ns, RET_HEADS, RET_DK, 1)
    v_row = h_s[:, O_RV:O_RG].reshape(ns, RET_HEADS, 1, RET_DV)
    g_row = h_s[:, O_RG:E_ODD].reshape(ns, RET_HEADS, 1, RET_DV)
    y_s, s_s = retention_step(q_col, k_col, v_row, g_row, jnp.full((1,), past), state_ret, o, ret_gn_g)
    y_s = _pad_rows(y_s.reshape(ns, RET_HEADS * RET_DV), rows).astype(MXU_DTYPE)
    return y_p, y_s, s_p, s_s


def kernel(x_prompt, x_sample, cache_nsa_cmp, cache_nsa_slc, cache_nsa_win, cache_fox_kv, cache_fox_logf,
           state_ret, page_table, w_ffn_in, w_ffn_out, ln_g, ln_b, w_in_even, w_out_even, fox_f_bias,
           nsa_cmp_pos, nsa_cmp_w1, nsa_cmp_w2, w_in_odd, ret_gn_g, w_out_odd):
    nb, t, d = x_prompt.shape
    ns, ts, _ = x_sample.shape
    depth = w_ffn_in.shape[0]
    n_pages = page_table.shape[1]
    page = cache_nsa_cmp.shape[2]
    past = n_pages * page
    assert ts == 1 and past >= WINDOW and past % SEL_LEN == 0 and t % RET_CHUNK == 0
    alpha = (2.0 * depth) ** 0.25
    rows = max(16, -(-ns // 16) * 16)

    xp = x_prompt.reshape(nb * t, d)
    xs = _pad_rows(x_sample.reshape(ns * ts, d), rows)
    xp_b, xs_b = xp.astype(MXU_DTYPE), xs.astype(MXU_DTYPE)

    caches = (cache_nsa_cmp, cache_nsa_slc, cache_nsa_win, cache_fox_kv, jnp.swapaxes(cache_fox_logf, 2, 3))
    w_ffn_out, w_out_even, w_out_odd = _mx(w_ffn_out), _mx(w_out_even), _mx(w_out_odd)

    def ffn(x, xb, l, s):
        hid = swiglu_hidden(xb, w_ffn_in, l, s)
        return matmul_postnorm(hid, w_ffn_out, (l, s), x, ln_g, ln_b, (l, 2 * s), alpha, 0.5)

    new_p = [[] for _ in range(6)]
    new_s = [[] for _ in range(6)]
    for l in range(depth):
        xp, xp_b = ffn(xp, xp_b, l, 0)
        xs, xs_b = ffn(xs, xs_b, l, 0)
        if l % 2 == 0:
            e = l // 2
            mp, ms, st_p, st_s = _even_layer(
                e, xp_b, xs_b, nb, t, ns, past, caches, page_table,
                (w_in_even, fox_f_bias, nsa_cmp_pos, nsa_cmp_w1, nsa_cmp_w2))
            w_out, widx = w_out_even, (e,)
            slots = (0, 1, 2, 3, 4)
        else:
            o = l // 2
            mp, ms, sp, ss = _odd_layer(o, xp_b, xs_b, nb, t, ns, past, state_ret, w_in_odd, ret_gn_g)
            st_p, st_s = (sp,), (ss,)
            w_out, widx = w_out_odd, (o,)
            slots = (5,)
        for i, a_p, a_s in zip(slots, st_p, st_s):
            new_p[i].append(a_p)
            new_s[i].append(a_s)
        xp, xp_b = matmul_postnorm(mp, w_out, widx, xp, ln_g, ln_b, (l, 1), alpha, 1.0)
        xs, xs_b = matmul_postnorm(ms, w_out, widx, xs, ln_g, ln_b, (l, 1), alpha, 1.0)
        xp, xp_b = ffn(xp, xp_b, l, 1)
        xs, xs_b = ffn(xs, xs_b, l, 1)

    outs = [xp.reshape(nb, t, d), xs[:ns].reshape(ns, ts, d)]
    for i in range(6):
        outs.append(jnp.stack(new_p[i]))
        outs.append(jnp.stack(new_s[i]))
    return tuple(outs)
```

```python
import functools
import math

import numpy as np
import jax
import jax.numpy as jnp
from jax import lax
from jax.experimental import pallas as pl
from jax.experimental.pallas import tpu as pltpu

F32 = jnp.float32
MXU_DTYPE = jnp.bfloat16

HEAD_DIM = 128
NSA_HEADS = 8
NSA_KV = 2
NSA_GROUP = NSA_HEADS // NSA_KV
CMP_LEN = 32
CMP_STRIDE = 16
CMP_HIDDEN = 256
SEL_LEN = 64
SEL_TOP = 16
WINDOW = 512
FOX_HEADS = 8
ROPE_THETA = 500000.0
ROPE_DIMS = HEAD_DIM // 4
RET_HEADS = 8
RET_DK = 256
RET_DV = 512
RET_CHUNK = 128
RET_THETA = 10000.0
LN_EPS = 1e-5
GN_EPS = 1e-6

O_NQ = 0
O_NKV = O_NQ + NSA_HEADS * HEAD_DIM
O_NG = O_NKV + 6 * NSA_KV * HEAD_DIM
O_FQ = O_NG + 3 * NSA_HEADS
O_FK = O_FQ + FOX_HEADS * HEAD_DIM
O_FV = O_FK + FOX_HEADS * HEAD_DIM
O_FF = O_FV + FOX_HEADS * HEAD_DIM
E_EVEN = O_FF + FOX_HEADS
P_NKV = NSA_HEADS * HEAD_DIM
P_FQ = P_NKV + 6 * NSA_KV * HEAD_DIM
P_SMALL = P_FQ + 3 * FOX_HEADS * HEAD_DIM
P_EVEN = P_SMALL + 128
N_GATE = 3 * NSA_HEADS
O_RK = RET_HEADS * RET_DK
O_RV = 2 * RET_HEADS * RET_DK
O_RG = O_RV + RET_HEADS * RET_DV
E_ODD = O_RG + RET_HEADS * RET_DV

V7X_VMEM_LIMIT = 56 * 1024 * 1024
LN_ROWS = 128
LOG2E = 1.4426950408889634
NEG_BIG = -1e30
NT_DIMS = (((1,), (1,)), ((), ()))
TN_DIMS = (((0,), (0,)), ((), ()))


def _params(*sem):
    return pltpu.CompilerParams(dimension_semantics=sem, vmem_limit_bytes=V7X_VMEM_LIMIT)


def _mx(x):
    return x.astype(MXU_DTYPE)


def _pick(n, prefs):
    for p in prefs:
        if n % p == 0:
            return p
    return n


def _swiglu_kernel(x_ref, wa_ref, wb_ref, o_ref, wa_s, wb_s):
    @pl.when(pl.program_id(1) == 0)
    def _():
        wa_s[...] = _mx(wa_ref[...])
        wb_s[...] = _mx(wb_ref[...])
    x = x_ref[...]
    a = jnp.dot(x, wa_s[...], preferred_element_type=F32)
    b = jnp.dot(x, wb_s[...], preferred_element_type=F32)
    o_ref[...] = (a * jax.nn.sigmoid(a) * b).astype(o_ref.dtype)


def swiglu_hidden(xb, w_in, l, s):
    m, d = xb.shape
    f = w_in.shape[-1] // 2
    tm = _pick(m, (1024, 512, 256, 128))
    tn = _pick(f, (512, 256, 128))
    nj = f // tn
    return pl.pallas_call(
        _swiglu_kernel,
        grid=(nj, m // tm),
        in_specs=[
            pl.BlockSpec((tm, d), lambda j, i: (i, 0)),
            pl.BlockSpec((None, None, d, tn), lambda j, i: (l, s, 0, j)),
            pl.BlockSpec((None, None, d, tn), lambda j, i: (l, s, 0, j + nj)),
        ],
        out_specs=pl.BlockSpec((tm, tn), lambda j, i: (i, j)),
        out_shape=jax.ShapeDtypeStruct((m, f), MXU_DTYPE),
        scratch_shapes=[pltpu.VMEM((d, tn), MXU_DTYPE), pltpu.VMEM((d, tn), MXU_DTYPE)],
        compiler_params=_params("arbitrary", "arbitrary"),
        name="swiglu_hidden",
    )(xb, w_in, w_in)


def _mm_ln_kernel(x_ref, w_ref, r_ref, g_ref, b_ref, o_ref, ob_ref, *, alpha, scale, nk):
    k = pl.program_id(1)
    def part():
        return jnp.dot(x_ref[...], _mx(w_ref[...]), preferred_element_type=F32)

    @pl.when(k == 0)
    def _():
        o_ref[...] = part()

    @pl.when(k > 0)
    def _():
        o_ref[...] += part()

    @pl.when(k == nk - 1)
    def _():
        rows = min(LN_ROWS, o_ref.shape[0])

        def norm_rows(c, carry):
            sl = pl.ds(pl.multiple_of(c * rows, rows), rows)
            z = alpha * r_ref[sl, :] + scale * o_ref[sl, :]
            mu = jnp.mean(z, axis=-1, keepdims=True)
            dz = z - mu
            var = jnp.mean(dz * dz, axis=-1, keepdims=True)
            y = dz * lax.rsqrt(var + LN_EPS) * g_ref[...] + b_ref[...]
            o_ref[sl, :] = y
            ob_ref[sl, :] = y.astype(ob_ref.dtype)
            return carry

        lax.fori_loop(0, o_ref.shape[0] // rows, norm_rows, 0)


def matmul_postnorm(xb, w, widx, res, g, b, gidx, alpha, scale):
    m, kdim = xb.shape
    n = w.shape[-1]
    tm = _pick(m, (512, 256, 128))
    tk = _pick(kdim, (1408, 1024, 512, 256, 128))
    nk = kdim // tk
    nlead = len(widx)
    w_spec = pl.BlockSpec((None,) * nlead + (tk, n), lambda i, k: tuple(widx) + (k, 0))
    g_spec = pl.BlockSpec((None, None, 1, n), lambda i, k: tuple(gidx) + (0, 0))
    return pl.pallas_call(
        functools.partial(_mm_ln_kernel, alpha=alpha, scale=scale, nk=nk),
        grid=(m // tm, nk),
        in_specs=[
            pl.BlockSpec((tm, tk), lambda i, k: (i, k)),
            w_spec,
            pl.BlockSpec((tm, n), lambda i, k: (i, 0)),
            g_spec, g_spec,
        ],
        out_specs=[pl.BlockSpec((tm, n), lambda i, k: (i, 0)), pl.BlockSpec((tm, n), lambda i, k: (i, 0))],
        out_shape=[jax.ShapeDtypeStruct((m, n), F32), jax.ShapeDtypeStruct((m, n), MXU_DTYPE)],
        compiler_params=_params("arbitrary", "arbitrary"),
        name="matmul_postnorm",
    )(xb, w, res, g[:, :, None, :], b[:, :, None, :])


def _mm_kernel(x_ref, w_ref, o_ref, w_s):
    @pl.when(pl.program_id(1) == 0)
    def _():
        w_s[...] = _mx(w_ref[...])
    o_ref[...] = jnp.dot(x_ref[...], w_s[...], preferred_element_type=F32)


def matmul(xb, w, widx, tn_prefs):
    m, kdim = xb.shape
    n = w.shape[-1]
    tm = _pick(m, (1024, 512, 256, 128))
    tn = _pick(n, tn_prefs)
    nlead = len(widx)
    return pl.pallas_call(
        _mm_kernel,
        grid=(n // tn, m // tm),
        in_specs=[
            pl.BlockSpec((tm, kdim), lambda j, i: (i, 0)),
            pl.BlockSpec((None,) * nlead + (kdim, tn), lambda j, i: tuple(widx) + (0, j)),
        ],
        out_specs=pl.BlockSpec((tm, tn), lambda j, i: (i, j)),
        out_shape=jax.ShapeDtypeStruct((m, n), F32),
        scratch_shapes=[pltpu.VMEM((kdim, tn), MXU_DTYPE)],
        compiler_params=_params("arbitrary", "arbitrary"),
        name="matmul",
    )(xb, w)


def _rope_tables(pos):
    half = ROPE_DIMS // 2
    inv = 1.0 / (ROPE_THETA ** (jnp.arange(half, dtype=F32) / half))
    ang = pos.astype(F32)[:, None] * inv[None, :]
    cos, sin = jnp.cos(ang), jnp.sin(ang)
    n = pos.shape[0]
    ones = jnp.ones((n, HEAD_DIM - ROPE_DIMS), F32)
    zeros = jnp.zeros((n, HEAD_DIM - ROPE_DIMS), F32)
    zh = jnp.zeros((n, half), F32)
    c = jnp.concatenate([cos, cos, ones], -1)
    a = jnp.concatenate([-sin, zh, zeros], -1)
    b = jnp.concatenate([zh, sin, zeros], -1)
    return c, a, b


def _rope(x, c, a, b):
    half = ROPE_DIMS // 2
    return x * c + pltpu.roll(x, HEAD_DIM - half, 1) * a + pltpu.roll(x, half, 1) * b


def _even_post_kernel(q_ref, slc_ref, win_ref, sm_ref, c_ref, a_ref, b_ref, fb_ref,
                      qr_ref, slco_ref, wino_ref, gate_ref, logf_ref):
    c, a, b = c_ref[...], a_ref[...], b_ref[...]
    for h in range(NSA_HEADS):
        sl = slice(h * HEAD_DIM, (h + 1) * HEAD_DIM)
        qr_ref[:, sl] = _rope(q_ref[:, sl], c, a, b)
    for src, dst in ((slc_ref, slco_ref), (win_ref, wino_ref)):
        for g in range(NSA_KV):
            sl = slice(g * HEAD_DIM, (g + 1) * HEAD_DIM)
            dst[:, sl] = _rope(src[:, sl], c, a, b)
        vs = slice(NSA_KV * HEAD_DIM, 2 * NSA_KV * HEAD_DIM)
        dst[:, vs] = src[:, vs]
    sm = sm_ref[...]
    gate_ref[...] = jax.nn.sigmoid(sm)
    z = sm + fb_ref[...]
    logf_ref[...] = jnp.minimum(z, 0.0) - jnp.log1p(jnp.exp(-jnp.abs(z)))


def even_post(h, tabs, fb_row, npos_blocks):
    m = h.shape[0]
    tm = _pick(m, (512, 256, 128))
    kvw = 2 * NSA_KV * HEAD_DIM
    tab_spec = pl.BlockSpec((tm, HEAD_DIM), lambda i: (i % npos_blocks, 0))
    return pl.pallas_call(
        _even_post_kernel,
        grid=(m // tm,),
        in_specs=[
            pl.BlockSpec((tm, P_NKV), lambda i: (i, 0)),
            pl.BlockSpec((tm, kvw), lambda i: (i, (P_NKV + kvw) // kvw)),
            pl.BlockSpec((tm, kvw), lambda i: (i, (P_NKV + 2 * kvw) // kvw)),
            pl.BlockSpec((tm, 128), lambda i: (i, P_SMALL // 128)),
            tab_spec, tab_spec, tab_spec,
            pl.BlockSpec((1, 128), lambda i: (0, 0)),
        ],
        out_specs=[
            pl.BlockSpec((tm, P_NKV), lambda i: (i, 0)),
            pl.BlockSpec((tm, kvw), lambda i: (i, 0)),
            pl.BlockSpec((tm, kvw), lambda i: (i, 0)),
            pl.BlockSpec((tm, 128), lambda i: (i, 0)),
            pl.BlockSpec((tm, 128), lambda i: (i, 0)),
        ],
        out_shape=[
            jax.ShapeDtypeStruct((m, P_NKV), F32),
            jax.ShapeDtypeStruct((m, kvw), F32),
            jax.ShapeDtypeStruct((m, kvw), F32),
            jax.ShapeDtypeStruct((m, 128), F32),
            jax.ShapeDtypeStruct((m, 128), F32),
        ],
        compiler_params=_params("arbitrary"),
        name="even_post",
    )(h, h, h, h, *tabs, fb_row)


def _cumsum_kernel(x_ref, tri_ref, o_ref, carry):
    @pl.when(pl.program_id(1) == 0)
    def _():
        carry[...] = jnp.zeros_like(carry)
    c = jnp.dot(tri_ref[...], x_ref[...], preferred_element_type=F32,
                precision=lax.Precision.HIGHEST) + carry[...]
    o_ref[...] = c
    carry[...] = c[-1:, :]


def cumsum_rows(x, nb, t):
    tc = _pick(t, (512, 256, 128))
    tri = jnp.tril(jnp.ones((tc, tc), F32))
    nt = t // tc
    return pl.pallas_call(
        _cumsum_kernel,
        grid=(nb, nt),
        in_specs=[pl.BlockSpec((tc, 128), lambda b, i: (b * nt + i, 0)),
                  pl.BlockSpec((tc, tc), lambda b, i: (0, 0))],
        out_specs=pl.BlockSpec((tc, 128), lambda b, i: (b * nt + i, 0)),
        out_shape=jax.ShapeDtypeStruct(x.shape, F32),
        scratch_shapes=[pltpu.VMEM((1, 128), F32)],
        compiler_params=_params("arbitrary", "arbitrary"),
        name="cumsum_rows",
    )(x, tri)


def _compress_kernel(x_ref, w1_ref, pe_ref, w2_ref, o_ref, u0, u1, *, n16):
    u0[...] = jnp.zeros_like(u0)
    u1[...] = jnp.zeros_like(u1)
    for r in range(CMP_STRIDE):
        xr = x_ref[pl.ds(r, n16, stride=CMP_STRIDE), :]
        u0[...] += jnp.dot(_mx(xr + pe_ref[r:r + 1, :]), _mx(w1_ref[r]), preferred_element_type=F32)
        u1[...] += jnp.dot(_mx(xr + pe_ref[CMP_STRIDE + r:CMP_STRIDE + r + 1, :]), _mx(w1_ref[CMP_STRIDE + r]),
                           preferred_element_type=F32)
    pre = u0[...] + pltpu.roll(u1[...], n16 - 1, 0)
    out = jnp.dot(_mx(jax.nn.gelu(pre)), _mx(w2_ref[...]), preferred_element_type=F32)
    row = lax.broadcasted_iota(jnp.int32, out.shape, 0)
    o_ref[...] = jnp.where(row < n16 - 1, out, 0.0)


def nsa_compress(x, x_spec, nb, length, e, w1, pe, w2):
    n16 = length // CMP_STRIDE
    w1r = w1.reshape(w1.shape[0], 2, CMP_LEN, HEAD_DIM, CMP_HIDDEN)
    return pl.pallas_call(
        functools.partial(_compress_kernel, n16=n16),
        grid=(nb, 2 * NSA_KV),
        in_specs=[
            x_spec,
            pl.BlockSpec((None, None, CMP_LEN, HEAD_DIM, CMP_HIDDEN), lambda b, c: (e, c // NSA_KV, 0, 0, 0)),
            pl.BlockSpec((None, None, CMP_LEN, HEAD_DIM), lambda b, c: (e, c // NSA_KV, 0, 0)),
            pl.BlockSpec((None, None, CMP_HIDDEN, HEAD_DIM), lambda b, c: (e, c // NSA_KV, 0, 0)),
        ],
        out_specs=pl.BlockSpec((None, None, n16, HEAD_DIM), lambda b, c: (b, c, 0, 0)),
        out_shape=jax.ShapeDtypeStruct((nb, 2 * NSA_KV, n16, HEAD_DIM), F32),
        scratch_shapes=[pltpu.VMEM((n16, CMP_HIDDEN), F32), pltpu.VMEM((n16, CMP_HIDDEN), F32)],
        compiler_params=_params("arbitrary", "arbitrary"),
        name="nsa_compress",
    )(x, w1r, pe, w2)


def _sel_matrix(n_c_pad, n_c, n_sel, width):
    ratio = SEL_LEN // CMP_STRIDE
    i = np.arange(n_c_pad)[:, None]
    j = np.arange(width)[None, :]
    a = (i >= ratio * j - 1) & (i <= ratio * j + ratio - 1) & (i < n_c) & (j < n_sel)
    return jnp.asarray(a.astype(np.float32))


def _masked_softmax(lg, mask):
    lg = jnp.where(mask, lg, -jnp.inf)
    m = jnp.max(lg, axis=-1, keepdims=True)
    m = jnp.where(m == -jnp.inf, 0.0, m)
    p = jnp.exp(lg - m)
    s = jnp.sum(p, axis=-1, keepdims=True)
    return p / jnp.where(s > 0, s, 1.0)


def _cmp_select_kernel(q_ref, kc_ref, vc_ref, a_ref, oc_ref, sel_ref, *, tq, n_sel, n_top):
    i = pl.program_id(2)
    ncp = kc_ref.shape[0]
    qpos = i * tq + lax.broadcasted_iota(jnp.int32, (tq, 1), 0)
    cend = lax.broadcasted_iota(jnp.int32, (1, ncp), 1) * CMP_STRIDE + CMP_LEN
    cmask = cend <= qpos + 1
    kc = _mx(kc_ref[...])
    vc = _mx(vc_ref[...])
    scale = HEAD_DIM ** -0.5
    imp = jnp.zeros((tq, ncp), F32)
    for m in range(NSA_GROUP):
        sl = slice(m * HEAD_DIM, (m + 1) * HEAD_DIM)
        lg = lax.dot_general(_mx(q_ref[:, sl]), kc, NT_DIMS, preferred_element_type=F32) * scale
        p = _masked_softmax(lg, cmask)
        oc_ref[:, sl] = jnp.dot(_mx(p), vc, preferred_element_type=F32)
        imp = imp + p
    s_sel = jnp.dot(imp, a_ref[...], preferred_element_type=F32, precision=lax.Precision.HIGHEST)
    st = s_sel.T[:n_sel, :]
    blk = lax.broadcasted_iota(jnp.int32, (n_sel, tq), 0)
    cur = (i * tq + lax.broadcasted_iota(jnp.int32, (n_sel, tq), 1)) // SEL_LEN
    forced = (blk == 0) | (blk == cur) | (blk == cur - 1)
    allowed = blk <= cur
    v = jnp.where(allowed, jnp.where(forced, jnp.inf, st), -jnp.inf)
    rank = jnp.zeros((n_sel, tq), jnp.int32)
    for r in range(n_sel):
        vr = v[r:r + 1, :]
        before = (vr > v) | ((vr == v) & (blk > r))
        rank = rank + before.astype(jnp.int32)
    sel = ((rank < n_top) & allowed).astype(F32)
    if n_sel < tq:
        sel = jnp.concatenate([sel, jnp.zeros((tq - n_sel, tq), F32)], axis=0)
    sel_ref[...] = sel.T.astype(sel_ref.dtype)


def cmp_select(h, kvc, a_mat, nb, t):
    tq = 128
    nt = t // tq
    n_sel = t // SEL_LEN
    assert n_sel <= tq and n_sel % 8 == 0
    ncp = kvc.shape[2]
    gw = NSA_GROUP * HEAD_DIM
    return pl.pallas_call(
        functools.partial(_cmp_select_kernel, tq=tq, n_sel=n_sel, n_top=min(SEL_TOP, n_sel)),
        grid=(nb, NSA_KV, nt),
        in_specs=[
            pl.BlockSpec((tq, gw), lambda b, g, i: (b * nt + i, g)),
            pl.BlockSpec((None, None, ncp, HEAD_DIM), lambda b, g, i: (b, g, 0, 0)),
            pl.BlockSpec((None, None, ncp, HEAD_DIM), lambda b, g, i: (b, NSA_KV + g, 0, 0)),
            pl.BlockSpec((ncp, 128), lambda b, g, i: (0, 0)),
        ],
        out_specs=[
            pl.BlockSpec((tq, gw), lambda b, g, i: (b * nt + i, g)),
            pl.BlockSpec((None, None, tq, 128), lambda b, g, i: (b, g, i, 0)),
        ],
        out_shape=[
            jax.ShapeDtypeStruct((nb * t, NSA_HEADS * HEAD_DIM), F32),
            jax.ShapeDtypeStruct((nb, NSA_KV, t, 128), MXU_DTYPE),
        ],
        compiler_params=_params("arbitrary", "arbitrary", "arbitrary"),
        name="cmp_select",
    )(h, kvc, kvc, a_mat)


def _flash_steps(mode, t, tq, tk):
    rows = []
    for qi in range(t // tq):
        q_lo, q_hi = qi * tq, qi * tq + tq - 1
        k_hi = q_hi // tk
        k_lo = max((q_lo - WINDOW + 1) // tk, 0) if mode == "win" else 0
        for kt in range(k_lo, k_hi + 1):
            every_key_visible = kt * tk + tk - 1 <= q_lo
            rows.append((qi, kt, int(kt == k_lo), int(kt == k_hi), int(not every_key_visible)))
    return jnp.asarray(np.array(rows, np.int32).T)


def _flash_kernel(*refs, mode, tq, tk, heads):
    if mode == "fox":
        tab, q_ref, k_ref, v_ref, cq_ref, ck_ref, o_ref, m_s, acc = refs
    elif mode == "slc":
        tab, q_ref, k_ref, v_ref, sel_ref, o_ref, m_s, acc = refs
    else:
        tab, q_ref, k_ref, v_ref, o_ref, m_s, acc = refs
    step = pl.program_id(2)
    qi, kt = tab[0, step], tab[1, step]

    @pl.when(tab[2, step] == 1)
    def _():
        m_s[...] = jnp.full_like(m_s, NEG_BIG)
        acc[...] = jnp.zeros_like(acc)

    def update(causal):
        ones = jnp.ones((tk, HEAD_DIM), MXU_DTYPE)
        if mode != "fox":
            kb = _mx(k_ref[...])
            v_aug = jnp.concatenate([_mx(v_ref[...]), ones], axis=1)
        valid = None
        if causal or mode == "win":
            qpos = qi * tq + lax.broadcasted_iota(jnp.int32, (tq, tk), 0)
            kpos = kt * tk + lax.broadcasted_iota(jnp.int32, (tq, tk), 1)
            valid = kpos <= qpos
            if mode == "win":
                valid = valid & (kpos > qpos - WINDOW)
        if mode == "slc":
            blk = lax.broadcasted_iota(jnp.int32, (128, tk), 0)
            kblk = (kt * tk + lax.broadcasted_iota(jnp.int32, (128, tk), 1)) // SEL_LEN
            chosen = jnp.dot(sel_ref[...], (blk == kblk).astype(MXU_DTYPE), preferred_element_type=F32) > 0.5
            valid = chosen if valid is None else valid & chosen
        for hh in range(heads):
            cols = slice(hh * HEAD_DIM, (hh + 1) * HEAD_DIM)
            if mode == "fox":
                kb = _mx(k_ref[:, cols])
                v_aug = jnp.concatenate([_mx(v_ref[:, cols]), ones], axis=1)
                ck2 = ck_ref[hh] * LOG2E
                cq2 = cq_ref[hh] * LOG2E
            s = lax.dot_general(_mx(q_ref[:, cols]), kb, NT_DIMS, preferred_element_type=F32)
            x = s * (HEAD_DIM ** -0.5 * LOG2E)
            if mode == "fox":
                x = x - ck2
            if valid is not None:
                x = jnp.where(valid, x, NEG_BIG)
            top = jnp.max(x, axis=-1, keepdims=True)
            if mode == "fox":
                top = top + cq2
            m_old = m_s[hh]
            m_new = jnp.maximum(m_old, top)
            p = jnp.exp2(x - (m_new - cq2 if mode == "fox" else m_new))
            if mode == "win":
                p = jnp.where(valid, p, 0.0)
            acc[hh] = jnp.exp2(m_old - m_new) * acc[hh] + jnp.dot(_mx(p), v_aug, preferred_element_type=F32)
            m_s[hh] = m_new

    if mode == "win":
        update(True)
    else:
        pl.when(tab[4, step] == 1)(functools.partial(update, True))
        pl.when(tab[4, step] == 0)(functools.partial(update, False))

    @pl.when(tab[3, step] == 1)
    def _():
        for hh in range(heads):
            l = acc[hh, :, HEAD_DIM:]
            o_ref[:, hh * HEAD_DIM:(hh + 1) * HEAD_DIM] = acc[hh, :, :HEAD_DIM] / jnp.where(l > 0, l, 1.0)


def flash_attention(mode, nb, t, q, qcol, k, kcol, v, vcol, extra=()):
    heads = NSA_GROUP
    tk = _pick(t, (512,) if mode == "win" else (1024, 512, 256, 128))
    tq = _pick(t, (512, 256, 128))
    ntq, ntk = t // tq, t // tk
    qw = heads * HEAD_DIM
    kw = qw if mode == "fox" else HEAD_DIM
    steps = _flash_steps(mode, t, tq, tk)
    in_specs = [
        pl.BlockSpec((tq, qw), lambda b, h, s, tab: (b * ntq + tab[0, s], qcol(h))),
        pl.BlockSpec((tk, kw), lambda b, h, s, tab: (b * ntk + tab[1, s], kcol(h))),
        pl.BlockSpec((tk, kw), lambda b, h, s, tab: (b * ntk + tab[1, s], vcol(h))),
    ]
    scratch = [pltpu.VMEM((heads, tq, 1), F32), pltpu.VMEM((heads, tq, 2 * HEAD_DIM), F32)]
    if mode == "fox":
        in_specs += [
            pl.BlockSpec((None, heads, tq, 1), lambda b, h, s, tab: (b, h, tab[0, s], 0)),
            pl.BlockSpec((None, heads, 1, tk), lambda b, h, s, tab: (b, h, 0, tab[1, s])),
        ]
    elif mode == "slc":
        in_specs += [pl.BlockSpec((None, None, tq, 128), lambda b, h, s, tab: (b, h, tab[0, s], 0))]
    return pl.pallas_call(
        functools.partial(_flash_kernel, mode=mode, tq=tq, tk=tk, heads=heads),
        grid_spec=pltpu.PrefetchScalarGridSpec(
            num_scalar_prefetch=1,
            grid=(nb, NSA_HEADS // heads, steps.shape[1]),
            in_specs=in_specs,
            out_specs=pl.BlockSpec((tq, qw), lambda b, h, s, tab: (b * ntq + tab[0, s], h)),
            scratch_shapes=scratch,
        ),
        out_shape=jax.ShapeDtypeStruct((nb * t, NSA_HEADS * HEAD_DIM), F32),
        compiler_params=_params("arbitrary", "arbitrary", "arbitrary"),
        name="flash_" + mode,
    )(steps, q, k, v, *extra)


def _combine_kernel(oc_ref, os_ref, ow_ref, of_ref, g_ref, o_ref):
    gates = g_ref[...]
    nw = NSA_HEADS * HEAD_DIM
    for h in range(NSA_HEADS):
        sl = slice(h * HEAD_DIM, (h + 1) * HEAD_DIM)
        o = (gates[:, 3 * h:3 * h + 1] * oc_ref[:, sl] + gates[:, 3 * h + 1:3 * h + 2] * os_ref[:, sl]
             + gates[:, 3 * h + 2:3 * h + 3] * ow_ref[:, sl])
        o_ref[:, sl] = o.astype(o_ref.dtype)
    o_ref[:, nw:] = of_ref[...].astype(o_ref.dtype)


def combine_heads(o_c, o_s, o_w, o_f, gates):
    m, nw = o_c.shape
    tm = _pick(m, (512, 256, 128))
    spec = pl.BlockSpec((tm, nw), lambda i: (i, 0))
    return pl.pallas_call(
        _combine_kernel,
        grid=(m // tm,),
        in_specs=[spec, spec, spec, spec, pl.BlockSpec((tm, 128), lambda i: (i, 0))],
        out_specs=pl.BlockSpec((tm, 2 * nw), lambda i: (i, 0)),
        out_shape=jax.ShapeDtypeStruct((m, 2 * nw), MXU_DTYPE),
        compiler_params=_params("arbitrary"),
        name="combine_heads",
    )(o_c, o_s, o_w, o_f, gates)


def _gather_pages_kernel(pt_ref, *refs, page):
    *x_refs, o_ref = refs
    for j, x_ref in enumerate(x_refs):
        for c in range(2 * NSA_KV):
            o_ref[c, j * page:(j + 1) * page, :] = x_ref[:, c // NSA_KV, c % NSA_KV, :]


def gather_cmp_pages(pool, e, page_table):
    nb, n_pages = page_table.shape
    page = pool.shape[2]
    per_step = _pick(n_pages, (4, 2, 1))

    def page_spec(j):
        return pl.BlockSpec((None, None, page, 2, NSA_KV, HEAD_DIM),
                            lambda b, p, pt: (e, pt[b, p * per_step + j], 0, 0, 0, 0))

    return pl.pallas_call(
        functools.partial(_gather_pages_kernel, page=page),
        grid_spec=pltpu.PrefetchScalarGridSpec(
            num_scalar_prefetch=1,
            grid=(nb, n_pages // per_step),
            in_specs=[page_spec(j) for j in range(per_step)],
            out_specs=pl.BlockSpec((None, 2 * NSA_KV, per_step * page, HEAD_DIM), lambda b, p, pt: (b, 0, p, 0)),
        ),
        out_shape=jax.ShapeDtypeStruct((nb, 2 * NSA_KV, n_pages * page, HEAD_DIM), F32),
        compiler_params=_params("arbitrary", "arbitrary"),
        name="gather_cmp_pages",
    )(page_table, *([pool] * per_step))


def _sample_cmp_win_kernel(qraw_ref, qrot_ref, kc_ref, vc_ref, a_ref, kw_ref, vw_ref,
                           oc_ref, ow_ref, idx_ref, *, n_c, n_sel, n_top, q_pos):
    scale = HEAD_DIM ** -0.5
    ncp = kc_ref.shape[0]
    rows = qraw_ref.shape[0]
    cend = lax.broadcasted_iota(jnp.int32, (1, ncp), 1) * CMP_STRIDE + CMP_LEN
    cidx = lax.broadcasted_iota(jnp.int32, (1, ncp), 1)
    cmask = (cend <= q_pos + 1) & (cidx < n_c)
    lg = lax.dot_general(_mx(qraw_ref[...]), _mx(kc_ref[...]), NT_DIMS, preferred_element_type=F32) * scale
    p = _masked_softmax(lg, cmask)
    oc_ref[...] = jnp.dot(_mx(p), _mx(vc_ref[...]), preferred_element_type=F32)
    head = lax.broadcasted_iota(jnp.int32, p.shape, 0)
    imp = jnp.sum(jnp.where(head < NSA_GROUP, p, 0.0), axis=0, keepdims=True)
    imp = jnp.broadcast_to(imp, (rows, ncp))
    s_sel = jnp.dot(imp, a_ref[...], preferred_element_type=F32, precision=lax.Precision.HIGHEST)
    nsp = s_sel.shape[1]
    lane = lax.broadcasted_iota(jnp.int32, (rows, nsp), 1).astype(F32)
    cur = q_pos // SEL_LEN
    forced = (lane == 0.0) | (lane == float(cur)) | (lane == float(cur - 1))
    v = jnp.where(lane <= float(cur), jnp.where(forced, jnp.inf, s_sel), -jnp.inf)
    out_lane = lax.broadcasted_iota(jnp.int32, (rows, 128), 1)
    picked = jnp.full((rows, 128), -1.0, F32)
    for r in range(n_top):
        mx = jnp.max(v, axis=1, keepdims=True)
        ix = jnp.min(jnp.where(v == mx, lane, float(nsp)), axis=1, keepdims=True)
        ix = jnp.where(mx > -jnp.inf, ix, -1.0)
        picked = jnp.where(out_lane == r, ix, picked)
        v = jnp.where(lane == ix, -jnp.inf, v)
    idx_ref[...] = picked.astype(jnp.int32)
    lw = lax.dot_general(_mx(qrot_ref[...]), _mx(kw_ref[...]), NT_DIMS, preferred_element_type=F32) * scale
    pw = _masked_softmax(lw, jnp.full(lw.shape, True))
    ow_ref[...] = jnp.dot(_mx(pw), _mx(vw_ref[...]), preferred_element_type=F32)


def sample_cmp_win(qraw, qrot, kvc, a_mat, win, n_c, n_sel, q_pos):
    nb, _, rows, _ = qraw.shape
    ncp = kvc.shape[2]
    nsp = a_mat.shape[1]
    wlen = win.shape[1]
    qspec = pl.BlockSpec((None, None, rows, HEAD_DIM), lambda b, g: (b, g, 0, 0))
    return pl.pallas_call(
        functools.partial(_sample_cmp_win_kernel, n_c=n_c, n_sel=n_sel, n_top=min(SEL_TOP, n_sel), q_pos=q_pos),
        grid=(nb, NSA_KV),
        in_specs=[
            qspec, qspec,
            pl.BlockSpec((None, None, ncp, HEAD_DIM), lambda b, g: (b, g, 0, 0)),
            pl.BlockSpec((None, None, ncp, HEAD_DIM), lambda b, g: (b, NSA_KV + g, 0, 0)),
            pl.BlockSpec((ncp, nsp), lambda b, g: (0, 0)),
            pl.BlockSpec((None, wlen, HEAD_DIM), lambda b, g: (b, 0, g)),
            pl.BlockSpec((None, wlen, HEAD_DIM), lambda b, g: (b, 0, NSA_KV + g)),
        ],
        out_specs=[qspec, qspec, pl.BlockSpec((None, None, rows, 128), lambda b, g: (b, g, 0, 0))],
        out_shape=[
            jax.ShapeDtypeStruct(qraw.shape, F32),
            jax.ShapeDtypeStruct(qraw.shape, F32),
            jax.ShapeDtypeStruct((nb, NSA_KV, rows, 128), jnp.int32),
        ],
        compiler_params=_params("arbitrary", "arbitrary"),
        name="sample_cmp_win",
    )(qraw, qrot, kvc, kvc, a_mat, win, win)


def _softmax_update(m_s, l_s, acc, s, v):
    m_old = m_s[...]
    m_new = jnp.maximum(m_old, jnp.max(s, axis=-1, keepdims=True))
    p = jnp.exp(s - m_new)
    alpha = jnp.exp(m_old - m_new)
    l_s[...] = alpha * l_s[...] + jnp.sum(p, axis=-1, keepdims=True)
    acc[...] = alpha * acc[...] + jnp.dot(_mx(p), _mx(v), preferred_element_type=F32)
    m_s[...] = m_new


def _sample_slc_kernel(pt_ref, idx_ref, q_ref, kv_ref, kn_ref, vn_ref, o_ref, m_s, l_s, acc,
                       *, n_top, n_past_blk):
    b, g, r = pl.program_id(0), pl.program_id(1), pl.program_id(2)
    row = b * NSA_KV + g
    scale = HEAD_DIM ** -0.5

    @pl.when(r == 0)
    def _():
        m_s[...] = jnp.full_like(m_s, NEG_BIG)
        l_s[...] = jnp.zeros_like(l_s)
        acc[...] = jnp.zeros_like(acc)

    blk = idx_ref[row, r]
    in_past = (blk >= 0) & (blk < n_past_blk)

    for gg in range(NSA_KV):
        @pl.when(in_past & (g == gg))
        def _(gg=gg):
            s = lax.dot_general(_mx(q_ref[...]), _mx(kv_ref[:, 0, gg, :]), NT_DIMS,
                                preferred_element_type=F32) * scale
            _softmax_update(m_s, l_s, acc, s, kv_ref[:, 1, gg, :])

    @pl.when(r == n_top - 1)
    def _():
        has_new = idx_ref[row, 0] == n_past_blk
        for t in range(1, n_top):
            has_new = has_new | (idx_ref[row, t] == n_past_blk)
        q = q_ref[...]
        s_new = jnp.sum(q * kn_ref[...], axis=-1, keepdims=True) * scale
        s_new = jnp.where(has_new, s_new, NEG_BIG)
        m_old = m_s[...]
        m_new = jnp.maximum(m_old, s_new)
        p = jnp.where(has_new, jnp.exp(s_new - m_new), 0.0)
        alpha = jnp.exp(m_old - m_new)
        l = alpha * l_s[...] + p
        o = alpha * acc[...] + p * vn_ref[...]
        o_ref[...] = o / jnp.where(l > 0, l, 1.0)


def sample_slc(qrot, pool, e, page_table, idx, slc_new, n_past_blk):
    nb, _, rows, _ = qrot.shape
    n_top = idx.shape[1]
    page = pool.shape[2]
    per_page = page // SEL_LEN

    def kv_map(b, g, r, pt, ix):
        blk = jnp.clip(ix[b * NSA_KV + g, r], 0, n_past_blk - 1)
        return (e, pt[b, blk // per_page], blk % per_page, 0, 0, 0)

    return pl.pallas_call(
        functools.partial(_sample_slc_kernel, n_top=n_top, n_past_blk=n_past_blk),
        grid_spec=pltpu.PrefetchScalarGridSpec(
            num_scalar_prefetch=2,
            grid=(nb, NSA_KV, n_top),
            in_specs=[
                pl.BlockSpec((None, None, rows, HEAD_DIM), lambda b, g, r, pt, ix: (b, g, 0, 0)),
                pl.BlockSpec((None, None, SEL_LEN, 2, NSA_KV, HEAD_DIM), kv_map),
                pl.BlockSpec((None, 1, HEAD_DIM), lambda b, g, r, pt, ix: (b, 0, g)),
                pl.BlockSpec((None, 1, HEAD_DIM), lambda b, g, r, pt, ix: (b, 0, NSA_KV + g)),
            ],
            out_specs=pl.BlockSpec((None, None, rows, HEAD_DIM), lambda b, g, r, pt, ix: (b, g, 0, 0)),
            scratch_shapes=[pltpu.VMEM((rows, 1), F32), pltpu.VMEM((rows, 1), F32), pltpu.VMEM((rows, HEAD_DIM), F32)],
        ),
        out_shape=jax.ShapeDtypeStruct(qrot.shape, F32),
        compiler_params=_params("arbitrary", "arbitrary", "arbitrary"),
        name="sample_slc",
    )(page_table, idx, qrot, pool, slc_new, slc_new)


def _sample_fox_kernel(pt_ref, q_ref, *refs, n_steps, per_step):
    page_refs = refs[:3 * per_step]
    tri_ref, kn_ref, vn_ref, lfn_ref, o_ref, m_s, l_s, acc, crun = refs[3 * per_step:]
    step = pl.program_id(1)
    scale = HEAD_DIM ** -0.5
    nh = FOX_HEADS
    page = page_refs[0].shape[0]
    cols = page * nh

    @pl.when(step == 0)
    def _():
        m_s[...] = jnp.full_like(m_s, NEG_BIG)
        l_s[...] = jnp.zeros_like(l_s)
        acc[...] = jnp.zeros_like(acc)
        crun[...] = jnp.zeros_like(crun)

    qb = _mx(q_ref[...])
    tri = _mx(tri_ref[...])
    lane = lax.broadcasted_iota(jnp.int32, (nh, HEAD_DIM), 1) // nh
    own = (lax.broadcasted_iota(jnp.int32, (nh, cols), 1) % nh) == lax.broadcasted_iota(jnp.int32, (nh, cols), 0)
    c_off = crun[...]
    logits, values = [], []
    for j in range(per_step):
        k_ref, v_ref, lf_ref = page_refs[3 * j:3 * j + 3]
        lf = lf_ref[...]
        hi = _mx(lf).astype(F32)
        mid = _mx(lf - hi).astype(F32)
        lo = lf - hi - mid
        cs = jnp.dot(_mx(jnp.concatenate([hi, mid, lo], axis=0)), tri, preferred_element_type=F32)
        ck = cs[:nh] + cs[nh:2 * nh] + cs[2 * nh:] + c_off
        c_off = ck[:, -1:]
        ck_x = jnp.concatenate(
            [jnp.take_along_axis(ck, lane + i * (HEAD_DIM // nh), axis=1) for i in range(cols // HEAD_DIM)], axis=1)
        k2 = k_ref[...].reshape(cols, HEAD_DIM)
        s = lax.dot_general(qb, _mx(k2), NT_DIMS, preferred_element_type=F32) * scale - ck_x
        logits.append(jnp.where(own, s, NEG_BIG))
        values.append(_mx(v_ref[...].reshape(cols, HEAD_DIM)))
    crun[...] = c_off
    m_old = m_s[...]
    m_new = m_old
    for s in logits:
        m_new = jnp.maximum(m_new, jnp.max(s, axis=-1, keepdims=True))
    alpha = jnp.exp(m_old - m_new)
    l_new = alpha * l_s[...]
    o_new = alpha * acc[...]
    for s, v2 in zip(logits, values):
        p = jnp.where(own, jnp.exp(s - m_new), 0.0)
        l_new = l_new + jnp.sum(p, axis=-1, keepdims=True)
        o_new = o_new + jnp.dot(_mx(p), v2, preferred_element_type=F32)
    l_s[...] = l_new
    acc[...] = o_new
    m_s[...] = m_new

    @pl.when(step == n_steps - 1)
    def _():
        cq = c_off + lfn_ref[...]
        s_new = jnp.sum(q_ref[...] * kn_ref[...], axis=-1, keepdims=True) * scale - cq
        m_fin = jnp.maximum(m_new, s_new)
        p_new = jnp.exp(s_new - m_fin)
        a_fin = jnp.exp(m_new - m_fin)
        o_ref[...] = (a_fin * o_new + p_new * vn_ref[...]) / (a_fin * l_new + p_new)


def sample_fox(fq, pool, logf_t, e, page_table, k_new, v_new, logf_new):
    nb, n_pages = page_table.shape
    page = pool.shape[2]
    per_step = _pick(n_pages, (4, 2, 1))
    tri = jnp.triu(jnp.ones((page, page), F32))

    def page_specs(j):
        def kv(which):
            return pl.BlockSpec((None, None, page, None, FOX_HEADS, HEAD_DIM),
                                lambda b, p, pt: (e, pt[b, p * per_step + j], 0, which, 0, 0))
        return [kv(0), kv(1),
                pl.BlockSpec((None, None, FOX_HEADS, page), lambda b, p, pt: (e, pt[b, p * per_step + j], 0, 0))]

    page_ops = []
    for j in range(per_step):
        page_ops += [pool, pool, logf_t]
    return pl.pallas_call(
        functools.partial(_sample_fox_kernel, n_steps=n_pages // per_step, per_step=per_step),
        grid_spec=pltpu.PrefetchScalarGridSpec(
            num_scalar_prefetch=1,
            grid=(nb, n_pages // per_step),
            in_specs=[
                pl.BlockSpec((None, FOX_HEADS, HEAD_DIM), lambda b, p, pt: (b, 0, 0)),
                *[spec for j in range(per_step) for spec in page_specs(j)],
                pl.BlockSpec((page, page), lambda b, p, pt: (0, 0)),
                pl.BlockSpec((None, FOX_HEADS, HEAD_DIM), lambda b, p, pt: (b, 0, 0)),
                pl.BlockSpec((None, FOX_HEADS, HEAD_DIM), lambda b, p, pt: (b, 0, 0)),
                pl.BlockSpec((None, FOX_HEADS, 1), lambda b, p, pt: (b, 0, 0)),
            ],
            out_specs=pl.BlockSpec((None, FOX_HEADS, HEAD_DIM), lambda b, p, pt: (b, 0, 0)),
            scratch_shapes=[
                pltpu.VMEM((FOX_HEADS, 1), F32), pltpu.VMEM((FOX_HEADS, 1), F32),
                pltpu.VMEM((FOX_HEADS, HEAD_DIM), F32), pltpu.VMEM((FOX_HEADS, 1), F32),
            ],
        ),
        out_shape=jax.ShapeDtypeStruct((nb, FOX_HEADS, HEAD_DIM), F32),
        compiler_params=_params("arbitrary", "arbitrary"),
        name="sample_fox",
    )(page_table, fq, *page_ops, tri, k_new, v_new, logf_new)


def _ret_tables(chunk):
    lg = jnp.log1p(-(2.0 ** (-5.0 - jnp.arange(RET_HEADS, dtype=F32))))
    i = jnp.arange(chunk, dtype=F32)
    diff = i[:, None] - i[None, :]
    dec = jnp.where(diff >= 0, jnp.exp(jnp.maximum(diff, 0.0)[None] * lg[:, None, None]), 0.0)
    xi = jnp.exp((i[None, :] + 1.0) * lg[:, None])
    zeta = jnp.exp((chunk - 1.0 - i)[None, :] * lg[:, None])
    g_c = jnp.exp(chunk * lg)
    return dec, xi, zeta, g_c


def _ret_rope_tables(pos):
    half = RET_DK // 2
    inv = 1.0 / (RET_THETA ** (jnp.arange(half, dtype=F32) / half))
    ang = pos.astype(F32)[:, None] * inv[None, :]
    return jnp.cos(ang), jnp.sin(ang)


def _group_norm_gate(o, gate, gn):
    mu = jnp.mean(o, axis=-1, keepdims=True)
    d = o - mu
    var = jnp.mean(d * d, axis=-1, keepdims=True)
    y = d * lax.rsqrt(var + GN_EPS) * gn
    return gate * jax.nn.sigmoid(gate) * y


def _retention_kernel(q_ref, k_ref, v_ref, gate_ref, cos_ref, sin_ref, dec_ref, coef_ref, gn_ref,
                      y_ref, s_ref, state, *, n_chunks):
    c = pl.program_id(2)
    half = RET_DK // 2

    @pl.when(c == 0)
    def _():
        state[...] = jnp.zeros_like(state)

    cos, sin = cos_ref[...], sin_ref[...]

    def rot(x_ref):
        x1, x2 = x_ref[:, :half], x_ref[:, half:]
        return jnp.concatenate([x1 * cos - x2 * sin, x2 * cos + x1 * sin], axis=-1)

    coef = coef_ref[...]
    xi, zeta, g_c = coef[:, 0:1], coef[:, 1:2], coef[0:1, 2:3]
    q = _mx(rot(q_ref))
    kf = rot(k_ref) * (RET_DK ** -0.5)
    v = _mx(v_ref[...])
    s_old = state[...]
    a = lax.dot_general(q, _mx(kf), NT_DIMS, preferred_element_type=F32) * dec_ref[...]
    o = (jnp.dot(_mx(a), v, preferred_element_type=F32)
         + jnp.dot(q, _mx(s_old), preferred_element_type=F32) * xi)
    s_new = s_old * g_c + lax.dot_general(_mx(kf * zeta), v, TN_DIMS, preferred_element_type=F32)
    state[...] = s_new
    y_ref[...] = _group_norm_gate(o, gate_ref[...], gn_ref[...]).astype(y_ref.dtype)

    @pl.when(c == n_chunks - 1)
    def _():
        s_ref[...] = s_new


def retention_prompt(h, nb, t, gn_g, o_idx):
    ch = RET_CHUNK
    n_chunks = t // ch
    dec, xi, zeta, g_c = _ret_tables(ch)
    coef = jnp.stack([xi, zeta, jnp.broadcast_to(g_c[:, None], xi.shape)], axis=-1)
    coef = jnp.pad(coef, ((0, 0), (0, 0), (0, 128 - 3)))
    cos, sin = _ret_rope_tables(jnp.arange(t))
    kb, vb = O_RK // RET_DK, O_RV // RET_DV
    gb = O_RG // RET_DV
    return pl.pallas_call(
        functools.partial(_retention_kernel, n_chunks=n_chunks),
        grid=(nb, RET_HEADS, n_chunks),
        in_specs=[
            pl.BlockSpec((ch, RET_DK), lambda b, hh, c: (b * n_chunks + c, hh)),
            pl.BlockSpec((ch, RET_DK), lambda b, hh, c: (b * n_chunks + c, kb + hh)),
            pl.BlockSpec((ch, RET_DV), lambda b, hh, c: (b * n_chunks + c, vb + hh)),
            pl.BlockSpec((ch, RET_DV), lambda b, hh, c: (b * n_chunks + c, gb + hh)),
            pl.BlockSpec((ch, RET_DK // 2), lambda b, hh, c: (c, 0)),
            pl.BlockSpec((ch, RET_DK // 2), lambda b, hh, c: (c, 0)),
            pl.BlockSpec((None, ch, ch), lambda b, hh, c: (hh, 0, 0)),
            pl.BlockSpec((None, ch, 128), lambda b, hh, c: (hh, 0, 0)),
            pl.BlockSpec((None, 1, RET_DV), lambda b, hh, c: (o_idx, 0, hh)),
        ],
        out_specs=[
            pl.BlockSpec((ch, RET_DV), lambda b, hh, c: (b * n_chunks + c, hh)),
            pl.BlockSpec((None, None, RET_DK, RET_DV), lambda b, hh, c: (b, hh, 0, 0)),
        ],
        out_shape=[
            jax.ShapeDtypeStruct((nb * t, RET_HEADS * RET_DV), MXU_DTYPE),
            jax.ShapeDtypeStruct((nb, RET_HEADS, RET_DK, RET_DV), F32),
        ],
        scratch_shapes=[pltpu.VMEM((RET_DK, RET_DV), F32)],
        compiler_params=_params("arbitrary", "arbitrary", "arbitrary"),
        name="retention_prompt",
    )(h, h, h, h, cos, sin, dec, coef, gn_g[:, None, :])


def _retention_step_kernel(q_ref, k_ref, v_ref, gate_ref, cos_ref, sin_ref, coef_ref, gn_ref, s0_ref,
                           y_ref, s_ref):
    half = RET_DK // 2
    hh = pl.program_id(1)
    cos, sin = cos_ref[...], sin_ref[...]

    def rot(x_ref):
        x1, x2 = x_ref[:half, :], x_ref[half:, :]
        return jnp.concatenate([x1 * cos - x2 * sin, x2 * cos + x1 * sin], axis=0)

    head = lax.broadcasted_iota(jnp.int32, coef_ref.shape, 0)
    coef = jnp.sum(jnp.where(head == hh, coef_ref[...], 0.0), axis=0, keepdims=True)
    dec, xi, zeta, g_c = coef[:, 0:1], coef[:, 1:2], coef[:, 2:3], coef[:, 3:4]
    q = rot(q_ref)
    k = rot(k_ref) * (RET_DK ** -0.5)
    v = v_ref[...]
    s_old = s0_ref[...]
    a = jnp.sum(q * k, axis=0, keepdims=True) * dec
    o = a * v + jnp.sum(q * s_old, axis=0, keepdims=True) * xi
    s_ref[...] = s_old * g_c + (k * zeta) * v
    y_ref[...] = _group_norm_gate(o, gate_ref[...], gn_ref[...])


def retention_step(q_col, k_col, v_row, gate_row, pos, s0, o_idx, gn_g):
    nb = q_col.shape[0]
    dec, xi, zeta, g_c = _ret_tables(1)
    coef = jnp.stack([dec[:, 0, 0], xi[:, 0], zeta[:, 0], g_c], axis=-1)
    coef = jnp.pad(coef, ((0, 0), (0, 128 - 4)))
    cos, sin = _ret_rope_tables(pos)
    col = pl.BlockSpec((None, None, RET_DK, 1), lambda b, hh: (b, hh, 0, 0))
    row = pl.BlockSpec((None, None, 1, RET_DV), lambda b, hh: (b, hh, 0, 0))
    tab = pl.BlockSpec((RET_DK // 2, 1), lambda b, hh: (0, 0))
    return pl.pallas_call(
        _retention_step_kernel,
        grid=(nb, RET_HEADS),
        in_specs=[
            col, col, row, row, tab, tab,
            pl.BlockSpec((RET_HEADS, 128), lambda b, hh: (0, 0)),
            pl.BlockSpec((None, 1, RET_DV), lambda b, hh: (o_idx, 0, hh)),
            pl.BlockSpec((None, None, None, RET_DK, RET_DV), lambda b, hh: (o_idx, b, hh, 0, 0)),
        ],
        out_specs=[row, pl.BlockSpec((None, None, RET_DK, RET_DV), lambda b, hh: (b, hh, 0, 0))],
        out_shape=[
            jax.ShapeDtypeStruct((nb, RET_HEADS, 1, RET_DV), F32),
            jax.ShapeDtypeStruct((nb, RET_HEADS, RET_DK, RET_DV), F32),
        ],
        compiler_params=_params("arbitrary", "arbitrary"),
        name="retention_step",
    )(q_col, k_col, v_row, gate_row, cos.reshape(-1, 1), sin.reshape(-1, 1), coef, gn_g[:, None, :], s0)


def _pad_rows(x, rows):
    return jnp.pad(x, ((0, rows - x.shape[0]),) + ((0, 0),) * (x.ndim - 1))


def _repack_even(w):
    small = jnp.concatenate([w[:, O_NG:O_FQ], w[:, O_FF:E_EVEN]], axis=1)
    small = jnp.pad(small, ((0, 0), (0, 128 - small.shape[1])))
    return jnp.concatenate([w[:, O_NQ:O_NG], w[:, O_FQ:O_FF], small], axis=1)


def _even_layer(e, xb_p, xb_s, nb, t, ns, past, caches, page_table, wts):
    cache_cmp, cache_slc, cache_win, cache_fox, logf_pool_t = caches
    w_in, fox_f_bias, cmp_pos, cmp_w1, cmp_w2 = wts
    w_rep = _repack_even(w_in[e])
    fb_row = jnp.zeros((1, 128), F32).at[0, N_GATE:N_GATE + FOX_HEADS].set(fox_f_bias[e])
    hq = P_FQ // HEAD_DIM
    kvw = 2 * NSA_KV * HEAD_DIM
    nw = NSA_HEADS * HEAD_DIM

    h_p = matmul(xb_p, w_rep, (), (1152, 640, 384, 128))
    tm_post = _pick(nb * t, (512, 256, 128))
    q_rot, slc_p, win_p, gates_p, logf_p = even_post(
        h_p, _rope_tables(jnp.arange(t)), fb_row, t // tm_post)
    c_p = cumsum_rows(logf_p, nb, t)
    c_heads = jnp.swapaxes(c_p[:, N_GATE:N_GATE + FOX_HEADS].reshape(nb, t, FOX_HEADS), 1, 2)
    n16 = t // CMP_STRIDE
    kvc_p = nsa_compress(
        h_p, pl.BlockSpec((t, HEAD_DIM), lambda b, c: (b, P_NKV // HEAD_DIM + c)), nb, t, e, cmp_w1, cmp_pos, cmp_w2)
    a_p = _sel_matrix(n16, n16 - 1, t // SEL_LEN, 128)
    o_c, sel = cmp_select(h_p, kvc_p, a_p, nb, t)
    o_s = flash_attention("slc", nb, t, q_rot, lambda g: g, slc_p, lambda g: g, slc_p, lambda g: NSA_KV + g, (sel,))
    o_w = flash_attention("win", nb, t, q_rot, lambda g: g, win_p, lambda g: g, win_p, lambda g: NSA_KV + g)
    fq0 = P_FQ // (NSA_GROUP * HEAD_DIM)
    fgroups = FOX_HEADS // NSA_GROUP
    o_f = flash_attention("fox", nb, t, h_p, lambda g: fq0 + g, h_p, lambda g: fq0 + fgroups + g,
                          h_p, lambda g: fq0 + 2 * fgroups + g, (c_heads[..., None], c_heads[:, :, None, :]))
    xcat_p = combine_heads(o_c, o_s, o_w, o_f, gates_p)
    st_p = (
        h_p[:, P_NKV:P_NKV + kvw].reshape(nb, t, 2, NSA_KV, HEAD_DIM),
        slc_p.reshape(nb, t, 2, NSA_KV, HEAD_DIM),
        win_p.reshape(nb, t, 2, NSA_KV, HEAD_DIM)[:, t - min(WINDOW, t):],
        h_p[:, P_FQ + nw:P_FQ + 3 * nw].reshape(nb, t, 2, FOX_HEADS, HEAD_DIM),
        logf_p[:, N_GATE:N_GATE + FOX_HEADS].reshape(nb, t, FOX_HEADS),
    )

    rows = xb_s.shape[0]
    h_s = matmul(xb_s, w_rep, (), (1152, 640, 384, 128))
    q_rot_s, slc_s, win_s, gates_s, logf_s = even_post(
        h_s, _rope_tables(jnp.full((rows,), past)), fb_row, 1)
    new_win = jnp.concatenate(
        [cache_win[e].reshape(ns, -1, kvw), win_s[:ns, None, :]], axis=1)[:, 1:]
    cmp_rows = gather_cmp_pages(cache_cmp, e, page_table)
    l_tot = past + 1
    n16_s = l_tot // CMP_STRIDE
    assert n16_s * CMP_STRIDE == past
    kvc_s = nsa_compress(
        cmp_rows, pl.BlockSpec((None, None, past, HEAD_DIM), lambda b, c: (b, c, 0, 0)), ns, past, e,
        cmp_w1, cmp_pos, cmp_w2)
    n_sel_s = -(-l_tot // SEL_LEN)
    nsp = -(-n_sel_s // 128) * 128
    a_s = _sel_matrix(n16_s, n16_s - 1, n_sel_s, nsp)

    def q_rows(x):
        x = x[:ns].reshape(ns, NSA_KV, NSA_GROUP, HEAD_DIM)
        return jnp.pad(x, ((0, 0), (0, 0), (0, 8 - NSA_GROUP), (0, 0)))

    qraw_g, qrot_g = q_rows(h_s[:, :nw]), q_rows(q_rot_s)
    o_c_s, o_w_s, idx = sample_cmp_win(qraw_g, qrot_g, kvc_s, a_s, new_win, n16_s - 1, n_sel_s, past)
    idx2 = idx[:, :, 0, :min(SEL_TOP, n_sel_s)].reshape(ns * NSA_KV, -1)
    o_s_s = sample_slc(qrot_g, cache_slc, e, page_table, idx2, slc_s[:ns, None, :], past // SEL_LEN)
    fq = h_s[:ns, P_FQ:P_FQ + nw].reshape(ns, FOX_HEADS, HEAD_DIM)
    fk = h_s[:ns, P_FQ + nw:P_FQ + 2 * nw]
    fv = h_s[:ns, P_FQ + 2 * nw:P_FQ + 3 * nw]
    logf_new = logf_s[:ns, N_GATE:N_GATE + FOX_HEADS]
    o_f_s = sample_fox(fq, cache_fox, logf_pool_t, e, page_table, fk.reshape(ns, FOX_HEADS, HEAD_DIM),
                       fv.reshape(ns, FOX_HEADS, HEAD_DIM), logf_new[:, :, None])

    def heads_flat(x):
        return _pad_rows(x[:, :, :NSA_GROUP].reshape(ns, nw), rows)

    xcat_s = combine_heads(heads_flat(o_c_s), heads_flat(o_s_s), heads_flat(o_w_s),
                           _pad_rows(o_f_s.reshape(ns, nw), rows), gates_s)
    st_s = (
        h_s[:ns, P_NKV:P_NKV + kvw].reshape(ns, 1, 2, NSA_KV, HEAD_DIM),
        slc_s[:ns].reshape(ns, 1, 2, NSA_KV, HEAD_DIM),
        new_win.reshape(ns, -1, 2, NSA_KV, HEAD_DIM),
        h_s[:ns, P_FQ + nw:P_FQ + 3 * nw].reshape(ns, 1, 2, FOX_HEADS, HEAD_DIM),
        logf_new.reshape(ns, 1, FOX_HEADS),
    )
    return xcat_p, xcat_s, st_p, st_s


def _odd_layer(o, xb_p, xb_s, nb, t, ns, past, state_ret, w_in_odd, ret_gn_g):
    h_p = matmul(xb_p, w_in_odd, (o,), (1024, 512, 256, 128))
    y_p, s_p = retention_prompt(h_p, nb, t, ret_gn_g, o)
    rows = xb_s.shape[0]
    h_s = matmul(xb_s, w_in_odd, (o,), (1024, 512, 256, 128))[:ns]
    q_col = h_s[:, :O_RK].reshape(ns, RET_HEADS, RET_DK, 1)
    k_col = h_s[:, O_RK:O_RV].reshape(ns, RET_HEADS, RET_DK, 1)
    v_row = h_s[:, O_RV:O_RG].reshape(ns, RET_HEADS, 1, RET_DV)
    g_row = h_s[:, O_RG:E_ODD].reshape(ns, RET_HEADS, 1, RET_DV)
    y_s, s_s = retention_step(q_col, k_col, v_row, g_row, jnp.full((1,), past), state_ret, o, ret_gn_g)
    y_s = _pad_rows(y_s.reshape(ns, RET_HEADS * RET_DV), rows).astype(MXU_DTYPE)
    return y_p, y_s, s_p, s_s


def kernel(x_prompt, x_sample, cache_nsa_cmp, cache_nsa_slc, cache_nsa_win, cache_fox_kv, cache_fox_logf,
           state_ret, page_table, w_ffn_in, w_ffn_out, ln_g, ln_b, w_in_even, w_out_even, fox_f_bias,
           nsa_cmp_pos, nsa_cmp_w1, nsa_cmp_w2, w_in_odd, ret_gn_g, w_out_odd):
    nb, t, d = x_prompt.shape
    ns, ts, _ = x_sample.shape
    depth = w_ffn_in.shape[0]
    n_pages = page_table.shape[1]
    page = cache_nsa_cmp.shape[2]
    past = n_pages * page
    assert ts == 1 and past >= WINDOW and past % SEL_LEN == 0 and t % RET_CHUNK == 0
    alpha = (2.0 * depth) ** 0.25
    rows = max(16, -(-ns // 16) * 16)

    xp = x_prompt.reshape(nb * t, d)
    xs = _pad_rows(x_sample.reshape(ns * ts, d), rows)
    xp_b, xs_b = xp.astype(MXU_DTYPE), xs.astype(MXU_DTYPE)

    caches = (cache_nsa_cmp, cache_nsa_slc, cache_nsa_win, cache_fox_kv, jnp.swapaxes(cache_fox_logf, 2, 3))
    w_ffn_out, w_out_even, w_out_odd = _mx(w_ffn_out), _mx(w_out_even), _mx(w_out_odd)

    def ffn(x, xb, l, s):
        hid = swiglu_hidden(xb, w_ffn_in, l, s)
        return matmul_postnorm(hid, w_ffn_out, (l, s), x, ln_g, ln_b, (l, 2 * s), alpha, 0.5)

    new_p = [[] for _ in range(6)]
    new_s = [[] for _ in range(6)]
    for l in range(depth):
        xp, xp_b = ffn(xp, xp_b, l, 0)
        xs, xs_b = ffn(xs, xs_b, l, 0)
        if l % 2 == 0:
            e = l // 2
            mp, ms, st_p, st_s = _even_layer(
                e, xp_b, xs_b, nb, t, ns, past, caches, page_table,
                (w_in_even, fox_f_bias, nsa_cmp_pos, nsa_cmp_w1, nsa_cmp_w2))
            w_out, widx = w_out_even, (e,)
            slots = (0, 1, 2, 3, 4)
        else:
            o = l // 2
            mp, ms, sp, ss = _odd_layer(o, xp_b, xs_b, nb, t, ns, past, state_ret, w_in_odd, ret_gn_g)
            st_p, st_s = (sp,), (ss,)
            w_out, widx = w_out_odd, (o,)
            slots = (5,)
        for i, a_p, a_s in zip(slots, st_p, st_s):
            new_p[i].append(a_p)
            new_s[i].append(a_s)
        xp, xp_b = matmul_postnorm(mp, w_out, widx, xp, ln_g, ln_b, (l, 1), alpha, 1.0)
        xs, xs_b = matmul_postnorm(ms, w_out, widx, xs, ln_g, ln_b, (l, 1), alpha, 1.0)
        xp, xp_b = ffn(xp, xp_b, l, 1)
        xs, xs_b = ffn(xs, xs_b, l, 1)

    outs = [xp.reshape(nb, t, d), xs[:ns].reshape(ns, ts, d)]
    for i in range(6):
        outs.append(jnp.stack(new_p[i]))
        outs.append(jnp.stack(new_s[i]))
    return tuple(outs)
```

```python
import functools
import math

import numpy as np
import jax
import jax.numpy as jnp
from jax import lax
from jax.experimental import pallas as pl
from jax.experimental.pallas import tpu as pltpu

F32 = jnp.float32
MXU_DTYPE = jnp.bfloat16

HEAD_DIM = 128
NSA_HEADS = 8
NSA_KV = 2
NSA_GROUP = NSA_HEADS // NSA_KV
CMP_LEN = 32
CMP_STRIDE = 16
CMP_HIDDEN = 256
SEL_LEN = 64
SEL_TOP = 16
WINDOW = 512
FOX_HEADS = 8
ROPE_THETA = 500000.0
ROPE_DIMS = HEAD_DIM // 4
RET_HEADS = 8
RET_DK = 256
RET_DV = 512
RET_CHUNK = 128
RET_THETA = 10000.0
RET_HEADS_PER_STEP = 4
LN_EPS = 1e-5
GN_EPS = 1e-6

O_NQ = 0
O_NKV = O_NQ + NSA_HEADS * HEAD_DIM
O_NG = O_NKV + 6 * NSA_KV * HEAD_DIM
O_FQ = O_NG + 3 * NSA_HEADS
O_FK = O_FQ + FOX_HEADS * HEAD_DIM
O_FV = O_FK + FOX_HEADS * HEAD_DIM
O_FF = O_FV + FOX_HEADS * HEAD_DIM
E_EVEN = O_FF + FOX_HEADS
P_NKV = NSA_HEADS * HEAD_DIM
P_FQ = P_NKV + 6 * NSA_KV * HEAD_DIM
P_SMALL = P_FQ + 3 * FOX_HEADS * HEAD_DIM
P_EVEN = P_SMALL + 128
N_GATE = 3 * NSA_HEADS
O_RK = RET_HEADS * RET_DK
O_RV = 2 * RET_HEADS * RET_DK
O_RG = O_RV + RET_HEADS * RET_DV
E_ODD = O_RG + RET_HEADS * RET_DV

V7X_VMEM_LIMIT = 56 * 1024 * 1024
LN_ROWS = 128
LOG2E = 1.4426950408889634
NEG_BIG = -1e30
NT_DIMS = (((1,), (1,)), ((), ()))
TN_DIMS = (((0,), (0,)), ((), ()))


def _params(*sem):
    return pltpu.CompilerParams(dimension_semantics=sem, vmem_limit_bytes=V7X_VMEM_LIMIT)


def _mx(x):
    return x.astype(MXU_DTYPE)


def _pick(n, prefs):
    for p in prefs:
        if n % p == 0:
            return p
    return n


def _swiglu_kernel(x_ref, xs_ref, wa_ref, wb_ref, o_ref, os_ref, wa_s, wb_s):
    def hidden(x):
        a = jnp.dot(x, wa_s[...], preferred_element_type=F32)
        b = jnp.dot(x, wb_s[...], preferred_element_type=F32)
        return (a * jax.nn.sigmoid(a) * b).astype(o_ref.dtype)

    @pl.when(pl.program_id(1) == 0)
    def _():
        wa_s[...] = _mx(wa_ref[...])
        wb_s[...] = _mx(wb_ref[...])
        os_ref[...] = hidden(xs_ref[...])

    o_ref[...] = hidden(x_ref[...])


def swiglu_hidden(xb, xsb, w_in, l, s):
    m, d = xb.shape
    ms = xsb.shape[0]
    f = w_in.shape[-1] // 2
    tm = _pick(m, (1024, 512, 256, 128))
    tn = _pick(f, (512, 256, 128))
    nj = f // tn
    return pl.pallas_call(
        _swiglu_kernel,
        grid=(nj, m // tm),
        in_specs=[
            pl.BlockSpec((tm, d), lambda j, i: (i, 0)),
            pl.BlockSpec((ms, d), lambda j, i: (0, 0)),
            pl.BlockSpec((None, None, d, tn), lambda j, i: (l, s, 0, j)),
            pl.BlockSpec((None, None, d, tn), lambda j, i: (l, s, 0, j + nj)),
        ],
        out_specs=[pl.BlockSpec((tm, tn), lambda j, i: (i, j)), pl.BlockSpec((ms, tn), lambda j, i: (0, j))],
        out_shape=[jax.ShapeDtypeStruct((m, f), MXU_DTYPE), jax.ShapeDtypeStruct((ms, f), MXU_DTYPE)],
        scratch_shapes=[pltpu.VMEM((d, tn), MXU_DTYPE), pltpu.VMEM((d, tn), MXU_DTYPE)],
        compiler_params=_params("arbitrary", "arbitrary"),
        name="swiglu_hidden",
    )(xb, xsb, w_in, w_in)


def _mm_ln_kernel(x_ref, w_ref, r_ref, g_ref, b_ref, o_ref, ob_ref, *, alpha, scale, nk):
    k = pl.program_id(1)
    def part():
        return jnp.dot(x_ref[...], _mx(w_ref[...]), preferred_element_type=F32)

    @pl.when(k == 0)
    def _():
        o_ref[...] = part()

    @pl.when(k > 0)
    def _():
        o_ref[...] += part()

    @pl.when(k == nk - 1)
    def _():
        rows = min(LN_ROWS, o_ref.shape[0])

        def norm_rows(c, carry):
            sl = pl.ds(pl.multiple_of(c * rows, rows), rows)
            z = alpha * r_ref[sl, :] + scale * o_ref[sl, :]
            mu = jnp.mean(z, axis=-1, keepdims=True)
            dz = z - mu
            var = jnp.mean(dz * dz, axis=-1, keepdims=True)
            y = dz * lax.rsqrt(var + LN_EPS) * g_ref[...] + b_ref[...]
            o_ref[sl, :] = y
            ob_ref[sl, :] = y.astype(ob_ref.dtype)
            return carry

        lax.fori_loop(0, o_ref.shape[0] // rows, norm_rows, 0)


def matmul_postnorm(xb, w, widx, res, g, b, gidx, alpha, scale):
    m, kdim = xb.shape
    n = w.shape[-1]
    tm = _pick(m, (512, 256, 128))
    tk = _pick(kdim, (2816, 2048, 1024, 512, 256, 128))
    nk = kdim // tk
    nlead = len(widx)
    w_spec = pl.BlockSpec((None,) * nlead + (tk, n), lambda i, k: tuple(widx) + (k, 0))
    g_spec = pl.BlockSpec((None, None, 1, n), lambda i, k: tuple(gidx) + (0, 0))
    return pl.pallas_call(
        functools.partial(_mm_ln_kernel, alpha=alpha, scale=scale, nk=nk),
        grid=(m // tm, nk),
        in_specs=[
            pl.BlockSpec((tm, tk), lambda i, k: (i, k)),
            w_spec,
            pl.BlockSpec((tm, n), lambda i, k: (i, 0)),
            g_spec, g_spec,
        ],
        out_specs=[pl.BlockSpec((tm, n), lambda i, k: (i, 0)), pl.BlockSpec((tm, n), lambda i, k: (i, 0))],
        out_shape=[jax.ShapeDtypeStruct((m, n), F32), jax.ShapeDtypeStruct((m, n), MXU_DTYPE)],
        compiler_params=_params("arbitrary", "arbitrary"),
        name="matmul_postnorm",
    )(xb, w, res, g[:, :, None, :], b[:, :, None, :])


def _mm_kernel(x_ref, xs_ref, w_ref, o_ref, os_ref, w_s):
    @pl.when(pl.program_id(1) == 0)
    def _():
        w_s[...] = _mx(w_ref[...])
        os_ref[...] = jnp.dot(xs_ref[...], w_s[...], preferred_element_type=F32)
    o_ref[...] = jnp.dot(x_ref[...], w_s[...], preferred_element_type=F32)


def matmul(xb, xsb, w, widx, tn_prefs):
    m, kdim = xb.shape
    ms = xsb.shape[0]
    n = w.shape[-1]
    tm = _pick(m, (1024, 512, 256, 128))
    tn = _pick(n, tn_prefs)
    nlead = len(widx)
    return pl.pallas_call(
        _mm_kernel,
        grid=(n // tn, m // tm),
        in_specs=[
            pl.BlockSpec((tm, kdim), lambda j, i: (i, 0)),
            pl.BlockSpec((ms, kdim), lambda j, i: (0, 0)),
            pl.BlockSpec((None,) * nlead + (kdim, tn), lambda j, i: tuple(widx) + (0, j)),
        ],
        out_specs=[pl.BlockSpec((tm, tn), lambda j, i: (i, j)), pl.BlockSpec((ms, tn), lambda j, i: (0, j))],
        out_shape=[jax.ShapeDtypeStruct((m, n), F32), jax.ShapeDtypeStruct((ms, n), F32)],
        scratch_shapes=[pltpu.VMEM((kdim, tn), MXU_DTYPE)],
        compiler_params=_params("arbitrary", "arbitrary"),
        name="matmul",
    )(xb, xsb, w)


def _rope_tables(pos):
    half = ROPE_DIMS // 2
    inv = 1.0 / (ROPE_THETA ** (jnp.arange(half, dtype=F32) / half))
    ang = pos.astype(F32)[:, None] * inv[None, :]
    cos, sin = jnp.cos(ang), jnp.sin(ang)
    n = pos.shape[0]
    ones = jnp.ones((n, HEAD_DIM - ROPE_DIMS), F32)
    zeros = jnp.zeros((n, HEAD_DIM - ROPE_DIMS), F32)
    zh = jnp.zeros((n, half), F32)
    c = jnp.concatenate([cos, cos, ones], -1)
    a = jnp.concatenate([-sin, zh, zeros], -1)
    b = jnp.concatenate([zh, sin, zeros], -1)
    return c, a, b


def _rope(x, c, a, b):
    half = ROPE_DIMS // 2
    return x * c + pltpu.roll(x, HEAD_DIM - half, 1) * a + pltpu.roll(x, half, 1) * b


def _even_post_kernel(q_ref, slc_ref, win_ref, sm_ref, c_ref, a_ref, b_ref, fb_ref,
                      qr_ref, slco_ref, wino_ref, gate_ref, logf_ref):
    c, a, b = c_ref[...], a_ref[...], b_ref[...]
    for h in range(NSA_HEADS):
        sl = slice(h * HEAD_DIM, (h + 1) * HEAD_DIM)
        qr_ref[:, sl] = _rope(q_ref[:, sl], c, a, b)
    for src, dst in ((slc_ref, slco_ref), (win_ref, wino_ref)):
        for g in range(NSA_KV):
            sl = slice(g * HEAD_DIM, (g + 1) * HEAD_DIM)
            dst[:, sl] = _rope(src[:, sl], c, a, b)
        vs = slice(NSA_KV * HEAD_DIM, 2 * NSA_KV * HEAD_DIM)
        dst[:, vs] = src[:, vs]
    sm = sm_ref[...]
    gate_ref[...] = jax.nn.sigmoid(sm)
    z = sm + fb_ref[...]
    logf_ref[...] = jnp.minimum(z, 0.0) - jnp.log1p(jnp.exp(-jnp.abs(z)))


def even_post(h, tabs, fb_row, npos_blocks):
    m = h.shape[0]
    tm = _pick(m, (512, 256, 128))
    kvw = 2 * NSA_KV * HEAD_DIM
    tab_spec = pl.BlockSpec((tm, HEAD_DIM), lambda i: (i % npos_blocks, 0))
    return pl.pallas_call(
        _even_post_kernel,
        grid=(m // tm,),
        in_specs=[
            pl.BlockSpec((tm, P_NKV), lambda i: (i, 0)),
            pl.BlockSpec((tm, kvw), lambda i: (i, (P_NKV + kvw) // kvw)),
            pl.BlockSpec((tm, kvw), lambda i: (i, (P_NKV + 2 * kvw) // kvw)),
            pl.BlockSpec((tm, 128), lambda i: (i, P_SMALL // 128)),
            tab_spec, tab_spec, tab_spec,
            pl.BlockSpec((1, 128), lambda i: (0, 0)),
        ],
        out_specs=[
            pl.BlockSpec((tm, P_NKV), lambda i: (i, 0)),
            pl.BlockSpec((tm, kvw), lambda i: (i, 0)),
            pl.BlockSpec((tm, kvw), lambda i: (i, 0)),
            pl.BlockSpec((tm, 128), lambda i: (i, 0)),
            pl.BlockSpec((tm, 128), lambda i: (i, 0)),
        ],
        out_shape=[
            jax.ShapeDtypeStruct((m, P_NKV), F32),
            jax.ShapeDtypeStruct((m, kvw), F32),
            jax.ShapeDtypeStruct((m, kvw), F32),
            jax.ShapeDtypeStruct((m, 128), F32),
            jax.ShapeDtypeStruct((m, 128), F32),
        ],
        compiler_params=_params("arbitrary"),
        name="even_post",
    )(h, h, h, h, *tabs, fb_row)


def _cumsum_kernel(x_ref, tri_ref, o_ref, carry):
    @pl.when(pl.program_id(1) == 0)
    def _():
        carry[...] = jnp.zeros_like(carry)
    c = jnp.dot(tri_ref[...], x_ref[...], preferred_element_type=F32,
                precision=lax.Precision.HIGHEST) + carry[...]
    o_ref[...] = c
    carry[...] = c[-1:, :]


def cumsum_rows(x, nb, t):
    tc = _pick(t, (512, 256, 128))
    tri = jnp.tril(jnp.ones((tc, tc), F32))
    nt = t // tc
    return pl.pallas_call(
        _cumsum_kernel,
        grid=(nb, nt),
        in_specs=[pl.BlockSpec((tc, 128), lambda b, i: (b * nt + i, 0)),
                  pl.BlockSpec((tc, tc), lambda b, i: (0, 0))],
        out_specs=pl.BlockSpec((tc, 128), lambda b, i: (b * nt + i, 0)),
        out_shape=jax.ShapeDtypeStruct(x.shape, F32),
        scratch_shapes=[pltpu.VMEM((1, 128), F32)],
        compiler_params=_params("arbitrary", "arbitrary"),
        name="cumsum_rows",
    )(x, tri)


def _compress_kernel(x_ref, w1_ref, pe_ref, w2_ref, o_ref, u0, u1, *, n16):
    u0[...] = jnp.zeros_like(u0)
    u1[...] = jnp.zeros_like(u1)
    for r in range(CMP_STRIDE):
        xr = x_ref[pl.ds(r, n16, stride=CMP_STRIDE), :]
        u0[...] += jnp.dot(_mx(xr + pe_ref[r:r + 1, :]), _mx(w1_ref[r]), preferred_element_type=F32)
        u1[...] += jnp.dot(_mx(xr + pe_ref[CMP_STRIDE + r:CMP_STRIDE + r + 1, :]), _mx(w1_ref[CMP_STRIDE + r]),
                           preferred_element_type=F32)
    pre = u0[...] + pltpu.roll(u1[...], n16 - 1, 0)
    out = jnp.dot(_mx(jax.nn.gelu(pre)), _mx(w2_ref[...]), preferred_element_type=F32)
    row = lax.broadcasted_iota(jnp.int32, out.shape, 0)
    o_ref[...] = jnp.where(row < n16 - 1, out, 0.0)


def nsa_compress(x, x_spec, nb, length, e, w1, pe, w2):
    n16 = length // CMP_STRIDE
    w1r = w1.reshape(w1.shape[0], 2, CMP_LEN, HEAD_DIM, CMP_HIDDEN)
    return pl.pallas_call(
        functools.partial(_compress_kernel, n16=n16),
        grid=(nb, 2 * NSA_KV),
        in_specs=[
            x_spec,
            pl.BlockSpec((None, None, CMP_LEN, HEAD_DIM, CMP_HIDDEN), lambda b, c: (e, c // NSA_KV, 0, 0, 0)),
            pl.BlockSpec((None, None, CMP_LEN, HEAD_DIM), lambda b, c: (e, c // NSA_KV, 0, 0)),
            pl.BlockSpec((None, None, CMP_HIDDEN, HEAD_DIM), lambda b, c: (e, c // NSA_KV, 0, 0)),
        ],
        out_specs=pl.BlockSpec((None, None, n16, HEAD_DIM), lambda b, c: (b, c, 0, 0)),
        out_shape=jax.ShapeDtypeStruct((nb, 2 * NSA_KV, n16, HEAD_DIM), F32),
        scratch_shapes=[pltpu.VMEM((n16, CMP_HIDDEN), F32), pltpu.VMEM((n16, CMP_HIDDEN), F32)],
        compiler_params=_params("arbitrary", "arbitrary"),
        name="nsa_compress",
    )(x, w1r, pe, w2)


def _sel_matrix(n_c_pad, n_c, n_sel, width):
    ratio = SEL_LEN // CMP_STRIDE
    i = np.arange(n_c_pad)[:, None]
    j = np.arange(width)[None, :]
    a = (i >= ratio * j - 1) & (i <= ratio * j + ratio - 1) & (i < n_c) & (j < n_sel)
    return jnp.asarray(a.astype(np.float32))


def _masked_softmax(lg, mask):
    lg = jnp.where(mask, lg, -jnp.inf)
    m = jnp.max(lg, axis=-1, keepdims=True)
    m = jnp.where(m == -jnp.inf, 0.0, m)
    p = jnp.exp(lg - m)
    s = jnp.sum(p, axis=-1, keepdims=True)
    return p / jnp.where(s > 0, s, 1.0)


def _cmp_select_kernel(q_ref, kc_ref, vc_ref, a_ref, oc_ref, sel_ref, *, tq, n_sel, n_top):
    i = pl.program_id(2)
    ncp = kc_ref.shape[0]
    qpos = i * tq + lax.broadcasted_iota(jnp.int32, (tq, 1), 0)
    cend = lax.broadcasted_iota(jnp.int32, (1, ncp), 1) * CMP_STRIDE + CMP_LEN
    cmask = cend <= qpos + 1
    kc = _mx(kc_ref[...])
    vc = _mx(vc_ref[...])
    scale = HEAD_DIM ** -0.5
    imp = jnp.zeros((tq, ncp), F32)
    for m in range(NSA_GROUP):
        sl = slice(m * HEAD_DIM, (m + 1) * HEAD_DIM)
        lg = lax.dot_general(_mx(q_ref[:, sl]), kc, NT_DIMS, preferred_element_type=F32) * scale
        p = _masked_softmax(lg, cmask)
        oc_ref[:, sl] = jnp.dot(_mx(p), vc, preferred_element_type=F32)
        imp = imp + p
    s_sel = jnp.dot(imp, a_ref[...], preferred_element_type=F32, precision=lax.Precision.HIGHEST)
    st = s_sel.T[:n_sel, :]
    blk = lax.broadcasted_iota(jnp.int32, (n_sel, tq), 0)
    cur = (i * tq + lax.broadcasted_iota(jnp.int32, (n_sel, tq), 1)) // SEL_LEN
    forced = (blk == 0) | (blk == cur) | (blk == cur - 1)
    allowed = blk <= cur
    v = jnp.where(allowed, jnp.where(forced, jnp.inf, st), -jnp.inf)
    rank = jnp.zeros((n_sel, tq), jnp.int32)
    for r in range(n_sel):
        vr = v[r:r + 1, :]
        before = (vr > v) | ((vr == v) & (blk > r))
        rank = rank + before.astype(jnp.int32)
    sel = ((rank < n_top) & allowed).astype(F32)
    if n_sel < tq:
        sel = jnp.concatenate([sel, jnp.zeros((tq - n_sel, tq), F32)], axis=0)
    sel_ref[...] = sel.T.astype(sel_ref.dtype)


def cmp_select(h, kvc, a_mat, nb, t):
    tq = 128
    nt = t // tq
    n_sel = t // SEL_LEN
    assert n_sel <= tq and n_sel % 8 == 0
    ncp = kvc.shape[2]
    gw = NSA_GROUP * HEAD_DIM
    return pl.pallas_call(
        functools.partial(_cmp_select_kernel, tq=tq, n_sel=n_sel, n_top=min(SEL_TOP, n_sel)),
        grid=(nb, NSA_KV, nt),
        in_specs=[
            pl.BlockSpec((tq, gw), lambda b, g, i: (b * nt + i, g)),
            pl.BlockSpec((None, None, ncp, HEAD_DIM), lambda b, g, i: (b, g, 0, 0)),
            pl.BlockSpec((None, None, ncp, HEAD_DIM), lambda b, g, i: (b, NSA_KV + g, 0, 0)),
            pl.BlockSpec((ncp, 128), lambda b, g, i: (0, 0)),
        ],
        out_specs=[
            pl.BlockSpec((tq, gw), lambda b, g, i: (b * nt + i, g)),
            pl.BlockSpec((None, None, tq, 128), lambda b, g, i: (b, g, i, 0)),
        ],
        out_shape=[
            jax.ShapeDtypeStruct((nb * t, NSA_HEADS * HEAD_DIM), F32),
            jax.ShapeDtypeStruct((nb, NSA_KV, t, 128), MXU_DTYPE),
        ],
        compiler_params=_params("arbitrary", "arbitrary", "arbitrary"),
        name="cmp_select",
    )(h, kvc, kvc, a_mat)


def _flash_steps(mode, t, tq, tk):
    rows = []
    for qi in range(t // tq):
        q_lo, q_hi = qi * tq, qi * tq + tq - 1
        k_hi = q_hi // tk
        k_lo = max((q_lo - WINDOW + 1) // tk, 0) if mode == "win" else 0
        for kt in range(k_lo, k_hi + 1):
            every_key_visible = kt * tk + tk - 1 <= q_lo
            rows.append((qi, kt, int(kt == k_lo), int(kt == k_hi), int(not every_key_visible)))
    return jnp.asarray(np.array(rows, np.int32).T)


def _flash_kernel(*refs, mode, tq, tk, heads):
    if mode == "fox":
        tab, q_ref, k_ref, v_ref, cq_ref, ck_ref, o_ref, m_s, acc = refs
    elif mode == "slc":
        tab, q_ref, k_ref, v_ref, sel_ref, o_ref, m_s, acc = refs
    else:
        tab, q_ref, k_ref, v_ref, o_ref, m_s, acc = refs
    step = pl.program_id(2)
    qi, kt = tab[0, step], tab[1, step]

    @pl.when(tab[2, step] == 1)
    def _():
        m_s[...] = jnp.full_like(m_s, NEG_BIG)
        acc[...] = jnp.zeros_like(acc)

    def update(causal):
        ones = jnp.ones((tk, HEAD_DIM), MXU_DTYPE)
        if mode != "fox":
            kb = _mx(k_ref[...])
            v_aug = jnp.concatenate([_mx(v_ref[...]), ones], axis=1)
        valid = None
        if causal or mode == "win":
            qpos = qi * tq + lax.broadcasted_iota(jnp.int32, (tq, tk), 0)
            kpos = kt * tk + lax.broadcasted_iota(jnp.int32, (tq, tk), 1)
            valid = kpos <= qpos
            if mode == "win":
                valid = valid & (kpos > qpos - WINDOW)
        if mode == "slc":
            blk = lax.broadcasted_iota(jnp.int32, (128, tk), 0)
            kblk = (kt * tk + lax.broadcasted_iota(jnp.int32, (128, tk), 1)) // SEL_LEN
            chosen = jnp.dot(sel_ref[...], (blk == kblk).astype(MXU_DTYPE), preferred_element_type=F32) > 0.5
            valid = chosen if valid is None else valid & chosen
        for hh in range(heads):
            cols = slice(hh * HEAD_DIM, (hh + 1) * HEAD_DIM)
            if mode == "fox":
                kb = _mx(k_ref[:, cols])
                v_aug = jnp.concatenate([_mx(v_ref[:, cols]), ones], axis=1)
                ck2 = ck_ref[hh] * LOG2E
                cq2 = cq_ref[hh] * LOG2E
            s = lax.dot_general(_mx(q_ref[:, cols]), kb, NT_DIMS, preferred_element_type=F32)
            x = s * (HEAD_DIM ** -0.5 * LOG2E)
            if mode == "fox":
                x = x - ck2
            if valid is not None:
                x = jnp.where(valid, x, NEG_BIG)
            top = jnp.max(x, axis=-1, keepdims=True)
            if mode == "fox":
                top = top + cq2
            m_old = m_s[hh]
            m_new = jnp.maximum(m_old, top)
            p = jnp.exp2(x - (m_new - cq2 if mode == "fox" else m_new))
            if mode == "win":
                p = jnp.where(valid, p, 0.0)
            acc[hh] = jnp.exp2(m_old - m_new) * acc[hh] + jnp.dot(_mx(p), v_aug, preferred_element_type=F32)
            m_s[hh] = m_new

    if mode == "win":
        update(True)
    else:
        pl.when(tab[4, step] == 1)(functools.partial(update, True))
        pl.when(tab[4, step] == 0)(functools.partial(update, False))

    @pl.when(tab[3, step] == 1)
    def _():
        for hh in range(heads):
            l = acc[hh, :, HEAD_DIM:]
            o_ref[:, hh * HEAD_DIM:(hh + 1) * HEAD_DIM] = acc[hh, :, :HEAD_DIM] / jnp.where(l > 0, l, 1.0)


def flash_attention(mode, nb, t, q, qcol, k, kcol, v, vcol, extra=()):
    heads = NSA_GROUP
    tk = _pick(t, (512,) if mode == "win" else (1024, 512, 256, 128))
    tq = _pick(t, (512, 256, 128))
    ntq, ntk = t // tq, t // tk
    qw = heads * HEAD_DIM
    kw = qw if mode == "fox" else HEAD_DIM
    steps = _flash_steps(mode, t, tq, tk)
    in_specs = [
        pl.BlockSpec((tq, qw), lambda b, h, s, tab: (b * ntq + tab[0, s], qcol(h))),
        pl.BlockSpec((tk, kw), lambda b, h, s, tab: (b * ntk + tab[1, s], kcol(h))),
        pl.BlockSpec((tk, kw), lambda b, h, s, tab: (b * ntk + tab[1, s], vcol(h))),
    ]
    scratch = [pltpu.VMEM((heads, tq, 1), F32), pltpu.VMEM((heads, tq, 2 * HEAD_DIM), F32)]
    if mode == "fox":
        in_specs += [
            pl.BlockSpec((None, heads, tq, 1), lambda b, h, s, tab: (b, h, tab[0, s], 0)),
            pl.BlockSpec((None, heads, 1, tk), lambda b, h, s, tab: (b, h, 0, tab[1, s])),
        ]
    elif mode == "slc":
        in_specs += [pl.BlockSpec((None, None, tq, 128), lambda b, h, s, tab: (b, h, tab[0, s], 0))]
    return pl.pallas_call(
        functools.partial(_flash_kernel, mode=mode, tq=tq, tk=tk, heads=heads),
        grid_spec=pltpu.PrefetchScalarGridSpec(
            num_scalar_prefetch=1,
            grid=(nb, NSA_HEADS // heads, steps.shape[1]),
            in_specs=in_specs,
            out_specs=pl.BlockSpec((tq, qw), lambda b, h, s, tab: (b * ntq + tab[0, s], h)),
            scratch_shapes=scratch,
        ),
        out_shape=jax.ShapeDtypeStruct((nb * t, NSA_HEADS * HEAD_DIM), F32),
        compiler_params=_params("arbitrary", "arbitrary", "arbitrary"),
        name="flash_" + mode,
    )(steps, q, k, v, *extra)


def _combine_kernel(oc_ref, os_ref, ow_ref, of_ref, g_ref, o_ref):
    gates = g_ref[...]
    nw = NSA_HEADS * HEAD_DIM
    for h in range(NSA_HEADS):
        sl = slice(h * HEAD_DIM, (h + 1) * HEAD_DIM)
        o = (gates[:, 3 * h:3 * h + 1] * oc_ref[:, sl] + gates[:, 3 * h + 1:3 * h + 2] * os_ref[:, sl]
             + gates[:, 3 * h + 2:3 * h + 3] * ow_ref[:, sl])
        o_ref[:, sl] = o.astype(o_ref.dtype)
    o_ref[:, nw:] = of_ref[...].astype(o_ref.dtype)


def combine_heads(o_c, o_s, o_w, o_f, gates):
    m, nw = o_c.shape
    tm = _pick(m, (512, 256, 128))
    spec = pl.BlockSpec((tm, nw), lambda i: (i, 0))
    return pl.pallas_call(
        _combine_kernel,
        grid=(m // tm,),
        in_specs=[spec, spec, spec, spec, pl.BlockSpec((tm, 128), lambda i: (i, 0))],
        out_specs=pl.BlockSpec((tm, 2 * nw), lambda i: (i, 0)),
        out_shape=jax.ShapeDtypeStruct((m, 2 * nw), MXU_DTYPE),
        compiler_params=_params("arbitrary"),
        name="combine_heads",
    )(o_c, o_s, o_w, o_f, gates)


def _gather_pages_kernel(pt_ref, *refs, page):
    *x_refs, o_ref = refs
    for j, x_ref in enumerate(x_refs):
        for c in range(2 * NSA_KV):
            o_ref[c, j * page:(j + 1) * page, :] = x_ref[:, c // NSA_KV, c % NSA_KV, :]


def gather_cmp_pages(pool, e, page_table):
    nb, n_pages = page_table.shape
    page = pool.shape[2]
    per_step = _pick(n_pages, (4, 2, 1))

    def page_spec(j):
        return pl.BlockSpec((None, None, page, 2, NSA_KV, HEAD_DIM),
                            lambda b, p, pt: (e, pt[b, p * per_step + j], 0, 0, 0, 0))

    return pl.pallas_call(
        functools.partial(_gather_pages_kernel, page=page),
        grid_spec=pltpu.PrefetchScalarGridSpec(
            num_scalar_prefetch=1,
            grid=(nb, n_pages // per_step),
            in_specs=[page_spec(j) for j in range(per_step)],
            out_specs=pl.BlockSpec((None, 2 * NSA_KV, per_step * page, HEAD_DIM), lambda b, p, pt: (b, 0, p, 0)),
        ),
        out_shape=jax.ShapeDtypeStruct((nb, 2 * NSA_KV, n_pages * page, HEAD_DIM), F32),
        compiler_params=_params("arbitrary", "arbitrary"),
        name="gather_cmp_pages",
    )(page_table, *([pool] * per_step))


def _sample_cmp_win_kernel(qraw_ref, qrot_ref, kc_ref, vc_ref, a_ref, kw_ref, vw_ref,
                           oc_ref, ow_ref, idx_ref, *, n_c, n_sel, n_top, q_pos):
    scale = HEAD_DIM ** -0.5
    ncp = kc_ref.shape[0]
    rows = qraw_ref.shape[0]
    cend = lax.broadcasted_iota(jnp.int32, (1, ncp), 1) * CMP_STRIDE + CMP_LEN
    cidx = lax.broadcasted_iota(jnp.int32, (1, ncp), 1)
    cmask = (cend <= q_pos + 1) & (cidx < n_c)
    lg = lax.dot_general(_mx(qraw_ref[...]), _mx(kc_ref[...]), NT_DIMS, preferred_element_type=F32) * scale
    p = _masked_softmax(lg, cmask)
    oc_ref[...] = jnp.dot(_mx(p), _mx(vc_ref[...]), preferred_element_type=F32)
    head = lax.broadcasted_iota(jnp.int32, p.shape, 0)
    imp = jnp.sum(jnp.where(head < NSA_GROUP, p, 0.0), axis=0, keepdims=True)
    imp = jnp.broadcast_to(imp, (rows, ncp))
    s_sel = jnp.dot(imp, a_ref[...], preferred_element_type=F32, precision=lax.Precision.HIGHEST)
    nsp = s_sel.shape[1]
    lane = lax.broadcasted_iota(jnp.int32, (rows, nsp), 1).astype(F32)
    cur = q_pos // SEL_LEN
    forced = (lane == 0.0) | (lane == float(cur)) | (lane == float(cur - 1))
    v = jnp.where(lane <= float(cur), jnp.where(forced, jnp.inf, s_sel), -jnp.inf)
    out_lane = lax.broadcasted_iota(jnp.int32, (rows, 128), 1)
    picked = jnp.full((rows, 128), -1.0, F32)
    for r in range(n_top):
        mx = jnp.max(v, axis=1, keepdims=True)
        ix = jnp.min(jnp.where(v == mx, lane, float(nsp)), axis=1, keepdims=True)
        ix = jnp.where(mx > -jnp.inf, ix, -1.0)
        picked = jnp.where(out_lane == r, ix, picked)
        v = jnp.where(lane == ix, -jnp.inf, v)
    idx_ref[...] = picked.astype(jnp.int32)
    lw = lax.dot_general(_mx(qrot_ref[...]), _mx(kw_ref[...]), NT_DIMS, preferred_element_type=F32) * scale
    pw = _masked_softmax(lw, jnp.full(lw.shape, True))
    ow_ref[...] = jnp.dot(_mx(pw), _mx(vw_ref[...]), preferred_element_type=F32)


def sample_cmp_win(qraw, qrot, kvc, a_mat, win, n_c, n_sel, q_pos):
    nb, _, rows, _ = qraw.shape
    ncp = kvc.shape[2]
    nsp = a_mat.shape[1]
    wlen = win.shape[1]
    qspec = pl.BlockSpec((None, None, rows, HEAD_DIM), lambda b, g: (b, g, 0, 0))
    return pl.pallas_call(
        functools.partial(_sample_cmp_win_kernel, n_c=n_c, n_sel=n_sel, n_top=min(SEL_TOP, n_sel), q_pos=q_pos),
        grid=(nb, NSA_KV),
        in_specs=[
            qspec, qspec,
            pl.BlockSpec((None, None, ncp, HEAD_DIM), lambda b, g: (b, g, 0, 0)),
            pl.BlockSpec((None, None, ncp, HEAD_DIM), lambda b, g: (b, NSA_KV + g, 0, 0)),
            pl.BlockSpec((ncp, nsp), lambda b, g: (0, 0)),
            pl.BlockSpec((None, wlen, HEAD_DIM), lambda b, g: (b, 0, g)),
            pl.BlockSpec((None, wlen, HEAD_DIM), lambda b, g: (b, 0, NSA_KV + g)),
        ],
        out_specs=[qspec, qspec, pl.BlockSpec((None, None, rows, 128), lambda b, g: (b, g, 0, 0))],
        out_shape=[
            jax.ShapeDtypeStruct(qraw.shape, F32),
            jax.ShapeDtypeStruct(qraw.shape, F32),
            jax.ShapeDtypeStruct((nb, NSA_KV, rows, 128), jnp.int32),
        ],
        compiler_params=_params("arbitrary", "arbitrary"),
        name="sample_cmp_win",
    )(qraw, qrot, kvc, kvc, a_mat, win, win)


def _softmax_update(m_s, l_s, acc, s, v):
    m_old = m_s[...]
    m_new = jnp.maximum(m_old, jnp.max(s, axis=-1, keepdims=True))
    p = jnp.exp(s - m_new)
    alpha = jnp.exp(m_old - m_new)
    l_s[...] = alpha * l_s[...] + jnp.sum(p, axis=-1, keepdims=True)
    acc[...] = alpha * acc[...] + jnp.dot(_mx(p), _mx(v), preferred_element_type=F32)
    m_s[...] = m_new


def _sample_slc_kernel(pt_ref, idx_ref, q_ref, kv_ref, kn_ref, vn_ref, o_ref, m_s, l_s, acc,
                       *, n_top, n_past_blk):
    b, g, r = pl.program_id(0), pl.program_id(1), pl.program_id(2)
    row = b * NSA_KV + g
    scale = HEAD_DIM ** -0.5

    @pl.when(r == 0)
    def _():
        m_s[...] = jnp.full_like(m_s, NEG_BIG)
        l_s[...] = jnp.zeros_like(l_s)
        acc[...] = jnp.zeros_like(acc)

    blk = idx_ref[row, r]
    in_past = (blk >= 0) & (blk < n_past_blk)

    for gg in range(NSA_KV):
        @pl.when(in_past & (g == gg))
        def _(gg=gg):
            s = lax.dot_general(_mx(q_ref[...]), _mx(kv_ref[:, 0, gg, :]), NT_DIMS,
                                preferred_element_type=F32) * scale
            _softmax_update(m_s, l_s, acc, s, kv_ref[:, 1, gg, :])

    @pl.when(r == n_top - 1)
    def _():
        has_new = idx_ref[row, 0] == n_past_blk
        for t in range(1, n_top):
            has_new = has_new | (idx_ref[row, t] == n_past_blk)
        q = q_ref[...]
        s_new = jnp.sum(q * kn_ref[...], axis=-1, keepdims=True) * scale
        s_new = jnp.where(has_new, s_new, NEG_BIG)
        m_old = m_s[...]
        m_new = jnp.maximum(m_old, s_new)
        p = jnp.where(has_new, jnp.exp(s_new - m_new), 0.0)
        alpha = jnp.exp(m_old - m_new)
        l = alpha * l_s[...] + p
        o = alpha * acc[...] + p * vn_ref[...]
        o_ref[...] = o / jnp.where(l > 0, l, 1.0)


def sample_slc(qrot, pool, e, page_table, idx, slc_new, n_past_blk):
    nb, _, rows, _ = qrot.shape
    n_top = idx.shape[1]
    page = pool.shape[2]
    per_page = page // SEL_LEN

    def kv_map(b, g, r, pt, ix):
        blk = jnp.clip(ix[b * NSA_KV + g, r], 0, n_past_blk - 1)
        return (e, pt[b, blk // per_page], blk % per_page, 0, 0, 0)

    return pl.pallas_call(
        functools.partial(_sample_slc_kernel, n_top=n_top, n_past_blk=n_past_blk),
        grid_spec=pltpu.PrefetchScalarGridSpec(
            num_scalar_prefetch=2,
            grid=(nb, NSA_KV, n_top),
            in_specs=[
                pl.BlockSpec((None, None, rows, HEAD_DIM), lambda b, g, r, pt, ix: (b, g, 0, 0)),
                pl.BlockSpec((None, None, SEL_LEN, 2, NSA_KV, HEAD_DIM), kv_map),
                pl.BlockSpec((None, 1, HEAD_DIM), lambda b, g, r, pt, ix: (b, 0, g)),
                pl.BlockSpec((None, 1, HEAD_DIM), lambda b, g, r, pt, ix: (b, 0, NSA_KV + g)),
            ],
            out_specs=pl.BlockSpec((None, None, rows, HEAD_DIM), lambda b, g, r, pt, ix: (b, g, 0, 0)),
            scratch_shapes=[pltpu.VMEM((rows, 1), F32), pltpu.VMEM((rows, 1), F32), pltpu.VMEM((rows, HEAD_DIM), F32)],
        ),
        out_shape=jax.ShapeDtypeStruct(qrot.shape, F32),
        compiler_params=_params("arbitrary", "arbitrary", "arbitrary"),
        name="sample_slc",
    )(page_table, idx, qrot, pool, slc_new, slc_new)


def _sample_fox_kernel(pt_ref, q_ref, *refs, n_steps, per_step):
    page_refs = refs[:3 * per_step]
    tri_ref, kn_ref, vn_ref, lfn_ref, o_ref, m_s, l_s, acc, crun = refs[3 * per_step:]
    step = pl.program_id(1)
    scale = HEAD_DIM ** -0.5
    nh = FOX_HEADS
    page = page_refs[0].shape[0]
    cols = page * nh

    @pl.when(step == 0)
    def _():
        m_s[...] = jnp.full_like(m_s, NEG_BIG)
        l_s[...] = jnp.zeros_like(l_s)
        acc[...] = jnp.zeros_like(acc)
        crun[...] = jnp.zeros_like(crun)

    qb = _mx(q_ref[...])
    tri = _mx(tri_ref[...])
    lane = lax.broadcasted_iota(jnp.int32, (nh, HEAD_DIM), 1) // nh
    own = (lax.broadcasted_iota(jnp.int32, (nh, cols), 1) % nh) == lax.broadcasted_iota(jnp.int32, (nh, cols), 0)
    c_off = crun[...]
    logits, values = [], []
    for j in range(per_step):
        k_ref, v_ref, lf_ref = page_refs[3 * j:3 * j + 3]
        lf = lf_ref[...]
        hi = _mx(lf).astype(F32)
        mid = _mx(lf - hi).astype(F32)
        lo = lf - hi - mid
        cs = jnp.dot(_mx(jnp.concatenate([hi, mid, lo], axis=0)), tri, preferred_element_type=F32)
        ck = cs[:nh] + cs[nh:2 * nh] + cs[2 * nh:] + c_off
        c_off = ck[:, -1:]
        ck_x = jnp.concatenate(
            [jnp.take_along_axis(ck, lane + i * (HEAD_DIM // nh), axis=1) for i in range(cols // HEAD_DIM)], axis=1)
        k2 = k_ref[...].reshape(cols, HEAD_DIM)
        s = lax.dot_general(qb, _mx(k2), NT_DIMS, preferred_element_type=F32) * scale - ck_x
        logits.append(jnp.where(own, s, NEG_BIG))
        values.append(_mx(v_ref[...].reshape(cols, HEAD_DIM)))
    crun[...] = c_off
    m_old = m_s[...]
    m_new = m_old
    for s in logits:
        m_new = jnp.maximum(m_new, jnp.max(s, axis=-1, keepdims=True))
    alpha = jnp.exp(m_old - m_new)
    l_new = alpha * l_s[...]
    o_new = alpha * acc[...]
    for s, v2 in zip(logits, values):
        p = jnp.where(own, jnp.exp(s - m_new), 0.0)
        l_new = l_new + jnp.sum(p, axis=-1, keepdims=True)
        o_new = o_new + jnp.dot(_mx(p), v2, preferred_element_type=F32)
    l_s[...] = l_new
    acc[...] = o_new
    m_s[...] = m_new

    @pl.when(step == n_steps - 1)
    def _():
        cq = c_off + lfn_ref[...]
        s_new = jnp.sum(q_ref[...] * kn_ref[...], axis=-1, keepdims=True) * scale - cq
        m_fin = jnp.maximum(m_new, s_new)
        p_new = jnp.exp(s_new - m_fin)
        a_fin = jnp.exp(m_new - m_fin)
        o_ref[...] = (a_fin * o_new + p_new * vn_ref[...]) / (a_fin * l_new + p_new)


def sample_fox(fq, pool, logf_t, e, page_table, k_new, v_new, logf_new):
    nb, n_pages = page_table.shape
    page = pool.shape[2]
    per_step = _pick(n_pages, (4, 2, 1))
    tri = jnp.triu(jnp.ones((page, page), F32))

    def page_specs(j):
        def kv(which):
            return pl.BlockSpec((None, None, page, None, FOX_HEADS, HEAD_DIM),
                                lambda b, p, pt: (e, pt[b, p * per_step + j], 0, which, 0, 0))
        return [kv(0), kv(1),
                pl.BlockSpec((None, None, FOX_HEADS, page), lambda b, p, pt: (e, pt[b, p * per_step + j], 0, 0))]

    page_ops = []
    for j in range(per_step):
        page_ops += [pool, pool, logf_t]
    return pl.pallas_call(
        functools.partial(_sample_fox_kernel, n_steps=n_pages // per_step, per_step=per_step),
        grid_spec=pltpu.PrefetchScalarGridSpec(
            num_scalar_prefetch=1,
            grid=(nb, n_pages // per_step),
            in_specs=[
                pl.BlockSpec((None, FOX_HEADS, HEAD_DIM), lambda b, p, pt: (b, 0, 0)),
                *[spec for j in range(per_step) for spec in page_specs(j)],
                pl.BlockSpec((page, page), lambda b, p, pt: (0, 0)),
                pl.BlockSpec((None, FOX_HEADS, HEAD_DIM), lambda b, p, pt: (b, 0, 0)),
                pl.BlockSpec((None, FOX_HEADS, HEAD_DIM), lambda b, p, pt: (b, 0, 0)),
                pl.BlockSpec((None, FOX_HEADS, 1), lambda b, p, pt: (b, 0, 0)),
            ],
            out_specs=pl.BlockSpec((None, FOX_HEADS, HEAD_DIM), lambda b, p, pt: (b, 0, 0)),
            scratch_shapes=[
                pltpu.VMEM((FOX_HEADS, 1), F32), pltpu.VMEM((FOX_HEADS, 1), F32),
                pltpu.VMEM((FOX_HEADS, HEAD_DIM), F32), pltpu.VMEM((FOX_HEADS, 1), F32),
            ],
        ),
        out_shape=jax.ShapeDtypeStruct((nb, FOX_HEADS, HEAD_DIM), F32),
        compiler_params=_params("arbitrary", "arbitrary"),
        name="sample_fox",
    )(page_table, fq, *page_ops, tri, k_new, v_new, logf_new)


def _ret_tables(chunk):
    lg = jnp.log1p(-(2.0 ** (-5.0 - jnp.arange(RET_HEADS, dtype=F32))))
    i = jnp.arange(chunk, dtype=F32)
    diff = i[:, None] - i[None, :]
    dec = jnp.where(diff >= 0, jnp.exp(jnp.maximum(diff, 0.0)[None] * lg[:, None, None]), 0.0)
    xi = jnp.exp((i[None, :] + 1.0) * lg[:, None])
    zeta = jnp.exp((chunk - 1.0 - i)[None, :] * lg[:, None])
    g_c = jnp.exp(chunk * lg)
    return dec, xi, zeta, g_c


def _ret_rope_tables(pos):
    half = RET_DK // 2
    inv = 1.0 / (RET_THETA ** (jnp.arange(half, dtype=F32) / half))
    ang = pos.astype(F32)[:, None] * inv[None, :]
    return jnp.cos(ang), jnp.sin(ang)


def _group_norm_gate(o, gate, gn):
    mu = jnp.mean(o, axis=-1, keepdims=True)
    d = o - mu
    var = jnp.mean(d * d, axis=-1, keepdims=True)
    y = d * lax.rsqrt(var + GN_EPS) * gn
    return gate * jax.nn.sigmoid(gate) * y


def _retention_kernel(q_ref, k_ref, v_ref, gate_ref, cos_ref, sin_ref, dec_ref, coef_ref, gn_ref,
                      y_ref, s_ref, state, *, n_chunks, heads):
    c = pl.program_id(2)
    half = RET_DK // 2

    @pl.when(c == 0)
    def _():
        state[...] = jnp.zeros_like(state)

    cos, sin = cos_ref[...], sin_ref[...]

    def rot(x_ref, hh):
        x1 = x_ref[:, hh * RET_DK:hh * RET_DK + half]
        x2 = x_ref[:, hh * RET_DK + half:(hh + 1) * RET_DK]
        return jnp.concatenate([x1 * cos - x2 * sin, x2 * cos + x1 * sin], axis=-1)

    for hh in range(heads):
        vcols = slice(hh * RET_DV, (hh + 1) * RET_DV)
        coef = coef_ref[hh]
        xi, zeta, g_c = coef[:, 0:1], coef[:, 1:2], coef[0:1, 2:3]
        q = _mx(rot(q_ref, hh))
        kf = rot(k_ref, hh) * (RET_DK ** -0.5)
        v = _mx(v_ref[:, vcols])
        s_old = state[hh]
        a = lax.dot_general(q, _mx(kf), NT_DIMS, preferred_element_type=F32) * dec_ref[hh]
        o = (jnp.dot(_mx(a), v, preferred_element_type=F32)
             + jnp.dot(q, _mx(s_old), preferred_element_type=F32) * xi)
        state[hh] = s_old * g_c + lax.dot_general(_mx(kf * zeta), v, TN_DIMS, preferred_element_type=F32)
        y_ref[:, vcols] = _group_norm_gate(o, gate_ref[:, vcols], gn_ref[:, vcols]).astype(y_ref.dtype)

    @pl.when(c == n_chunks - 1)
    def _():
        s_ref[...] = state[...]


def retention_prompt(h, nb, t, gn_g, o_idx):
    ch = RET_CHUNK
    n_chunks = t // ch
    dec, xi, zeta, g_c = _ret_tables(ch)
    coef = jnp.stack([xi, zeta, jnp.broadcast_to(g_c[:, None], xi.shape)], axis=-1)
    coef = jnp.pad(coef, ((0, 0), (0, 0), (0, 128 - 3)))
    cos, sin = _ret_rope_tables(jnp.arange(t))
    hs = RET_HEADS_PER_STEP
    groups = RET_HEADS // hs
    kw, vw = hs * RET_DK, hs * RET_DV
    kb, vb, gb = O_RK // kw, O_RV // vw, O_RG // vw
    return pl.pallas_call(
        functools.partial(_retention_kernel, n_chunks=n_chunks, heads=hs),
        grid=(nb, groups, n_chunks),
        in_specs=[
            pl.BlockSpec((ch, kw), lambda b, hg, c: (b * n_chunks + c, hg)),
            pl.BlockSpec((ch, kw), lambda b, hg, c: (b * n_chunks + c, kb + hg)),
            pl.BlockSpec((ch, vw), lambda b, hg, c: (b * n_chunks + c, vb + hg)),
            pl.BlockSpec((ch, vw), lambda b, hg, c: (b * n_chunks + c, gb + hg)),
            pl.BlockSpec((ch, RET_DK // 2), lambda b, hg, c: (c, 0)),
            pl.BlockSpec((ch, RET_DK // 2), lambda b, hg, c: (c, 0)),
            pl.BlockSpec((hs, ch, ch), lambda b, hg, c: (hg, 0, 0)),
            pl.BlockSpec((hs, ch, 128), lambda b, hg, c: (hg, 0, 0)),
            pl.BlockSpec((None, 1, vw), lambda b, hg, c: (o_idx, 0, hg)),
        ],
        out_specs=[
            pl.BlockSpec((ch, vw), lambda b, hg, c: (b * n_chunks + c, hg)),
            pl.BlockSpec((None, hs, RET_DK, RET_DV), lambda b, hg, c: (b, hg, 0, 0)),
        ],
        out_shape=[
            jax.ShapeDtypeStruct((nb * t, RET_HEADS * RET_DV), MXU_DTYPE),
            jax.ShapeDtypeStruct((nb, RET_HEADS, RET_DK, RET_DV), F32),
        ],
        scratch_shapes=[pltpu.VMEM((hs, RET_DK, RET_DV), F32)],
        compiler_params=_params("arbitrary", "arbitrary", "arbitrary"),
        name="retention_prompt",
    )(h, h, h, h, cos, sin, dec, coef, gn_g[:, None, :])


def _retention_step_kernel(q_ref, k_ref, v_ref, gate_ref, cos_ref, sin_ref, coef_ref, gn_ref, s0_ref,
                           y_ref, s_ref):
    half = RET_DK // 2
    hh = pl.program_id(1)
    cos, sin = cos_ref[...], sin_ref[...]

    def rot(x_ref):
        x1, x2 = x_ref[:half, :], x_ref[half:, :]
        return jnp.concatenate([x1 * cos - x2 * sin, x2 * cos + x1 * sin], axis=0)

    head = lax.broadcasted_iota(jnp.int32, coef_ref.shape, 0)
    coef = jnp.sum(jnp.where(head == hh, coef_ref[...], 0.0), axis=0, keepdims=True)
    dec, xi, zeta, g_c = coef[:, 0:1], coef[:, 1:2], coef[:, 2:3], coef[:, 3:4]
    q = rot(q_ref)
    k = rot(k_ref) * (RET_DK ** -0.5)
    v = v_ref[...]
    s_old = s0_ref[...]
    a = jnp.sum(q * k, axis=0, keepdims=True) * dec
    o = a * v + jnp.sum(q * s_old, axis=0, keepdims=True) * xi
    s_ref[...] = s_old * g_c + (k * zeta) * v
    y_ref[...] = _group_norm_gate(o, gate_ref[...], gn_ref[...])


def retention_step(q_col, k_col, v_row, gate_row, pos, s0, o_idx, gn_g):
    nb = q_col.shape[0]
    dec, xi, zeta, g_c = _ret_tables(1)
    coef = jnp.stack([dec[:, 0, 0], xi[:, 0], zeta[:, 0], g_c], axis=-1)
    coef = jnp.pad(coef, ((0, 0), (0, 128 - 4)))
    cos, sin = _ret_rope_tables(pos)
    col = pl.BlockSpec((None, None, RET_DK, 1), lambda b, hh: (b, hh, 0, 0))
    row = pl.BlockSpec((None, None, 1, RET_DV), lambda b, hh: (b, hh, 0, 0))
    tab = pl.BlockSpec((RET_DK // 2, 1), lambda b, hh: (0, 0))
    return pl.pallas_call(
        _retention_step_kernel,
        grid=(nb, RET_HEADS),
        in_specs=[
            col, col, row, row, tab, tab,
            pl.BlockSpec((RET_HEADS, 128), lambda b, hh: (0, 0)),
            pl.BlockSpec((None, 1, RET_DV), lambda b, hh: (o_idx, 0, hh)),
            pl.BlockSpec((None, None, None, RET_DK, RET_DV), lambda b, hh: (o_idx, b, hh, 0, 0)),
        ],
        out_specs=[row, pl.BlockSpec((None, None, RET_DK, RET_DV), lambda b, hh: (b, hh, 0, 0))],
        out_shape=[
            jax.ShapeDtypeStruct((nb, RET_HEADS, 1, RET_DV), F32),
            jax.ShapeDtypeStruct((nb, RET_HEADS, RET_DK, RET_DV), F32),
        ],
        compiler_params=_params("arbitrary", "arbitrary"),
        name="retention_step",
    )(q_col, k_col, v_row, gate_row, cos.reshape(-1, 1), sin.reshape(-1, 1), coef, gn_g[:, None, :], s0)


def _pad_rows(x, rows):
    return jnp.pad(x, ((0, rows - x.shape[0]),) + ((0, 0),) * (x.ndim - 1))


def _repack_even(w):
    small = jnp.concatenate([w[:, O_NG:O_FQ], w[:, O_FF:E_EVEN]], axis=1)
    small = jnp.pad(small, ((0, 0), (0, 128 - small.shape[1])))
    return jnp.concatenate([w[:, O_NQ:O_NG], w[:, O_FQ:O_FF], small], axis=1)


def _even_layer(e, xb_p, xb_s, nb, t, ns, past, caches, page_table, wts):
    cache_cmp, cache_slc, cache_win, cache_fox, logf_pool_t = caches
    w_in, fox_f_bias, cmp_pos, cmp_w1, cmp_w2 = wts
    w_rep = _repack_even(w_in[e])
    fb_row = jnp.zeros((1, 128), F32).at[0, N_GATE:N_GATE + FOX_HEADS].set(fox_f_bias[e])
    hq = P_FQ // HEAD_DIM
    kvw = 2 * NSA_KV * HEAD_DIM
    nw = NSA_HEADS * HEAD_DIM

    h_p, h_s = matmul(xb_p, xb_s, w_rep, (), (1152, 640, 384, 128))
    tm_post = _pick(nb * t, (512, 256, 128))
    q_rot, slc_p, win_p, gates_p, logf_p = even_post(
        h_p, _rope_tables(jnp.arange(t)), fb_row, t // tm_post)
    c_p = cumsum_rows(logf_p, nb, t)
    c_heads = jnp.swapaxes(c_p[:, N_GATE:N_GATE + FOX_HEADS].reshape(nb, t, FOX_HEADS), 1, 2)
    n16 = t // CMP_STRIDE
    kvc_p = nsa_compress(
        h_p, pl.BlockSpec((t, HEAD_DIM), lambda b, c: (b, P_NKV // HEAD_DIM + c)), nb, t, e, cmp_w1, cmp_pos, cmp_w2)
    a_p = _sel_matrix(n16, n16 - 1, t // SEL_LEN, 128)
    o_c, sel = cmp_select(h_p, kvc_p, a_p, nb, t)
    o_s = flash_attention("slc", nb, t, q_rot, lambda g: g, slc_p, lambda g: g, slc_p, lambda g: NSA_KV + g, (sel,))
    o_w = flash_attention("win", nb, t, q_rot, lambda g: g, win_p, lambda g: g, win_p, lambda g: NSA_KV + g)
    fq0 = P_FQ // (NSA_GROUP * HEAD_DIM)
    fgroups = FOX_HEADS // NSA_GROUP
    o_f = flash_attention("fox", nb, t, h_p, lambda g: fq0 + g, h_p, lambda g: fq0 + fgroups + g,
                          h_p, lambda g: fq0 + 2 * fgroups + g, (c_heads[..., None], c_heads[:, :, None, :]))
    xcat_p = combine_heads(o_c, o_s, o_w, o_f, gates_p)
    st_p = (
        h_p[:, P_NKV:P_NKV + kvw].reshape(nb, t, 2, NSA_KV, HEAD_DIM),
        slc_p.reshape(nb, t, 2, NSA_KV, HEAD_DIM),
        win_p.reshape(nb, t, 2, NSA_KV, HEAD_DIM)[:, t - min(WINDOW, t):],
        h_p[:, P_FQ + nw:P_FQ + 3 * nw].reshape(nb, t, 2, FOX_HEADS, HEAD_DIM),
        logf_p[:, N_GATE:N_GATE + FOX_HEADS].reshape(nb, t, FOX_HEADS),
    )

    rows = xb_s.shape[0]
    q_rot_s, slc_s, win_s, gates_s, logf_s = even_post(
        h_s, _rope_tables(jnp.full((rows,), past)), fb_row, 1)
    new_win = jnp.concatenate(
        [cache_win[e].reshape(ns, -1, kvw), win_s[:ns, None, :]], axis=1)[:, 1:]
    cmp_rows = gather_cmp_pages(cache_cmp, e, page_table)
    l_tot = past + 1
    n16_s = l_tot // CMP_STRIDE
    assert n16_s * CMP_STRIDE == past
    kvc_s = nsa_compress(
        cmp_rows, pl.BlockSpec((None, None, past, HEAD_DIM), lambda b, c: (b, c, 0, 0)), ns, past, e,
        cmp_w1, cmp_pos, cmp_w2)
    n_sel_s = -(-l_tot // SEL_LEN)
    nsp = -(-n_sel_s // 128) * 128
    a_s = _sel_matrix(n16_s, n16_s - 1, n_sel_s, nsp)

    def q_rows(x):
        x = x[:ns].reshape(ns, NSA_KV, NSA_GROUP, HEAD_DIM)
        return jnp.pad(x, ((0, 0), (0, 0), (0, 8 - NSA_GROUP), (0, 0)))

    qraw_g, qrot_g = q_rows(h_s[:, :nw]), q_rows(q_rot_s)
    o_c_s, o_w_s, idx = sample_cmp_win(qraw_g, qrot_g, kvc_s, a_s, new_win, n16_s - 1, n_sel_s, past)
    idx2 = idx[:, :, 0, :min(SEL_TOP, n_sel_s)].reshape(ns * NSA_KV, -1)
    o_s_s = sample_slc(qrot_g, cache_slc, e, page_table, idx2, slc_s[:ns, None, :], past // SEL_LEN)
    fq = h_s[:ns, P_FQ:P_FQ + nw].reshape(ns, FOX_HEADS, HEAD_DIM)
    fk = h_s[:ns, P_FQ + nw:P_FQ + 2 * nw]
    fv = h_s[:ns, P_FQ + 2 * nw:P_FQ + 3 * nw]
    logf_new = logf_s[:ns, N_GATE:N_GATE + FOX_HEADS]
    o_f_s = sample_fox(fq, cache_fox, logf_pool_t, e, page_table, fk.reshape(ns, FOX_HEADS, HEAD_DIM),
                       fv.reshape(ns, FOX_HEADS, HEAD_DIM), logf_new[:, :, None])

    def heads_flat(x):
        return _pad_rows(x[:, :, :NSA_GROUP].reshape(ns, nw), rows)

    xcat_s = combine_heads(heads_flat(o_c_s), heads_flat(o_s_s), heads_flat(o_w_s),
                           _pad_rows(o_f_s.reshape(ns, nw), rows), gates_s)
    st_s = (
        h_s[:ns, P_NKV:P_NKV + kvw].reshape(ns, 1, 2, NSA_KV, HEAD_DIM),
        slc_s[:ns].reshape(ns, 1, 2, NSA_KV, HEAD_DIM),
        new_win.reshape(ns, -1, 2, NSA_KV, HEAD_DIM),
        h_s[:ns, P_FQ + nw:P_FQ + 3 * nw].reshape(ns, 1, 2, FOX_HEADS, HEAD_DIM),
        logf_new.reshape(ns, 1, FOX_HEADS),
    )
    return xcat_p, xcat_s, st_p, st_s


def _odd_layer(o, xb_p, xb_s, nb, t, ns, past, state_ret, w_in_odd, ret_gn_g):
    h_p, h_s = matmul(xb_p, xb_s, w_in_odd, (o,), (1024, 512, 256, 128))
    y_p, s_p = retention_prompt(h_p, nb, t, ret_gn_g, o)
    rows = xb_s.shape[0]
    h_s = h_s[:ns]
    q_col = h_s[:, :O_RK].reshape(ns, RET_HEADS, RET_DK, 1)
    k_col = h_s[:, O_RK:O_RV].reshape(ns, RET_HEADS, RET_DK, 1)
    v_row = h_s[:, O_RV:O_RG].reshape(ns, RET_HEADS, 1, RET_DV)
    g_row = h_s[:, O_RG:E_ODD].reshape(ns, RET_HEADS, 1, RET_DV)
    y_s, s_s = retention_step(q_col, k_col, v_row, g_row, jnp.full((1,), past), state_ret, o, ret_gn_g)
    y_s = _pad_rows(y_s.reshape(ns, RET_HEADS * RET_DV), rows).astype(MXU_DTYPE)
    return y_p, y_s, s_p, s_s


def kernel(x_prompt, x_sample, cache_nsa_cmp, cache_nsa_slc, cache_nsa_win, cache_fox_kv, cache_fox_logf,
           state_ret, page_table, w_ffn_in, w_ffn_out, ln_g, ln_b, w_in_even, w_out_even, fox_f_bias,
           nsa_cmp_pos, nsa_cmp_w1, nsa_cmp_w2, w_in_odd, ret_gn_g, w_out_odd):
    nb, t, d = x_prompt.shape
    ns, ts, _ = x_sample.shape
    depth = w_ffn_in.shape[0]
    n_pages = page_table.shape[1]
    page = cache_nsa_cmp.shape[2]
    past = n_pages * page
    assert ts == 1 and past >= WINDOW and past % SEL_LEN == 0 and t % RET_CHUNK == 0
    alpha = (2.0 * depth) ** 0.25
    rows = max(16, -(-ns // 16) * 16)

    xp = x_prompt.reshape(nb * t, d)
    xs = _pad_rows(x_sample.reshape(ns * ts, d), rows)
    xp_b, xs_b = xp.astype(MXU_DTYPE), xs.astype(MXU_DTYPE)

    caches = (cache_nsa_cmp, cache_nsa_slc, cache_nsa_win, cache_fox_kv, jnp.swapaxes(cache_fox_logf, 2, 3))
    w_ffn_out, w_out_even, w_out_odd = _mx(w_ffn_out), _mx(w_out_even), _mx(w_out_odd)

    def ffn(xp, xp_b, xs, xs_b, l, s):
        hid_p, hid_s = swiglu_hidden(xp_b, xs_b, w_ffn_in, l, s)
        return (matmul_postnorm(hid_p, w_ffn_out, (l, s), xp, ln_g, ln_b, (l, 2 * s), alpha, 0.5)
                + matmul_postnorm(hid_s, w_ffn_out, (l, s), xs, ln_g, ln_b, (l, 2 * s), alpha, 0.5))

    new_p = [[] for _ in range(6)]
    new_s = [[] for _ in range(6)]
    for l in range(depth):
        xp, xp_b, xs, xs_b = ffn(xp, xp_b, xs, xs_b, l, 0)
        if l % 2 == 0:
            e = l // 2
            mp, ms, st_p, st_s = _even_layer(
                e, xp_b, xs_b, nb, t, ns, past, caches, page_table,
                (w_in_even, fox_f_bias, nsa_cmp_pos, nsa_cmp_w1, nsa_cmp_w2))
            w_out, widx = w_out_even, (e,)
            slots = (0, 1, 2, 3, 4)
        else:
            o = l // 2
            mp, ms, sp, ss = _odd_layer(o, xp_b, xs_b, nb, t, ns, past, state_ret, w_in_odd, ret_gn_g)
            st_p, st_s = (sp,), (ss,)
            w_out, widx = w_out_odd, (o,)
            slots = (5,)
        for i, a_p, a_s in zip(slots, st_p, st_s):
            new_p[i].append(a_p)
            new_s[i].append(a_s)
        xp, xp_b = matmul_postnorm(mp, w_out, widx, xp, ln_g, ln_b, (l, 1), alpha, 1.0)
        xs, xs_b = matmul_postnorm(ms, w_out, widx, xs, ln_g, ln_b, (l, 1), alpha, 1.0)
        xp, xp_b, xs, xs_b = ffn(xp, xp_b, xs, xs_b, l, 1)

    outs = [xp.reshape(nb, t, d), xs[:ns].reshape(ns, ts, d)]
    for i in range(6):
        outs.append(jnp.stack(new_p[i]))
        outs.append(jnp.stack(new_s[i]))
    return tuple(outs)
```

```python
import functools
import math

import numpy as np
import jax
import jax.numpy as jnp
from jax import lax
from jax.experimental import pallas as pl
from jax.experimental.pallas import tpu as pltpu

F32 = jnp.float32
MXU_DTYPE = jnp.bfloat16

HEAD_DIM = 128
NSA_HEADS = 8
NSA_KV = 2
NSA_GROUP = NSA_HEADS // NSA_KV
CMP_LEN = 32
CMP_STRIDE = 16
CMP_HIDDEN = 256
SEL_LEN = 64
SEL_TOP = 16
WINDOW = 512
FOX_HEADS = 8
ROPE_THETA = 500000.0
ROPE_DIMS = HEAD_DIM // 4
RET_HEADS = 8
RET_DK = 256
RET_DV = 512
RET_CHUNK = 128
RET_THETA = 10000.0
RET_HEADS_PER_STEP = 4
LN_EPS = 1e-5
GN_EPS = 1e-6

O_NQ = 0
O_NKV = O_NQ + NSA_HEADS * HEAD_DIM
O_NG = O_NKV + 6 * NSA_KV * HEAD_DIM
O_FQ = O_NG + 3 * NSA_HEADS
O_FK = O_FQ + FOX_HEADS * HEAD_DIM
O_FV = O_FK + FOX_HEADS * HEAD_DIM
O_FF = O_FV + FOX_HEADS * HEAD_DIM
E_EVEN = O_FF + FOX_HEADS
P_NKV = NSA_HEADS * HEAD_DIM
P_FQ = P_NKV + 6 * NSA_KV * HEAD_DIM
P_SMALL = P_FQ + 3 * FOX_HEADS * HEAD_DIM
P_EVEN = P_SMALL + 128
N_GATE = 3 * NSA_HEADS
O_RK = RET_HEADS * RET_DK
O_RV = 2 * RET_HEADS * RET_DK
O_RG = O_RV + RET_HEADS * RET_DV
E_ODD = O_RG + RET_HEADS * RET_DV

V7X_VMEM_LIMIT = 56 * 1024 * 1024
LN_ROWS = 128
LOG2E = 1.4426950408889634
NEG_BIG = -1e30
NT_DIMS = (((1,), (1,)), ((), ()))
TN_DIMS = (((0,), (0,)), ((), ()))


def _params(*sem):
    return pltpu.CompilerParams(dimension_semantics=sem, vmem_limit_bytes=V7X_VMEM_LIMIT)


def _mx(x):
    return x.astype(MXU_DTYPE)


def _pick(n, prefs):
    for p in prefs:
        if n % p == 0:
            return p
    return n


def _swiglu_kernel(x_ref, xs_ref, wa_ref, wb_ref, o_ref, os_ref, wa_s, wb_s):
    def hidden(x):
        a = jnp.dot(x, wa_s[...], preferred_element_type=F32)
        b = jnp.dot(x, wb_s[...], preferred_element_type=F32)
        return (a * jax.nn.sigmoid(a) * b).astype(o_ref.dtype)

    @pl.when(pl.program_id(1) == 0)
    def _():
        wa_s[...] = _mx(wa_ref[...])
        wb_s[...] = _mx(wb_ref[...])
        os_ref[...] = hidden(xs_ref[...])

    o_ref[...] = hidden(x_ref[...])


def swiglu_hidden(xb, xsb, w_in, l, s):
    m, d = xb.shape
    ms = xsb.shape[0]
    f = w_in.shape[-1] // 2
    tm = _pick(m, (1024, 512, 256, 128))
    tn = _pick(f, (512, 256, 128))
    nj = f // tn
    return pl.pallas_call(
        _swiglu_kernel,
        grid=(nj, m // tm),
        in_specs=[
            pl.BlockSpec((tm, d), lambda j, i: (i, 0)),
            pl.BlockSpec((ms, d), lambda j, i: (0, 0)),
            pl.BlockSpec((None, None, d, tn), lambda j, i: (l, s, 0, j)),
            pl.BlockSpec((None, None, d, tn), lambda j, i: (l, s, 0, j + nj)),
        ],
        out_specs=[pl.BlockSpec((tm, tn), lambda j, i: (i, j)), pl.BlockSpec((ms, tn), lambda j, i: (0, j))],
        out_shape=[jax.ShapeDtypeStruct((m, f), MXU_DTYPE), jax.ShapeDtypeStruct((ms, f), MXU_DTYPE)],
        scratch_shapes=[pltpu.VMEM((d, tn), MXU_DTYPE), pltpu.VMEM((d, tn), MXU_DTYPE)],
        compiler_params=_params("arbitrary", "arbitrary"),
        name="swiglu_hidden",
    )(xb, xsb, w_in, w_in)


def _mm_ln_kernel(x_ref, w_ref, r_ref, g_ref, b_ref, o_ref, ob_ref, *, alpha, scale, nk):
    k = pl.program_id(1)
    def part():
        return jnp.dot(x_ref[...], _mx(w_ref[...]), preferred_element_type=F32)

    @pl.when(k == 0)
    def _():
        o_ref[...] = part()

    @pl.when(k > 0)
    def _():
        o_ref[...] += part()

    @pl.when(k == nk - 1)
    def _():
        rows = min(LN_ROWS, o_ref.shape[0])

        def norm_rows(c, carry):
            sl = pl.ds(pl.multiple_of(c * rows, rows), rows)
            z = alpha * r_ref[sl, :] + scale * o_ref[sl, :]
            mu = jnp.mean(z, axis=-1, keepdims=True)
            dz = z - mu
            var = jnp.mean(dz * dz, axis=-1, keepdims=True)
            y = dz * lax.rsqrt(var + LN_EPS) * g_ref[...] + b_ref[...]
            o_ref[sl, :] = y
            ob_ref[sl, :] = y.astype(ob_ref.dtype)
            return carry

        lax.fori_loop(0, o_ref.shape[0] // rows, norm_rows, 0)


def matmul_postnorm(xb, w, widx, res, g, b, gidx, alpha, scale):
    m, kdim = xb.shape
    n = w.shape[-1]
    tm = _pick(m, (512, 256, 128))
    tk = _pick(kdim, (2816, 2048, 1024, 512, 256, 128))
    nk = kdim // tk
    nlead = len(widx)
    w_spec = pl.BlockSpec((None,) * nlead + (tk, n), lambda i, k: tuple(widx) + (k, 0))
    g_spec = pl.BlockSpec((None, None, 1, n), lambda i, k: tuple(gidx) + (0, 0))
    return pl.pallas_call(
        functools.partial(_mm_ln_kernel, alpha=alpha, scale=scale, nk=nk),
        grid=(m // tm, nk),
        in_specs=[
            pl.BlockSpec((tm, tk), lambda i, k: (i, k)),
            w_spec,
            pl.BlockSpec((tm, n), lambda i, k: (i, 0)),
            g_spec, g_spec,
        ],
        out_specs=[pl.BlockSpec((tm, n), lambda i, k: (i, 0)), pl.BlockSpec((tm, n), lambda i, k: (i, 0))],
        out_shape=[jax.ShapeDtypeStruct((m, n), F32), jax.ShapeDtypeStruct((m, n), MXU_DTYPE)],
        compiler_params=_params("arbitrary", "arbitrary"),
        name="matmul_postnorm",
    )(xb, w, res, g[:, :, None, :], b[:, :, None, :])


def _mm_kernel(x_ref, xs_ref, w_ref, o_ref, os_ref, w_s):
    @pl.when(pl.program_id(1) == 0)
    def _():
        w_s[...] = _mx(w_ref[...])
        os_ref[...] = jnp.dot(xs_ref[...], w_s[...], preferred_element_type=F32)
    o_ref[...] = jnp.dot(x_ref[...], w_s[...], preferred_element_type=F32)


def matmul(xb, xsb, w, widx, tn_prefs):
    m, kdim = xb.shape
    ms = xsb.shape[0]
    n = w.shape[-1]
    tm = _pick(m, (1024, 512, 256, 128))
    tn = _pick(n, tn_prefs)
    nlead = len(widx)
    return pl.pallas_call(
        _mm_kernel,
        grid=(n // tn, m // tm),
        in_specs=[
            pl.BlockSpec((tm, kdim), lambda j, i: (i, 0)),
            pl.BlockSpec((ms, kdim), lambda j, i: (0, 0)),
            pl.BlockSpec((None,) * nlead + (kdim, tn), lambda j, i: tuple(widx) + (0, j)),
        ],
        out_specs=[pl.BlockSpec((tm, tn), lambda j, i: (i, j)), pl.BlockSpec((ms, tn), lambda j, i: (0, j))],
        out_shape=[jax.ShapeDtypeStruct((m, n), F32), jax.ShapeDtypeStruct((ms, n), F32)],
        scratch_shapes=[pltpu.VMEM((kdim, tn), MXU_DTYPE)],
        compiler_params=_params("arbitrary", "arbitrary"),
        name="matmul",
    )(xb, xsb, w)


def _rope_tables(pos):
    half = ROPE_DIMS // 2
    inv = 1.0 / (ROPE_THETA ** (jnp.arange(half, dtype=F32) / half))
    ang = pos.astype(F32)[:, None] * inv[None, :]
    cos, sin = jnp.cos(ang), jnp.sin(ang)
    n = pos.shape[0]
    ones = jnp.ones((n, HEAD_DIM - ROPE_DIMS), F32)
    zeros = jnp.zeros((n, HEAD_DIM - ROPE_DIMS), F32)
    zh = jnp.zeros((n, half), F32)
    c = jnp.concatenate([cos, cos, ones], -1)
    a = jnp.concatenate([-sin, zh, zeros], -1)
    b = jnp.concatenate([zh, sin, zeros], -1)
    return c, a, b


def _rope(x, c, a, b):
    half = ROPE_DIMS // 2
    return x * c + pltpu.roll(x, HEAD_DIM - half, 1) * a + pltpu.roll(x, half, 1) * b


def _even_post_kernel(q_ref, slc_ref, win_ref, sm_ref, c_ref, a_ref, b_ref, fb_ref,
                      qr_ref, slco_ref, wino_ref, gate_ref, logf_ref):
    c, a, b = c_ref[...], a_ref[...], b_ref[...]
    for h in range(NSA_HEADS):
        sl = slice(h * HEAD_DIM, (h + 1) * HEAD_DIM)
        qr_ref[:, sl] = _rope(q_ref[:, sl], c, a, b)
    for src, dst in ((slc_ref, slco_ref), (win_ref, wino_ref)):
        for g in range(NSA_KV):
            sl = slice(g * HEAD_DIM, (g + 1) * HEAD_DIM)
            dst[:, sl] = _rope(src[:, sl], c, a, b)
        vs = slice(NSA_KV * HEAD_DIM, 2 * NSA_KV * HEAD_DIM)
        dst[:, vs] = src[:, vs]
    sm = sm_ref[...]
    gate_ref[...] = jax.nn.sigmoid(sm)
    z = sm + fb_ref[...]
    logf_ref[...] = jnp.minimum(z, 0.0) - jnp.log1p(jnp.exp(-jnp.abs(z)))


def even_post(h, tabs, fb_row, npos_blocks):
    m = h.shape[0]
    tm = _pick(m, (512, 256, 128))
    kvw = 2 * NSA_KV * HEAD_DIM
    tab_spec = pl.BlockSpec((tm, HEAD_DIM), lambda i: (i % npos_blocks, 0))
    return pl.pallas_call(
        _even_post_kernel,
        grid=(m // tm,),
        in_specs=[
            pl.BlockSpec((tm, P_NKV), lambda i: (i, 0)),
            pl.BlockSpec((tm, kvw), lambda i: (i, (P_NKV + kvw) // kvw)),
            pl.BlockSpec((tm, kvw), lambda i: (i, (P_NKV + 2 * kvw) // kvw)),
            pl.BlockSpec((tm, 128), lambda i: (i, P_SMALL // 128)),
            tab_spec, tab_spec, tab_spec,
            pl.BlockSpec((1, 128), lambda i: (0, 0)),
        ],
        out_specs=[
            pl.BlockSpec((tm, P_NKV), lambda i: (i, 0)),
            pl.BlockSpec((tm, kvw), lambda i: (i, 0)),
            pl.BlockSpec((tm, kvw), lambda i: (i, 0)),
            pl.BlockSpec((tm, 128), lambda i: (i, 0)),
            pl.BlockSpec((tm, 128), lambda i: (i, 0)),
        ],
        out_shape=[
            jax.ShapeDtypeStruct((m, P_NKV), F32),
            jax.ShapeDtypeStruct((m, kvw), F32),
            jax.ShapeDtypeStruct((m, kvw), F32),
            jax.ShapeDtypeStruct((m, 128), F32),
            jax.ShapeDtypeStruct((m, 128), F32),
        ],
        compiler_params=_params("arbitrary"),
        name="even_post",
    )(h, h, h, h, *tabs, fb_row)


def _cumsum_kernel(x_ref, tri_ref, o_ref, carry):
    @pl.when(pl.program_id(1) == 0)
    def _():
        carry[...] = jnp.zeros_like(carry)
    c = jnp.dot(tri_ref[...], x_ref[...], preferred_element_type=F32,
                precision=lax.Precision.HIGHEST) + carry[...]
    o_ref[...] = c
    carry[...] = c[-1:, :]


def cumsum_rows(x, nb, t):
    tc = _pick(t, (512, 256, 128))
    tri = jnp.tril(jnp.ones((tc, tc), F32))
    nt = t // tc
    return pl.pallas_call(
        _cumsum_kernel,
        grid=(nb, nt),
        in_specs=[pl.BlockSpec((tc, 128), lambda b, i: (b * nt + i, 0)),
                  pl.BlockSpec((tc, tc), lambda b, i: (0, 0))],
        out_specs=pl.BlockSpec((tc, 128), lambda b, i: (b * nt + i, 0)),
        out_shape=jax.ShapeDtypeStruct(x.shape, F32),
        scratch_shapes=[pltpu.VMEM((1, 128), F32)],
        compiler_params=_params("arbitrary", "arbitrary"),
        name="cumsum_rows",
    )(x, tri)


def _compress_kernel(x_ref, w1_ref, pe_ref, w2_ref, o_ref, u0, u1, *, n16):
    u0[...] = jnp.zeros_like(u0)
    u1[...] = jnp.zeros_like(u1)
    for r in range(0, CMP_STRIDE, 2):
        xa = x_ref[pl.ds(r, n16, stride=CMP_STRIDE), :]
        xb = x_ref[pl.ds(r + 1, n16, stride=CMP_STRIDE), :]
        for u, j in ((u0, r), (u1, CMP_STRIDE + r)):
            lhs = jnp.concatenate([_mx(xa + pe_ref[j:j + 1, :]), _mx(xb + pe_ref[j + 1:j + 2, :])], axis=1)
            w = _mx(w1_ref[j:j + 2]).reshape(2 * HEAD_DIM, CMP_HIDDEN)
            u[...] += jnp.dot(lhs, w, preferred_element_type=F32)
    pre = u0[...] + pltpu.roll(u1[...], n16 - 1, 0)
    out = jnp.dot(_mx(jax.nn.gelu(pre)), _mx(w2_ref[...]), preferred_element_type=F32)
    row = lax.broadcasted_iota(jnp.int32, out.shape, 0)
    o_ref[...] = jnp.where(row < n16 - 1, out, 0.0)


def nsa_compress(x, x_spec, nb, length, e, w1, pe, w2):
    n16 = length // CMP_STRIDE
    w1r = w1.reshape(w1.shape[0], 2, CMP_LEN, HEAD_DIM, CMP_HIDDEN)
    return pl.pallas_call(
        functools.partial(_compress_kernel, n16=n16),
        grid=(nb, 2 * NSA_KV),
        in_specs=[
            x_spec,
            pl.BlockSpec((None, None, CMP_LEN, HEAD_DIM, CMP_HIDDEN), lambda b, c: (e, c // NSA_KV, 0, 0, 0)),
            pl.BlockSpec((None, None, CMP_LEN, HEAD_DIM), lambda b, c: (e, c // NSA_KV, 0, 0)),
            pl.BlockSpec((None, None, CMP_HIDDEN, HEAD_DIM), lambda b, c: (e, c // NSA_KV, 0, 0)),
        ],
        out_specs=pl.BlockSpec((None, None, n16, HEAD_DIM), lambda b, c: (b, c, 0, 0)),
        out_shape=jax.ShapeDtypeStruct((nb, 2 * NSA_KV, n16, HEAD_DIM), F32),
        scratch_shapes=[pltpu.VMEM((n16, CMP_HIDDEN), F32), pltpu.VMEM((n16, CMP_HIDDEN), F32)],
        compiler_params=_params("arbitrary", "arbitrary"),
        name="nsa_compress",
    )(x, w1r, pe, w2)


def _sel_matrix(n_c_pad, n_c, n_sel, width):
    ratio = SEL_LEN // CMP_STRIDE
    i = np.arange(n_c_pad)[:, None]
    j = np.arange(width)[None, :]
    a = (i >= ratio * j - 1) & (i <= ratio * j + ratio - 1) & (i < n_c) & (j < n_sel)
    return jnp.asarray(a.astype(np.float32))


def _masked_softmax(lg, mask):
    lg = jnp.where(mask, lg, -jnp.inf)
    m = jnp.max(lg, axis=-1, keepdims=True)
    m = jnp.where(m == -jnp.inf, 0.0, m)
    p = jnp.exp(lg - m)
    s = jnp.sum(p, axis=-1, keepdims=True)
    return p / jnp.where(s > 0, s, 1.0)


def _cmp_select_kernel(q_ref, kc_ref, vc_ref, a_ref, oc_ref, sel_ref, *, tq, n_sel, n_top):
    i = pl.program_id(2)
    ncp = kc_ref.shape[0]
    qpos = i * tq + lax.broadcasted_iota(jnp.int32, (tq, 1), 0)
    cend = lax.broadcasted_iota(jnp.int32, (1, ncp), 1) * CMP_STRIDE + CMP_LEN
    cmask = cend <= qpos + 1
    kc = _mx(kc_ref[...])
    vc = _mx(vc_ref[...])
    scale = HEAD_DIM ** -0.5
    imp = jnp.zeros((tq, ncp), F32)
    for m in range(NSA_GROUP):
        sl = slice(m * HEAD_DIM, (m + 1) * HEAD_DIM)
        lg = lax.dot_general(_mx(q_ref[:, sl]), kc, NT_DIMS, preferred_element_type=F32) * scale
        p = _masked_softmax(lg, cmask)
        oc_ref[:, sl] = jnp.dot(_mx(p), vc, preferred_element_type=F32)
        imp = imp + p
    s_sel = jnp.dot(imp, a_ref[...], preferred_element_type=F32, precision=lax.Precision.HIGHEST)
    st = s_sel.T[:n_sel, :]
    blk = lax.broadcasted_iota(jnp.int32, (n_sel, tq), 0)
    cur = (i * tq + lax.broadcasted_iota(jnp.int32, (n_sel, tq), 1)) // SEL_LEN
    forced = (blk == 0) | (blk == cur) | (blk == cur - 1)
    allowed = blk <= cur
    v = jnp.where(allowed, jnp.where(forced, jnp.inf, st), -jnp.inf)
    rank = jnp.zeros((n_sel, tq), jnp.int32)
    for r in range(n_sel):
        vr = v[r:r + 1, :]
        before = (vr > v) | ((vr == v) & (blk > r))
        rank = rank + before.astype(jnp.int32)
    sel = ((rank < n_top) & allowed).astype(F32)
    if n_sel < 128:
        sel = jnp.concatenate([sel, jnp.zeros((128 - n_sel, tq), F32)], axis=0)
    sel_ref[...] = sel.T.astype(sel_ref.dtype)


def cmp_select(h, kvc, a_mat, nb, t):
    tq = _pick(t, (256, 128))
    nt = t // tq
    n_sel = t // SEL_LEN
    assert n_sel <= 128 and n_sel % 8 == 0
    ncp = kvc.shape[2]
    gw = NSA_GROUP * HEAD_DIM
    return pl.pallas_call(
        functools.partial(_cmp_select_kernel, tq=tq, n_sel=n_sel, n_top=min(SEL_TOP, n_sel)),
        grid=(nb, NSA_KV, nt),
        in_specs=[
            pl.BlockSpec((tq, gw), lambda b, g, i: (b * nt + i, g)),
            pl.BlockSpec((None, None, ncp, HEAD_DIM), lambda b, g, i: (b, g, 0, 0)),
            pl.BlockSpec((None, None, ncp, HEAD_DIM), lambda b, g, i: (b, NSA_KV + g, 0, 0)),
            pl.BlockSpec((ncp, 128), lambda b, g, i: (0, 0)),
        ],
        out_specs=[
            pl.BlockSpec((tq, gw), lambda b, g, i: (b * nt + i, g)),
            pl.BlockSpec((None, None, tq, 128), lambda b, g, i: (b, g, i, 0)),
        ],
        out_shape=[
            jax.ShapeDtypeStruct((nb * t, NSA_HEADS * HEAD_DIM), F32),
            jax.ShapeDtypeStruct((nb, NSA_KV, t, 128), MXU_DTYPE),
        ],
        compiler_params=_params("arbitrary", "arbitrary", "arbitrary"),
        name="cmp_select",
    )(h, kvc, kvc, a_mat)


def _flash_steps(mode, t, tq, tk):
    rows = []
    for qi in range(t // tq):
        q_lo, q_hi = qi * tq, qi * tq + tq - 1
        k_hi = q_hi // tk
        k_lo = max((q_lo - WINDOW + 1) // tk, 0) if mode == "win" else 0
        for kt in range(k_lo, k_hi + 1):
            every_key_visible = kt * tk + tk - 1 <= q_lo
            rows.append((qi, kt, int(kt == k_lo), int(kt == k_hi), int(not every_key_visible)))
    return jnp.asarray(np.array(rows, np.int32).T)


def _flash_kernel(*refs, mode, tq, tk, heads):
    if mode == "fox":
        tab, q_ref, k_ref, v_ref, cq_ref, ck_ref, o_ref, m_s, acc = refs
    elif mode == "slc":
        tab, q_ref, k_ref, v_ref, sel_ref, o_ref, m_s, acc = refs
    else:
        tab, q_ref, k_ref, v_ref, o_ref, m_s, acc = refs
    step = pl.program_id(2)
    qi, kt = tab[0, step], tab[1, step]

    @pl.when(tab[2, step] == 1)
    def _():
        m_s[...] = jnp.full_like(m_s, NEG_BIG)
        acc[...] = jnp.zeros_like(acc)

    def update(causal):
        ones = jnp.ones((tk, HEAD_DIM), MXU_DTYPE)
        if mode != "fox":
            kb = _mx(k_ref[...])
            v_aug = jnp.concatenate([_mx(v_ref[...]), ones], axis=1)
        valid = None
        if causal or mode == "win":
            qpos = qi * tq + lax.broadcasted_iota(jnp.int32, (tq, tk), 0)
            kpos = kt * tk + lax.broadcasted_iota(jnp.int32, (tq, tk), 1)
            valid = kpos <= qpos
            if mode == "win":
                valid = valid & (kpos > qpos - WINDOW)
        if mode == "slc":
            blk = lax.broadcasted_iota(jnp.int32, (128, tk), 0)
            kblk = (kt * tk + lax.broadcasted_iota(jnp.int32, (128, tk), 1)) // SEL_LEN
            chosen = jnp.dot(sel_ref[...], (blk == kblk).astype(MXU_DTYPE), preferred_element_type=F32) > 0.5
            valid = chosen if valid is None else valid & chosen
        for hh in range(heads):
            cols = slice(hh * HEAD_DIM, (hh + 1) * HEAD_DIM)
            if mode == "fox":
                kb = _mx(k_ref[:, cols])
                v_aug = jnp.concatenate([_mx(v_ref[:, cols]), ones], axis=1)
                ck2 = ck_ref[hh] * LOG2E
                cq2 = cq_ref[hh] * LOG2E
            s = lax.dot_general(_mx(q_ref[:, cols]), kb, NT_DIMS, preferred_element_type=F32)
            x = s * (HEAD_DIM ** -0.5 * LOG2E)
            if mode == "fox":
                x = x - ck2
            if valid is not None:
                x = jnp.where(valid, x, NEG_BIG)
            top = jnp.max(x, axis=-1, keepdims=True)
            if mode == "fox":
                top = top + cq2
            m_old = m_s[hh]
            m_new = jnp.maximum(m_old, top)
            p = jnp.exp2(x - (m_new - cq2 if mode == "fox" else m_new))
            if mode == "win":
                p = jnp.where(valid, p, 0.0)
            acc[hh] = jnp.exp2(m_old - m_new) * acc[hh] + jnp.dot(_mx(p), v_aug, preferred_element_type=F32)
            m_s[hh] = m_new

    if mode == "win":
        update(True)
    else:
        pl.when(tab[4, step] == 1)(functools.partial(update, True))
        pl.when(tab[4, step] == 0)(functools.partial(update, False))

    @pl.when(tab[3, step] == 1)
    def _():
        for hh in range(heads):
            l = acc[hh, :, HEAD_DIM:]
            o_ref[:, hh * HEAD_DIM:(hh + 1) * HEAD_DIM] = acc[hh, :, :HEAD_DIM] / jnp.where(l > 0, l, 1.0)


def flash_attention(mode, nb, t, q, qcol, k, kcol, v, vcol, extra=()):
    heads = NSA_GROUP
    tk = _pick(t, (512,) if mode == "win" else (1024, 512, 256, 128))
    tq = _pick(t, (512, 256, 128))
    ntq, ntk = t // tq, t // tk
    qw = heads * HEAD_DIM
    kw = qw if mode == "fox" else HEAD_DIM
    steps = _flash_steps(mode, t, tq, tk)
    in_specs = [
        pl.BlockSpec((tq, qw), lambda b, h, s, tab: (b * ntq + tab[0, s], qcol(h))),
        pl.BlockSpec((tk, kw), lambda b, h, s, tab: (b * ntk + tab[1, s], kcol(h))),
        pl.BlockSpec((tk, kw), lambda b, h, s, tab: (b * ntk + tab[1, s], vcol(h))),
    ]
    scratch = [pltpu.VMEM((heads, tq, 1), F32), pltpu.VMEM((heads, tq, 2 * HEAD_DIM), F32)]
    if mode == "fox":
        in_specs += [
            pl.BlockSpec((None, heads, tq, 1), lambda b, h, s, tab: (b, h, tab[0, s], 0)),
            pl.BlockSpec((None, heads, 1, tk), lambda b, h, s, tab: (b, h, 0, tab[1, s])),
        ]
    elif mode == "slc":
        in_specs += [pl.BlockSpec((None, None, tq, 128), lambda b, h, s, tab: (b, h, tab[0, s], 0))]
    return pl.pallas_call(
        functools.partial(_flash_kernel, mode=mode, tq=tq, tk=tk, heads=heads),
        grid_spec=pltpu.PrefetchScalarGridSpec(
            num_scalar_prefetch=1,
            grid=(nb, NSA_HEADS // heads, steps.shape[1]),
            in_specs=in_specs,
            out_specs=pl.BlockSpec((tq, qw), lambda b, h, s, tab: (b * ntq + tab[0, s], h)),
            scratch_shapes=scratch,
        ),
        out_shape=jax.ShapeDtypeStruct((nb * t, NSA_HEADS * HEAD_DIM), F32),
        compiler_params=_params("arbitrary", "arbitrary", "arbitrary"),
        name="flash_" + mode,
    )(steps, q, k, v, *extra)


def _combine_kernel(oc_ref, os_ref, ow_ref, of_ref, g_ref, o_ref):
    gates = g_ref[...]
    nw = NSA_HEADS * HEAD_DIM
    for h in range(NSA_HEADS):
        sl = slice(h * HEAD_DIM, (h + 1) * HEAD_DIM)
        o = (gates[:, 3 * h:3 * h + 1] * oc_ref[:, sl] + gates[:, 3 * h + 1:3 * h + 2] * os_ref[:, sl]
             + gates[:, 3 * h + 2:3 * h + 3] * ow_ref[:, sl])
        o_ref[:, sl] = o.astype(o_ref.dtype)
    o_ref[:, nw:] = of_ref[...].astype(o_ref.dtype)


def combine_heads(o_c, o_s, o_w, o_f, gates):
    m, nw = o_c.shape
    tm = _pick(m, (512, 256, 128))
    spec = pl.BlockSpec((tm, nw), lambda i: (i, 0))
    return pl.pallas_call(
        _combine_kernel,
        grid=(m // tm,),
        in_specs=[spec, spec, spec, spec, pl.BlockSpec((tm, 128), lambda i: (i, 0))],
        out_specs=pl.BlockSpec((tm, 2 * nw), lambda i: (i, 0)),
        out_shape=jax.ShapeDtypeStruct((m, 2 * nw), MXU_DTYPE),
        compiler_params=_params("arbitrary"),
        name="combine_heads",
    )(o_c, o_s, o_w, o_f, gates)


def _gather_pages_kernel(pt_ref, *refs, page):
    *x_refs, o_ref = refs
    for j, x_ref in enumerate(x_refs):
        for c in range(2 * NSA_KV):
            o_ref[c, j * page:(j + 1) * page, :] = x_ref[:, c // NSA_KV, c % NSA_KV, :]


def gather_cmp_pages(pool, e, page_table):
    nb, n_pages = page_table.shape
    page = pool.shape[2]
    per_step = _pick(n_pages, (4, 2, 1))

    def page_spec(j):
        return pl.BlockSpec((None, None, page, 2, NSA_KV, HEAD_DIM),
                            lambda b, p, pt: (e, pt[b, p * per_step + j], 0, 0, 0, 0))

    return pl.pallas_call(
        functools.partial(_gather_pages_kernel, page=page),
        grid_spec=pltpu.PrefetchScalarGridSpec(
            num_scalar_prefetch=1,
            grid=(nb, n_pages // per_step),
            in_specs=[page_spec(j) for j in range(per_step)],
            out_specs=pl.BlockSpec((None, 2 * NSA_KV, per_step * page, HEAD_DIM), lambda b, p, pt: (b, 0, p, 0)),
        ),
        out_shape=jax.ShapeDtypeStruct((nb, 2 * NSA_KV, n_pages * page, HEAD_DIM), F32),
        compiler_params=_params("arbitrary", "arbitrary"),
        name="gather_cmp_pages",
    )(page_table, *([pool] * per_step))


def _sample_cmp_win_kernel(qraw_ref, qrot_ref, kc_ref, vc_ref, a_ref, kw_ref, vw_ref,
                           oc_ref, ow_ref, idx_ref, *, n_c, n_sel, n_top, q_pos):
    scale = HEAD_DIM ** -0.5
    ncp = kc_ref.shape[0]
    rows = qraw_ref.shape[0]
    cend = lax.broadcasted_iota(jnp.int32, (1, ncp), 1) * CMP_STRIDE + CMP_LEN
    cidx = lax.broadcasted_iota(jnp.int32, (1, ncp), 1)
    cmask = (cend <= q_pos + 1) & (cidx < n_c)
    lg = lax.dot_general(_mx(qraw_ref[...]), _mx(kc_ref[...]), NT_DIMS, preferred_element_type=F32) * scale
    p = _masked_softmax(lg, cmask)
    oc_ref[...] = jnp.dot(_mx(p), _mx(vc_ref[...]), preferred_element_type=F32)
    head = lax.broadcasted_iota(jnp.int32, p.shape, 0)
    imp = jnp.sum(jnp.where(head < NSA_GROUP, p, 0.0), axis=0, keepdims=True)
    imp = jnp.broadcast_to(imp, (rows, ncp))
    s_sel = jnp.dot(imp, a_ref[...], preferred_element_type=F32, precision=lax.Precision.HIGHEST)
    nsp = s_sel.shape[1]
    lane = lax.broadcasted_iota(jnp.int32, (rows, nsp), 1).astype(F32)
    cur = q_pos // SEL_LEN
    forced = (lane == 0.0) | (lane == float(cur)) | (lane == float(cur - 1))
    v = jnp.where(lane <= float(cur), jnp.where(forced, jnp.inf, s_sel), -jnp.inf)
    out_lane = lax.broadcasted_iota(jnp.int32, (rows, 128), 1)
    picked = jnp.full((rows, 128), -1.0, F32)
    for r in range(n_top):
        mx = jnp.max(v, axis=1, keepdims=True)
        ix = jnp.min(jnp.where(v == mx, lane, float(nsp)), axis=1, keepdims=True)
        ix = jnp.where(mx > -jnp.inf, ix, -1.0)
        picked = jnp.where(out_lane == r, ix, picked)
        v = jnp.where(lane == ix, -jnp.inf, v)
    idx_ref[...] = picked.astype(jnp.int32)
    lw = lax.dot_general(_mx(qrot_ref[...]), _mx(kw_ref[...]), NT_DIMS, preferred_element_type=F32) * scale
    pw = _masked_softmax(lw, jnp.full(lw.shape, True))
    ow_ref[...] = jnp.dot(_mx(pw), _mx(vw_ref[...]), preferred_element_type=F32)


def sample_cmp_win(qraw, qrot, kvc, a_mat, win, n_c, n_sel, q_pos):
    nb, _, rows, _ = qraw.shape
    ncp = kvc.shape[2]
    nsp = a_mat.shape[1]
    wlen = win.shape[1]
    qspec = pl.BlockSpec((None, None, rows, HEAD_DIM), lambda b, g: (b, g, 0, 0))
    return pl.pallas_call(
        functools.partial(_sample_cmp_win_kernel, n_c=n_c, n_sel=n_sel, n_top=min(SEL_TOP, n_sel), q_pos=q_pos),
        grid=(nb, NSA_KV),
        in_specs=[
            qspec, qspec,
            pl.BlockSpec((None, None, ncp, HEAD_DIM), lambda b, g: (b, g, 0, 0)),
            pl.BlockSpec((None, None, ncp, HEAD_DIM), lambda b, g: (b, NSA_KV + g, 0, 0)),
            pl.BlockSpec((ncp, nsp), lambda b, g: (0, 0)),
            pl.BlockSpec((None, wlen, HEAD_DIM), lambda b, g: (b, 0, g)),
            pl.BlockSpec((None, wlen, HEAD_DIM), lambda b, g: (b, 0, NSA_KV + g)),
        ],
        out_specs=[qspec, qspec, pl.BlockSpec((None, None, rows, 128), lambda b, g: (b, g, 0, 0))],
        out_shape=[
            jax.ShapeDtypeStruct(qraw.shape, F32),
            jax.ShapeDtypeStruct(qraw.shape, F32),
            jax.ShapeDtypeStruct((nb, NSA_KV, rows, 128), jnp.int32),
        ],
        compiler_params=_params("arbitrary", "arbitrary"),
        name="sample_cmp_win",
    )(qraw, qrot, kvc, kvc, a_mat, win, win)


def _sample_slc_kernel(pt_ref, idx_ref, q_ref, *refs, n_top, n_past_blk):
    kv_refs = refs[:n_top]
    kn_ref, vn_ref, o_ref = refs[n_top:]
    b, g = pl.program_id(0), pl.program_id(1)
    row = b * NSA_KV + g
    scale = HEAD_DIM ** -0.5

    def attend(gg):
        q = q_ref[...]
        qb = _mx(q)
        picked = [idx_ref[row, j] for j in range(n_top)]
        in_past = [(blk >= 0) & (blk < n_past_blk) for blk in picked]
        has_new = picked[0] == n_past_blk
        for blk in picked[1:]:
            has_new = has_new | (blk == n_past_blk)
        s_new = jnp.where(has_new, jnp.sum(q * kn_ref[...], axis=-1, keepdims=True) * scale, NEG_BIG)
        logits = []
        for j in range(n_top):
            s = lax.dot_general(qb, _mx(kv_refs[j][:, 0, gg, :]), NT_DIMS, preferred_element_type=F32) * scale
            logits.append(jnp.where(in_past[j], s, NEG_BIG))
        m = s_new
        for s in logits:
            m = jnp.maximum(m, jnp.max(s, axis=-1, keepdims=True))
        p_new = jnp.where(has_new, jnp.exp(s_new - m), 0.0)
        l = p_new
        o = p_new * vn_ref[...]
        for j in range(n_top):
            p = jnp.where(in_past[j], jnp.exp(logits[j] - m), 0.0)
            l = l + jnp.sum(p, axis=-1, keepdims=True)
            o = o + jnp.dot(_mx(p), _mx(kv_refs[j][:, 1, gg, :]), preferred_element_type=F32)
        o_ref[...] = o / jnp.where(l > 0, l, 1.0)

    for gg in range(NSA_KV):
        pl.when(g == gg)(functools.partial(attend, gg))


def sample_slc(qrot, pool, e, page_table, idx, slc_new, n_past_blk):
    nb, _, rows, _ = qrot.shape
    n_top = idx.shape[1]
    page = pool.shape[2]
    per_page = page // SEL_LEN

    def kv_spec(j):
        def index(b, g, pt, ix):
            blk = jnp.clip(ix[b * NSA_KV + g, j], 0, n_past_blk - 1)
            return (e, pt[b, blk // per_page], blk % per_page, 0, 0, 0)
        return pl.BlockSpec((None, None, SEL_LEN, 2, NSA_KV, HEAD_DIM), index)

    return pl.pallas_call(
        functools.partial(_sample_slc_kernel, n_top=n_top, n_past_blk=n_past_blk),
        grid_spec=pltpu.PrefetchScalarGridSpec(
            num_scalar_prefetch=2,
            grid=(nb, NSA_KV),
            in_specs=[
                pl.BlockSpec((None, None, rows, HEAD_DIM), lambda b, g, pt, ix: (b, g, 0, 0)),
                *[kv_spec(j) for j in range(n_top)],
                pl.BlockSpec((None, 1, HEAD_DIM), lambda b, g, pt, ix: (b, 0, g)),
                pl.BlockSpec((None, 1, HEAD_DIM), lambda b, g, pt, ix: (b, 0, NSA_KV + g)),
            ],
            out_specs=pl.BlockSpec((None, None, rows, HEAD_DIM), lambda b, g, pt, ix: (b, g, 0, 0)),
        ),
        out_shape=jax.ShapeDtypeStruct(qrot.shape, F32),
        compiler_params=_params("arbitrary", "arbitrary"),
        name="sample_slc",
    )(page_table, idx, qrot, *([pool] * n_top), slc_new, slc_new)


def _sample_fox_kernel(pt_ref, q_ref, *refs, n_steps, per_step):
    page_refs = refs[:3 * per_step]
    tri_ref, kn_ref, vn_ref, lfn_ref, o_ref, m_s, l_s, acc, crun = refs[3 * per_step:]
    step = pl.program_id(1)
    scale = HEAD_DIM ** -0.5
    nh = FOX_HEADS
    page = page_refs[0].shape[0]
    cols = page * nh

    @pl.when(step == 0)
    def _():
        m_s[...] = jnp.full_like(m_s, NEG_BIG)
        l_s[...] = jnp.zeros_like(l_s)
        acc[...] = jnp.zeros_like(acc)
        crun[...] = jnp.zeros_like(crun)

    qb = _mx(q_ref[...])
    tri = _mx(tri_ref[...])
    lane = lax.broadcasted_iota(jnp.int32, (nh, HEAD_DIM), 1) // nh
    own = (lax.broadcasted_iota(jnp.int32, (nh, cols), 1) % nh) == lax.broadcasted_iota(jnp.int32, (nh, cols), 0)
    c_off = crun[...]
    logits, values = [], []
    for j in range(per_step):
        k_ref, v_ref, lf_ref = page_refs[3 * j:3 * j + 3]
        lf = lf_ref[...]
        hi = _mx(lf).astype(F32)
        mid = _mx(lf - hi).astype(F32)
        lo = lf - hi - mid
        cs = jnp.dot(_mx(jnp.concatenate([hi, mid, lo], axis=0)), tri, preferred_element_type=F32)
        ck = cs[:nh] + cs[nh:2 * nh] + cs[2 * nh:] + c_off
        c_off = ck[:, -1:]
        ck_x = jnp.concatenate(
            [jnp.take_along_axis(ck, lane + i * (HEAD_DIM // nh), axis=1) for i in range(cols // HEAD_DIM)], axis=1)
        k2 = k_ref[...].reshape(cols, HEAD_DIM)
        s = lax.dot_general(qb, _mx(k2), NT_DIMS, preferred_element_type=F32) * scale - ck_x
        logits.append(jnp.where(own, s, NEG_BIG))
        values.append(_mx(v_ref[...].reshape(cols, HEAD_DIM)))
    crun[...] = c_off
    m_old = m_s[...]
    m_new = m_old
    for s in logits:
        m_new = jnp.maximum(m_new, jnp.max(s, axis=-1, keepdims=True))
    alpha = jnp.exp(m_old - m_new)
    l_new = alpha * l_s[...]
    o_new = alpha * acc[...]
    for s, v2 in zip(logits, values):
        p = jnp.where(own, jnp.exp(s - m_new), 0.0)
        l_new = l_new + jnp.sum(p, axis=-1, keepdims=True)
        o_new = o_new + jnp.dot(_mx(p), v2, preferred_element_type=F32)
    l_s[...] = l_new
    acc[...] = o_new
    m_s[...] = m_new

    @pl.when(step == n_steps - 1)
    def _():
        cq = c_off + lfn_ref[...]
        s_new = jnp.sum(q_ref[...] * kn_ref[...], axis=-1, keepdims=True) * scale - cq
        m_fin = jnp.maximum(m_new, s_new)
        p_new = jnp.exp(s_new - m_fin)
        a_fin = jnp.exp(m_new - m_fin)
        o_ref[...] = (a_fin * o_new + p_new * vn_ref[...]) / (a_fin * l_new + p_new)


def sample_fox(fq, pool, logf_t, e, page_table, k_new, v_new, logf_new):
    nb, n_pages = page_table.shape
    page = pool.shape[2]
    per_step = _pick(n_pages, (8, 4, 2, 1))
    tri = jnp.triu(jnp.ones((page, page), F32))

    def page_specs(j):
        def kv(which):
            return pl.BlockSpec((None, None, page, None, FOX_HEADS, HEAD_DIM),
                                lambda b, p, pt: (e, pt[b, p * per_step + j], 0, which, 0, 0))
        return [kv(0), kv(1),
                pl.BlockSpec((None, None, FOX_HEADS, page), lambda b, p, pt: (e, pt[b, p * per_step + j], 0, 0))]

    page_ops = []
    for j in range(per_step):
        page_ops += [pool, pool, logf_t]
    return pl.pallas_call(
        functools.partial(_sample_fox_kernel, n_steps=n_pages // per_step, per_step=per_step),
        grid_spec=pltpu.PrefetchScalarGridSpec(
            num_scalar_prefetch=1,
            grid=(nb, n_pages // per_step),
            in_specs=[
                pl.BlockSpec((None, FOX_HEADS, HEAD_DIM), lambda b, p, pt: (b, 0, 0)),
                *[spec for j in range(per_step) for spec in page_specs(j)],
                pl.BlockSpec((page, page), lambda b, p, pt: (0, 0)),
                pl.BlockSpec((None, FOX_HEADS, HEAD_DIM), lambda b, p, pt: (b, 0, 0)),
                pl.BlockSpec((None, FOX_HEADS, HEAD_DIM), lambda b, p, pt: (b, 0, 0)),
                pl.BlockSpec((None, FOX_HEADS, 1), lambda b, p, pt: (b, 0, 0)),
            ],
            out_specs=pl.BlockSpec((None, FOX_HEADS, HEAD_DIM), lambda b, p, pt: (b, 0, 0)),
            scratch_shapes=[
                pltpu.VMEM((FOX_HEADS, 1), F32), pltpu.VMEM((FOX_HEADS, 1), F32),
                pltpu.VMEM((FOX_HEADS, HEAD_DIM), F32), pltpu.VMEM((FOX_HEADS, 1), F32),
            ],
        ),
        out_shape=jax.ShapeDtypeStruct((nb, FOX_HEADS, HEAD_DIM), F32),
        compiler_params=_params("arbitrary", "arbitrary"),
        name="sample_fox",
    )(page_table, fq, *page_ops, tri, k_new, v_new, logf_new)


def _ret_tables(chunk):
    lg = jnp.log1p(-(2.0 ** (-5.0 - jnp.arange(RET_HEADS, dtype=F32))))
    i = jnp.arange(chunk, dtype=F32)
    diff = i[:, None] - i[None, :]
    dec = jnp.where(diff >= 0, jnp.exp(jnp.maximum(diff, 0.0)[None] * lg[:, None, None]), 0.0)
    xi = jnp.exp((i[None, :] + 1.0) * lg[:, None])
    zeta = jnp.exp((chunk - 1.0 - i)[None, :] * lg[:, None])
    g_c = jnp.exp(chunk * lg)
    return dec, xi, zeta, g_c


def _ret_rope_tables(pos):
    half = RET_DK // 2
    inv = 1.0 / (RET_THETA ** (jnp.arange(half, dtype=F32) / half))
    ang = pos.astype(F32)[:, None] * inv[None, :]
    return jnp.cos(ang), jnp.sin(ang)


def _group_norm_gate(o, gate, gn):
    mu = jnp.mean(o, axis=-1, keepdims=True)
    d = o - mu
    var = jnp.mean(d * d, axis=-1, keepdims=True)
    y = d * lax.rsqrt(var + GN_EPS) * gn
    return gate * jax.nn.sigmoid(gate) * y


def _retention_kernel(q_ref, k_ref, v_ref, gate_ref, cos_ref, sin_ref, dec_ref, coef_ref, gn_ref,
                      y_ref, s_ref, state, *, n_chunks, heads):
    c = pl.program_id(2)
    half = RET_DK // 2

    @pl.when(c == 0)
    def _():
        state[...] = jnp.zeros_like(state)

    cos, sin = cos_ref[...], sin_ref[...]

    def rot(x_ref, hh):
        x1 = x_ref[:, hh * RET_DK:hh * RET_DK + half]
        x2 = x_ref[:, hh * RET_DK + half:(hh + 1) * RET_DK]
        return jnp.concatenate([x1 * cos - x2 * sin, x2 * cos + x1 * sin], axis=-1)

    for hh in range(heads):
        vcols = slice(hh * RET_DV, (hh + 1) * RET_DV)
        coef = coef_ref[hh]
        xi, zeta, g_c = coef[:, 0:1], coef[:, 1:2], coef[0:1, 2:3]
        q = _mx(rot(q_ref, hh))
        kf = rot(k_ref, hh) * (RET_DK ** -0.5)
        v = _mx(v_ref[:, vcols])
        s_old = state[hh]
        a = lax.dot_general(q, _mx(kf), NT_DIMS, preferred_element_type=F32) * dec_ref[hh]
        o = (jnp.dot(_mx(a), v, preferred_element_type=F32)
             + jnp.dot(q, _mx(s_old), preferred_element_type=F32) * xi)
        state[hh] = s_old * g_c + lax.dot_general(_mx(kf * zeta), v, TN_DIMS, preferred_element_type=F32)
        y_ref[:, vcols] = _group_norm_gate(o, gate_ref[:, vcols], gn_ref[:, vcols]).astype(y_ref.dtype)

    @pl.when(c == n_chunks - 1)
    def _():
        s_ref[...] = state[...]


def retention_prompt(h, nb, t, gn_g, o_idx):
    ch = RET_CHUNK
    n_chunks = t // ch
    dec, xi, zeta, g_c = _ret_tables(ch)
    coef = jnp.stack([xi, zeta, jnp.broadcast_to(g_c[:, None], xi.shape)], axis=-1)
    coef = jnp.pad(coef, ((0, 0), (0, 0), (0, 128 - 3)))
    cos, sin = _ret_rope_tables(jnp.arange(t))
    hs = RET_HEADS_PER_STEP
    groups = RET_HEADS // hs
    kw, vw = hs * RET_DK, hs * RET_DV
    kb, vb, gb = O_RK // kw, O_RV // vw, O_RG // vw
    return pl.pallas_call(
        functools.partial(_retention_kernel, n_chunks=n_chunks, heads=hs),
        grid=(nb, groups, n_chunks),
        in_specs=[
            pl.BlockSpec((ch, kw), lambda b, hg, c: (b * n_chunks + c, hg)),
            pl.BlockSpec((ch, kw), lambda b, hg, c: (b * n_chunks + c, kb + hg)),
            pl.BlockSpec((ch, vw), lambda b, hg, c: (b * n_chunks + c, vb + hg)),
            pl.BlockSpec((ch, vw), lambda b, hg, c: (b * n_chunks + c, gb + hg)),
            pl.BlockSpec((ch, RET_DK // 2), lambda b, hg, c: (c, 0)),
            pl.BlockSpec((ch, RET_DK // 2), lambda b, hg, c: (c, 0)),
            pl.BlockSpec((hs, ch, ch), lambda b, hg, c: (hg, 0, 0)),
            pl.BlockSpec((hs, ch, 128), lambda b, hg, c: (hg, 0, 0)),
            pl.BlockSpec((None, 1, vw), lambda b, hg, c: (o_idx, 0, hg)),
        ],
        out_specs=[
            pl.BlockSpec((ch, vw), lambda b, hg, c: (b * n_chunks + c, hg)),
            pl.BlockSpec((None, hs, RET_DK, RET_DV), lambda b, hg, c: (b, hg, 0, 0)),
        ],
        out_shape=[
            jax.ShapeDtypeStruct((nb * t, RET_HEADS * RET_DV), MXU_DTYPE),
            jax.ShapeDtypeStruct((nb, RET_HEADS, RET_DK, RET_DV), F32),
        ],
        scratch_shapes=[pltpu.VMEM((hs, RET_DK, RET_DV), F32)],
        compiler_params=_params("arbitrary", "arbitrary", "arbitrary"),
        name="retention_prompt",
    )(h, h, h, h, cos, sin, dec, coef, gn_g[:, None, :])


def _retention_step_kernel(q_ref, k_ref, v_ref, gate_ref, cos_ref, sin_ref, coef_ref, gn_ref, s0_ref,
                           y_ref, s_ref):
    half = RET_DK // 2
    hh = pl.program_id(1)
    cos, sin = cos_ref[...], sin_ref[...]

    def rot(x_ref):
        x1, x2 = x_ref[:half, :], x_ref[half:, :]
        return jnp.concatenate([x1 * cos - x2 * sin, x2 * cos + x1 * sin], axis=0)

    head = lax.broadcasted_iota(jnp.int32, coef_ref.shape, 0)
    coef = jnp.sum(jnp.where(head == hh, coef_ref[...], 0.0), axis=0, keepdims=True)
    dec, xi, zeta, g_c = coef[:, 0:1], coef[:, 1:2], coef[:, 2:3], coef[:, 3:4]
    q = rot(q_ref)
    k = rot(k_ref) * (RET_DK ** -0.5)
    v = v_ref[...]
    s_old = s0_ref[...]
    a = jnp.sum(q * k, axis=0, keepdims=True) * dec
    o = a * v + jnp.sum(q * s_old, axis=0, keepdims=True) * xi
    s_ref[...] = s_old * g_c + (k * zeta) * v
    y_ref[...] = _group_norm_gate(o, gate_ref[...], gn_ref[...])


def retention_step(q_col, k_col, v_row, gate_row, pos, s0, o_idx, gn_g):
    nb = q_col.shape[0]
    dec, xi, zeta, g_c = _ret_tables(1)
    coef = jnp.stack([dec[:, 0, 0], xi[:, 0], zeta[:, 0], g_c], axis=-1)
    coef = jnp.pad(coef, ((0, 0), (0, 128 - 4)))
    cos, sin = _ret_rope_tables(pos)
    col = pl.BlockSpec((None, None, RET_DK, 1), lambda b, hh: (b, hh, 0, 0))
    row = pl.BlockSpec((None, None, 1, RET_DV), lambda b, hh: (b, hh, 0, 0))
    tab = pl.BlockSpec((RET_DK // 2, 1), lambda b, hh: (0, 0))
    return pl.pallas_call(
        _retention_step_kernel,
        grid=(nb, RET_HEADS),
        in_specs=[
            col, col, row, row, tab, tab,
            pl.BlockSpec((RET_HEADS, 128), lambda b, hh: (0, 0)),
            pl.BlockSpec((None, 1, RET_DV), lambda b, hh: (o_idx, 0, hh)),
            pl.BlockSpec((None, None, None, RET_DK, RET_DV), lambda b, hh: (o_idx, b, hh, 0, 0)),
        ],
        out_specs=[row, pl.BlockSpec((None, None, RET_DK, RET_DV), lambda b, hh: (b, hh, 0, 0))],
        out_shape=[
            jax.ShapeDtypeStruct((nb, RET_HEADS, 1, RET_DV), F32),
            jax.ShapeDtypeStruct((nb, RET_HEADS, RET_DK, RET_DV), F32),
        ],
        compiler_params=_params("arbitrary", "arbitrary"),
        name="retention_step",
    )(q_col, k_col, v_row, gate_row, cos.reshape(-1, 1), sin.reshape(-1, 1), coef, gn_g[:, None, :], s0)


def _pad_rows(x, rows):
    return jnp.pad(x, ((0, rows - x.shape[0]),) + ((0, 0),) * (x.ndim - 1))


def _repack_even(w):
    small = jnp.concatenate([w[:, O_NG:O_FQ], w[:, O_FF:E_EVEN]], axis=1)
    small = jnp.pad(small, ((0, 0), (0, 128 - small.shape[1])))
    return jnp.concatenate([w[:, O_NQ:O_NG], w[:, O_FQ:O_FF], small], axis=1)


def _even_layer(e, xb_p, xb_s, nb, t, ns, past, caches, page_table, wts):
    cache_cmp, cache_slc, cache_win, cache_fox, logf_pool_t = caches
    w_in, fox_f_bias, cmp_pos, cmp_w1, cmp_w2 = wts
    w_rep = _repack_even(w_in[e])
    fb_row = jnp.zeros((1, 128), F32).at[0, N_GATE:N_GATE + FOX_HEADS].set(fox_f_bias[e])
    hq = P_FQ // HEAD_DIM
    kvw = 2 * NSA_KV * HEAD_DIM
    nw = NSA_HEADS * HEAD_DIM

    h_p, h_s = matmul(xb_p, xb_s, w_rep, (), (1152, 640, 384, 128))
    tm_post = _pick(nb * t, (512, 256, 128))
    q_rot, slc_p, win_p, gates_p, logf_p = even_post(
        h_p, _rope_tables(jnp.arange(t)), fb_row, t // tm_post)
    c_p = cumsum_rows(logf_p, nb, t)
    c_heads = jnp.swapaxes(c_p[:, N_GATE:N_GATE + FOX_HEADS].reshape(nb, t, FOX_HEADS), 1, 2)
    n16 = t // CMP_STRIDE
    kvc_p = nsa_compress(
        h_p, pl.BlockSpec((t, HEAD_DIM), lambda b, c: (b, P_NKV // HEAD_DIM + c)), nb, t, e, cmp_w1, cmp_pos, cmp_w2)
    a_p = _sel_matrix(n16, n16 - 1, t // SEL_LEN, 128)
    o_c, sel = cmp_select(h_p, kvc_p, a_p, nb, t)
    o_s = flash_attention("slc", nb, t, q_rot, lambda g: g, slc_p, lambda g: g, slc_p, lambda g: NSA_KV + g, (sel,))
    o_w = flash_attention("win", nb, t, q_rot, lambda g: g, win_p, lambda g: g, win_p, lambda g: NSA_KV + g)
    fq0 = P_FQ // (NSA_GROUP * HEAD_DIM)
    fgroups = FOX_HEADS // NSA_GROUP
    o_f = flash_attention("fox", nb, t, h_p, lambda g: fq0 + g, h_p, lambda g: fq0 + fgroups + g,
                          h_p, lambda g: fq0 + 2 * fgroups + g, (c_heads[..., None], c_heads[:, :, None, :]))
    xcat_p = combine_heads(o_c, o_s, o_w, o_f, gates_p)
    st_p = (
        h_p[:, P_NKV:P_NKV + kvw].reshape(nb, t, 2, NSA_KV, HEAD_DIM),
        slc_p.reshape(nb, t, 2, NSA_KV, HEAD_DIM),
        win_p.reshape(nb, t, 2, NSA_KV, HEAD_DIM)[:, t - min(WINDOW, t):],
        h_p[:, P_FQ + nw:P_FQ + 3 * nw].reshape(nb, t, 2, FOX_HEADS, HEAD_DIM),
        logf_p[:, N_GATE:N_GATE + FOX_HEADS].reshape(nb, t, FOX_HEADS),
    )

    rows = xb_s.shape[0]
    q_rot_s, slc_s, win_s, gates_s, logf_s = even_post(
        h_s, _rope_tables(jnp.full((rows,), past)), fb_row, 1)
    new_win = jnp.concatenate(
        [cache_win[e].reshape(ns, -1, kvw), win_s[:ns, None, :]], axis=1)[:, 1:]
    cmp_rows = gather_cmp_pages(cache_cmp, e, page_table)
    l_tot = past + 1
    n16_s = l_tot // CMP_STRIDE
    assert n16_s * CMP_STRIDE == past
    kvc_s = nsa_compress(
        cmp_rows, pl.BlockSpec((None, None, past, HEAD_DIM), lambda b, c: (b, c, 0, 0)), ns, past, e,
        cmp_w1, cmp_pos, cmp_w2)
    n_sel_s = -(-l_tot // SEL_LEN)
    nsp = -(-n_sel_s // 128) * 128
    a_s = _sel_matrix(n16_s, n16_s - 1, n_sel_s, nsp)

    def q_rows(x):
        x = x[:ns].reshape(ns, NSA_KV, NSA_GROUP, HEAD_DIM)
        return jnp.pad(x, ((0, 0), (0, 0), (0, 8 - NSA_GROUP), (0, 0)))

    qraw_g, qrot_g = q_rows(h_s[:, :nw]), q_rows(q_rot_s)
    o_c_s, o_w_s, idx = sample_cmp_win(qraw_g, qrot_g, kvc_s, a_s, new_win, n16_s - 1, n_sel_s, past)
    idx2 = idx[:, :, 0, :min(SEL_TOP, n_sel_s)].reshape(ns * NSA_KV, -1)
    o_s_s = sample_slc(qrot_g, cache_slc, e, page_table, idx2, slc_s[:ns, None, :], past // SEL_LEN)
    fq = h_s[:ns, P_FQ:P_FQ + nw].reshape(ns, FOX_HEADS, HEAD_DIM)
    fk = h_s[:ns, P_FQ + nw:P_FQ + 2 * nw]
    fv = h_s[:ns, P_FQ + 2 * nw:P_FQ + 3 * nw]
    logf_new = logf_s[:ns, N_GATE:N_GATE + FOX_HEADS]
    o_f_s = sample_fox(fq, cache_fox, logf_pool_t, e, page_table, fk.reshape(ns, FOX_HEADS, HEAD_DIM),
                       fv.reshape(ns, FOX_HEADS, HEAD_DIM), logf_new[:, :, None])

    def heads_flat(x):
        return _pad_rows(x[:, :, :NSA_GROUP].reshape(ns, nw), rows)

    xcat_s = combine_heads(heads_flat(o_c_s), heads_flat(o_s_s), heads_flat(o_w_s),
                           _pad_rows(o_f_s.reshape(ns, nw), rows), gates_s)
    st_s = (
        h_s[:ns, P_NKV:P_NKV + kvw].reshape(ns, 1, 2, NSA_KV, HEAD_DIM),
        slc_s[:ns].reshape(ns, 1, 2, NSA_KV, HEAD_DIM),
        new_win.reshape(ns, -1, 2, NSA_KV, HEAD_DIM),
        h_s[:ns, P_FQ + nw:P_FQ + 3 * nw].reshape(ns, 1, 2, FOX_HEADS, HEAD_DIM),
        logf_new.reshape(ns, 1, FOX_HEADS),
    )
    return xcat_p, xcat_s, st_p, st_s


def _odd_layer(o, xb_p, xb_s, nb, t, ns, past, state_ret, w_in_odd, ret_gn_g):
    h_p, h_s = matmul(xb_p, xb_s, w_in_odd, (o,), (1024, 512, 256, 128))
    y_p, s_p = retention_prompt(h_p, nb, t, ret_gn_g, o)
    rows = xb_s.shape[0]
    h_s = h_s[:ns]
    q_col = h_s[:, :O_RK].reshape(ns, RET_HEADS, RET_DK, 1)
    k_col = h_s[:, O_RK:O_RV].reshape(ns, RET_HEADS, RET_DK, 1)
    v_row = h_s[:, O_RV:O_RG].reshape(ns, RET_HEADS, 1, RET_DV)
    g_row = h_s[:, O_RG:E_ODD].reshape(ns, RET_HEADS, 1, RET_DV)
    y_s, s_s = retention_step(q_col, k_col, v_row, g_row, jnp.full((1,), past), state_ret, o, ret_gn_g)
    y_s = _pad_rows(y_s.reshape(ns, RET_HEADS * RET_DV), rows).astype(MXU_DTYPE)
    return y_p, y_s, s_p, s_s


def kernel(x_prompt, x_sample, cache_nsa_cmp, cache_nsa_slc, cache_nsa_win, cache_fox_kv, cache_fox_logf,
           state_ret, page_table, w_ffn_in, w_ffn_out, ln_g, ln_b, w_in_even, w_out_even, fox_f_bias,
           nsa_cmp_pos, nsa_cmp_w1, nsa_cmp_w2, w_in_odd, ret_gn_g, w_out_odd):
    nb, t, d = x_prompt.shape
    ns, ts, _ = x_sample.shape
    depth = w_ffn_in.shape[0]
    n_pages = page_table.shape[1]
    page = cache_nsa_cmp.shape[2]
    past = n_pages * page
    assert ts == 1 and past >= WINDOW and past % SEL_LEN == 0 and t % RET_CHUNK == 0
    alpha = (2.0 * depth) ** 0.25
    rows = max(16, -(-ns // 16) * 16)

    xp = x_prompt.reshape(nb * t, d)
    xs = _pad_rows(x_sample.reshape(ns * ts, d), rows)
    xp_b, xs_b = xp.astype(MXU_DTYPE), xs.astype(MXU_DTYPE)

    caches = (cache_nsa_cmp, cache_nsa_slc, cache_nsa_win, cache_fox_kv, jnp.swapaxes(cache_fox_logf, 2, 3))
    w_ffn_out, w_out_even, w_out_odd = _mx(w_ffn_out), _mx(w_out_even), _mx(w_out_odd)

    def ffn(xp, xp_b, xs, xs_b, l, s):
        hid_p, hid_s = swiglu_hidden(xp_b, xs_b, w_ffn_in, l, s)
        return (matmul_postnorm(hid_p, w_ffn_out, (l, s), xp, ln_g, ln_b, (l, 2 * s), alpha, 0.5)
                + matmul_postnorm(hid_s, w_ffn_out, (l, s), xs, ln_g, ln_b, (l, 2 * s), alpha, 0.5))

    new_p = [[] for _ in range(6)]
    new_s = [[] for _ in range(6)]
    for l in range(depth):
        xp, xp_b, xs, xs_b = ffn(xp, xp_b, xs, xs_b, l, 0)
        if l % 2 == 0:
            e = l // 2
            mp, ms, st_p, st_s = _even_layer(
                e, xp_b, xs_b, nb, t, ns, past, caches, page_table,
                (w_in_even, fox_f_bias, nsa_cmp_pos, nsa_cmp_w1, nsa_cmp_w2))
            w_out, widx = w_out_even, (e,)
            slots = (0, 1, 2, 3, 4)
        else:
            o = l // 2
            mp, ms, sp, ss = _odd_layer(o, xp_b, xs_b, nb, t, ns, past, state_ret, w_in_odd, ret_gn_g)
            st_p, st_s = (sp,), (ss,)
            w_out, widx = w_out_odd, (o,)
            slots = (5,)
        for i, a_p, a_s in zip(slots, st_p, st_s):
            new_p[i].append(a_p)
            new_s[i].append(a_s)
        xp, xp_b = matmul_postnorm(mp, w_out, widx, xp, ln_g, ln_b, (l, 1), alpha, 1.0)
        xs, xs_b = matmul_postnorm(ms, w_out, widx, xs, ln_g, ln_b, (l, 1), alpha, 1.0)
        xp, xp_b, xs, xs_b = ffn(xp, xp_b, xs, xs_b, l, 1)

    outs = [xp.reshape(nb, t, d), xs[:ns].reshape(ns, ts, d)]
    for i in range(6):
        outs.append(jnp.stack(new_p[i]))
        outs.append(jnp.stack(new_s[i]))
    return tuple(outs)
```

```python
import functools
import math

import numpy as np
import jax
import jax.numpy as jnp
from jax import lax
from jax.experimental import pallas as pl
from jax.experimental.pallas import tpu as pltpu

F32 = jnp.float32
MXU_DTYPE = jnp.bfloat16

HEAD_DIM = 128
NSA_HEADS = 8
NSA_KV = 2
NSA_GROUP = NSA_HEADS // NSA_KV
CMP_LEN = 32
CMP_STRIDE = 16
CMP_HIDDEN = 256
SEL_LEN = 64
SEL_TOP = 16
WINDOW = 512
FOX_HEADS = 8
ROPE_THETA = 500000.0
ROPE_DIMS = HEAD_DIM // 4
RET_HEADS = 8
RET_DK = 256
RET_DV = 512
RET_CHUNK = 128
RET_THETA = 10000.0
RET_HEADS_PER_STEP = 4
LN_EPS = 1e-5
GN_EPS = 1e-6

O_NQ = 0
O_NKV = O_NQ + NSA_HEADS * HEAD_DIM
O_NG = O_NKV + 6 * NSA_KV * HEAD_DIM
O_FQ = O_NG + 3 * NSA_HEADS
O_FK = O_FQ + FOX_HEADS * HEAD_DIM
O_FV = O_FK + FOX_HEADS * HEAD_DIM
O_FF = O_FV + FOX_HEADS * HEAD_DIM
E_EVEN = O_FF + FOX_HEADS
P_NKV = NSA_HEADS * HEAD_DIM
P_FQ = P_NKV + 6 * NSA_KV * HEAD_DIM
P_SMALL = P_FQ + 3 * FOX_HEADS * HEAD_DIM
P_EVEN = P_SMALL + 128
N_GATE = 3 * NSA_HEADS
O_RK = RET_HEADS * RET_DK
O_RV = 2 * RET_HEADS * RET_DK
O_RG = O_RV + RET_HEADS * RET_DV
E_ODD = O_RG + RET_HEADS * RET_DV

V7X_VMEM_LIMIT = 56 * 1024 * 1024
LN_ROWS = 128
LOG2E = 1.4426950408889634
NEG_BIG = -1e30
NT_DIMS = (((1,), (1,)), ((), ()))
TN_DIMS = (((0,), (0,)), ((), ()))


def _params(*sem):
    return pltpu.CompilerParams(dimension_semantics=sem, vmem_limit_bytes=V7X_VMEM_LIMIT)


def _mx(x):
    return x.astype(MXU_DTYPE)


def _pick(n, prefs):
    for p in prefs:
        if n % p == 0:
            return p
    return n


def _swiglu_kernel(x_ref, xs_ref, wa_ref, wb_ref, o_ref, os_ref, wa_s, wb_s):
    def hidden(x):
        a = jnp.dot(x, wa_s[...], preferred_element_type=F32)
        b = jnp.dot(x, wb_s[...], preferred_element_type=F32)
        return (a * jax.nn.sigmoid(a) * b).astype(o_ref.dtype)

    @pl.when(pl.program_id(1) == 0)
    def _():
        wa_s[...] = _mx(wa_ref[...])
        wb_s[...] = _mx(wb_ref[...])
        os_ref[...] = hidden(xs_ref[...])

    o_ref[...] = hidden(x_ref[...])


def swiglu_hidden(xb, xsb, w_in, l, s):
    m, d = xb.shape
    ms = xsb.shape[0]
    f = w_in.shape[-1] // 2
    tm = _pick(m, (1024, 512, 256, 128))
    tn = _pick(f, (512, 256, 128))
    nj = f // tn
    return pl.pallas_call(
        _swiglu_kernel,
        grid=(nj, m // tm),
        in_specs=[
            pl.BlockSpec((tm, d), lambda j, i: (i, 0)),
            pl.BlockSpec((ms, d), lambda j, i: (0, 0)),
            pl.BlockSpec((None, None, d, tn), lambda j, i: (l, s, 0, j)),
            pl.BlockSpec((None, None, d, tn), lambda j, i: (l, s, 0, j + nj)),
        ],
        out_specs=[pl.BlockSpec((tm, tn), lambda j, i: (i, j)), pl.BlockSpec((ms, tn), lambda j, i: (0, j))],
        out_shape=[jax.ShapeDtypeStruct((m, f), MXU_DTYPE), jax.ShapeDtypeStruct((ms, f), MXU_DTYPE)],
        scratch_shapes=[pltpu.VMEM((d, tn), MXU_DTYPE), pltpu.VMEM((d, tn), MXU_DTYPE)],
        compiler_params=_params("arbitrary", "arbitrary"),
        name="swiglu_hidden",
    )(xb, xsb, w_in, w_in)


def _mm_ln_kernel(x_ref, w_ref, r_ref, g_ref, b_ref, o_ref, ob_ref, *, alpha, scale, nk):
    k = pl.program_id(1)
    def part():
        return jnp.dot(x_ref[...], _mx(w_ref[...]), preferred_element_type=F32)

    @pl.when(k == 0)
    def _():
        o_ref[...] = part()

    @pl.when(k > 0)
    def _():
        o_ref[...] += part()

    @pl.when(k == nk - 1)
    def _():
        rows = min(LN_ROWS, o_ref.shape[0])

        def norm_rows(c, carry):
            sl = pl.ds(pl.multiple_of(c * rows, rows), rows)
            z = alpha * r_ref[sl, :] + scale * o_ref[sl, :]
            mu = jnp.mean(z, axis=-1, keepdims=True)
            dz = z - mu
            var = jnp.mean(dz * dz, axis=-1, keepdims=True)
            y = dz * lax.rsqrt(var + LN_EPS) * g_ref[...] + b_ref[...]
            o_ref[sl, :] = y
            ob_ref[sl, :] = y.astype(ob_ref.dtype)
            return carry

        lax.fori_loop(0, o_ref.shape[0] // rows, norm_rows, 0)


def matmul_postnorm(xb, w, widx, res, g, b, gidx, alpha, scale):
    m, kdim = xb.shape
    n = w.shape[-1]
    tm = _pick(m, (512, 256, 128))
    tk = _pick(kdim, (2816, 2048, 1024, 512, 256, 128))
    nk = kdim // tk
    nlead = len(widx)
    w_spec = pl.BlockSpec((None,) * nlead + (tk, n), lambda i, k: tuple(widx) + (k, 0))
    g_spec = pl.BlockSpec((None, None, 1, n), lambda i, k: tuple(gidx) + (0, 0))
    return pl.pallas_call(
        functools.partial(_mm_ln_kernel, alpha=alpha, scale=scale, nk=nk),
        grid=(m // tm, nk),
        in_specs=[
            pl.BlockSpec((tm, tk), lambda i, k: (i, k)),
            w_spec,
            pl.BlockSpec((tm, n), lambda i, k: (i, 0)),
            g_spec, g_spec,
        ],
        out_specs=[pl.BlockSpec((tm, n), lambda i, k: (i, 0)), pl.BlockSpec((tm, n), lambda i, k: (i, 0))],
        out_shape=[jax.ShapeDtypeStruct((m, n), F32), jax.ShapeDtypeStruct((m, n), MXU_DTYPE)],
        compiler_params=_params("arbitrary", "arbitrary"),
        name="matmul_postnorm",
    )(xb, w, res, g[:, :, None, :], b[:, :, None, :])


def _mm_kernel(x_ref, xs_ref, w_ref, o_ref, os_ref, w_s):
    @pl.when(pl.program_id(1) == 0)
    def _():
        w_s[...] = _mx(w_ref[...])
        os_ref[...] = jnp.dot(xs_ref[...], w_s[...], preferred_element_type=F32)
    o_ref[...] = jnp.dot(x_ref[...], w_s[...], preferred_element_type=F32)


def matmul(xb, xsb, w, widx, tn_prefs):
    m, kdim = xb.shape
    ms = xsb.shape[0]
    n = w.shape[-1]
    tm = _pick(m, (1024, 512, 256, 128))
    tn = _pick(n, tn_prefs)
    nlead = len(widx)
    return pl.pallas_call(
        _mm_kernel,
        grid=(n // tn, m // tm),
        in_specs=[
            pl.BlockSpec((tm, kdim), lambda j, i: (i, 0)),
            pl.BlockSpec((ms, kdim), lambda j, i: (0, 0)),
            pl.BlockSpec((None,) * nlead + (kdim, tn), lambda j, i: tuple(widx) + (0, j)),
        ],
        out_specs=[pl.BlockSpec((tm, tn), lambda j, i: (i, j)), pl.BlockSpec((ms, tn), lambda j, i: (0, j))],
        out_shape=[jax.ShapeDtypeStruct((m, n), F32), jax.ShapeDtypeStruct((ms, n), F32)],
        scratch_shapes=[pltpu.VMEM((kdim, tn), MXU_DTYPE)],
        compiler_params=_params("arbitrary", "arbitrary"),
        name="matmul",
    )(xb, xsb, w)


def _rope_tables(pos):
    half = ROPE_DIMS // 2
    inv = 1.0 / (ROPE_THETA ** (jnp.arange(half, dtype=F32) / half))
    ang = pos.astype(F32)[:, None] * inv[None, :]
    cos, sin = jnp.cos(ang), jnp.sin(ang)
    n = pos.shape[0]
    ones = jnp.ones((n, HEAD_DIM - ROPE_DIMS), F32)
    zeros = jnp.zeros((n, HEAD_DIM - ROPE_DIMS), F32)
    zh = jnp.zeros((n, half), F32)
    c = jnp.concatenate([cos, cos, ones], -1)
    a = jnp.concatenate([-sin, zh, zeros], -1)
    b = jnp.concatenate([zh, sin, zeros], -1)
    return c, a, b


def _rope(x, c, a, b):
    half = ROPE_DIMS // 2
    return x * c + pltpu.roll(x, HEAD_DIM - half, 1) * a + pltpu.roll(x, half, 1) * b


def _even_post_kernel(q_ref, slc_ref, win_ref, sm_ref, c_ref, a_ref, b_ref, fb_ref,
                      qr_ref, slco_ref, wino_ref, gate_ref, logf_ref):
    c, a, b = c_ref[...], a_ref[...], b_ref[...]
    for h in range(NSA_HEADS):
        sl = slice(h * HEAD_DIM, (h + 1) * HEAD_DIM)
        qr_ref[:, sl] = _rope(q_ref[:, sl], c, a, b)
    for src, dst in ((slc_ref, slco_ref), (win_ref, wino_ref)):
        for g in range(NSA_KV):
            sl = slice(g * HEAD_DIM, (g + 1) * HEAD_DIM)
            dst[:, sl] = _rope(src[:, sl], c, a, b)
        vs = slice(NSA_KV * HEAD_DIM, 2 * NSA_KV * HEAD_DIM)
        dst[:, vs] = src[:, vs]
    sm = sm_ref[...]
    gate_ref[...] = jax.nn.sigmoid(sm)
    z = sm + fb_ref[...]
    logf_ref[...] = jnp.minimum(z, 0.0) - jnp.log1p(jnp.exp(-jnp.abs(z)))


def even_post(h, tabs, fb_row, npos_blocks):
    m = h.shape[0]
    tm = _pick(m, (512, 256, 128))
    kvw = 2 * NSA_KV * HEAD_DIM
    tab_spec = pl.BlockSpec((tm, HEAD_DIM), lambda i: (i % npos_blocks, 0))
    return pl.pallas_call(
        _even_post_kernel,
        grid=(m // tm,),
        in_specs=[
            pl.BlockSpec((tm, P_NKV), lambda i: (i, 0)),
            pl.BlockSpec((tm, kvw), lambda i: (i, (P_NKV + kvw) // kvw)),
            pl.BlockSpec((tm, kvw), lambda i: (i, (P_NKV + 2 * kvw) // kvw)),
            pl.BlockSpec((tm, 128), lambda i: (i, P_SMALL // 128)),
            tab_spec, tab_spec, tab_spec,
            pl.BlockSpec((1, 128), lambda i: (0, 0)),
        ],
        out_specs=[
            pl.BlockSpec((tm, P_NKV), lambda i: (i, 0)),
            pl.BlockSpec((tm, kvw), lambda i: (i, 0)),
            pl.BlockSpec((tm, kvw), lambda i: (i, 0)),
            pl.BlockSpec((tm, 128), lambda i: (i, 0)),
            pl.BlockSpec((tm, 128), lambda i: (i, 0)),
        ],
        out_shape=[
            jax.ShapeDtypeStruct((m, P_NKV), F32),
            jax.ShapeDtypeStruct((m, kvw), F32),
            jax.ShapeDtypeStruct((m, kvw), F32),
            jax.ShapeDtypeStruct((m, 128), F32),
            jax.ShapeDtypeStruct((m, 128), F32),
        ],
        compiler_params=_params("arbitrary"),
        name="even_post",
    )(h, h, h, h, *tabs, fb_row)


def _cumsum_kernel(x_ref, tri_ref, o_ref, carry):
    @pl.when(pl.program_id(1) == 0)
    def _():
        carry[...] = jnp.zeros_like(carry)
    c = jnp.dot(tri_ref[...], x_ref[...], preferred_element_type=F32,
                precision=lax.Precision.HIGHEST) + carry[...]
    o_ref[...] = c
    carry[...] = c[-1:, :]


def cumsum_rows(x, nb, t):
    tc = _pick(t, (512, 256, 128))
    tri = jnp.tril(jnp.ones((tc, tc), F32))
    nt = t // tc
    return pl.pallas_call(
        _cumsum_kernel,
        grid=(nb, nt),
        in_specs=[pl.BlockSpec((tc, 128), lambda b, i: (b * nt + i, 0)),
                  pl.BlockSpec((tc, tc), lambda b, i: (0, 0))],
        out_specs=pl.BlockSpec((tc, 128), lambda b, i: (b * nt + i, 0)),
        out_shape=jax.ShapeDtypeStruct(x.shape, F32),
        scratch_shapes=[pltpu.VMEM((1, 128), F32)],
        compiler_params=_params("arbitrary", "arbitrary"),
        name="cumsum_rows",
    )(x, tri)


def _compress_kernel(x_ref, w1_ref, pe_ref, w2_ref, o_ref, u0, u1, *, n16):
    u0[...] = jnp.zeros_like(u0)
    u1[...] = jnp.zeros_like(u1)
    for r in range(0, CMP_STRIDE, 2):
        xa = x_ref[pl.ds(r, n16, stride=CMP_STRIDE), :]
        xb = x_ref[pl.ds(r + 1, n16, stride=CMP_STRIDE), :]
        for u, j in ((u0, r), (u1, CMP_STRIDE + r)):
            lhs = jnp.concatenate([_mx(xa + pe_ref[j:j + 1, :]), _mx(xb + pe_ref[j + 1:j + 2, :])], axis=1)
            w = _mx(w1_ref[j:j + 2]).reshape(2 * HEAD_DIM, CMP_HIDDEN)
            u[...] += jnp.dot(lhs, w, preferred_element_type=F32)
    pre = u0[...] + pltpu.roll(u1[...], n16 - 1, 0)
    out = jnp.dot(_mx(jax.nn.gelu(pre)), _mx(w2_ref[...]), preferred_element_type=F32)
    row = lax.broadcasted_iota(jnp.int32, out.shape, 0)
    o_ref[...] = jnp.where(row < n16 - 1, out, 0.0)


def nsa_compress(x, x_spec, nb, length, e, w1, pe, w2):
    n16 = length // CMP_STRIDE
    w1r = w1.reshape(w1.shape[0], 2, CMP_LEN, HEAD_DIM, CMP_HIDDEN)
    return pl.pallas_call(
        functools.partial(_compress_kernel, n16=n16),
        grid=(nb, 2 * NSA_KV),
        in_specs=[
            x_spec,
            pl.BlockSpec((None, None, CMP_LEN, HEAD_DIM, CMP_HIDDEN), lambda b, c: (e, c // NSA_KV, 0, 0, 0)),
            pl.BlockSpec((None, None, CMP_LEN, HEAD_DIM), lambda b, c: (e, c // NSA_KV, 0, 0)),
            pl.BlockSpec((None, None, CMP_HIDDEN, HEAD_DIM), lambda b, c: (e, c // NSA_KV, 0, 0)),
        ],
        out_specs=pl.BlockSpec((None, None, n16, HEAD_DIM), lambda b, c: (b, c, 0, 0)),
        out_shape=jax.ShapeDtypeStruct((nb, 2 * NSA_KV, n16, HEAD_DIM), F32),
        scratch_shapes=[pltpu.VMEM((n16, CMP_HIDDEN), F32), pltpu.VMEM((n16, CMP_HIDDEN), F32)],
        compiler_params=_params("arbitrary", "arbitrary"),
        name="nsa_compress",
    )(x, w1r, pe, w2)


def _sel_matrix(n_c_pad, n_c, n_sel, width):
    ratio = SEL_LEN // CMP_STRIDE
    i = np.arange(n_c_pad)[:, None]
    j = np.arange(width)[None, :]
    a = (i >= ratio * j - 1) & (i <= ratio * j + ratio - 1) & (i < n_c) & (j < n_sel)
    return jnp.asarray(a.astype(np.float32))


def _masked_softmax(lg, mask):
    lg = jnp.where(mask, lg, -jnp.inf)
    m = jnp.max(lg, axis=-1, keepdims=True)
    m = jnp.where(m == -jnp.inf, 0.0, m)
    p = jnp.exp(lg - m)
    s = jnp.sum(p, axis=-1, keepdims=True)
    return p / jnp.where(s > 0, s, 1.0)


def _cmp_select_kernel(q_ref, kc_ref, vc_ref, a_ref, oc_ref, sel_ref, *, tq, n_sel, n_top):
    i = pl.program_id(2)
    ncp = kc_ref.shape[0]
    qpos = i * tq + lax.broadcasted_iota(jnp.int32, (tq, 1), 0)
    cend = lax.broadcasted_iota(jnp.int32, (1, ncp), 1) * CMP_STRIDE + CMP_LEN
    cmask = cend <= qpos + 1
    kc = _mx(kc_ref[...])
    vc = _mx(vc_ref[...])
    scale = HEAD_DIM ** -0.5
    imp = jnp.zeros((tq, ncp), F32)
    for m in range(NSA_GROUP):
        sl = slice(m * HEAD_DIM, (m + 1) * HEAD_DIM)
        lg = lax.dot_general(_mx(q_ref[:, sl]), kc, NT_DIMS, preferred_element_type=F32) * scale
        p = _masked_softmax(lg, cmask)
        oc_ref[:, sl] = jnp.dot(_mx(p), vc, preferred_element_type=F32)
        imp = imp + p
    s_sel = jnp.dot(imp, a_ref[...], preferred_element_type=F32, precision=lax.Precision.HIGHEST)
    st = s_sel.T[:n_sel, :]
    blk = lax.broadcasted_iota(jnp.int32, (n_sel, tq), 0)
    cur = (i * tq + lax.broadcasted_iota(jnp.int32, (n_sel, tq), 1)) // SEL_LEN
    forced = (blk == 0) | (blk == cur) | (blk == cur - 1)
    allowed = blk <= cur
    v = jnp.where(allowed, jnp.where(forced, jnp.inf, st), -jnp.inf)
    rank = jnp.zeros((n_sel, tq), jnp.int32)
    for r in range(n_sel):
        vr = v[r:r + 1, :]
        before = (vr > v) | ((vr == v) & (blk > r))
        rank = rank + before.astype(jnp.int32)
    sel = ((rank < n_top) & allowed).astype(F32)
    if n_sel < 128:
        sel = jnp.concatenate([sel, jnp.zeros((128 - n_sel, tq), F32)], axis=0)
    sel_ref[...] = sel.T.astype(sel_ref.dtype)


def cmp_select(h, kvc, a_mat, nb, t):
    tq = _pick(t, (256, 128))
    nt = t // tq
    n_sel = t // SEL_LEN
    assert n_sel <= 128 and n_sel % 8 == 0
    ncp = kvc.shape[2]
    gw = NSA_GROUP * HEAD_DIM
    return pl.pallas_call(
        functools.partial(_cmp_select_kernel, tq=tq, n_sel=n_sel, n_top=min(SEL_TOP, n_sel)),
        grid=(nb, NSA_KV, nt),
        in_specs=[
            pl.BlockSpec((tq, gw), lambda b, g, i: (b * nt + i, g)),
            pl.BlockSpec((None, None, ncp, HEAD_DIM), lambda b, g, i: (b, g, 0, 0)),
            pl.BlockSpec((None, None, ncp, HEAD_DIM), lambda b, g, i: (b, NSA_KV + g, 0, 0)),
            pl.BlockSpec((ncp, 128), lambda b, g, i: (0, 0)),
        ],
        out_specs=[
            pl.BlockSpec((tq, gw), lambda b, g, i: (b * nt + i, g)),
            pl.BlockSpec((None, None, tq, 128), lambda b, g, i: (b, g, i, 0)),
        ],
        out_shape=[
            jax.ShapeDtypeStruct((nb * t, NSA_HEADS * HEAD_DIM), F32),
            jax.ShapeDtypeStruct((nb, NSA_KV, t, 128), MXU_DTYPE),
        ],
        compiler_params=_params("arbitrary", "arbitrary", "arbitrary"),
        name="cmp_select",
    )(h, kvc, kvc, a_mat)


def _flash_steps(mode, t, tq, tk):
    rows = []
    for qi in range(t // tq):
        q_lo, q_hi = qi * tq, qi * tq + tq - 1
        k_hi = q_hi // tk
        k_lo = max((q_lo - WINDOW + 1) // tk, 0) if mode == "win" else 0
        for kt in range(k_lo, k_hi + 1):
            every_key_visible = kt * tk + tk - 1 <= q_lo
            rows.append((qi, kt, int(kt == k_lo), int(kt == k_hi), int(not every_key_visible)))
    return jnp.asarray(np.array(rows, np.int32).T)


def _flash_kernel(*refs, mode, tq, tk, heads):
    if mode == "fox":
        tab, q_ref, k_ref, v_ref, cq_ref, ck_ref, o_ref, m_s, acc = refs
    elif mode == "slc":
        tab, q_ref, k_ref, v_ref, sel_ref, o_ref, m_s, acc = refs
    else:
        tab, q_ref, k_ref, v_ref, o_ref, m_s, acc = refs
    step = pl.program_id(2)
    qi, kt = tab[0, step], tab[1, step]

    @pl.when(tab[2, step] == 1)
    def _():
        m_s[...] = jnp.full_like(m_s, NEG_BIG)
        acc[...] = jnp.zeros_like(acc)

    def update(causal):
        ones = jnp.ones((tk, HEAD_DIM), MXU_DTYPE)
        if mode != "fox":
            kb = _mx(k_ref[...])
            v_aug = jnp.concatenate([_mx(v_ref[...]), ones], axis=1)
        valid = None
        if causal or mode == "win":
            qpos = qi * tq + lax.broadcasted_iota(jnp.int32, (tq, tk), 0)
            kpos = kt * tk + lax.broadcasted_iota(jnp.int32, (tq, tk), 1)
            valid = kpos <= qpos
            if mode == "win":
                valid = valid & (kpos > qpos - WINDOW)
        if mode == "slc":
            blk = lax.broadcasted_iota(jnp.int32, (128, tk), 0)
            kblk = (kt * tk + lax.broadcasted_iota(jnp.int32, (128, tk), 1)) // SEL_LEN
            chosen = jnp.dot(sel_ref[...], (blk == kblk).astype(MXU_DTYPE), preferred_element_type=F32) > 0.5
            valid = chosen if valid is None else valid & chosen
        for hh in range(heads):
            cols = slice(hh * HEAD_DIM, (hh + 1) * HEAD_DIM)
            if mode == "fox":
                kb = _mx(k_ref[:, cols])
                v_aug = jnp.concatenate([_mx(v_ref[:, cols]), ones], axis=1)
                ck2 = ck_ref[hh] * LOG2E
                cq2 = cq_ref[hh] * LOG2E
            s = lax.dot_general(_mx(q_ref[:, cols]), kb, NT_DIMS, preferred_element_type=F32)
            x = s * (HEAD_DIM ** -0.5 * LOG2E)
            if mode == "fox":
                x = x - ck2
            if valid is not None:
                x = jnp.where(valid, x, NEG_BIG)
            top = jnp.max(x, axis=-1, keepdims=True)
            if mode == "fox":
                top = top + cq2
            m_old = m_s[hh]
            m_new = jnp.maximum(m_old, top)
            p = jnp.exp2(x - (m_new - cq2 if mode == "fox" else m_new))
            if mode == "win":
                p = jnp.where(valid, p, 0.0)
            acc[hh] = jnp.exp2(m_old - m_new) * acc[hh] + jnp.dot(_mx(p), v_aug, preferred_element_type=F32)
            m_s[hh] = m_new

    if mode == "win":
        update(True)
    else:
        pl.when(tab[4, step] == 1)(functools.partial(update, True))
        pl.when(tab[4, step] == 0)(functools.partial(update, False))

    @pl.when(tab[3, step] == 1)
    def _():
        for hh in range(heads):
            l = acc[hh, :, HEAD_DIM:]
            o_ref[:, hh * HEAD_DIM:(hh + 1) * HEAD_DIM] = acc[hh, :, :HEAD_DIM] / jnp.where(l > 0, l, 1.0)


def flash_attention(mode, nb, t, q, qcol, k, kcol, v, vcol, extra=()):
    heads = NSA_GROUP
    tk = _pick(t, (512,) if mode == "win" else (1024, 512, 256, 128))
    tq = _pick(t, (512, 256, 128))
    ntq, ntk = t // tq, t // tk
    qw = heads * HEAD_DIM
    kw = qw if mode == "fox" else HEAD_DIM
    steps = _flash_steps(mode, t, tq, tk)
    in_specs = [
        pl.BlockSpec((tq, qw), lambda b, h, s, tab: (b * ntq + tab[0, s], qcol(h))),
        pl.BlockSpec((tk, kw), lambda b, h, s, tab: (b * ntk + tab[1, s], kcol(h))),
        pl.BlockSpec((tk, kw), lambda b, h, s, tab: (b * ntk + tab[1, s], vcol(h))),
    ]
    scratch = [pltpu.VMEM((heads, tq, 1), F32), pltpu.VMEM((heads, tq, 2 * HEAD_DIM), F32)]
    if mode == "fox":
        in_specs += [
            pl.BlockSpec((None, heads, tq, 1), lambda b, h, s, tab: (b, h, tab[0, s], 0)),
            pl.BlockSpec((None, heads, 1, tk), lambda b, h, s, tab: (b, h, 0, tab[1, s])),
        ]
    elif mode == "slc":
        in_specs += [pl.BlockSpec((None, None, tq, 128), lambda b, h, s, tab: (b, h, tab[0, s], 0))]
    return pl.pallas_call(
        functools.partial(_flash_kernel, mode=mode, tq=tq, tk=tk, heads=heads),
        grid_spec=pltpu.PrefetchScalarGridSpec(
            num_scalar_prefetch=1,
            grid=(nb, NSA_HEADS // heads, steps.shape[1]),
            in_specs=in_specs,
            out_specs=pl.BlockSpec((tq, qw), lambda b, h, s, tab: (b * ntq + tab[0, s], h)),
            scratch_shapes=scratch,
        ),
        out_shape=jax.ShapeDtypeStruct((nb * t, NSA_HEADS * HEAD_DIM), F32),
        compiler_params=_params("arbitrary", "arbitrary", "arbitrary"),
        name="flash_" + mode,
    )(steps, q, k, v, *extra)


def _combine_kernel(oc_ref, os_ref, ow_ref, of_ref, g_ref, o_ref):
    gates = g_ref[...]
    nw = NSA_HEADS * HEAD_DIM
    for h in range(NSA_HEADS):
        sl = slice(h * HEAD_DIM, (h + 1) * HEAD_DIM)
        o = (gates[:, 3 * h:3 * h + 1] * oc_ref[:, sl] + gates[:, 3 * h + 1:3 * h + 2] * os_ref[:, sl]
             + gates[:, 3 * h + 2:3 * h + 3] * ow_ref[:, sl])
        o_ref[:, sl] = o.astype(o_ref.dtype)
    o_ref[:, nw:] = of_ref[...].astype(o_ref.dtype)


def combine_heads(o_c, o_s, o_w, o_f, gates):
    m, nw = o_c.shape
    tm = _pick(m, (512, 256, 128))
    spec = pl.BlockSpec((tm, nw), lambda i: (i, 0))
    return pl.pallas_call(
        _combine_kernel,
        grid=(m // tm,),
        in_specs=[spec, spec, spec, spec, pl.BlockSpec((tm, 128), lambda i: (i, 0))],
        out_specs=pl.BlockSpec((tm, 2 * nw), lambda i: (i, 0)),
        out_shape=jax.ShapeDtypeStruct((m, 2 * nw), MXU_DTYPE),
        compiler_params=_params("arbitrary"),
        name="combine_heads",
    )(o_c, o_s, o_w, o_f, gates)


GATHER_SLOTS = 16


def _gather_pages_kernel(pt_ref, pool_ref, out_ref, sems, *, e, n_pages, page):
    b = pl.program_id(0)
    n_streams = 2 * NSA_KV
    n_copies = n_pages * n_streams

    def page_copy(i):
        p = i // n_streams
        c = i % n_streams
        return pltpu.make_async_copy(
            pool_ref.at[e, pt_ref[b, p], :, c // NSA_KV, c % NSA_KV, :],
            out_ref.at[b, c, pl.ds(p * page, page), :],
            sems.at[i % GATHER_SLOTS])

    def issue(i, carry):
        @pl.when(i >= GATHER_SLOTS)
        def _():
            page_copy(i - GATHER_SLOTS).wait()
        page_copy(i).start()
        return carry

    lax.fori_loop(0, n_copies, issue, 0)

    def drain(i, carry):
        page_copy(i).wait()
        return carry

    lax.fori_loop(max(n_copies - GATHER_SLOTS, 0), n_copies, drain, 0)


def gather_cmp_pages(pool, e, page_table):
    nb, n_pages = page_table.shape
    page = pool.shape[2]
    assert n_pages * 2 * NSA_KV >= GATHER_SLOTS
    return pl.pallas_call(
        functools.partial(_gather_pages_kernel, e=e, n_pages=n_pages, page=page),
        grid_spec=pltpu.PrefetchScalarGridSpec(
            num_scalar_prefetch=1,
            grid=(nb,),
            in_specs=[pl.BlockSpec(memory_space=pl.ANY)],
            out_specs=pl.BlockSpec(memory_space=pl.ANY),
            scratch_shapes=[pltpu.SemaphoreType.DMA((GATHER_SLOTS,))],
        ),
        out_shape=jax.ShapeDtypeStruct((nb, 2 * NSA_KV, n_pages * page, HEAD_DIM), F32),
        compiler_params=_params("arbitrary"),
        name="gather_cmp_pages",
    )(page_table, pool)


def _sample_cmp_win_kernel(qraw_ref, qrot_ref, kc_ref, vc_ref, a_ref, kw_ref, vw_ref,
                           oc_ref, ow_ref, idx_ref, *, n_c, n_sel, n_top, q_pos):
    scale = HEAD_DIM ** -0.5
    ncp = kc_ref.shape[0]
    rows = qraw_ref.shape[0]
    cend = lax.broadcasted_iota(jnp.int32, (1, ncp), 1) * CMP_STRIDE + CMP_LEN
    cidx = lax.broadcasted_iota(jnp.int32, (1, ncp), 1)
    cmask = (cend <= q_pos + 1) & (cidx < n_c)
    lg = lax.dot_general(_mx(qraw_ref[...]), _mx(kc_ref[...]), NT_DIMS, preferred_element_type=F32) * scale
    p = _masked_softmax(lg, cmask)
    oc_ref[...] = jnp.dot(_mx(p), _mx(vc_ref[...]), preferred_element_type=F32)
    head = lax.broadcasted_iota(jnp.int32, p.shape, 0)
    imp = jnp.sum(jnp.where(head < NSA_GROUP, p, 0.0), axis=0, keepdims=True)
    imp = jnp.broadcast_to(imp, (rows, ncp))
    s_sel = jnp.dot(imp, a_ref[...], preferred_element_type=F32, precision=lax.Precision.HIGHEST)
    nsp = s_sel.shape[1]
    lane = lax.broadcasted_iota(jnp.int32, (rows, nsp), 1).astype(F32)
    cur = q_pos // SEL_LEN
    forced = (lane == 0.0) | (lane == float(cur)) | (lane == float(cur - 1))
    v = jnp.where(lane <= float(cur), jnp.where(forced, jnp.inf, s_sel), -jnp.inf)
    out_lane = lax.broadcasted_iota(jnp.int32, (rows, 128), 1)
    picked = jnp.full((rows, 128), -1.0, F32)
    for r in range(n_top):
        mx = jnp.max(v, axis=1, keepdims=True)
        ix = jnp.min(jnp.where(v == mx, lane, float(nsp)), axis=1, keepdims=True)
        ix = jnp.where(mx > -jnp.inf, ix, -1.0)
        picked = jnp.where(out_lane == r, ix, picked)
        v = jnp.where(lane == ix, -jnp.inf, v)
    idx_ref[...] = picked.astype(jnp.int32)
    lw = lax.dot_general(_mx(qrot_ref[...]), _mx(kw_ref[...]), NT_DIMS, preferred_element_type=F32) * scale
    pw = _masked_softmax(lw, jnp.full(lw.shape, True))
    ow_ref[...] = jnp.dot(_mx(pw), _mx(vw_ref[...]), preferred_element_type=F32)


def sample_cmp_win(qraw, qrot, kvc, a_mat, win, n_c, n_sel, q_pos):
    nb, _, rows, _ = qraw.shape
    ncp = kvc.shape[2]
    nsp = a_mat.shape[1]
    wlen = win.shape[1]
    qspec = pl.BlockSpec((None, None, rows, HEAD_DIM), lambda b, g: (b, g, 0, 0))
    return pl.pallas_call(
        functools.partial(_sample_cmp_win_kernel, n_c=n_c, n_sel=n_sel, n_top=min(SEL_TOP, n_sel), q_pos=q_pos),
        grid=(nb, NSA_KV),
        in_specs=[
            qspec, qspec,
            pl.BlockSpec((None, None, ncp, HEAD_DIM), lambda b, g: (b, g, 0, 0)),
            pl.BlockSpec((None, None, ncp, HEAD_DIM), lambda b, g: (b, NSA_KV + g, 0, 0)),
            pl.BlockSpec((ncp, nsp), lambda b, g: (0, 0)),
            pl.BlockSpec((None, wlen, HEAD_DIM), lambda b, g: (b, 0, g)),
            pl.BlockSpec((None, wlen, HEAD_DIM), lambda b, g: (b, 0, NSA_KV + g)),
        ],
        out_specs=[qspec, qspec, pl.BlockSpec((None, None, rows, 128), lambda b, g: (b, g, 0, 0))],
        out_shape=[
            jax.ShapeDtypeStruct(qraw.shape, F32),
            jax.ShapeDtypeStruct(qraw.shape, F32),
            jax.ShapeDtypeStruct((nb, NSA_KV, rows, 128), jnp.int32),
        ],
        compiler_params=_params("arbitrary", "arbitrary"),
        name="sample_cmp_win",
    )(qraw, qrot, kvc, kvc, a_mat, win, win)


def _sample_slc_kernel(pt_ref, idx_ref, q_ref, *refs, n_top, n_past_blk):
    kv_refs = refs[:n_top]
    kn_ref, vn_ref, o_ref = refs[n_top:]
    b, g = pl.program_id(0), pl.program_id(1)
    row = b * NSA_KV + g
    scale = HEAD_DIM ** -0.5

    def attend(gg):
        q = q_ref[...]
        qb = _mx(q)
        picked = [idx_ref[row, j] for j in range(n_top)]
        in_past = [(blk >= 0) & (blk < n_past_blk) for blk in picked]
        has_new = picked[0] == n_past_blk
        for blk in picked[1:]:
            has_new = has_new | (blk == n_past_blk)
        s_new = jnp.where(has_new, jnp.sum(q * kn_ref[...], axis=-1, keepdims=True) * scale, NEG_BIG)
        logits = []
        for j in range(n_top):
            s = lax.dot_general(qb, _mx(kv_refs[j][:, 0, gg, :]), NT_DIMS, preferred_element_type=F32) * scale
            logits.append(jnp.where(in_past[j], s, NEG_BIG))
        m = s_new
        for s in logits:
            m = jnp.maximum(m, jnp.max(s, axis=-1, keepdims=True))
        p_new = jnp.where(has_new, jnp.exp(s_new - m), 0.0)
        l = p_new
        o = p_new * vn_ref[...]
        for j in range(n_top):
            p = jnp.where(in_past[j], jnp.exp(logits[j] - m), 0.0)
            l = l + jnp.sum(p, axis=-1, keepdims=True)
            o = o + jnp.dot(_mx(p), _mx(kv_refs[j][:, 1, gg, :]), preferred_element_type=F32)
        o_ref[...] = o / jnp.where(l > 0, l, 1.0)

    for gg in range(NSA_KV):
        pl.when(g == gg)(functools.partial(attend, gg))


def sample_slc(qrot, pool, e, page_table, idx, slc_new, n_past_blk):
    nb, _, rows, _ = qrot.shape
    n_top = idx.shape[1]
    page = pool.shape[2]
    per_page = page // SEL_LEN

    def kv_spec(j):
        def index(b, g, pt, ix):
            blk = jnp.clip(ix[b * NSA_KV + g, j], 0, n_past_blk - 1)
            return (e, pt[b, blk // per_page], blk % per_page, 0, 0, 0)
        return pl.BlockSpec((None, None, SEL_LEN, 2, NSA_KV, HEAD_DIM), index)

    return pl.pallas_call(
        functools.partial(_sample_slc_kernel, n_top=n_top, n_past_blk=n_past_blk),
        grid_spec=pltpu.PrefetchScalarGridSpec(
            num_scalar_prefetch=2,
            grid=(nb, NSA_KV),
            in_specs=[
                pl.BlockSpec((None, None, rows, HEAD_DIM), lambda b, g, pt, ix: (b, g, 0, 0)),
                *[kv_spec(j) for j in range(n_top)],
                pl.BlockSpec((None, 1, HEAD_DIM), lambda b, g, pt, ix: (b, 0, g)),
                pl.BlockSpec((None, 1, HEAD_DIM), lambda b, g, pt, ix: (b, 0, NSA_KV + g)),
            ],
            out_specs=pl.BlockSpec((None, None, rows, HEAD_DIM), lambda b, g, pt, ix: (b, g, 0, 0)),
        ),
        out_shape=jax.ShapeDtypeStruct(qrot.shape, F32),
        compiler_params=_params("arbitrary", "arbitrary"),
        name="sample_slc",
    )(page_table, idx, qrot, *([pool] * n_top), slc_new, slc_new)


def _sample_fox_kernel(pt_ref, q_ref, *refs, n_steps, per_step):
    page_refs = refs[:3 * per_step]
    tri_ref, kn_ref, vn_ref, lfn_ref, o_ref, m_s, l_s, acc, crun = refs[3 * per_step:]
    step = pl.program_id(1)
    scale = HEAD_DIM ** -0.5
    nh = FOX_HEADS
    page = page_refs[0].shape[0]
    cols = page * nh

    @pl.when(step == 0)
    def _():
        m_s[...] = jnp.full_like(m_s, NEG_BIG)
        l_s[...] = jnp.zeros_like(l_s)
        acc[...] = jnp.zeros_like(acc)
        crun[...] = jnp.zeros_like(crun)

    qb = _mx(q_ref[...])
    tri = _mx(tri_ref[...])
    lane = lax.broadcasted_iota(jnp.int32, (nh, HEAD_DIM), 1) // nh
    own = (lax.broadcasted_iota(jnp.int32, (nh, cols), 1) % nh) == lax.broadcasted_iota(jnp.int32, (nh, cols), 0)
    c_off = crun[...]
    logits, values = [], []
    for j in range(per_step):
        k_ref, v_ref, lf_ref = page_refs[3 * j:3 * j + 3]
        lf = lf_ref[...]
        hi = _mx(lf).astype(F32)
        mid = _mx(lf - hi).astype(F32)
        lo = lf - hi - mid
        cs = jnp.dot(_mx(jnp.concatenate([hi, mid, lo], axis=0)), tri, preferred_element_type=F32)
        ck = cs[:nh] + cs[nh:2 * nh] + cs[2 * nh:] + c_off
        c_off = ck[:, -1:]
        ck_x = jnp.concatenate(
            [jnp.take_along_axis(ck, lane + i * (HEAD_DIM // nh), axis=1) for i in range(cols // HEAD_DIM)], axis=1)
        k2 = k_ref[...].reshape(cols, HEAD_DIM)
        s = lax.dot_general(qb, _mx(k2), NT_DIMS, preferred_element_type=F32) * scale - ck_x
        logits.append(jnp.where(own, s, NEG_BIG))
        values.append(_mx(v_ref[...].reshape(cols, HEAD_DIM)))
    crun[...] = c_off
    m_old = m_s[...]
    m_new = m_old
    for s in logits:
        m_new = jnp.maximum(m_new, jnp.max(s, axis=-1, keepdims=True))
    alpha = jnp.exp(m_old - m_new)
    l_new = alpha * l_s[...]
    o_new = alpha * acc[...]
    for s, v2 in zip(logits, values):
        p = jnp.where(own, jnp.exp(s - m_new), 0.0)
        l_new = l_new + jnp.sum(p, axis=-1, keepdims=True)
        o_new = o_new + jnp.dot(_mx(p), v2, preferred_element_type=F32)
    l_s[...] = l_new
    acc[...] = o_new
    m_s[...] = m_new

    @pl.when(step == n_steps - 1)
    def _():
        cq = c_off + lfn_ref[...]
        s_new = jnp.sum(q_ref[...] * kn_ref[...], axis=-1, keepdims=True) * scale - cq
        m_fin = jnp.maximum(m_new, s_new)
        p_new = jnp.exp(s_new - m_fin)
        a_fin = jnp.exp(m_new - m_fin)
        o_ref[...] = (a_fin * o_new + p_new * vn_ref[...]) / (a_fin * l_new + p_new)


def sample_fox(fq, pool, logf_t, e, page_table, k_new, v_new, logf_new):
    nb, n_pages = page_table.shape
    page = pool.shape[2]
    per_step = _pick(n_pages, (8, 4, 2, 1))
    tri = jnp.triu(jnp.ones((page, page), F32))

    def page_specs(j):
        def kv(which):
            return pl.BlockSpec((None, None, page, None, FOX_HEADS, HEAD_DIM),
                                lambda b, p, pt: (e, pt[b, p * per_step + j], 0, which, 0, 0))
        return [kv(0), kv(1),
                pl.BlockSpec((None, None, FOX_HEADS, page), lambda b, p, pt: (e, pt[b, p * per_step + j], 0, 0))]

    page_ops = []
    for j in range(per_step):
        page_ops += [pool, pool, logf_t]
    return pl.pallas_call(
        functools.partial(_sample_fox_kernel, n_steps=n_pages // per_step, per_step=per_step),
        grid_spec=pltpu.PrefetchScalarGridSpec(
            num_scalar_prefetch=1,
            grid=(nb, n_pages // per_step),
            in_specs=[
                pl.BlockSpec((None, FOX_HEADS, HEAD_DIM), lambda b, p, pt: (b, 0, 0)),
                *[spec for j in range(per_step) for spec in page_specs(j)],
                pl.BlockSpec((page, page), lambda b, p, pt: (0, 0)),
                pl.BlockSpec((None, FOX_HEADS, HEAD_DIM), lambda b, p, pt: (b, 0, 0)),
                pl.BlockSpec((None, FOX_HEADS, HEAD_DIM), lambda b, p, pt: (b, 0, 0)),
                pl.BlockSpec((None, FOX_HEADS, 1), lambda b, p, pt: (b, 0, 0)),
            ],
            out_specs=pl.BlockSpec((None, FOX_HEADS, HEAD_DIM), lambda b, p, pt: (b, 0, 0)),
            scratch_shapes=[
                pltpu.VMEM((FOX_HEADS, 1), F32), pltpu.VMEM((FOX_HEADS, 1), F32),
                pltpu.VMEM((FOX_HEADS, HEAD_DIM), F32), pltpu.VMEM((FOX_HEADS, 1), F32),
            ],
        ),
        out_shape=jax.ShapeDtypeStruct((nb, FOX_HEADS, HEAD_DIM), F32),
        compiler_params=_params("arbitrary", "arbitrary"),
        name="sample_fox",
    )(page_table, fq, *page_ops, tri, k_new, v_new, logf_new)


def _ret_tables(chunk):
    lg = jnp.log1p(-(2.0 ** (-5.0 - jnp.arange(RET_HEADS, dtype=F32))))
    i = jnp.arange(chunk, dtype=F32)
    diff = i[:, None] - i[None, :]
    dec = jnp.where(diff >= 0, jnp.exp(jnp.maximum(diff, 0.0)[None] * lg[:, None, None]), 0.0)
    xi = jnp.exp((i[None, :] + 1.0) * lg[:, None])
    zeta = jnp.exp((chunk - 1.0 - i)[None, :] * lg[:, None])
    g_c = jnp.exp(chunk * lg)
    return dec, xi, zeta, g_c


def _ret_rope_tables(pos):
    half = RET_DK // 2
    inv = 1.0 / (RET_THETA ** (jnp.arange(half, dtype=F32) / half))
    ang = pos.astype(F32)[:, None] * inv[None, :]
    return jnp.cos(ang), jnp.sin(ang)


def _group_norm_gate(o, gate, gn):
    mu = jnp.mean(o, axis=-1, keepdims=True)
    d = o - mu
    var = jnp.mean(d * d, axis=-1, keepdims=True)
    y = d * lax.rsqrt(var + GN_EPS) * gn
    return gate * jax.nn.sigmoid(gate) * y


def _retention_kernel(q_ref, k_ref, v_ref, gate_ref, cos_ref, sin_ref, dec_ref, coef_ref, gn_ref,
                      y_ref, s_ref, state, *, n_chunks, heads):
    c = pl.program_id(2)
    half = RET_DK // 2

    @pl.when(c == 0)
    def _():
        state[...] = jnp.zeros_like(state)

    cos, sin = cos_ref[...], sin_ref[...]

    def rot(x_ref, hh):
        x1 = x_ref[:, hh * RET_DK:hh * RET_DK + half]
        x2 = x_ref[:, hh * RET_DK + half:(hh + 1) * RET_DK]
        return jnp.concatenate([x1 * cos - x2 * sin, x2 * cos + x1 * sin], axis=-1)

    for hh in range(heads):
        vcols = slice(hh * RET_DV, (hh + 1) * RET_DV)
        coef = coef_ref[hh]
        xi, zeta, g_c = coef[:, 0:1], coef[:, 1:2], coef[0:1, 2:3]
        q = _mx(rot(q_ref, hh))
        kf = rot(k_ref, hh) * (RET_DK ** -0.5)
        v = _mx(v_ref[:, vcols])
        s_old = state[hh]
        a = lax.dot_general(q, _mx(kf), NT_DIMS, preferred_element_type=F32) * dec_ref[hh]
        o = (jnp.dot(_mx(a), v, preferred_element_type=F32)
             + jnp.dot(q, _mx(s_old), preferred_element_type=F32) * xi)
        state[hh] = s_old * g_c + lax.dot_general(_mx(kf * zeta), v, TN_DIMS, preferred_element_type=F32)
        y_ref[:, vcols] = _group_norm_gate(o, gate_ref[:, vcols], gn_ref[:, vcols]).astype(y_ref.dtype)

    @pl.when(c == n_chunks - 1)
    def _():
        s_ref[...] = state[...]


def retention_prompt(h, nb, t, gn_g, o_idx):
    ch = RET_CHUNK
    n_chunks = t // ch
    dec, xi, zeta, g_c = _ret_tables(ch)
    coef = jnp.stack([xi, zeta, jnp.broadcast_to(g_c[:, None], xi.shape)], axis=-1)
    coef = jnp.pad(coef, ((0, 0), (0, 0), (0, 128 - 3)))
    cos, sin = _ret_rope_tables(jnp.arange(t))
    hs = RET_HEADS_PER_STEP
    groups = RET_HEADS // hs
    kw, vw = hs * RET_DK, hs * RET_DV
    kb, vb, gb = O_RK // kw, O_RV // vw, O_RG // vw
    return pl.pallas_call(
        functools.partial(_retention_kernel, n_chunks=n_chunks, heads=hs),
        grid=(nb, groups, n_chunks),
        in_specs=[
            pl.BlockSpec((ch, kw), lambda b, hg, c: (b * n_chunks + c, hg)),
            pl.BlockSpec((ch, kw), lambda b, hg, c: (b * n_chunks + c, kb + hg)),
            pl.BlockSpec((ch, vw), lambda b, hg, c: (b * n_chunks + c, vb + hg)),
            pl.BlockSpec((ch, vw), lambda b, hg, c: (b * n_chunks + c, gb + hg)),
            pl.BlockSpec((ch, RET_DK // 2), lambda b, hg, c: (c, 0)),
            pl.BlockSpec((ch, RET_DK // 2), lambda b, hg, c: (c, 0)),
            pl.BlockSpec((hs, ch, ch), lambda b, hg, c: (hg, 0, 0)),
            pl.BlockSpec((hs, ch, 128), lambda b, hg, c: (hg, 0, 0)),
            pl.BlockSpec((None, 1, vw), lambda b, hg, c: (o_idx, 0, hg)),
        ],
        out_specs=[
            pl.BlockSpec((ch, vw), lambda b, hg, c: (b * n_chunks + c, hg)),
            pl.BlockSpec((None, hs, RET_DK, RET_DV), lambda b, hg, c: (b, hg, 0, 0)),
        ],
        out_shape=[
            jax.ShapeDtypeStruct((nb * t, RET_HEADS * RET_DV), MXU_DTYPE),
            jax.ShapeDtypeStruct((nb, RET_HEADS, RET_DK, RET_DV), F32),
        ],
        scratch_shapes=[pltpu.VMEM((hs, RET_DK, RET_DV), F32)],
        compiler_params=_params("arbitrary", "arbitrary", "arbitrary"),
        name="retention_prompt",
    )(h, h, h, h, cos, sin, dec, coef, gn_g[:, None, :])


def _retention_step_kernel(q_ref, k_ref, v_ref, gate_ref, cos_ref, sin_ref, coef_ref, gn_ref, s0_ref,
                           y_ref, s_ref):
    half = RET_DK // 2
    hh = pl.program_id(1)
    cos, sin = cos_ref[...], sin_ref[...]

    def rot(x_ref):
        x1, x2 = x_ref[:half, :], x_ref[half:, :]
        return jnp.concatenate([x1 * cos - x2 * sin, x2 * cos + x1 * sin], axis=0)

    head = lax.broadcasted_iota(jnp.int32, coef_ref.shape, 0)
    coef = jnp.sum(jnp.where(head == hh, coef_ref[...], 0.0), axis=0, keepdims=True)
    dec, xi, zeta, g_c = coef[:, 0:1], coef[:, 1:2], coef[:, 2:3], coef[:, 3:4]
    q = rot(q_ref)
    k = rot(k_ref) * (RET_DK ** -0.5)
    v = v_ref[...]
    s_old = s0_ref[...]
    a = jnp.sum(q * k, axis=0, keepdims=True) * dec
    o = a * v + jnp.sum(q * s_old, axis=0, keepdims=True) * xi
    s_ref[...] = s_old * g_c + (k * zeta) * v
    y_ref[...] = _group_norm_gate(o, gate_ref[...], gn_ref[...])


def retention_step(q_col, k_col, v_row, gate_row, pos, s0, o_idx, gn_g):
    nb = q_col.shape[0]
    dec, xi, zeta, g_c = _ret_tables(1)
    coef = jnp.stack([dec[:, 0, 0], xi[:, 0], zeta[:, 0], g_c], axis=-1)
    coef = jnp.pad(coef, ((0, 0), (0, 128 - 4)))
    cos, sin = _ret_rope_tables(pos)
    col = pl.BlockSpec((None, None, RET_DK, 1), lambda b, hh: (b, hh, 0, 0))
    row = pl.BlockSpec((None, None, 1, RET_DV), lambda b, hh: (b, hh, 0, 0))
    tab = pl.BlockSpec((RET_DK // 2, 1), lambda b, hh: (0, 0))
    return pl.pallas_call(
        _retention_step_kernel,
        grid=(nb, RET_HEADS),
        in_specs=[
            col, col, row, row, tab, tab,
            pl.BlockSpec((RET_HEADS, 128), lambda b, hh: (0, 0)),
            pl.BlockSpec((None, 1, RET_DV), lambda b, hh: (o_idx, 0, hh)),
            pl.BlockSpec((None, None, None, RET_DK, RET_DV), lambda b, hh: (o_idx, b, hh, 0, 0)),
        ],
        out_specs=[row, pl.BlockSpec((None, None, RET_DK, RET_DV), lambda b, hh: (b, hh, 0, 0))],
        out_shape=[
            jax.ShapeDtypeStruct((nb, RET_HEADS, 1, RET_DV), F32),
            jax.ShapeDtypeStruct((nb, RET_HEADS, RET_DK, RET_DV), F32),
        ],
        compiler_params=_params("arbitrary", "arbitrary"),
        name="retention_step",
    )(q_col, k_col, v_row, gate_row, cos.reshape(-1, 1), sin.reshape(-1, 1), coef, gn_g[:, None, :], s0)


def _pad_rows(x, rows):
    return jnp.pad(x, ((0, rows - x.shape[0]),) + ((0, 0),) * (x.ndim - 1))


def _repack_even(w):
    small = jnp.concatenate([w[:, O_NG:O_FQ], w[:, O_FF:E_EVEN]], axis=1)
    small = jnp.pad(small, ((0, 0), (0, 128 - small.shape[1])))
    return jnp.concatenate([w[:, O_NQ:O_NG], w[:, O_FQ:O_FF], small], axis=1)


def _even_layer(e, xb_p, xb_s, nb, t, ns, past, caches, page_table, wts):
    cache_cmp, cache_slc, cache_win, cache_fox, logf_pool_t = caches
    w_in, fox_f_bias, cmp_pos, cmp_w1, cmp_w2 = wts
    w_rep = _repack_even(w_in[e])
    fb_row = jnp.zeros((1, 128), F32).at[0, N_GATE:N_GATE + FOX_HEADS].set(fox_f_bias[e])
    hq = P_FQ // HEAD_DIM
    kvw = 2 * NSA_KV * HEAD_DIM
    nw = NSA_HEADS * HEAD_DIM

    h_p, h_s = matmul(xb_p, xb_s, w_rep, (), (1152, 640, 384, 128))
    tm_post = _pick(nb * t, (512, 256, 128))
    q_rot, slc_p, win_p, gates_p, logf_p = even_post(
        h_p, _rope_tables(jnp.arange(t)), fb_row, t // tm_post)
    c_p = cumsum_rows(logf_p, nb, t)
    c_heads = jnp.swapaxes(c_p[:, N_GATE:N_GATE + FOX_HEADS].reshape(nb, t, FOX_HEADS), 1, 2)
    n16 = t // CMP_STRIDE
    kvc_p = nsa_compress(
        h_p, pl.BlockSpec((t, HEAD_DIM), lambda b, c: (b, P_NKV // HEAD_DIM + c)), nb, t, e, cmp_w1, cmp_pos, cmp_w2)
    a_p = _sel_matrix(n16, n16 - 1, t // SEL_LEN, 128)
    o_c, sel = cmp_select(h_p, kvc_p, a_p, nb, t)
    o_s = flash_attention("slc", nb, t, q_rot, lambda g: g, slc_p, lambda g: g, slc_p, lambda g: NSA_KV + g, (sel,))
    o_w = flash_attention("win", nb, t, q_rot, lambda g: g, win_p, lambda g: g, win_p, lambda g: NSA_KV + g)
    fq0 = P_FQ // (NSA_GROUP * HEAD_DIM)
    fgroups = FOX_HEADS // NSA_GROUP
    o_f = flash_attention("fox", nb, t, h_p, lambda g: fq0 + g, h_p, lambda g: fq0 + fgroups + g,
                          h_p, lambda g: fq0 + 2 * fgroups + g, (c_heads[..., None], c_heads[:, :, None, :]))
    xcat_p = combine_heads(o_c, o_s, o_w, o_f, gates_p)
    st_p = (
        h_p[:, P_NKV:P_NKV + kvw].reshape(nb, t, 2, NSA_KV, HEAD_DIM),
        slc_p.reshape(nb, t, 2, NSA_KV, HEAD_DIM),
        win_p.reshape(nb, t, 2, NSA_KV, HEAD_DIM)[:, t - min(WINDOW, t):],
        h_p[:, P_FQ + nw:P_FQ + 3 * nw].reshape(nb, t, 2, FOX_HEADS, HEAD_DIM),
        logf_p[:, N_GATE:N_GATE + FOX_HEADS].reshape(nb, t, FOX_HEADS),
    )

    rows = xb_s.shape[0]
    q_rot_s, slc_s, win_s, gates_s, logf_s = even_post(
        h_s, _rope_tables(jnp.full((rows,), past)), fb_row, 1)
    new_win = jnp.concatenate(
        [cache_win[e].reshape(ns, -1, kvw), win_s[:ns, None, :]], axis=1)[:, 1:]
    cmp_rows = gather_cmp_pages(cache_cmp, e, page_table)
    l_tot = past + 1
    n16_s = l_tot // CMP_STRIDE
    assert n16_s * CMP_STRIDE == past
    kvc_s = nsa_compress(
        cmp_rows, pl.BlockSpec((None, None, past, HEAD_DIM), lambda b, c: (b, c, 0, 0)), ns, past, e,
        cmp_w1, cmp_pos, cmp_w2)
    n_sel_s = -(-l_tot // SEL_LEN)
    nsp = -(-n_sel_s // 128) * 128
    a_s = _sel_matrix(n16_s, n16_s - 1, n_sel_s, nsp)

    def q_rows(x):
        x = x[:ns].reshape(ns, NSA_KV, NSA_GROUP, HEAD_DIM)
        return jnp.pad(x, ((0, 0), (0, 0), (0, 8 - NSA_GROUP), (0, 0)))

    qraw_g, qrot_g = q_rows(h_s[:, :nw]), q_rows(q_rot_s)
    o_c_s, o_w_s, idx = sample_cmp_win(qraw_g, qrot_g, kvc_s, a_s, new_win, n16_s - 1, n_sel_s, past)
    idx2 = idx[:, :, 0, :min(SEL_TOP, n_sel_s)].reshape(ns * NSA_KV, -1)
    o_s_s = sample_slc(qrot_g, cache_slc, e, page_table, idx2, slc_s[:ns, None, :], past // SEL_LEN)
    fq = h_s[:ns, P_FQ:P_FQ + nw].reshape(ns, FOX_HEADS, HEAD_DIM)
    fk = h_s[:ns, P_FQ + nw:P_FQ + 2 * nw]
    fv = h_s[:ns, P_FQ + 2 * nw:P_FQ + 3 * nw]
    logf_new = logf_s[:ns, N_GATE:N_GATE + FOX_HEADS]
    o_f_s = sample_fox(fq, cache_fox, logf_pool_t, e, page_table, fk.reshape(ns, FOX_HEADS, HEAD_DIM),
                       fv.reshape(ns, FOX_HEADS, HEAD_DIM), logf_new[:, :, None])

    def heads_flat(x):
        return _pad_rows(x[:, :, :NSA_GROUP].reshape(ns, nw), rows)

    xcat_s = combine_heads(heads_flat(o_c_s), heads_flat(o_s_s), heads_flat(o_w_s),
                           _pad_rows(o_f_s.reshape(ns, nw), rows), gates_s)
    st_s = (
        h_s[:ns, P_NKV:P_NKV + kvw].reshape(ns, 1, 2, NSA_KV, HEAD_DIM),
        slc_s[:ns].reshape(ns, 1, 2, NSA_KV, HEAD_DIM),
        new_win.reshape(ns, -1, 2, NSA_KV, HEAD_DIM),
        h_s[:ns, P_FQ + nw:P_FQ + 3 * nw].reshape(ns, 1, 2, FOX_HEADS, HEAD_DIM),
        logf_new.reshape(ns, 1, FOX_HEADS),
    )
    return xcat_p, xcat_s, st_p, st_s


def _odd_layer(o, xb_p, xb_s, nb, t, ns, past, state_ret, w_in_odd, ret_gn_g):
    h_p, h_s = matmul(xb_p, xb_s, w_in_odd, (o,), (1024, 512, 256, 128))
    y_p, s_p = retention_prompt(h_p, nb, t, ret_gn_g, o)
    rows = xb_s.shape[0]
    h_s = h_s[:ns]
    q_col = h_s[:, :O_RK].reshape(ns, RET_HEADS, RET_DK, 1)
    k_col = h_s[:, O_RK:O_RV].reshape(ns, RET_HEADS, RET_DK, 1)
    v_row = h_s[:, O_RV:O_RG].reshape(ns, RET_HEADS, 1, RET_DV)
    g_row = h_s[:, O_RG:E_ODD].reshape(ns, RET_HEADS, 1, RET_DV)
    y_s, s_s = retention_step(q_col, k_col, v_row, g_row, jnp.full((1,), past), state_ret, o, ret_gn_g)
    y_s = _pad_rows(y_s.reshape(ns, RET_HEADS * RET_DV), rows).astype(MXU_DTYPE)
    return y_p, y_s, s_p, s_s


def kernel(x_prompt, x_sample, cache_nsa_cmp, cache_nsa_slc, cache_nsa_win, cache_fox_kv, cache_fox_logf,
           state_ret, page_table, w_ffn_in, w_ffn_out, ln_g, ln_b, w_in_even, w_out_even, fox_f_bias,
           nsa_cmp_pos, nsa_cmp_w1, nsa_cmp_w2, w_in_odd, ret_gn_g, w_out_odd):
    nb, t, d = x_prompt.shape
    ns, ts, _ = x_sample.shape
    depth = w_ffn_in.shape[0]
    n_pages = page_table.shape[1]
    page = cache_nsa_cmp.shape[2]
    past = n_pages * page
    assert ts == 1 and past >= WINDOW and past % SEL_LEN == 0 and t % RET_CHUNK == 0
    alpha = (2.0 * depth) ** 0.25
    rows = max(16, -(-ns // 16) * 16)

    xp = x_prompt.reshape(nb * t, d)
    xs = _pad_rows(x_sample.reshape(ns * ts, d), rows)
    xp_b, xs_b = xp.astype(MXU_DTYPE), xs.astype(MXU_DTYPE)

    caches = (cache_nsa_cmp, cache_nsa_slc, cache_nsa_win, cache_fox_kv, jnp.swapaxes(cache_fox_logf, 2, 3))
    w_ffn_out, w_out_even, w_out_odd = _mx(w_ffn_out), _mx(w_out_even), _mx(w_out_odd)

    def ffn(xp, xp_b, xs, xs_b, l, s):
        hid_p, hid_s = swiglu_hidden(xp_b, xs_b, w_ffn_in, l, s)
        return (matmul_postnorm(hid_p, w_ffn_out, (l, s), xp, ln_g, ln_b, (l, 2 * s), alpha, 0.5)
                + matmul_postnorm(hid_s, w_ffn_out, (l, s), xs, ln_g, ln_b, (l, 2 * s), alpha, 0.5))

    new_p = [[] for _ in range(6)]
    new_s = [[] for _ in range(6)]
    for l in range(depth):
        xp, xp_b, xs, xs_b = ffn(xp, xp_b, xs, xs_b, l, 0)
        if l % 2 == 0:
            e = l // 2
            mp, ms, st_p, st_s = _even_layer(
                e, xp_b, xs_b, nb, t, ns, past, caches, page_table,
                (w_in_even, fox_f_bias, nsa_cmp_pos, nsa_cmp_w1, nsa_cmp_w2))
            w_out, widx = w_out_even, (e,)
            slots = (0, 1, 2, 3, 4)
        else:
            o = l // 2
            mp, ms, sp, ss = _odd_layer(o, xp_b, xs_b, nb, t, ns, past, state_ret, w_in_odd, ret_gn_g)
            st_p, st_s = (sp,), (ss,)
            w_out, widx = w_out_odd, (o,)
            slots = (5,)
        for i, a_p, a_s in zip(slots, st_p, st_s):
            new_p[i].append(a_p)
            new_s[i].append(a_s)
        xp, xp_b = matmul_postnorm(mp, w_out, widx, xp, ln_g, ln_b, (l, 1), alpha, 1.0)
        xs, xs_b = matmul_postnorm(ms, w_out, widx, xs, ln_g, ln_b, (l, 1), alpha, 1.0)
        xp, xp_b, xs, xs_b = ffn(xp, xp_b, xs, xs_b, l, 1)

    outs = [xp.reshape(nb, t, d), xs[:ns].reshape(ns, ts, d)]
    for i in range(6):
        outs.append(jnp.stack(new_p[i]))
        outs.append(jnp.stack(new_s[i]))
    return tuple(outs)
```

```python
import functools
import math

import numpy as np
import jax
import jax.numpy as jnp
from jax import lax
from jax.experimental import pallas as pl
from jax.experimental.pallas import tpu as pltpu

F32 = jnp.float32
MXU_DTYPE = jnp.bfloat16

HEAD_DIM = 128
NSA_HEADS = 8
NSA_KV = 2
NSA_GROUP = NSA_HEADS // NSA_KV
CMP_LEN = 32
CMP_STRIDE = 16
CMP_HIDDEN = 256
SEL_LEN = 64
SEL_TOP = 16
WINDOW = 512
FOX_HEADS = 8
ROPE_THETA = 500000.0
ROPE_DIMS = HEAD_DIM // 4
RET_HEADS = 8
RET_DK = 256
RET_DV = 512
RET_CHUNK = 128
RET_THETA = 10000.0
RET_HEADS_PER_STEP = 4
LN_EPS = 1e-5
GN_EPS = 1e-6

O_NQ = 0
O_NKV = O_NQ + NSA_HEADS * HEAD_DIM
O_NG = O_NKV + 6 * NSA_KV * HEAD_DIM
O_FQ = O_NG + 3 * NSA_HEADS
O_FK = O_FQ + FOX_HEADS * HEAD_DIM
O_FV = O_FK + FOX_HEADS * HEAD_DIM
O_FF = O_FV + FOX_HEADS * HEAD_DIM
E_EVEN = O_FF + FOX_HEADS
P_NKV = NSA_HEADS * HEAD_DIM
P_FQ = P_NKV + 6 * NSA_KV * HEAD_DIM
P_SMALL = P_FQ + 3 * FOX_HEADS * HEAD_DIM
P_EVEN = P_SMALL + 128
N_GATE = 3 * NSA_HEADS
O_RK = RET_HEADS * RET_DK
O_RV = 2 * RET_HEADS * RET_DK
O_RG = O_RV + RET_HEADS * RET_DV
E_ODD = O_RG + RET_HEADS * RET_DV

V7X_VMEM_LIMIT = 56 * 1024 * 1024
LN_ROWS = 128
LOG2E = 1.4426950408889634
NEG_BIG = -1e30
NT_DIMS = (((1,), (1,)), ((), ()))
TN_DIMS = (((0,), (0,)), ((), ()))


def _params(*sem):
    return pltpu.CompilerParams(dimension_semantics=sem, vmem_limit_bytes=V7X_VMEM_LIMIT)


def _mx(x):
    return x.astype(MXU_DTYPE)


def _pick(n, prefs):
    for p in prefs:
        if n % p == 0:
            return p
    return n


def _swiglu_kernel(x_ref, xs_ref, wa_ref, wb_ref, o_ref, os_ref, wa_s, wb_s):
    def hidden(x):
        a = jnp.dot(x, wa_s[...], preferred_element_type=F32)
        b = jnp.dot(x, wb_s[...], preferred_element_type=F32)
        return (a * jax.nn.sigmoid(a) * b).astype(o_ref.dtype)

    @pl.when(pl.program_id(1) == 0)
    def _():
        wa_s[...] = _mx(wa_ref[...])
        wb_s[...] = _mx(wb_ref[...])
        os_ref[...] = hidden(xs_ref[...])

    o_ref[...] = hidden(x_ref[...])


def swiglu_hidden(xb, xsb, w_in, l, s):
    m, d = xb.shape
    ms = xsb.shape[0]
    f = w_in.shape[-1] // 2
    tm = _pick(m, (1024, 512, 256, 128))
    tn = _pick(f, (512, 256, 128))
    nj = f // tn
    return pl.pallas_call(
        _swiglu_kernel,
        grid=(nj, m // tm),
        in_specs=[
            pl.BlockSpec((tm, d), lambda j, i: (i, 0)),
            pl.BlockSpec((ms, d), lambda j, i: (0, 0)),
            pl.BlockSpec((None, None, d, tn), lambda j, i: (l, s, 0, j)),
            pl.BlockSpec((None, None, d, tn), lambda j, i: (l, s, 0, j + nj)),
        ],
        out_specs=[pl.BlockSpec((tm, tn), lambda j, i: (i, j)), pl.BlockSpec((ms, tn), lambda j, i: (0, j))],
        out_shape=[jax.ShapeDtypeStruct((m, f), MXU_DTYPE), jax.ShapeDtypeStruct((ms, f), MXU_DTYPE)],
        scratch_shapes=[pltpu.VMEM((d, tn), MXU_DTYPE), pltpu.VMEM((d, tn), MXU_DTYPE)],
        compiler_params=_params("arbitrary", "arbitrary"),
        name="swiglu_hidden",
    )(xb, xsb, w_in, w_in)


def _mm_ln_kernel(x_ref, w_ref, r_ref, g_ref, b_ref, o_ref, ob_ref, *, alpha, scale, nk):
    k = pl.program_id(1)
    def part():
        return jnp.dot(x_ref[...], _mx(w_ref[...]), preferred_element_type=F32)

    @pl.when(k == 0)
    def _():
        o_ref[...] = part()

    @pl.when(k > 0)
    def _():
        o_ref[...] += part()

    @pl.when(k == nk - 1)
    def _():
        rows = min(LN_ROWS, o_ref.shape[0])

        def norm_rows(c, carry):
            sl = pl.ds(pl.multiple_of(c * rows, rows), rows)
            z = alpha * r_ref[sl, :] + scale * o_ref[sl, :]
            mu = jnp.mean(z, axis=-1, keepdims=True)
            dz = z - mu
            var = jnp.mean(dz * dz, axis=-1, keepdims=True)
            y = dz * lax.rsqrt(var + LN_EPS) * g_ref[...] + b_ref[...]
            o_ref[sl, :] = y
            ob_ref[sl, :] = y.astype(ob_ref.dtype)
            return carry

        lax.fori_loop(0, o_ref.shape[0] // rows, norm_rows, 0)


def matmul_postnorm(xb, w, widx, res, g, b, gidx, alpha, scale):
    m, kdim = xb.shape
    n = w.shape[-1]
    tm = _pick(m, (512, 256, 128))
    tk = _pick(kdim, (2816, 2048, 1024, 512, 256, 128))
    nk = kdim // tk
    nlead = len(widx)
    w_spec = pl.BlockSpec((None,) * nlead + (tk, n), lambda i, k: tuple(widx) + (k, 0))
    g_spec = pl.BlockSpec((None, None, 1, n), lambda i, k: tuple(gidx) + (0, 0))
    return pl.pallas_call(
        functools.partial(_mm_ln_kernel, alpha=alpha, scale=scale, nk=nk),
        grid=(m // tm, nk),
        in_specs=[
            pl.BlockSpec((tm, tk), lambda i, k: (i, k)),
            w_spec,
            pl.BlockSpec((tm, n), lambda i, k: (i, 0)),
            g_spec, g_spec,
        ],
        out_specs=[pl.BlockSpec((tm, n), lambda i, k: (i, 0)), pl.BlockSpec((tm, n), lambda i, k: (i, 0))],
        out_shape=[jax.ShapeDtypeStruct((m, n), F32), jax.ShapeDtypeStruct((m, n), MXU_DTYPE)],
        compiler_params=_params("arbitrary", "arbitrary"),
        name="matmul_postnorm",
    )(xb, w, res, g[:, :, None, :], b[:, :, None, :])


def _mm_kernel(x_ref, xs_ref, w_ref, o_ref, os_ref, w_s):
    @pl.when(pl.program_id(1) == 0)
    def _():
        w_s[...] = _mx(w_ref[...])
        os_ref[...] = jnp.dot(xs_ref[...], w_s[...], preferred_element_type=F32)
    o_ref[...] = jnp.dot(x_ref[...], w_s[...], preferred_element_type=F32)


def matmul(xb, xsb, w, widx, tn_prefs):
    m, kdim = xb.shape
    ms = xsb.shape[0]
    n = w.shape[-1]
    tm = _pick(m, (1024, 512, 256, 128))
    tn = _pick(n, tn_prefs)
    nlead = len(widx)
    return pl.pallas_call(
        _mm_kernel,
        grid=(n // tn, m // tm),
        in_specs=[
            pl.BlockSpec((tm, kdim), lambda j, i: (i, 0)),
            pl.BlockSpec((ms, kdim), lambda j, i: (0, 0)),
            pl.BlockSpec((None,) * nlead + (kdim, tn), lambda j, i: tuple(widx) + (0, j)),
        ],
        out_specs=[pl.BlockSpec((tm, tn), lambda j, i: (i, j)), pl.BlockSpec((ms, tn), lambda j, i: (0, j))],
        out_shape=[jax.ShapeDtypeStruct((m, n), F32), jax.ShapeDtypeStruct((ms, n), F32)],
        scratch_shapes=[pltpu.VMEM((kdim, tn), MXU_DTYPE)],
        compiler_params=_params("arbitrary", "arbitrary"),
        name="matmul",
    )(xb, xsb, w)


def _rope_tables(pos):
    half = ROPE_DIMS // 2
    inv = 1.0 / (ROPE_THETA ** (jnp.arange(half, dtype=F32) / half))
    ang = pos.astype(F32)[:, None] * inv[None, :]
    cos, sin = jnp.cos(ang), jnp.sin(ang)
    n = pos.shape[0]
    ones = jnp.ones((n, HEAD_DIM - ROPE_DIMS), F32)
    zeros = jnp.zeros((n, HEAD_DIM - ROPE_DIMS), F32)
    zh = jnp.zeros((n, half), F32)
    c = jnp.concatenate([cos, cos, ones], -1)
    a = jnp.concatenate([-sin, zh, zeros], -1)
    b = jnp.concatenate([zh, sin, zeros], -1)
    return c, a, b


def _rope(x, c, a, b):
    half = ROPE_DIMS // 2
    return x * c + pltpu.roll(x, HEAD_DIM - half, 1) * a + pltpu.roll(x, half, 1) * b


def _even_post_kernel(q_ref, slc_ref, win_ref, sm_ref, c_ref, a_ref, b_ref, fb_ref,
                      qr_ref, slco_ref, wino_ref, gate_ref, logf_ref):
    c, a, b = c_ref[...], a_ref[...], b_ref[...]
    for h in range(NSA_HEADS):
        sl = slice(h * HEAD_DIM, (h + 1) * HEAD_DIM)
        qr_ref[:, sl] = _rope(q_ref[:, sl], c, a, b)
    for src, dst in ((slc_ref, slco_ref), (win_ref, wino_ref)):
        for g in range(NSA_KV):
            sl = slice(g * HEAD_DIM, (g + 1) * HEAD_DIM)
            dst[:, sl] = _rope(src[:, sl], c, a, b)
        vs = slice(NSA_KV * HEAD_DIM, 2 * NSA_KV * HEAD_DIM)
        dst[:, vs] = src[:, vs]
    sm = sm_ref[...]
    gate_ref[...] = jax.nn.sigmoid(sm)
    z = sm + fb_ref[...]
    logf_ref[...] = jnp.minimum(z, 0.0) - jnp.log1p(jnp.exp(-jnp.abs(z)))


def even_post(h, tabs, fb_row, npos_blocks):
    m = h.shape[0]
    tm = _pick(m, (512, 256, 128))
    kvw = 2 * NSA_KV * HEAD_DIM
    tab_spec = pl.BlockSpec((tm, HEAD_DIM), lambda i: (i % npos_blocks, 0))
    return pl.pallas_call(
        _even_post_kernel,
        grid=(m // tm,),
        in_specs=[
            pl.BlockSpec((tm, P_NKV), lambda i: (i, 0)),
            pl.BlockSpec((tm, kvw), lambda i: (i, (P_NKV + kvw) // kvw)),
            pl.BlockSpec((tm, kvw), lambda i: (i, (P_NKV + 2 * kvw) // kvw)),
            pl.BlockSpec((tm, 128), lambda i: (i, P_SMALL // 128)),
            tab_spec, tab_spec, tab_spec,
            pl.BlockSpec((1, 128), lambda i: (0, 0)),
        ],
        out_specs=[
            pl.BlockSpec((tm, P_NKV), lambda i: (i, 0)),
            pl.BlockSpec((tm, kvw), lambda i: (i, 0)),
            pl.BlockSpec((tm, kvw), lambda i: (i, 0)),
            pl.BlockSpec((tm, 128), lambda i: (i, 0)),
            pl.BlockSpec((tm, 128), lambda i: (i, 0)),
        ],
        out_shape=[
            jax.ShapeDtypeStruct((m, P_NKV), F32),
            jax.ShapeDtypeStruct((m, kvw), F32),
            jax.ShapeDtypeStruct((m, kvw), F32),
            jax.ShapeDtypeStruct((m, 128), F32),
            jax.ShapeDtypeStruct((m, 128), F32),
        ],
        compiler_params=_params("arbitrary"),
        name="even_post",
    )(h, h, h, h, *tabs, fb_row)


def _cumsum_kernel(x_ref, tri_ref, o_ref, carry):
    @pl.when(pl.program_id(1) == 0)
    def _():
        carry[...] = jnp.zeros_like(carry)
    c = jnp.dot(tri_ref[...], x_ref[...], preferred_element_type=F32,
                precision=lax.Precision.HIGHEST) + carry[...]
    o_ref[...] = c
    carry[...] = c[-1:, :]


def cumsum_rows(x, nb, t):
    tc = _pick(t, (512, 256, 128))
    tri = jnp.tril(jnp.ones((tc, tc), F32))
    nt = t // tc
    return pl.pallas_call(
        _cumsum_kernel,
        grid=(nb, nt),
        in_specs=[pl.BlockSpec((tc, 128), lambda b, i: (b * nt + i, 0)),
                  pl.BlockSpec((tc, tc), lambda b, i: (0, 0))],
        out_specs=pl.BlockSpec((tc, 128), lambda b, i: (b * nt + i, 0)),
        out_shape=jax.ShapeDtypeStruct(x.shape, F32),
        scratch_shapes=[pltpu.VMEM((1, 128), F32)],
        compiler_params=_params("arbitrary", "arbitrary"),
        name="cumsum_rows",
    )(x, tri)


def _compress_kernel(x_ref, w1_ref, pe_ref, w2_ref, o_ref, u0, u1, *, n16):
    u0[...] = jnp.zeros_like(u0)
    u1[...] = jnp.zeros_like(u1)
    for r in range(0, CMP_STRIDE, 2):
        xa = x_ref[pl.ds(r, n16, stride=CMP_STRIDE), :]
        xb = x_ref[pl.ds(r + 1, n16, stride=CMP_STRIDE), :]
        for u, j in ((u0, r), (u1, CMP_STRIDE + r)):
            lhs = jnp.concatenate([_mx(xa + pe_ref[j:j + 1, :]), _mx(xb + pe_ref[j + 1:j + 2, :])], axis=1)
            w = _mx(w1_ref[j:j + 2]).reshape(2 * HEAD_DIM, CMP_HIDDEN)
            u[...] += jnp.dot(lhs, w, preferred_element_type=F32)
    pre = u0[...] + pltpu.roll(u1[...], n16 - 1, 0)
    out = jnp.dot(_mx(jax.nn.gelu(pre)), _mx(w2_ref[...]), preferred_element_type=F32)
    row = lax.broadcasted_iota(jnp.int32, out.shape, 0)
    o_ref[...] = jnp.where(row < n16 - 1, out, 0.0)


def nsa_compress(x, x_spec, nb, length, e, w1, pe, w2):
    n16 = length // CMP_STRIDE
    w1r = w1.reshape(w1.shape[0], 2, CMP_LEN, HEAD_DIM, CMP_HIDDEN)
    return pl.pallas_call(
        functools.partial(_compress_kernel, n16=n16),
        grid=(nb, 2 * NSA_KV),
        in_specs=[
            x_spec,
            pl.BlockSpec((None, None, CMP_LEN, HEAD_DIM, CMP_HIDDEN), lambda b, c: (e, c // NSA_KV, 0, 0, 0)),
            pl.BlockSpec((None, None, CMP_LEN, HEAD_DIM), lambda b, c: (e, c // NSA_KV, 0, 0)),
            pl.BlockSpec((None, None, CMP_HIDDEN, HEAD_DIM), lambda b, c: (e, c // NSA_KV, 0, 0)),
        ],
        out_specs=pl.BlockSpec((None, None, n16, HEAD_DIM), lambda b, c: (b, c, 0, 0)),
        out_shape=jax.ShapeDtypeStruct((nb, 2 * NSA_KV, n16, HEAD_DIM), F32),
        scratch_shapes=[pltpu.VMEM((n16, CMP_HIDDEN), F32), pltpu.VMEM((n16, CMP_HIDDEN), F32)],
        compiler_params=_params("arbitrary", "arbitrary"),
        name="nsa_compress",
    )(x, w1r, pe, w2)


def _sel_matrix(n_c_pad, n_c, n_sel, width):
    ratio = SEL_LEN // CMP_STRIDE
    i = np.arange(n_c_pad)[:, None]
    j = np.arange(width)[None, :]
    a = (i >= ratio * j - 1) & (i <= ratio * j + ratio - 1) & (i < n_c) & (j < n_sel)
    return jnp.asarray(a.astype(np.float32))


def _masked_softmax(lg, mask):
    lg = jnp.where(mask, lg, -jnp.inf)
    m = jnp.max(lg, axis=-1, keepdims=True)
    m = jnp.where(m == -jnp.inf, 0.0, m)
    p = jnp.exp(lg - m)
    s = jnp.sum(p, axis=-1, keepdims=True)
    return p / jnp.where(s > 0, s, 1.0)


def _cmp_select_kernel(q_ref, kc_ref, vc_ref, a_ref, oc_ref, sel_ref, *, tq, n_sel, n_top):
    i = pl.program_id(2)
    ncp = kc_ref.shape[0]
    qpos = i * tq + lax.broadcasted_iota(jnp.int32, (tq, 1), 0)
    cend = lax.broadcasted_iota(jnp.int32, (1, ncp), 1) * CMP_STRIDE + CMP_LEN
    cmask = cend <= qpos + 1
    kc = _mx(kc_ref[...])
    vc = _mx(vc_ref[...])
    scale = HEAD_DIM ** -0.5
    imp = jnp.zeros((tq, ncp), F32)
    for m in range(NSA_GROUP):
        sl = slice(m * HEAD_DIM, (m + 1) * HEAD_DIM)
        lg = lax.dot_general(_mx(q_ref[:, sl]), kc, NT_DIMS, preferred_element_type=F32) * scale
        p = _masked_softmax(lg, cmask)
        oc_ref[:, sl] = jnp.dot(_mx(p), vc, preferred_element_type=F32)
        imp = imp + p
    s_sel = jnp.dot(imp, a_ref[...], preferred_element_type=F32, precision=lax.Precision.HIGHEST)
    st = s_sel.T[:n_sel, :]
    blk = lax.broadcasted_iota(jnp.int32, (n_sel, tq), 0)
    cur = (i * tq + lax.broadcasted_iota(jnp.int32, (n_sel, tq), 1)) // SEL_LEN
    forced = (blk == 0) | (blk == cur) | (blk == cur - 1)
    allowed = blk <= cur
    v = jnp.where(allowed, jnp.where(forced, jnp.inf, st), -jnp.inf)
    rank = jnp.zeros((n_sel, tq), jnp.int32)
    for r in range(n_sel):
        vr = v[r:r + 1, :]
        before = (vr > v) | ((vr == v) & (blk > r))
        rank = rank + before.astype(jnp.int32)
    sel = ((rank < n_top) & allowed).astype(F32)
    if n_sel < 128:
        sel = jnp.concatenate([sel, jnp.zeros((128 - n_sel, tq), F32)], axis=0)
    sel_ref[...] = sel.T.astype(sel_ref.dtype)


def cmp_select(h, kvc, a_mat, nb, t):
    tq = _pick(t, (256, 128))
    nt = t // tq
    n_sel = t // SEL_LEN
    assert n_sel <= 128 and n_sel % 8 == 0
    ncp = kvc.shape[2]
    gw = NSA_GROUP * HEAD_DIM
    return pl.pallas_call(
        functools.partial(_cmp_select_kernel, tq=tq, n_sel=n_sel, n_top=min(SEL_TOP, n_sel)),
        grid=(nb, NSA_KV, nt),
        in_specs=[
            pl.BlockSpec((tq, gw), lambda b, g, i: (b * nt + i, g)),
            pl.BlockSpec((None, None, ncp, HEAD_DIM), lambda b, g, i: (b, g, 0, 0)),
            pl.BlockSpec((None, None, ncp, HEAD_DIM), lambda b, g, i: (b, NSA_KV + g, 0, 0)),
            pl.BlockSpec((ncp, 128), lambda b, g, i: (0, 0)),
        ],
        out_specs=[
            pl.BlockSpec((tq, gw), lambda b, g, i: (b * nt + i, g)),
            pl.BlockSpec((None, None, tq, 128), lambda b, g, i: (b, g, i, 0)),
        ],
        out_shape=[
            jax.ShapeDtypeStruct((nb * t, NSA_HEADS * HEAD_DIM), F32),
            jax.ShapeDtypeStruct((nb, NSA_KV, t, 128), MXU_DTYPE),
        ],
        compiler_params=_params("arbitrary", "arbitrary", "arbitrary"),
        name="cmp_select",
    )(h, kvc, kvc, a_mat)


def _flash_steps(mode, t, tq, tk):
    rows = []
    for qi in range(t // tq):
        q_lo, q_hi = qi * tq, qi * tq + tq - 1
        k_hi = q_hi // tk
        k_lo = k_hi if mode == "win" else 0
        for kt in range(k_lo, k_hi + 1):
            every_key_visible = kt * tk + tk - 1 <= q_lo
            rows.append((qi, kt, int(kt == k_lo), int(kt == k_hi), int(not every_key_visible)))
    return jnp.asarray(np.array(rows, np.int32).T)


def _flash_kernel(*refs, mode, tq, tk, heads):
    if mode == "fox":
        tab, q_ref, k_ref, v_ref, cq_ref, ck_ref, o_ref, m_s, acc = refs
    elif mode == "slc":
        tab, q_ref, k_ref, v_ref, sel_ref, o_ref, m_s, acc = refs
    else:
        tab, q_ref, k_ref, v_ref, kp_ref, vp_ref, o_ref, m_s, acc = refs
    step = pl.program_id(2)
    qi, kt = tab[0, step], tab[1, step]

    @pl.when(tab[2, step] == 1)
    def _():
        m_s[...] = jnp.full_like(m_s, NEG_BIG)
        acc[...] = jnp.zeros_like(acc)

    def update(causal):
        nkeys = 2 * tk if mode == "win" else tk
        ones = jnp.ones((nkeys, HEAD_DIM), MXU_DTYPE)
        if mode == "slc":
            kb = _mx(k_ref[...])
            v_aug = jnp.concatenate([_mx(v_ref[...]), ones], axis=1)
        elif mode == "win":
            kb = jnp.concatenate([_mx(kp_ref[...]), _mx(k_ref[...])], axis=0)
            v_aug = jnp.concatenate([jnp.concatenate([_mx(vp_ref[...]), _mx(v_ref[...])], axis=0), ones], axis=1)
        valid = None
        if causal or mode == "win":
            qpos = qi * tq + lax.broadcasted_iota(jnp.int32, (tq, nkeys), 0)
            kpos = (kt * tk - (nkeys - tk)) + lax.broadcasted_iota(jnp.int32, (tq, nkeys), 1)
            valid = kpos <= qpos
            if mode == "win":
                valid = valid & (kpos > qpos - WINDOW) & (kpos >= 0)
        if mode == "slc":
            blk = lax.broadcasted_iota(jnp.int32, (128, tk), 0)
            kblk = (kt * tk + lax.broadcasted_iota(jnp.int32, (128, tk), 1)) // SEL_LEN
            chosen = jnp.dot(sel_ref[...], (blk == kblk).astype(MXU_DTYPE), preferred_element_type=F32) > 0.5
            valid = chosen if valid is None else valid & chosen
        for hh in range(heads):
            cols = slice(hh * HEAD_DIM, (hh + 1) * HEAD_DIM)
            if mode == "fox":
                kb = _mx(k_ref[:, cols])
                v_aug = jnp.concatenate([_mx(v_ref[:, cols]), ones], axis=1)
                ck2 = ck_ref[hh] * LOG2E
                cq2 = cq_ref[hh] * LOG2E
            s = lax.dot_general(_mx(q_ref[:, cols]), kb, NT_DIMS, preferred_element_type=F32)
            x = s * (HEAD_DIM ** -0.5 * LOG2E)
            if mode == "fox":
                x = x - ck2
            if valid is not None:
                x = jnp.where(valid, x, NEG_BIG)
            top = jnp.max(x, axis=-1, keepdims=True)
            if mode == "fox":
                top = top + cq2
            m_old = m_s[hh]
            m_new = jnp.maximum(m_old, top)
            p = jnp.exp2(x - (m_new - cq2 if mode == "fox" else m_new))
            acc[hh] = jnp.exp2(m_old - m_new) * acc[hh] + jnp.dot(_mx(p), v_aug, preferred_element_type=F32)
            m_s[hh] = m_new

    if mode == "win":
        update(True)
    else:
        pl.when(tab[4, step] == 1)(functools.partial(update, True))
        pl.when(tab[4, step] == 0)(functools.partial(update, False))

    @pl.when(tab[3, step] == 1)
    def _():
        for hh in range(heads):
            l = acc[hh, :, HEAD_DIM:]
            o_ref[:, hh * HEAD_DIM:(hh + 1) * HEAD_DIM] = acc[hh, :, :HEAD_DIM] / jnp.where(l > 0, l, 1.0)


def flash_attention(mode, nb, t, q, qcol, k, kcol, v, vcol, extra=()):
    heads = NSA_GROUP
    tk = _pick(t, (512,) if mode == "win" else (1024, 512, 256, 128))
    tq = _pick(t, (512, 256, 128))
    ntq, ntk = t // tq, t // tk
    qw = heads * HEAD_DIM
    kw = qw if mode == "fox" else HEAD_DIM
    steps = _flash_steps(mode, t, tq, tk)
    in_specs = [
        pl.BlockSpec((tq, qw), lambda b, h, s, tab: (b * ntq + tab[0, s], qcol(h))),
        pl.BlockSpec((tk, kw), lambda b, h, s, tab: (b * ntk + tab[1, s], kcol(h))),
        pl.BlockSpec((tk, kw), lambda b, h, s, tab: (b * ntk + tab[1, s], vcol(h))),
    ]
    scratch = [pltpu.VMEM((heads, tq, 1), F32), pltpu.VMEM((heads, tq, 2 * HEAD_DIM), F32)]
    if mode == "fox":
        in_specs += [
            pl.BlockSpec((None, heads, tq, 1), lambda b, h, s, tab: (b, h, tab[0, s], 0)),
            pl.BlockSpec((None, heads, 1, tk), lambda b, h, s, tab: (b, h, 0, tab[1, s])),
        ]
    elif mode == "slc":
        in_specs += [pl.BlockSpec((None, None, tq, 128), lambda b, h, s, tab: (b, h, tab[0, s], 0))]
    else:
        assert tq == tk == WINDOW
        in_specs += [
            pl.BlockSpec((tk, kw), lambda b, h, s, tab: (b * ntk + jnp.maximum(tab[1, s] - 1, 0), kcol(h))),
            pl.BlockSpec((tk, kw), lambda b, h, s, tab: (b * ntk + jnp.maximum(tab[1, s] - 1, 0), vcol(h))),
        ]
        extra = (k, v)
    return pl.pallas_call(
        functools.partial(_flash_kernel, mode=mode, tq=tq, tk=tk, heads=heads),
        grid_spec=pltpu.PrefetchScalarGridSpec(
            num_scalar_prefetch=1,
            grid=(nb, NSA_HEADS // heads, steps.shape[1]),
            in_specs=in_specs,
            out_specs=pl.BlockSpec((tq, qw), lambda b, h, s, tab: (b * ntq + tab[0, s], h)),
            scratch_shapes=scratch,
        ),
        out_shape=jax.ShapeDtypeStruct((nb * t, NSA_HEADS * HEAD_DIM), F32),
        compiler_params=_params("arbitrary", "arbitrary", "arbitrary"),
        name="flash_" + mode,
    )(steps, q, k, v, *extra)


def _combine_kernel(oc_ref, os_ref, ow_ref, of_ref, g_ref, o_ref):
    gates = g_ref[...]
    nw = NSA_HEADS * HEAD_DIM
    for h in range(NSA_HEADS):
        sl = slice(h * HEAD_DIM, (h + 1) * HEAD_DIM)
        o = (gates[:, 3 * h:3 * h + 1] * oc_ref[:, sl] + gates[:, 3 * h + 1:3 * h + 2] * os_ref[:, sl]
             + gates[:, 3 * h + 2:3 * h + 3] * ow_ref[:, sl])
        o_ref[:, sl] = o.astype(o_ref.dtype)
    o_ref[:, nw:] = of_ref[...].astype(o_ref.dtype)


def combine_heads(o_c, o_s, o_w, o_f, gates):
    m, nw = o_c.shape
    tm = _pick(m, (512, 256, 128))
    spec = pl.BlockSpec((tm, nw), lambda i: (i, 0))
    return pl.pallas_call(
        _combine_kernel,
        grid=(m // tm,),
        in_specs=[spec, spec, spec, spec, pl.BlockSpec((tm, 128), lambda i: (i, 0))],
        out_specs=pl.BlockSpec((tm, 2 * nw), lambda i: (i, 0)),
        out_shape=jax.ShapeDtypeStruct((m, 2 * nw), MXU_DTYPE),
        compiler_params=_params("arbitrary"),
        name="combine_heads",
    )(o_c, o_s, o_w, o_f, gates)


def _gather_pages_kernel(pt_ref, *refs, page):
    *x_refs, o_ref = refs
    for j, x_ref in enumerate(x_refs):
        for c in range(2 * NSA_KV):
            o_ref[c, j * page:(j + 1) * page, :] = x_ref[:, c // NSA_KV, c % NSA_KV, :]


def gather_cmp_pages(pool, e, page_table):
    nb, n_pages = page_table.shape
    page = pool.shape[2]
    per_step = _pick(n_pages, (4, 2, 1))

    def page_spec(j):
        return pl.BlockSpec((None, None, page, 2, NSA_KV, HEAD_DIM),
                            lambda b, p, pt: (e, pt[b, p * per_step + j], 0, 0, 0, 0))

    return pl.pallas_call(
        functools.partial(_gather_pages_kernel, page=page),
        grid_spec=pltpu.PrefetchScalarGridSpec(
            num_scalar_prefetch=1,
            grid=(nb, n_pages // per_step),
            in_specs=[page_spec(j) for j in range(per_step)],
            out_specs=pl.BlockSpec((None, 2 * NSA_KV, per_step * page, HEAD_DIM), lambda b, p, pt: (b, 0, p, 0)),
        ),
        out_shape=jax.ShapeDtypeStruct((nb, 2 * NSA_KV, n_pages * page, HEAD_DIM), F32),
        compiler_params=_params("arbitrary", "arbitrary"),
        name="gather_cmp_pages",
    )(page_table, *([pool] * per_step))


def _sample_cmp_win_kernel(qraw_ref, qrot_ref, kc_ref, vc_ref, a_ref, kw_ref, vw_ref,
                           oc_ref, ow_ref, idx_ref, *, n_c, n_sel, n_top, q_pos):
    scale = HEAD_DIM ** -0.5
    ncp = kc_ref.shape[0]
    rows = qraw_ref.shape[0]
    cend = lax.broadcasted_iota(jnp.int32, (1, ncp), 1) * CMP_STRIDE + CMP_LEN
    cidx = lax.broadcasted_iota(jnp.int32, (1, ncp), 1)
    cmask = (cend <= q_pos + 1) & (cidx < n_c)
    lg = lax.dot_general(_mx(qraw_ref[...]), _mx(kc_ref[...]), NT_DIMS, preferred_element_type=F32) * scale
    p = _masked_softmax(lg, cmask)
    oc_ref[...] = jnp.dot(_mx(p), _mx(vc_ref[...]), preferred_element_type=F32)
    head = lax.broadcasted_iota(jnp.int32, p.shape, 0)
    imp = jnp.sum(jnp.where(head < NSA_GROUP, p, 0.0), axis=0, keepdims=True)
    imp = jnp.broadcast_to(imp, (rows, ncp))
    s_sel = jnp.dot(imp, a_ref[...], preferred_element_type=F32, precision=lax.Precision.HIGHEST)
    nsp = s_sel.shape[1]
    lane = lax.broadcasted_iota(jnp.int32, (rows, nsp), 1).astype(F32)
    cur = q_pos // SEL_LEN
    forced = (lane == 0.0) | (lane == float(cur)) | (lane == float(cur - 1))
    v = jnp.where(lane <= float(cur), jnp.where(forced, jnp.inf, s_sel), -jnp.inf)
    out_lane = lax.broadcasted_iota(jnp.int32, (rows, 128), 1)
    picked = jnp.full((rows, 128), -1.0, F32)
    for r in range(n_top):
        mx = jnp.max(v, axis=1, keepdims=True)
        ix = jnp.min(jnp.where(v == mx, lane, float(nsp)), axis=1, keepdims=True)
        ix = jnp.where(mx > -jnp.inf, ix, -1.0)
        picked = jnp.where(out_lane == r, ix, picked)
        v = jnp.where(lane == ix, -jnp.inf, v)
    idx_ref[...] = picked.astype(jnp.int32)
    lw = lax.dot_general(_mx(qrot_ref[...]), _mx(kw_ref[...]), NT_DIMS, preferred_element_type=F32) * scale
    pw = _masked_softmax(lw, jnp.full(lw.shape, True))
    ow_ref[...] = jnp.dot(_mx(pw), _mx(vw_ref[...]), preferred_element_type=F32)


def sample_cmp_win(qraw, qrot, kvc, a_mat, win, n_c, n_sel, q_pos):
    nb, _, rows, _ = qraw.shape
    ncp = kvc.shape[2]
    nsp = a_mat.shape[1]
    wlen = win.shape[1]
    qspec = pl.BlockSpec((None, None, rows, HEAD_DIM), lambda b, g: (b, g, 0, 0))
    return pl.pallas_call(
        functools.partial(_sample_cmp_win_kernel, n_c=n_c, n_sel=n_sel, n_top=min(SEL_TOP, n_sel), q_pos=q_pos),
        grid=(nb, NSA_KV),
        in_specs=[
            qspec, qspec,
            pl.BlockSpec((None, None, ncp, HEAD_DIM), lambda b, g: (b, g, 0, 0)),
            pl.BlockSpec((None, None, ncp, HEAD_DIM), lambda b, g: (b, NSA_KV + g, 0, 0)),
            pl.BlockSpec((ncp, nsp), lambda b, g: (0, 0)),
            pl.BlockSpec((None, wlen, HEAD_DIM), lambda b, g: (b, 0, g)),
            pl.BlockSpec((None, wlen, HEAD_DIM), lambda b, g: (b, 0, NSA_KV + g)),
        ],
        out_specs=[qspec, qspec, pl.BlockSpec((None, None, rows, 128), lambda b, g: (b, g, 0, 0))],
        out_shape=[
            jax.ShapeDtypeStruct(qraw.shape, F32),
            jax.ShapeDtypeStruct(qraw.shape, F32),
            jax.ShapeDtypeStruct((nb, NSA_KV, rows, 128), jnp.int32),
        ],
        compiler_params=_params("arbitrary", "arbitrary"),
        name="sample_cmp_win",
    )(qraw, qrot, kvc, kvc, a_mat, win, win)


def _sample_slc_kernel(pt_ref, idx_ref, q_ref, *refs, n_top, n_past_blk):
    kv_refs = refs[:n_top]
    kn_ref, vn_ref, o_ref = refs[n_top:]
    b, g = pl.program_id(0), pl.program_id(1)
    row = b * NSA_KV + g
    scale = HEAD_DIM ** -0.5

    def attend(gg):
        q = q_ref[...]
        qb = _mx(q)
        picked = [idx_ref[row, j] for j in range(n_top)]
        in_past = [(blk >= 0) & (blk < n_past_blk) for blk in picked]
        has_new = picked[0] == n_past_blk
        for blk in picked[1:]:
            has_new = has_new | (blk == n_past_blk)
        s_new = jnp.where(has_new, jnp.sum(q * kn_ref[...], axis=-1, keepdims=True) * scale, NEG_BIG)
        logits = []
        for j in range(n_top):
            s = lax.dot_general(qb, _mx(kv_refs[j][:, 0, gg, :]), NT_DIMS, preferred_element_type=F32) * scale
            logits.append(jnp.where(in_past[j], s, NEG_BIG))
        m = s_new
        for s in logits:
            m = jnp.maximum(m, jnp.max(s, axis=-1, keepdims=True))
        p_new = jnp.where(has_new, jnp.exp(s_new - m), 0.0)
        l = p_new
        o = p_new * vn_ref[...]
        for j in range(n_top):
            p = jnp.where(in_past[j], jnp.exp(logits[j] - m), 0.0)
            l = l + jnp.sum(p, axis=-1, keepdims=True)
            o = o + jnp.dot(_mx(p), _mx(kv_refs[j][:, 1, gg, :]), preferred_element_type=F32)
        o_ref[...] = o / jnp.where(l > 0, l, 1.0)

    for gg in range(NSA_KV):
        pl.when(g == gg)(functools.partial(attend, gg))


def sample_slc(qrot, pool, e, page_table, idx, slc_new, n_past_blk):
    nb, _, rows, _ = qrot.shape
    n_top = idx.shape[1]
    page = pool.shape[2]
    per_page = page // SEL_LEN

    def kv_spec(j):
        def index(b, g, pt, ix):
            blk = jnp.clip(ix[b * NSA_KV + g, j], 0, n_past_blk - 1)
            return (e, pt[b, blk // per_page], blk % per_page, 0, 0, 0)
        return pl.BlockSpec((None, None, SEL_LEN, 2, NSA_KV, HEAD_DIM), index)

    return pl.pallas_call(
        functools.partial(_sample_slc_kernel, n_top=n_top, n_past_blk=n_past_blk),
        grid_spec=pltpu.PrefetchScalarGridSpec(
            num_scalar_prefetch=2,
            grid=(nb, NSA_KV),
            in_specs=[
                pl.BlockSpec((None, None, rows, HEAD_DIM), lambda b, g, pt, ix: (b, g, 0, 0)),
                *[kv_spec(j) for j in range(n_top)],
                pl.BlockSpec((None, 1, HEAD_DIM), lambda b, g, pt, ix: (b, 0, g)),
                pl.BlockSpec((None, 1, HEAD_DIM), lambda b, g, pt, ix: (b, 0, NSA_KV + g)),
            ],
            out_specs=pl.BlockSpec((None, None, rows, HEAD_DIM), lambda b, g, pt, ix: (b, g, 0, 0)),
        ),
        out_shape=jax.ShapeDtypeStruct(qrot.shape, F32),
        compiler_params=_params("arbitrary", "arbitrary"),
        name="sample_slc",
    )(page_table, idx, qrot, *([pool] * n_top), slc_new, slc_new)


def _sample_fox_kernel(pt_ref, q_ref, *refs, n_steps, per_step):
    page_refs = refs[:3 * per_step]
    tri_ref, kn_ref, vn_ref, lfn_ref, o_ref, m_s, l_s, acc, crun = refs[3 * per_step:]
    step = pl.program_id(1)
    scale = HEAD_DIM ** -0.5
    nh = FOX_HEADS
    page = page_refs[0].shape[0]
    cols = page * nh

    @pl.when(step == 0)
    def _():
        m_s[...] = jnp.full_like(m_s, NEG_BIG)
        l_s[...] = jnp.zeros_like(l_s)
        acc[...] = jnp.zeros_like(acc)
        crun[...] = jnp.zeros_like(crun)

    qb = _mx(q_ref[...])
    tri = _mx(tri_ref[...])
    lane = lax.broadcasted_iota(jnp.int32, (nh, HEAD_DIM), 1) // nh
    own = (lax.broadcasted_iota(jnp.int32, (nh, cols), 1) % nh) == lax.broadcasted_iota(jnp.int32, (nh, cols), 0)
    c_off = crun[...]
    logits, values = [], []
    for j in range(per_step):
        k_ref, v_ref, lf_ref = page_refs[3 * j:3 * j + 3]
        lf = lf_ref[...]
        hi = _mx(lf).astype(F32)
        mid = _mx(lf - hi).astype(F32)
        lo = lf - hi - mid
        cs = jnp.dot(_mx(jnp.concatenate([hi, mid, lo], axis=0)), tri, preferred_element_type=F32)
        ck = cs[:nh] + cs[nh:2 * nh] + cs[2 * nh:] + c_off
        c_off = ck[:, -1:]
        ck_x = jnp.concatenate(
            [jnp.take_along_axis(ck, lane + i * (HEAD_DIM // nh), axis=1) for i in range(cols // HEAD_DIM)], axis=1)
        k2 = k_ref[...].reshape(cols, HEAD_DIM)
        s = lax.dot_general(qb, _mx(k2), NT_DIMS, preferred_element_type=F32) * scale - ck_x
        logits.append(jnp.where(own, s, NEG_BIG))
        values.append(_mx(v_ref[...].reshape(cols, HEAD_DIM)))
    crun[...] = c_off
    m_old = m_s[...]
    m_new = m_old
    for s in logits:
        m_new = jnp.maximum(m_new, jnp.max(s, axis=-1, keepdims=True))
    alpha = jnp.exp(m_old - m_new)
    l_new = alpha * l_s[...]
    o_new = alpha * acc[...]
    for s, v2 in zip(logits, values):
        p = jnp.where(own, jnp.exp(s - m_new), 0.0)
        l_new = l_new + jnp.sum(p, axis=-1, keepdims=True)
        o_new = o_new + jnp.dot(_mx(p), v2, preferred_element_type=F32)
    l_s[...] = l_new
    acc[...] = o_new
    m_s[...] = m_new

    @pl.when(step == n_steps - 1)
    def _():
        cq = c_off + lfn_ref[...]
        s_new = jnp.sum(q_ref[...] * kn_ref[...], axis=-1, keepdims=True) * scale - cq
        m_fin = jnp.maximum(m_new, s_new)
        p_new = jnp.exp(s_new - m_fin)
        a_fin = jnp.exp(m_new - m_fin)
        o_ref[...] = (a_fin * o_new + p_new * vn_ref[...]) / (a_fin * l_new + p_new)


def sample_fox(fq, pool, logf_t, e, page_table, k_new, v_new, logf_new):
    nb, n_pages = page_table.shape
    page = pool.shape[2]
    per_step = _pick(n_pages, (8, 4, 2, 1))
    tri = jnp.triu(jnp.ones((page, page), F32))

    def page_specs(j):
        def kv(which):
            return pl.BlockSpec((None, None, page, None, FOX_HEADS, HEAD_DIM),
                                lambda b, p, pt: (e, pt[b, p * per_step + j], 0, which, 0, 0))
        return [kv(0), kv(1),
                pl.BlockSpec((None, None, FOX_HEADS, page), lambda b, p, pt: (e, pt[b, p * per_step + j], 0, 0))]

    page_ops = []
    for j in range(per_step):
        page_ops += [pool, pool, logf_t]
    return pl.pallas_call(
        functools.partial(_sample_fox_kernel, n_steps=n_pages // per_step, per_step=per_step),
        grid_spec=pltpu.PrefetchScalarGridSpec(
            num_scalar_prefetch=1,
            grid=(nb, n_pages // per_step),
            in_specs=[
                pl.BlockSpec((None, FOX_HEADS, HEAD_DIM), lambda b, p, pt: (b, 0, 0)),
                *[spec for j in range(per_step) for spec in page_specs(j)],
                pl.BlockSpec((page, page), lambda b, p, pt: (0, 0)),
                pl.BlockSpec((None, FOX_HEADS, HEAD_DIM), lambda b, p, pt: (b, 0, 0)),
                pl.BlockSpec((None, FOX_HEADS, HEAD_DIM), lambda b, p, pt: (b, 0, 0)),
                pl.BlockSpec((None, FOX_HEADS, 1), lambda b, p, pt: (b, 0, 0)),
            ],
            out_specs=pl.BlockSpec((None, FOX_HEADS, HEAD_DIM), lambda b, p, pt: (b, 0, 0)),
            scratch_shapes=[
                pltpu.VMEM((FOX_HEADS, 1), F32), pltpu.VMEM((FOX_HEADS, 1), F32),
                pltpu.VMEM((FOX_HEADS, HEAD_DIM), F32), pltpu.VMEM((FOX_HEADS, 1), F32),
            ],
        ),
        out_shape=jax.ShapeDtypeStruct((nb, FOX_HEADS, HEAD_DIM), F32),
        compiler_params=_params("arbitrary", "arbitrary"),
        name="sample_fox",
    )(page_table, fq, *page_ops, tri, k_new, v_new, logf_new)


def _ret_tables(chunk):
    lg = jnp.log1p(-(2.0 ** (-5.0 - jnp.arange(RET_HEADS, dtype=F32))))
    i = jnp.arange(chunk, dtype=F32)
    diff = i[:, None] - i[None, :]
    dec = jnp.where(diff >= 0, jnp.exp(jnp.maximum(diff, 0.0)[None] * lg[:, None, None]), 0.0)
    xi = jnp.exp((i[None, :] + 1.0) * lg[:, None])
    zeta = jnp.exp((chunk - 1.0 - i)[None, :] * lg[:, None])
    g_c = jnp.exp(chunk * lg)
    return dec, xi, zeta, g_c


def _ret_rope_tables(pos):
    half = RET_DK // 2
    inv = 1.0 / (RET_THETA ** (jnp.arange(half, dtype=F32) / half))
    ang = pos.astype(F32)[:, None] * inv[None, :]
    return jnp.cos(ang), jnp.sin(ang)


def _group_norm_gate(o, gate, gn):
    mu = jnp.mean(o, axis=-1, keepdims=True)
    d = o - mu
    var = jnp.mean(d * d, axis=-1, keepdims=True)
    y = d * lax.rsqrt(var + GN_EPS) * gn
    return gate * jax.nn.sigmoid(gate) * y


def _retention_kernel(q_ref, k_ref, v_ref, gate_ref, cos_ref, sin_ref, dec_ref, coef_ref, gn_ref,
                      y_ref, s_ref, state, *, n_chunks, heads):
    c = pl.program_id(2)
    half = RET_DK // 2

    @pl.when(c == 0)
    def _():
        state[...] = jnp.zeros_like(state)

    cos, sin = cos_ref[...], sin_ref[...]

    def rot(x_ref, hh):
        x1 = x_ref[:, hh * RET_DK:hh * RET_DK + half]
        x2 = x_ref[:, hh * RET_DK + half:(hh + 1) * RET_DK]
        return jnp.concatenate([x1 * cos - x2 * sin, x2 * cos + x1 * sin], axis=-1)

    for hh in range(heads):
        vcols = slice(hh * RET_DV, (hh + 1) * RET_DV)
        coef = coef_ref[hh]
        xi, zeta, g_c = coef[:, 0:1], coef[:, 1:2], coef[0:1, 2:3]
        q = _mx(rot(q_ref, hh))
        kf = rot(k_ref, hh) * (RET_DK ** -0.5)
        v = _mx(v_ref[:, vcols])
        s_old = state[hh]
        a = lax.dot_general(q, _mx(kf), NT_DIMS, preferred_element_type=F32) * dec_ref[hh]
        o = (jnp.dot(_mx(a), v, preferred_element_type=F32)
             + jnp.dot(q, _mx(s_old), preferred_element_type=F32) * xi)
        state[hh] = s_old * g_c + lax.dot_general(_mx(kf * zeta), v, TN_DIMS, preferred_element_type=F32)
        y_ref[:, vcols] = _group_norm_gate(o, gate_ref[:, vcols], gn_ref[:, vcols]).astype(y_ref.dtype)

    @pl.when(c == n_chunks - 1)
    def _():
        s_ref[...] = state[...]


def retention_prompt(h, nb, t, gn_g, o_idx):
    ch = RET_CHUNK
    n_chunks = t // ch
    dec, xi, zeta, g_c = _ret_tables(ch)
    coef = jnp.stack([xi, zeta, jnp.broadcast_to(g_c[:, None], xi.shape)], axis=-1)
    coef = jnp.pad(coef, ((0, 0), (0, 0), (0, 128 - 3)))
    cos, sin = _ret_rope_tables(jnp.arange(t))
    hs = RET_HEADS_PER_STEP
    groups = RET_HEADS // hs
    kw, vw = hs * RET_DK, hs * RET_DV
    kb, vb, gb = O_RK // kw, O_RV // vw, O_RG // vw
    return pl.pallas_call(
        functools.partial(_retention_kernel, n_chunks=n_chunks, heads=hs),
        grid=(nb, groups, n_chunks),
        in_specs=[
            pl.BlockSpec((ch, kw), lambda b, hg, c: (b * n_chunks + c, hg)),
            pl.BlockSpec((ch, kw), lambda b, hg, c: (b * n_chunks + c, kb + hg)),
            pl.BlockSpec((ch, vw), lambda b, hg, c: (b * n_chunks + c, vb + hg)),
            pl.BlockSpec((ch, vw), lambda b, hg, c: (b * n_chunks + c, gb + hg)),
            pl.BlockSpec((ch, RET_DK // 2), lambda b, hg, c: (c, 0)),
            pl.BlockSpec((ch, RET_DK // 2), lambda b, hg, c: (c, 0)),
            pl.BlockSpec((hs, ch, ch), lambda b, hg, c: (hg, 0, 0)),
            pl.BlockSpec((hs, ch, 128), lambda b, hg, c: (hg, 0, 0)),
            pl.BlockSpec((None, 1, vw), lambda b, hg, c: (o_idx, 0, hg)),
        ],
        out_specs=[
            pl.BlockSpec((ch, vw), lambda b, hg, c: (b * n_chunks + c, hg)),
            pl.BlockSpec((None, hs, RET_DK, RET_DV), lambda b, hg, c: (b, hg, 0, 0)),
        ],
        out_shape=[
            jax.ShapeDtypeStruct((nb * t, RET_HEADS * RET_DV), MXU_DTYPE),
            jax.ShapeDtypeStruct((nb, RET_HEADS, RET_DK, RET_DV), F32),
        ],
        scratch_shapes=[pltpu.VMEM((hs, RET_DK, RET_DV), F32)],
        compiler_params=_params("arbitrary", "arbitrary", "arbitrary"),
        name="retention_prompt",
    )(h, h, h, h, cos, sin, dec, coef, gn_g[:, None, :])


def _retention_step_kernel(q_ref, k_ref, v_ref, gate_ref, cos_ref, sin_ref, coef_ref, gn_ref, s0_ref,
                           y_ref, s_ref):
    half = RET_DK // 2
    hh = pl.program_id(1)
    cos, sin = cos_ref[...], sin_ref[...]

    def rot(x_ref):
        x1, x2 = x_ref[:half, :], x_ref[half:, :]
        return jnp.concatenate([x1 * cos - x2 * sin, x2 * cos + x1 * sin], axis=0)

    head = lax.broadcasted_iota(jnp.int32, coef_ref.shape, 0)
    coef = jnp.sum(jnp.where(head == hh, coef_ref[...], 0.0), axis=0, keepdims=True)
    dec, xi, zeta, g_c = coef[:, 0:1], coef[:, 1:2], coef[:, 2:3], coef[:, 3:4]
    q = rot(q_ref)
    k = rot(k_ref) * (RET_DK ** -0.5)
    v = v_ref[...]
    s_old = s0_ref[...]
    a = jnp.sum(q * k, axis=0, keepdims=True) * dec
    o = a * v + jnp.sum(q * s_old, axis=0, keepdims=True) * xi
    s_ref[...] = s_old * g_c + (k * zeta) * v
    y_ref[...] = _group_norm_gate(o, gate_ref[...], gn_ref[...])


def retention_step(q_col, k_col, v_row, gate_row, pos, s0, o_idx, gn_g):
    nb = q_col.shape[0]
    dec, xi, zeta, g_c = _ret_tables(1)
    coef = jnp.stack([dec[:, 0, 0], xi[:, 0], zeta[:, 0], g_c], axis=-1)
    coef = jnp.pad(coef, ((0, 0), (0, 128 - 4)))
    cos, sin = _ret_rope_tables(pos)
    col = pl.BlockSpec((None, None, RET_DK, 1), lambda b, hh: (b, hh, 0, 0))
    row = pl.BlockSpec((None, None, 1, RET_DV), lambda b, hh: (b, hh, 0, 0))
    tab = pl.BlockSpec((RET_DK // 2, 1), lambda b, hh: (0, 0))
    return pl.pallas_call(
        _retention_step_kernel,
        grid=(nb, RET_HEADS),
        in_specs=[
            col, col, row, row, tab, tab,
            pl.BlockSpec((RET_HEADS, 128), lambda b, hh: (0, 0)),
            pl.BlockSpec((None, 1, RET_DV), lambda b, hh: (o_idx, 0, hh)),
            pl.BlockSpec((None, None, None, RET_DK, RET_DV), lambda b, hh: (o_idx, b, hh, 0, 0)),
        ],
        out_specs=[row, pl.BlockSpec((None, None, RET_DK, RET_DV), lambda b, hh: (b, hh, 0, 0))],
        out_shape=[
            jax.ShapeDtypeStruct((nb, RET_HEADS, 1, RET_DV), F32),
            jax.ShapeDtypeStruct((nb, RET_HEADS, RET_DK, RET_DV), F32),
        ],
        compiler_params=_params("arbitrary", "arbitrary"),
        name="retention_step",
    )(q_col, k_col, v_row, gate_row, cos.reshape(-1, 1), sin.reshape(-1, 1), coef, gn_g[:, None, :], s0)


def _pad_rows(x, rows):
    return jnp.pad(x, ((0, rows - x.shape[0]),) + ((0, 0),) * (x.ndim - 1))


def _repack_even(w):
    small = jnp.concatenate([w[:, O_NG:O_FQ], w[:, O_FF:E_EVEN]], axis=1)
    small = jnp.pad(small, ((0, 0), (0, 128 - small.shape[1])))
    return jnp.concatenate([w[:, O_NQ:O_NG], w[:, O_FQ:O_FF], small], axis=1)


def _even_layer(e, xb_p, xb_s, nb, t, ns, past, caches, page_table, wts):
    cache_cmp, cache_slc, cache_win, cache_fox, logf_pool_t = caches
    w_in, fox_f_bias, cmp_pos, cmp_w1, cmp_w2 = wts
    w_rep = _repack_even(w_in[e])
    fb_row = jnp.zeros((1, 128), F32).at[0, N_GATE:N_GATE + FOX_HEADS].set(fox_f_bias[e])
    kvw = 2 * NSA_KV * HEAD_DIM
    nw = NSA_HEADS * HEAD_DIM

    h_p, h_s = matmul(xb_p, xb_s, w_rep, (), (1152, 640, 384, 128))
    tm_post = _pick(nb * t, (512, 256, 128))
    q_rot, slc_p, win_p, gates_p, logf_p = even_post(
        h_p, _rope_tables(jnp.arange(t)), fb_row, t // tm_post)
    c_p = cumsum_rows(logf_p, nb, t)
    c_heads = jnp.swapaxes(c_p[:, N_GATE:N_GATE + FOX_HEADS].reshape(nb, t, FOX_HEADS), 1, 2)
    n16 = t // CMP_STRIDE
    kvc_p = nsa_compress(
        h_p, pl.BlockSpec((t, HEAD_DIM), lambda b, c: (b, P_NKV // HEAD_DIM + c)), nb, t, e, cmp_w1, cmp_pos, cmp_w2)
    a_p = _sel_matrix(n16, n16 - 1, t // SEL_LEN, 128)
    o_c, sel = cmp_select(h_p, kvc_p, a_p, nb, t)
    o_s = flash_attention("slc", nb, t, q_rot, lambda g: g, slc_p, lambda g: g, slc_p, lambda g: NSA_KV + g, (sel,))
    o_w = flash_attention("win", nb, t, q_rot, lambda g: g, win_p, lambda g: g, win_p, lambda g: NSA_KV + g)
    fq0 = P_FQ // (NSA_GROUP * HEAD_DIM)
    fgroups = FOX_HEADS // NSA_GROUP
    o_f = flash_attention("fox", nb, t, h_p, lambda g: fq0 + g, h_p, lambda g: fq0 + fgroups + g,
                          h_p, lambda g: fq0 + 2 * fgroups + g, (c_heads[..., None], c_heads[:, :, None, :]))
    xcat_p = combine_heads(o_c, o_s, o_w, o_f, gates_p)
    st_p = (
        h_p[:, P_NKV:P_NKV + kvw].reshape(nb, t, 2, NSA_KV, HEAD_DIM),
        slc_p.reshape(nb, t, 2, NSA_KV, HEAD_DIM),
        win_p.reshape(nb, t, 2, NSA_KV, HEAD_DIM)[:, t - min(WINDOW, t):],
        h_p[:, P_FQ + nw:P_FQ + 3 * nw].reshape(nb, t, 2, FOX_HEADS, HEAD_DIM),
        logf_p[:, N_GATE:N_GATE + FOX_HEADS].reshape(nb, t, FOX_HEADS),
    )

    rows = xb_s.shape[0]
    q_rot_s, slc_s, win_s, gates_s, logf_s = even_post(
        h_s, _rope_tables(jnp.full((rows,), past)), fb_row, 1)
    new_win = jnp.concatenate(
        [cache_win[e].reshape(ns, -1, kvw), win_s[:ns, None, :]], axis=1)[:, 1:]
    cmp_rows = gather_cmp_pages(cache_cmp, e, page_table)
    l_tot = past + 1
    n16_s = l_tot // CMP_STRIDE
    assert n16_s * CMP_STRIDE == past
    kvc_s = nsa_compress(
        cmp_rows, pl.BlockSpec((None, None, past, HEAD_DIM), lambda b, c: (b, c, 0, 0)), ns, past, e,
        cmp_w1, cmp_pos, cmp_w2)
    n_sel_s = -(-l_tot // SEL_LEN)
    nsp = -(-n_sel_s // 128) * 128
    a_s = _sel_matrix(n16_s, n16_s - 1, n_sel_s, nsp)

    def q_rows(x):
        x = x[:ns].reshape(ns, NSA_KV, NSA_GROUP, HEAD_DIM)
        return jnp.pad(x, ((0, 0), (0, 0), (0, 8 - NSA_GROUP), (0, 0)))

    qraw_g, qrot_g = q_rows(h_s[:, :nw]), q_rows(q_rot_s)
    o_c_s, o_w_s, idx = sample_cmp_win(qraw_g, qrot_g, kvc_s, a_s, new_win, n16_s - 1, n_sel_s, past)
    idx2 = idx[:, :, 0, :min(SEL_TOP, n_sel_s)].reshape(ns * NSA_KV, -1)
    o_s_s = sample_slc(qrot_g, cache_slc, e, page_table, idx2, slc_s[:ns, None, :], past // SEL_LEN)
    fq = h_s[:ns, P_FQ:P_FQ + nw].reshape(ns, FOX_HEADS, HEAD_DIM)
    fk = h_s[:ns, P_FQ + nw:P_FQ + 2 * nw]
    fv = h_s[:ns, P_FQ + 2 * nw:P_FQ + 3 * nw]
    logf_new = logf_s[:ns, N_GATE:N_GATE + FOX_HEADS]
    o_f_s = sample_fox(fq, cache_fox, logf_pool_t, e, page_table, fk.reshape(ns, FOX_HEADS, HEAD_DIM),
                       fv.reshape(ns, FOX_HEADS, HEAD_DIM), logf_new[:, :, None])

    def heads_flat(x):
        return _pad_rows(x[:, :, :NSA_GROUP].reshape(ns, nw), rows)

    xcat_s = combine_heads(heads_flat(o_c_s), heads_flat(o_s_s), heads_flat(o_w_s),
                           _pad_rows(o_f_s.reshape(ns, nw), rows), gates_s)
    st_s = (
        h_s[:ns, P_NKV:P_NKV + kvw].reshape(ns, 1, 2, NSA_KV, HEAD_DIM),
        slc_s[:ns].reshape(ns, 1, 2, NSA_KV, HEAD_DIM),
        new_win.reshape(ns, -1, 2, NSA_KV, HEAD_DIM),
        h_s[:ns, P_FQ + nw:P_FQ + 3 * nw].reshape(ns, 1, 2, FOX_HEADS, HEAD_DIM),
        logf_new.reshape(ns, 1, FOX_HEADS),
    )
    return xcat_p, xcat_s, st_p, st_s


def _odd_layer(o, xb_p, xb_s, nb, t, ns, past, state_ret, w_in_odd, ret_gn_g):
    h_p, h_s = matmul(xb_p, xb_s, w_in_odd, (o,), (1024, 512, 256, 128))
    y_p, s_p = retention_prompt(h_p, nb, t, ret_gn_g, o)
    rows = xb_s.shape[0]
    h_s = h_s[:ns]
    q_col = h_s[:, :O_RK].reshape(ns, RET_HEADS, RET_DK, 1)
    k_col = h_s[:, O_RK:O_RV].reshape(ns, RET_HEADS, RET_DK, 1)
    v_row = h_s[:, O_RV:O_RG].reshape(ns, RET_HEADS, 1, RET_DV)
    g_row = h_s[:, O_RG:E_ODD].reshape(ns, RET_HEADS, 1, RET_DV)
    y_s, s_s = retention_step(q_col, k_col, v_row, g_row, jnp.full((1,), past), state_ret, o, ret_gn_g)
    y_s = _pad_rows(y_s.reshape(ns, RET_HEADS * RET_DV), rows).astype(MXU_DTYPE)
    return y_p, y_s, s_p, s_s


def kernel(x_prompt, x_sample, cache_nsa_cmp, cache_nsa_slc, cache_nsa_win, cache_fox_kv, cache_fox_logf,
           state_ret, page_table, w_ffn_in, w_ffn_out, ln_g, ln_b, w_in_even, w_out_even, fox_f_bias,
           nsa_cmp_pos, nsa_cmp_w1, nsa_cmp_w2, w_in_odd, ret_gn_g, w_out_odd):
    nb, t, d = x_prompt.shape
    ns, ts, _ = x_sample.shape
    depth = w_ffn_in.shape[0]
    n_pages = page_table.shape[1]
    page = cache_nsa_cmp.shape[2]
    past = n_pages * page
    assert ts == 1 and past >= WINDOW and past % SEL_LEN == 0 and t % RET_CHUNK == 0
    alpha = (2.0 * depth) ** 0.25
    rows = max(16, -(-ns // 16) * 16)

    xp = x_prompt.reshape(nb * t, d)
    xs = _pad_rows(x_sample.reshape(ns * ts, d), rows)
    xp_b, xs_b = xp.astype(MXU_DTYPE), xs.astype(MXU_DTYPE)

    caches = (cache_nsa_cmp, cache_nsa_slc, cache_nsa_win, cache_fox_kv, jnp.swapaxes(cache_fox_logf, 2, 3))
    w_ffn_out, w_out_even, w_out_odd = _mx(w_ffn_out), _mx(w_out_even), _mx(w_out_odd)

    def ffn(xp, xp_b, xs, xs_b, l, s):
        hid_p, hid_s = swiglu_hidden(xp_b, xs_b, w_ffn_in, l, s)
        return (matmul_postnorm(hid_p, w_ffn_out, (l, s), xp, ln_g, ln_b, (l, 2 * s), alpha, 0.5)
                + matmul_postnorm(hid_s, w_ffn_out, (l, s), xs, ln_g, ln_b, (l, 2 * s), alpha, 0.5))

    new_p = [[] for _ in range(6)]
    new_s = [[] for _ in range(6)]
    for l in range(depth):
        xp, xp_b, xs, xs_b = ffn(xp, xp_b, xs, xs_b, l, 0)
        if l % 2 == 0:
            e = l // 2
            mp, ms, st_p, st_s = _even_layer(
                e, xp_b, xs_b, nb, t, ns, past, caches, page_table,
                (w_in_even, fox_f_bias, nsa_cmp_pos, nsa_cmp_w1, nsa_cmp_w2))
            w_out, widx = w_out_even, (e,)
            slots = (0, 1, 2, 3, 4)
        else:
            o = l // 2
            mp, ms, sp, ss = _odd_layer(o, xp_b, xs_b, nb, t, ns, past, state_ret, w_in_odd, ret_gn_g)
            st_p, st_s = (sp,), (ss,)
            w_out, widx = w_out_odd, (o,)
            slots = (5,)
        for i, a_p, a_s in zip(slots, st_p, st_s):
            new_p[i].append(a_p)
            new_s[i].append(a_s)
        xp, xp_b = matmul_postnorm(mp, w_out, widx, xp, ln_g, ln_b, (l, 1), alpha, 1.0)
        xs, xs_b = matmul_postnorm(ms, w_out, widx, xs, ln_g, ln_b, (l, 1), alpha, 1.0)
        xp, xp_b, xs, xs_b = ffn(xp, xp_b, xs, xs_b, l, 1)

    outs = [xp.reshape(nb, t, d), xs[:ns].reshape(ns, ts, d)]
    for i in range(6):
        outs.append(jnp.stack(new_p[i]))
        outs.append(jnp.stack(new_s[i]))
    return tuple(outs)
```

```python
import functools
import math

import numpy as np
import jax
import jax.numpy as jnp
from jax import lax
from jax.experimental import pallas as pl
from jax.experimental.pallas import tpu as pltpu

F32 = jnp.float32
MXU_DTYPE = jnp.bfloat16

HEAD_DIM = 128
NSA_HEADS = 8
NSA_KV = 2
NSA_GROUP = NSA_HEADS // NSA_KV
CMP_LEN = 32
CMP_STRIDE = 16
CMP_HIDDEN = 256
SEL_LEN = 64
SEL_TOP = 16
WINDOW = 512
FOX_HEADS = 8
ROPE_THETA = 500000.0
ROPE_DIMS = HEAD_DIM // 4
RET_HEADS = 8
RET_DK = 256
RET_DV = 512
RET_CHUNK = 128
RET_THETA = 10000.0
RET_HEADS_PER_STEP = 8
LN_EPS = 1e-5
GN_EPS = 1e-6

O_NQ = 0
O_NKV = O_NQ + NSA_HEADS * HEAD_DIM
O_NG = O_NKV + 6 * NSA_KV * HEAD_DIM
O_FQ = O_NG + 3 * NSA_HEADS
O_FK = O_FQ + FOX_HEADS * HEAD_DIM
O_FV = O_FK + FOX_HEADS * HEAD_DIM
O_FF = O_FV + FOX_HEADS * HEAD_DIM
E_EVEN = O_FF + FOX_HEADS
P_NKV = NSA_HEADS * HEAD_DIM
P_FQ = P_NKV + 6 * NSA_KV * HEAD_DIM
P_SMALL = P_FQ + 3 * FOX_HEADS * HEAD_DIM
P_EVEN = P_SMALL + 128
N_GATE = 3 * NSA_HEADS
O_RK = RET_HEADS * RET_DK
O_RV = 2 * RET_HEADS * RET_DK
O_RG = O_RV + RET_HEADS * RET_DV
E_ODD = O_RG + RET_HEADS * RET_DV

V7X_VMEM_LIMIT = 56 * 1024 * 1024
LN_ROWS = 128
LOG2E = 1.4426950408889634
NEG_BIG = -1e30
NT_DIMS = (((1,), (1,)), ((), ()))
TN_DIMS = (((0,), (0,)), ((), ()))


def _params(*sem):
    return pltpu.CompilerParams(dimension_semantics=sem, vmem_limit_bytes=V7X_VMEM_LIMIT)


def _mx(x):
    return x.astype(MXU_DTYPE)


def _pick(n, prefs):
    for p in prefs:
        if n % p == 0:
            return p
    return n


def _swiglu_kernel(x_ref, xs_ref, wa_ref, wb_ref, o_ref, os_ref, wa_s, wb_s):
    def hidden(x):
        a = jnp.dot(x, wa_s[...], preferred_element_type=F32)
        b = jnp.dot(x, wb_s[...], preferred_element_type=F32)
        return (a * jax.nn.sigmoid(a) * b).astype(o_ref.dtype)

    @pl.when(pl.program_id(1) == 0)
    def _():
        wa_s[...] = _mx(wa_ref[...])
        wb_s[...] = _mx(wb_ref[...])
        os_ref[...] = hidden(xs_ref[...])

    o_ref[...] = hidden(x_ref[...])


def swiglu_hidden(xb, xsb, w_in, l, s):
    m, d = xb.shape
    ms = xsb.shape[0]
    f = w_in.shape[-1] // 2
    tm = _pick(m, (1024, 512, 256, 128))
    tn = _pick(f, (512, 256, 128))
    nj = f // tn
    return pl.pallas_call(
        _swiglu_kernel,
        grid=(nj, m // tm),
        in_specs=[
            pl.BlockSpec((tm, d), lambda j, i: (i, 0)),
            pl.BlockSpec((ms, d), lambda j, i: (0, 0)),
            pl.BlockSpec((None, None, d, tn), lambda j, i: (l, s, 0, j)),
            pl.BlockSpec((None, None, d, tn), lambda j, i: (l, s, 0, j + nj)),
        ],
        out_specs=[pl.BlockSpec((tm, tn), lambda j, i: (i, j)), pl.BlockSpec((ms, tn), lambda j, i: (0, j))],
        out_shape=[jax.ShapeDtypeStruct((m, f), MXU_DTYPE), jax.ShapeDtypeStruct((ms, f), MXU_DTYPE)],
        scratch_shapes=[pltpu.VMEM((d, tn), MXU_DTYPE), pltpu.VMEM((d, tn), MXU_DTYPE)],
        compiler_params=_params("arbitrary", "arbitrary"),
        name="swiglu_hidden",
    )(xb, xsb, w_in, w_in)


def _mm_ln_kernel(x_ref, w_ref, r_ref, g_ref, b_ref, o_ref, ob_ref, *, alpha, scale, nk):
    k = pl.program_id(1)
    def part():
        return jnp.dot(x_ref[...], _mx(w_ref[...]), preferred_element_type=F32)

    @pl.when(k == 0)
    def _():
        o_ref[...] = part()

    @pl.when(k > 0)
    def _():
        o_ref[...] += part()

    @pl.when(k == nk - 1)
    def _():
        rows = min(LN_ROWS, o_ref.shape[0])

        def norm_rows(c, carry):
            sl = pl.ds(pl.multiple_of(c * rows, rows), rows)
            z = alpha * r_ref[sl, :] + scale * o_ref[sl, :]
            mu = jnp.mean(z, axis=-1, keepdims=True)
            dz = z - mu
            var = jnp.mean(dz * dz, axis=-1, keepdims=True)
            y = dz * lax.rsqrt(var + LN_EPS) * g_ref[...] + b_ref[...]
            o_ref[sl, :] = y
            ob_ref[sl, :] = y.astype(ob_ref.dtype)
            return carry

        lax.fori_loop(0, o_ref.shape[0] // rows, norm_rows, 0)


def matmul_postnorm(xb, w, widx, res, g, b, gidx, alpha, scale):
    m, kdim = xb.shape
    n = w.shape[-1]
    tm = _pick(m, (512, 256, 128))
    tk = _pick(kdim, (2816, 2048, 1024, 512, 256, 128))
    nk = kdim // tk
    nlead = len(widx)
    w_spec = pl.BlockSpec((None,) * nlead + (tk, n), lambda i, k: tuple(widx) + (k, 0))
    g_spec = pl.BlockSpec((None, None, 1, n), lambda i, k: tuple(gidx) + (0, 0))
    return pl.pallas_call(
        functools.partial(_mm_ln_kernel, alpha=alpha, scale=scale, nk=nk),
        grid=(m // tm, nk),
        in_specs=[
            pl.BlockSpec((tm, tk), lambda i, k: (i, k)),
            w_spec,
            pl.BlockSpec((tm, n), lambda i, k: (i, 0)),
            g_spec, g_spec,
        ],
        out_specs=[pl.BlockSpec((tm, n), lambda i, k: (i, 0)), pl.BlockSpec((tm, n), lambda i, k: (i, 0))],
        out_shape=[jax.ShapeDtypeStruct((m, n), F32), jax.ShapeDtypeStruct((m, n), MXU_DTYPE)],
        compiler_params=_params("arbitrary", "arbitrary"),
        name="matmul_postnorm",
    )(xb, w, res, g[:, :, None, :], b[:, :, None, :])


def _mm_kernel(x_ref, xs_ref, w_ref, o_ref, os_ref, w_s):
    @pl.when(pl.program_id(1) == 0)
    def _():
        w_s[...] = _mx(w_ref[...])
        os_ref[...] = jnp.dot(xs_ref[...], w_s[...], preferred_element_type=F32)
    o_ref[...] = jnp.dot(x_ref[...], w_s[...], preferred_element_type=F32)


def matmul(xb, xsb, w, widx, tn_prefs):
    m, kdim = xb.shape
    ms = xsb.shape[0]
    n = w.shape[-1]
    tm = _pick(m, (1024, 512, 256, 128))
    tn = _pick(n, tn_prefs)
    nlead = len(widx)
    return pl.pallas_call(
        _mm_kernel,
        grid=(n // tn, m // tm),
        in_specs=[
            pl.BlockSpec((tm, kdim), lambda j, i: (i, 0)),
            pl.BlockSpec((ms, kdim), lambda j, i: (0, 0)),
            pl.BlockSpec((None,) * nlead + (kdim, tn), lambda j, i: tuple(widx) + (0, j)),
        ],
        out_specs=[pl.BlockSpec((tm, tn), lambda j, i: (i, j)), pl.BlockSpec((ms, tn), lambda j, i: (0, j))],
        out_shape=[jax.ShapeDtypeStruct((m, n), F32), jax.ShapeDtypeStruct((ms, n), F32)],
        scratch_shapes=[pltpu.VMEM((kdim, tn), MXU_DTYPE)],
        compiler_params=_params("arbitrary", "arbitrary"),
        name="matmul",
    )(xb, xsb, w)


def _rope_tables(pos):
    half = ROPE_DIMS // 2
    inv = 1.0 / (ROPE_THETA ** (jnp.arange(half, dtype=F32) / half))
    ang = pos.astype(F32)[:, None] * inv[None, :]
    cos, sin = jnp.cos(ang), jnp.sin(ang)
    n = pos.shape[0]
    ones = jnp.ones((n, HEAD_DIM - ROPE_DIMS), F32)
    zeros = jnp.zeros((n, HEAD_DIM - ROPE_DIMS), F32)
    zh = jnp.zeros((n, half), F32)
    c = jnp.concatenate([cos, cos, ones], -1)
    a = jnp.concatenate([-sin, zh, zeros], -1)
    b = jnp.concatenate([zh, sin, zeros], -1)
    return c, a, b


def _rope(x, c, a, b):
    half = ROPE_DIMS // 2
    return x * c + pltpu.roll(x, HEAD_DIM - half, 1) * a + pltpu.roll(x, half, 1) * b


def _even_post_kernel(q_ref, slc_ref, win_ref, sm_ref, c_ref, a_ref, b_ref, fb_ref,
                      qr_ref, slco_ref, wino_ref, gate_ref, logf_ref):
    c, a, b = c_ref[...], a_ref[...], b_ref[...]
    for h in range(NSA_HEADS):
        sl = slice(h * HEAD_DIM, (h + 1) * HEAD_DIM)
        qr_ref[:, sl] = _rope(q_ref[:, sl], c, a, b)
    for src, dst in ((slc_ref, slco_ref), (win_ref, wino_ref)):
        for g in range(NSA_KV):
            sl = slice(g * HEAD_DIM, (g + 1) * HEAD_DIM)
            dst[:, sl] = _rope(src[:, sl], c, a, b)
        vs = slice(NSA_KV * HEAD_DIM, 2 * NSA_KV * HEAD_DIM)
        dst[:, vs] = src[:, vs]
    sm = sm_ref[...]
    gate_ref[...] = jax.nn.sigmoid(sm)
    z = sm + fb_ref[...]
    logf_ref[...] = jnp.minimum(z, 0.0) - jnp.log1p(jnp.exp(-jnp.abs(z)))


def even_post(h, tabs, fb_row, npos_blocks):
    m = h.shape[0]
    tm = _pick(m, (512, 256, 128))
    kvw = 2 * NSA_KV * HEAD_DIM
    tab_spec = pl.BlockSpec((tm, HEAD_DIM), lambda i: (i % npos_blocks, 0))
    return pl.pallas_call(
        _even_post_kernel,
        grid=(m // tm,),
        in_specs=[
            pl.BlockSpec((tm, P_NKV), lambda i: (i, 0)),
            pl.BlockSpec((tm, kvw), lambda i: (i, (P_NKV + kvw) // kvw)),
            pl.BlockSpec((tm, kvw), lambda i: (i, (P_NKV + 2 * kvw) // kvw)),
            pl.BlockSpec((tm, 128), lambda i: (i, P_SMALL // 128)),
            tab_spec, tab_spec, tab_spec,
            pl.BlockSpec((1, 128), lambda i: (0, 0)),
        ],
        out_specs=[
            pl.BlockSpec((tm, P_NKV), lambda i: (i, 0)),
            pl.BlockSpec((tm, kvw), lambda i: (i, 0)),
            pl.BlockSpec((tm, kvw), lambda i: (i, 0)),
            pl.BlockSpec((tm, 128), lambda i: (i, 0)),
            pl.BlockSpec((tm, 128), lambda i: (i, 0)),
        ],
        out_shape=[
            jax.ShapeDtypeStruct((m, P_NKV), F32),
            jax.ShapeDtypeStruct((m, kvw), F32),
            jax.ShapeDtypeStruct((m, kvw), F32),
            jax.ShapeDtypeStruct((m, 128), F32),
            jax.ShapeDtypeStruct((m, 128), F32),
        ],
        compiler_params=_params("arbitrary"),
        name="even_post",
    )(h, h, h, h, *tabs, fb_row)


def _cumsum_kernel(x_ref, tri_ref, o_ref, carry):
    @pl.when(pl.program_id(1) == 0)
    def _():
        carry[...] = jnp.zeros_like(carry)
    c = jnp.dot(tri_ref[...], x_ref[...], preferred_element_type=F32,
                precision=lax.Precision.HIGHEST) + carry[...]
    o_ref[...] = c
    carry[...] = c[-1:, :]


def cumsum_rows(x, nb, t):
    tc = _pick(t, (512, 256, 128))
    tri = jnp.tril(jnp.ones((tc, tc), F32))
    nt = t // tc
    return pl.pallas_call(
        _cumsum_kernel,
        grid=(nb, nt),
        in_specs=[pl.BlockSpec((tc, 128), lambda b, i: (b * nt + i, 0)),
                  pl.BlockSpec((tc, tc), lambda b, i: (0, 0))],
        out_specs=pl.BlockSpec((tc, 128), lambda b, i: (b * nt + i, 0)),
        out_shape=jax.ShapeDtypeStruct(x.shape, F32),
        scratch_shapes=[pltpu.VMEM((1, 128), F32)],
        compiler_params=_params("arbitrary", "arbitrary"),
        name="cumsum_rows",
    )(x, tri)


def _compress_kernel(x_ref, w1_ref, pe_ref, w2_ref, o_ref, u0, u1, *, n16):
    u0[...] = jnp.zeros_like(u0)
    u1[...] = jnp.zeros_like(u1)
    for r in range(0, CMP_STRIDE, 2):
        xa = x_ref[pl.ds(r, n16, stride=CMP_STRIDE), :]
        xb = x_ref[pl.ds(r + 1, n16, stride=CMP_STRIDE), :]
        for u, j in ((u0, r), (u1, CMP_STRIDE + r)):
            lhs = jnp.concatenate([_mx(xa + pe_ref[j:j + 1, :]), _mx(xb + pe_ref[j + 1:j + 2, :])], axis=1)
            w = _mx(w1_ref[j:j + 2]).reshape(2 * HEAD_DIM, CMP_HIDDEN)
            u[...] += jnp.dot(lhs, w, preferred_element_type=F32)
    pre = u0[...] + pltpu.roll(u1[...], n16 - 1, 0)
    out = jnp.dot(_mx(jax.nn.gelu(pre)), _mx(w2_ref[...]), preferred_element_type=F32)
    row = lax.broadcasted_iota(jnp.int32, out.shape, 0)
    o_ref[...] = jnp.where(row < n16 - 1, out, 0.0)


def nsa_compress(x, x_spec, nb, length, e, w1, pe, w2):
    n16 = length // CMP_STRIDE
    w1r = w1.reshape(w1.shape[0], 2, CMP_LEN, HEAD_DIM, CMP_HIDDEN)
    return pl.pallas_call(
        functools.partial(_compress_kernel, n16=n16),
        grid=(nb, 2 * NSA_KV),
        in_specs=[
            x_spec,
            pl.BlockSpec((None, None, CMP_LEN, HEAD_DIM, CMP_HIDDEN), lambda b, c: (e, c // NSA_KV, 0, 0, 0)),
            pl.BlockSpec((None, None, CMP_LEN, HEAD_DIM), lambda b, c: (e, c // NSA_KV, 0, 0)),
            pl.BlockSpec((None, None, CMP_HIDDEN, HEAD_DIM), lambda b, c: (e, c // NSA_KV, 0, 0)),
        ],
        out_specs=pl.BlockSpec((None, None, n16, HEAD_DIM), lambda b, c: (b, c, 0, 0)),
        out_shape=jax.ShapeDtypeStruct((nb, 2 * NSA_KV, n16, HEAD_DIM), F32),
        scratch_shapes=[pltpu.VMEM((n16, CMP_HIDDEN), F32), pltpu.VMEM((n16, CMP_HIDDEN), F32)],
        compiler_params=_params("arbitrary", "arbitrary"),
        name="nsa_compress",
    )(x, w1r, pe, w2)


def _sel_matrix(n_c_pad, n_c, n_sel, width):
    ratio = SEL_LEN // CMP_STRIDE
    i = np.arange(n_c_pad)[:, None]
    j = np.arange(width)[None, :]
    a = (i >= ratio * j - 1) & (i <= ratio * j + ratio - 1) & (i < n_c) & (j < n_sel)
    return jnp.asarray(a.astype(np.float32))


def _masked_softmax(lg, mask):
    lg = jnp.where(mask, lg, -jnp.inf)
    m = jnp.max(lg, axis=-1, keepdims=True)
    m = jnp.where(m == -jnp.inf, 0.0, m)
    p = jnp.exp(lg - m)
    s = jnp.sum(p, axis=-1, keepdims=True)
    return p / jnp.where(s > 0, s, 1.0)


def _cmp_select_kernel(q_ref, kc_ref, vc_ref, a_ref, oc_ref, sel_ref, *, tq, n_sel, n_top):
    i = pl.program_id(2)
    ncp = kc_ref.shape[0]
    qpos = i * tq + lax.broadcasted_iota(jnp.int32, (tq, 1), 0)
    cend = lax.broadcasted_iota(jnp.int32, (1, ncp), 1) * CMP_STRIDE + CMP_LEN
    cmask = cend <= qpos + 1
    kc = _mx(kc_ref[...])
    vc = _mx(vc_ref[...])
    scale = HEAD_DIM ** -0.5
    imp = jnp.zeros((tq, ncp), F32)
    for m in range(NSA_GROUP):
        sl = slice(m * HEAD_DIM, (m + 1) * HEAD_DIM)
        lg = lax.dot_general(_mx(q_ref[:, sl]), kc, NT_DIMS, preferred_element_type=F32) * scale
        p = _masked_softmax(lg, cmask)
        oc_ref[:, sl] = jnp.dot(_mx(p), vc, preferred_element_type=F32)
        imp = imp + p
    s_sel = jnp.dot(imp, a_ref[...], preferred_element_type=F32, precision=lax.Precision.HIGHEST)
    st = s_sel.T[:n_sel, :]
    blk = lax.broadcasted_iota(jnp.int32, (n_sel, tq), 0)
    cur = (i * tq + lax.broadcasted_iota(jnp.int32, (n_sel, tq), 1)) // SEL_LEN
    forced = (blk == 0) | (blk == cur) | (blk == cur - 1)
    allowed = blk <= cur
    v = jnp.where(allowed, jnp.where(forced, jnp.inf, st), -jnp.inf)
    rank = jnp.zeros((n_sel, tq), jnp.int32)
    for r in range(n_sel):
        vr = v[r:r + 1, :]
        before = (vr > v) | ((vr == v) & (blk > r))
        rank = rank + before.astype(jnp.int32)
    sel = ((rank < n_top) & allowed).astype(F32)
    if n_sel < 128:
        sel = jnp.concatenate([sel, jnp.zeros((128 - n_sel, tq), F32)], axis=0)
    sel_ref[...] = sel.T.astype(sel_ref.dtype)


def cmp_select(h, kvc, a_mat, nb, t):
    tq = _pick(t, (256, 128))
    nt = t // tq
    n_sel = t // SEL_LEN
    assert n_sel <= 128 and n_sel % 8 == 0
    ncp = kvc.shape[2]
    gw = NSA_GROUP * HEAD_DIM
    return pl.pallas_call(
        functools.partial(_cmp_select_kernel, tq=tq, n_sel=n_sel, n_top=min(SEL_TOP, n_sel)),
        grid=(nb, NSA_KV, nt),
        in_specs=[
            pl.BlockSpec((tq, gw), lambda b, g, i: (b * nt + i, g)),
            pl.BlockSpec((None, None, ncp, HEAD_DIM), lambda b, g, i: (b, g, 0, 0)),
            pl.BlockSpec((None, None, ncp, HEAD_DIM), lambda b, g, i: (b, NSA_KV + g, 0, 0)),
            pl.BlockSpec((ncp, 128), lambda b, g, i: (0, 0)),
        ],
        out_specs=[
            pl.BlockSpec((tq, gw), lambda b, g, i: (b * nt + i, g)),
            pl.BlockSpec((None, None, tq, 128), lambda b, g, i: (b, g, i, 0)),
        ],
        out_shape=[
            jax.ShapeDtypeStruct((nb * t, NSA_HEADS * HEAD_DIM), F32),
            jax.ShapeDtypeStruct((nb, NSA_KV, t, 128), MXU_DTYPE),
        ],
        compiler_params=_params("arbitrary", "arbitrary", "arbitrary"),
        name="cmp_select",
    )(h, kvc, kvc, a_mat)


def _flash_steps(mode, t, tq, tk):
    rows = []
    for qi in range(t // tq):
        q_lo, q_hi = qi * tq, qi * tq + tq - 1
        k_hi = q_hi // tk
        k_lo = k_hi if mode == "win" else 0
        for kt in range(k_lo, k_hi + 1):
            every_key_visible = kt * tk + tk - 1 <= q_lo
            rows.append((qi, kt, int(kt == k_lo), int(kt == k_hi), int(not every_key_visible)))
    return jnp.asarray(np.array(rows, np.int32).T)


def _flash_kernel(*refs, mode, tq, tk, heads):
    if mode == "fox":
        tab, q_ref, k_ref, v_ref, cq_ref, ck_ref, o_ref, m_s, acc = refs
    elif mode == "slc":
        tab, q_ref, k_ref, v_ref, sel_ref, o_ref, m_s, acc = refs
    else:
        tab, q_ref, k_ref, v_ref, kp_ref, vp_ref, o_ref, m_s, acc = refs
    step = pl.program_id(2)
    qi, kt = tab[0, step], tab[1, step]

    @pl.when(tab[2, step] == 1)
    def _():
        m_s[...] = jnp.full_like(m_s, NEG_BIG)
        acc[...] = jnp.zeros_like(acc)

    def update(causal):
        nkeys = 2 * tk if mode == "win" else tk
        ones = jnp.ones((nkeys, HEAD_DIM), MXU_DTYPE)
        if mode == "slc":
            kb = _mx(k_ref[...])
            v_aug = jnp.concatenate([_mx(v_ref[...]), ones], axis=1)
        elif mode == "win":
            kb = jnp.concatenate([_mx(kp_ref[...]), _mx(k_ref[...])], axis=0)
            v_aug = jnp.concatenate([jnp.concatenate([_mx(vp_ref[...]), _mx(v_ref[...])], axis=0), ones], axis=1)
        valid = None
        if causal or mode == "win":
            qpos = qi * tq + lax.broadcasted_iota(jnp.int32, (tq, nkeys), 0)
            kpos = (kt * tk - (nkeys - tk)) + lax.broadcasted_iota(jnp.int32, (tq, nkeys), 1)
            valid = kpos <= qpos
            if mode == "win":
                valid = valid & (kpos > qpos - WINDOW) & (kpos >= 0)
        if mode == "slc":
            blk = lax.broadcasted_iota(jnp.int32, (128, tk), 0)
            kblk = (kt * tk + lax.broadcasted_iota(jnp.int32, (128, tk), 1)) // SEL_LEN
            chosen = jnp.dot(sel_ref[...], (blk == kblk).astype(MXU_DTYPE), preferred_element_type=F32) > 0.5
            valid = chosen if valid is None else valid & chosen
        for hh in range(heads):
            cols = slice(hh * HEAD_DIM, (hh + 1) * HEAD_DIM)
            if mode == "fox":
                kb = _mx(k_ref[:, cols])
                v_aug = jnp.concatenate([_mx(v_ref[:, cols]), ones], axis=1)
                ck2 = ck_ref[hh] * LOG2E
                cq2 = cq_ref[hh] * LOG2E
            s = lax.dot_general(_mx(q_ref[:, cols]), kb, NT_DIMS, preferred_element_type=F32)
            x = s * (HEAD_DIM ** -0.5 * LOG2E)
            if mode == "fox":
                x = x - ck2
            if valid is not None:
                x = jnp.where(valid, x, NEG_BIG)
            top = jnp.max(x, axis=-1, keepdims=True)
            if mode == "fox":
                top = top + cq2
            m_old = m_s[hh]
            m_new = jnp.maximum(m_old, top)
            p = jnp.exp2(x - (m_new - cq2 if mode == "fox" else m_new))
            acc[hh] = jnp.exp2(m_old - m_new) * acc[hh] + jnp.dot(_mx(p), v_aug, preferred_element_type=F32)
            m_s[hh] = m_new

    if mode == "win":
        update(True)
    else:
        pl.when(tab[4, step] == 1)(functools.partial(update, True))
        pl.when(tab[4, step] == 0)(functools.partial(update, False))

    @pl.when(tab[3, step] == 1)
    def _():
        for hh in range(heads):
            l = acc[hh, :, HEAD_DIM:]
            o_ref[:, hh * HEAD_DIM:(hh + 1) * HEAD_DIM] = acc[hh, :, :HEAD_DIM] / jnp.where(l > 0, l, 1.0)


def flash_attention(mode, nb, t, q, qcol, k, kcol, v, vcol, extra=()):
    heads = NSA_GROUP
    tk = _pick(t, (512,) if mode == "win" else (1024, 512, 256, 128))
    tq = _pick(t, (512, 256, 128))
    ntq, ntk = t // tq, t // tk
    qw = heads * HEAD_DIM
    kw = qw if mode == "fox" else HEAD_DIM
    steps = _flash_steps(mode, t, tq, tk)
    in_specs = [
        pl.BlockSpec((tq, qw), lambda b, h, s, tab: (b * ntq + tab[0, s], qcol(h))),
        pl.BlockSpec((tk, kw), lambda b, h, s, tab: (b * ntk + tab[1, s], kcol(h))),
        pl.BlockSpec((tk, kw), lambda b, h, s, tab: (b * ntk + tab[1, s], vcol(h))),
    ]
    scratch = [pltpu.VMEM((heads, tq, 1), F32), pltpu.VMEM((heads, tq, 2 * HEAD_DIM), F32)]
    if mode == "fox":
        in_specs += [
            pl.BlockSpec((None, heads, tq, 1), lambda b, h, s, tab: (b, h, tab[0, s], 0)),
            pl.BlockSpec((None, heads, 1, tk), lambda b, h, s, tab: (b, h, 0, tab[1, s])),
        ]
    elif mode == "slc":
        in_specs += [pl.BlockSpec((None, None, tq, 128), lambda b, h, s, tab: (b, h, tab[0, s], 0))]
    else:
        assert tq == tk == WINDOW
        in_specs += [
            pl.BlockSpec((tk, kw), lambda b, h, s, tab: (b * ntk + jnp.maximum(tab[1, s] - 1, 0), kcol(h))),
            pl.BlockSpec((tk, kw), lambda b, h, s, tab: (b * ntk + jnp.maximum(tab[1, s] - 1, 0), vcol(h))),
        ]
        extra = (k, v)
    return pl.pallas_call(
        functools.partial(_flash_kernel, mode=mode, tq=tq, tk=tk, heads=heads),
        grid_spec=pltpu.PrefetchScalarGridSpec(
            num_scalar_prefetch=1,
            grid=(nb, NSA_HEADS // heads, steps.shape[1]),
            in_specs=in_specs,
            out_specs=pl.BlockSpec((tq, qw), lambda b, h, s, tab: (b * ntq + tab[0, s], h)),
            scratch_shapes=scratch,
        ),
        out_shape=jax.ShapeDtypeStruct((nb * t, NSA_HEADS * HEAD_DIM), F32),
        compiler_params=_params("arbitrary", "arbitrary", "arbitrary"),
        name="flash_" + mode,
    )(steps, q, k, v, *extra)


def _combine_kernel(oc_ref, os_ref, ow_ref, of_ref, g_ref, o_ref):
    gates = g_ref[...]
    nw = NSA_HEADS * HEAD_DIM
    for h in range(NSA_HEADS):
        sl = slice(h * HEAD_DIM, (h + 1) * HEAD_DIM)
        o = (gates[:, 3 * h:3 * h + 1] * oc_ref[:, sl] + gates[:, 3 * h + 1:3 * h + 2] * os_ref[:, sl]
             + gates[:, 3 * h + 2:3 * h + 3] * ow_ref[:, sl])
        o_ref[:, sl] = o.astype(o_ref.dtype)
    o_ref[:, nw:] = of_ref[...].astype(o_ref.dtype)


def combine_heads(o_c, o_s, o_w, o_f, gates):
    m, nw = o_c.shape
    tm = _pick(m, (512, 256, 128))
    spec = pl.BlockSpec((tm, nw), lambda i: (i, 0))
    return pl.pallas_call(
        _combine_kernel,
        grid=(m // tm,),
        in_specs=[spec, spec, spec, spec, pl.BlockSpec((tm, 128), lambda i: (i, 0))],
        out_specs=pl.BlockSpec((tm, 2 * nw), lambda i: (i, 0)),
        out_shape=jax.ShapeDtypeStruct((m, 2 * nw), MXU_DTYPE),
        compiler_params=_params("arbitrary"),
        name="combine_heads",
    )(o_c, o_s, o_w, o_f, gates)


def _gather_pages_kernel(pt_ref, *refs, page):
    *x_refs, o_ref = refs
    for j, x_ref in enumerate(x_refs):
        for c in range(2 * NSA_KV):
            o_ref[c, j * page:(j + 1) * page, :] = x_ref[:, c // NSA_KV, c % NSA_KV, :]


def gather_cmp_pages(pool, e, page_table):
    nb, n_pages = page_table.shape
    page = pool.shape[2]
    per_step = _pick(n_pages, (8, 4, 2, 1))

    def page_spec(j):
        return pl.BlockSpec((None, None, page, 2, NSA_KV, HEAD_DIM),
                            lambda b, p, pt: (e, pt[b, p * per_step + j], 0, 0, 0, 0))

    return pl.pallas_call(
        functools.partial(_gather_pages_kernel, page=page),
        grid_spec=pltpu.PrefetchScalarGridSpec(
            num_scalar_prefetch=1,
            grid=(nb, n_pages // per_step),
            in_specs=[page_spec(j) for j in range(per_step)],
            out_specs=pl.BlockSpec((None, 2 * NSA_KV, per_step * page, HEAD_DIM), lambda b, p, pt: (b, 0, p, 0)),
        ),
        out_shape=jax.ShapeDtypeStruct((nb, 2 * NSA_KV, n_pages * page, HEAD_DIM), F32),
        compiler_params=_params("arbitrary", "arbitrary"),
        name="gather_cmp_pages",
    )(page_table, *([pool] * per_step))


def _sample_cmp_win_kernel(qraw_ref, qrot_ref, kc_ref, vc_ref, a_ref, kw_ref, vw_ref,
                           oc_ref, ow_ref, idx_ref, *, n_c, n_sel, n_top, q_pos):
    scale = HEAD_DIM ** -0.5
    ncp = kc_ref.shape[0]
    rows = qraw_ref.shape[0]
    cend = lax.broadcasted_iota(jnp.int32, (1, ncp), 1) * CMP_STRIDE + CMP_LEN
    cidx = lax.broadcasted_iota(jnp.int32, (1, ncp), 1)
    cmask = (cend <= q_pos + 1) & (cidx < n_c)
    lg = lax.dot_general(_mx(qraw_ref[...]), _mx(kc_ref[...]), NT_DIMS, preferred_element_type=F32) * scale
    p = _masked_softmax(lg, cmask)
    oc_ref[...] = jnp.dot(_mx(p), _mx(vc_ref[...]), preferred_element_type=F32)
    head = lax.broadcasted_iota(jnp.int32, p.shape, 0)
    imp = jnp.sum(jnp.where(head < NSA_GROUP, p, 0.0), axis=0, keepdims=True)
    imp = jnp.broadcast_to(imp, (rows, ncp))
    s_sel = jnp.dot(imp, a_ref[...], preferred_element_type=F32, precision=lax.Precision.HIGHEST)
    nsp = s_sel.shape[1]
    lane = lax.broadcasted_iota(jnp.int32, (rows, nsp), 1).astype(F32)
    cur = q_pos // SEL_LEN
    forced = (lane == 0.0) | (lane == float(cur)) | (lane == float(cur - 1))
    v = jnp.where(lane <= float(cur), jnp.where(forced, jnp.inf, s_sel), -jnp.inf)
    out_lane = lax.broadcasted_iota(jnp.int32, (rows, 128), 1)
    picked = jnp.full((rows, 128), -1.0, F32)
    for r in range(n_top):
        mx = jnp.max(v, axis=1, keepdims=True)
        ix = jnp.min(jnp.where(v == mx, lane, float(nsp)), axis=1, keepdims=True)
        ix = jnp.where(mx > -jnp.inf, ix, -1.0)
        picked = jnp.where(out_lane == r, ix, picked)
        v = jnp.where(lane == ix, -jnp.inf, v)
    idx_ref[...] = picked.astype(jnp.int32)
    lw = lax.dot_general(_mx(qrot_ref[...]), _mx(kw_ref[...]), NT_DIMS, preferred_element_type=F32) * scale
    pw = _masked_softmax(lw, jnp.full(lw.shape, True))
    ow_ref[...] = jnp.dot(_mx(pw), _mx(vw_ref[...]), preferred_element_type=F32)


def sample_cmp_win(qraw, qrot, kvc, a_mat, win, n_c, n_sel, q_pos):
    nb, _, rows, _ = qraw.shape
    ncp = kvc.shape[2]
    nsp = a_mat.shape[1]
    wlen = win.shape[1]
    qspec = pl.BlockSpec((None, None, rows, HEAD_DIM), lambda b, g: (b, g, 0, 0))
    return pl.pallas_call(
        functools.partial(_sample_cmp_win_kernel, n_c=n_c, n_sel=n_sel, n_top=min(SEL_TOP, n_sel), q_pos=q_pos),
        grid=(nb, NSA_KV),
        in_specs=[
            qspec, qspec,
            pl.BlockSpec((None, None, ncp, HEAD_DIM), lambda b, g: (b, g, 0, 0)),
            pl.BlockSpec((None, None, ncp, HEAD_DIM), lambda b, g: (b, NSA_KV + g, 0, 0)),
            pl.BlockSpec((ncp, nsp), lambda b, g: (0, 0)),
            pl.BlockSpec((None, wlen, HEAD_DIM), lambda b, g: (b, 0, g)),
            pl.BlockSpec((None, wlen, HEAD_DIM), lambda b, g: (b, 0, NSA_KV + g)),
        ],
        out_specs=[qspec, qspec, pl.BlockSpec((None, None, rows, 128), lambda b, g: (b, g, 0, 0))],
        out_shape=[
            jax.ShapeDtypeStruct(qraw.shape, F32),
            jax.ShapeDtypeStruct(qraw.shape, F32),
            jax.ShapeDtypeStruct((nb, NSA_KV, rows, 128), jnp.int32),
        ],
        compiler_params=_params("arbitrary", "arbitrary"),
        name="sample_cmp_win",
    )(qraw, qrot, kvc, kvc, a_mat, win, win)


def _sample_slc_kernel(pt_ref, idx_ref, q_ref, *refs, n_top, n_past_blk):
    kv_refs = refs[:n_top]
    kn_ref, vn_ref, o_ref = refs[n_top:]
    b, g = pl.program_id(0), pl.program_id(1)
    row = b * NSA_KV + g
    scale = HEAD_DIM ** -0.5

    def attend(gg):
        q = q_ref[...]
        qb = _mx(q)
        picked = [idx_ref[row, j] for j in range(n_top)]
        in_past = [(blk >= 0) & (blk < n_past_blk) for blk in picked]
        has_new = picked[0] == n_past_blk
        for blk in picked[1:]:
            has_new = has_new | (blk == n_past_blk)
        s_new = jnp.where(has_new, jnp.sum(q * kn_ref[...], axis=-1, keepdims=True) * scale, NEG_BIG)
        logits = []
        for j in range(n_top):
            s = lax.dot_general(qb, _mx(kv_refs[j][:, 0, gg, :]), NT_DIMS, preferred_element_type=F32) * scale
            logits.append(jnp.where(in_past[j], s, NEG_BIG))
        m = s_new
        for s in logits:
            m = jnp.maximum(m, jnp.max(s, axis=-1, keepdims=True))
        p_new = jnp.where(has_new, jnp.exp(s_new - m), 0.0)
        l = p_new
        o = p_new * vn_ref[...]
        for j in range(n_top):
            p = jnp.where(in_past[j], jnp.exp(logits[j] - m), 0.0)
            l = l + jnp.sum(p, axis=-1, keepdims=True)
            o = o + jnp.dot(_mx(p), _mx(kv_refs[j][:, 1, gg, :]), preferred_element_type=F32)
        o_ref[...] = o / jnp.where(l > 0, l, 1.0)

    for gg in range(NSA_KV):
        pl.when(g == gg)(functools.partial(attend, gg))


def sample_slc(qrot, pool, e, page_table, idx, slc_new, n_past_blk):
    nb, _, rows, _ = qrot.shape
    n_top = idx.shape[1]
    page = pool.shape[2]
    per_page = page // SEL_LEN

    def kv_spec(j):
        def index(b, g, pt, ix):
            blk = jnp.clip(ix[b * NSA_KV + g, j], 0, n_past_blk - 1)
            return (e, pt[b, blk // per_page], blk % per_page, 0, 0, 0)
        return pl.BlockSpec((None, None, SEL_LEN, 2, NSA_KV, HEAD_DIM), index)

    return pl.pallas_call(
        functools.partial(_sample_slc_kernel, n_top=n_top, n_past_blk=n_past_blk),
        grid_spec=pltpu.PrefetchScalarGridSpec(
            num_scalar_prefetch=2,
            grid=(nb, NSA_KV),
            in_specs=[
                pl.BlockSpec((None, None, rows, HEAD_DIM), lambda b, g, pt, ix: (b, g, 0, 0)),
                *[kv_spec(j) for j in range(n_top)],
                pl.BlockSpec((None, 1, HEAD_DIM), lambda b, g, pt, ix: (b, 0, g)),
                pl.BlockSpec((None, 1, HEAD_DIM), lambda b, g, pt, ix: (b, 0, NSA_KV + g)),
            ],
            out_specs=pl.BlockSpec((None, None, rows, HEAD_DIM), lambda b, g, pt, ix: (b, g, 0, 0)),
        ),
        out_shape=jax.ShapeDtypeStruct(qrot.shape, F32),
        compiler_params=_params("arbitrary", "arbitrary"),
        name="sample_slc",
    )(page_table, idx, qrot, *([pool] * n_top), slc_new, slc_new)


def _sample_fox_kernel(pt_ref, q_ref, *refs, n_steps, per_step):
    page_refs = refs[:3 * per_step]
    tri_ref, kn_ref, vn_ref, lfn_ref, o_ref, m_s, l_s, acc, crun = refs[3 * per_step:]
    step = pl.program_id(1)
    scale = HEAD_DIM ** -0.5
    nh = FOX_HEADS
    page = page_refs[0].shape[0]
    cols = page * nh

    @pl.when(step == 0)
    def _():
        m_s[...] = jnp.full_like(m_s, NEG_BIG)
        l_s[...] = jnp.zeros_like(l_s)
        acc[...] = jnp.zeros_like(acc)
        crun[...] = jnp.zeros_like(crun)

    qb = _mx(q_ref[...])
    tri = _mx(tri_ref[...])
    lane = lax.broadcasted_iota(jnp.int32, (nh, HEAD_DIM), 1) // nh
    own = (lax.broadcasted_iota(jnp.int32, (nh, cols), 1) % nh) == lax.broadcasted_iota(jnp.int32, (nh, cols), 0)
    c_off = crun[...]
    logits, values = [], []
    for j in range(per_step):
        k_ref, v_ref, lf_ref = page_refs[3 * j:3 * j + 3]
        lf = lf_ref[...]
        hi = _mx(lf).astype(F32)
        mid = _mx(lf - hi).astype(F32)
        lo = lf - hi - mid
        cs = jnp.dot(_mx(jnp.concatenate([hi, mid, lo], axis=0)), tri, preferred_element_type=F32)
        ck = cs[:nh] + cs[nh:2 * nh] + cs[2 * nh:] + c_off
        c_off = ck[:, -1:]
        ck_x = jnp.concatenate(
            [jnp.take_along_axis(ck, lane + i * (HEAD_DIM // nh), axis=1) for i in range(cols // HEAD_DIM)], axis=1)
        k2 = k_ref[...].reshape(cols, HEAD_DIM)
        s = lax.dot_general(qb, _mx(k2), NT_DIMS, preferred_element_type=F32) * scale - ck_x
        logits.append(jnp.where(own, s, NEG_BIG))
        values.append(_mx(v_ref[...].reshape(cols, HEAD_DIM)))
    crun[...] = c_off
    m_old = m_s[...]
    m_new = m_old
    for s in logits:
        m_new = jnp.maximum(m_new, jnp.max(s, axis=-1, keepdims=True))
    alpha = jnp.exp(m_old - m_new)
    l_new = alpha * l_s[...]
    o_new = alpha * acc[...]
    for s, v2 in zip(logits, values):
        p = jnp.where(own, jnp.exp(s - m_new), 0.0)
        l_new = l_new + jnp.sum(p, axis=-1, keepdims=True)
        o_new = o_new + jnp.dot(_mx(p), v2, preferred_element_type=F32)
    l_s[...] = l_new
    acc[...] = o_new
    m_s[...] = m_new

    @pl.when(step == n_steps - 1)
    def _():
        cq = c_off + lfn_ref[...]
        s_new = jnp.sum(q_ref[...] * kn_ref[...], axis=-1, keepdims=True) * scale - cq
        m_fin = jnp.maximum(m_new, s_new)
        p_new = jnp.exp(s_new - m_fin)
        a_fin = jnp.exp(m_new - m_fin)
        o_ref[...] = (a_fin * o_new + p_new * vn_ref[...]) / (a_fin * l_new + p_new)


def sample_fox(fq, pool, logf_t, e, page_table, k_new, v_new, logf_new):
    nb, n_pages = page_table.shape
    page = pool.shape[2]
    per_step = _pick(n_pages, (8, 4, 2, 1))
    tri = jnp.triu(jnp.ones((page, page), F32))

    def page_specs(j):
        def kv(which):
            return pl.BlockSpec((None, None, page, None, FOX_HEADS, HEAD_DIM),
                                lambda b, p, pt: (e, pt[b, p * per_step + j], 0, which, 0, 0))
        return [kv(0), kv(1),
                pl.BlockSpec((None, None, FOX_HEADS, page), lambda b, p, pt: (e, pt[b, p * per_step + j], 0, 0))]

    page_ops = []
    for j in range(per_step):
        page_ops += [pool, pool, logf_t]
    return pl.pallas_call(
        functools.partial(_sample_fox_kernel, n_steps=n_pages // per_step, per_step=per_step),
        grid_spec=pltpu.PrefetchScalarGridSpec(
            num_scalar_prefetch=1,
            grid=(nb, n_pages // per_step),
            in_specs=[
                pl.BlockSpec((None, FOX_HEADS, HEAD_DIM), lambda b, p, pt: (b, 0, 0)),
                *[spec for j in range(per_step) for spec in page_specs(j)],
                pl.BlockSpec((page, page), lambda b, p, pt: (0, 0)),
                pl.BlockSpec((None, FOX_HEADS, HEAD_DIM), lambda b, p, pt: (b, 0, 0)),
                pl.BlockSpec((None, FOX_HEADS, HEAD_DIM), lambda b, p, pt: (b, 0, 0)),
                pl.BlockSpec((None, FOX_HEADS, 1), lambda b, p, pt: (b, 0, 0)),
            ],
            out_specs=pl.BlockSpec((None, FOX_HEADS, HEAD_DIM), lambda b, p, pt: (b, 0, 0)),
            scratch_shapes=[
                pltpu.VMEM((FOX_HEADS, 1), F32), pltpu.VMEM((FOX_HEADS, 1), F32),
                pltpu.VMEM((FOX_HEADS, HEAD_DIM), F32), pltpu.VMEM((FOX_HEADS, 1), F32),
            ],
        ),
        out_shape=jax.ShapeDtypeStruct((nb, FOX_HEADS, HEAD_DIM), F32),
        compiler_params=_params("arbitrary", "arbitrary"),
        name="sample_fox",
    )(page_table, fq, *page_ops, tri, k_new, v_new, logf_new)


def _ret_tables(chunk):
    lg = jnp.log1p(-(2.0 ** (-5.0 - jnp.arange(RET_HEADS, dtype=F32))))
    i = jnp.arange(chunk, dtype=F32)
    diff = i[:, None] - i[None, :]
    dec = jnp.where(diff >= 0, jnp.exp(jnp.maximum(diff, 0.0)[None] * lg[:, None, None]), 0.0)
    xi = jnp.exp((i[None, :] + 1.0) * lg[:, None])
    zeta = jnp.exp((chunk - 1.0 - i)[None, :] * lg[:, None])
    g_c = jnp.exp(chunk * lg)
    return dec, xi, zeta, g_c


def _ret_rope_tables(pos):
    half = RET_DK // 2
    inv = 1.0 / (RET_THETA ** (jnp.arange(half, dtype=F32) / half))
    ang = pos.astype(F32)[:, None] * inv[None, :]
    return jnp.cos(ang), jnp.sin(ang)


def _group_norm_gate(o, gate, gn):
    mu = jnp.mean(o, axis=-1, keepdims=True)
    d = o - mu
    var = jnp.mean(d * d, axis=-1, keepdims=True)
    y = d * lax.rsqrt(var + GN_EPS) * gn
    return gate * jax.nn.sigmoid(gate) * y


def _retention_kernel(q_ref, k_ref, v_ref, gate_ref, cos_ref, sin_ref, dec_ref, coef_ref, gn_ref,
                      y_ref, s_ref, state, *, n_chunks, heads):
    c = pl.program_id(2)
    half = RET_DK // 2

    @pl.when(c == 0)
    def _():
        state[...] = jnp.zeros_like(state)

    cos, sin = cos_ref[...], sin_ref[...]

    def rot(x_ref, hh):
        x1 = x_ref[:, hh * RET_DK:hh * RET_DK + half]
        x2 = x_ref[:, hh * RET_DK + half:(hh + 1) * RET_DK]
        return jnp.concatenate([x1 * cos - x2 * sin, x2 * cos + x1 * sin], axis=-1)

    for hh in range(heads):
        vcols = slice(hh * RET_DV, (hh + 1) * RET_DV)
        coef = coef_ref[hh]
        xi, zeta, g_c = coef[:, 0:1], coef[:, 1:2], coef[0:1, 2:3]
        q = _mx(rot(q_ref, hh))
        kf = rot(k_ref, hh) * (RET_DK ** -0.5)
        v = _mx(v_ref[:, vcols])
        s_old = state[hh]
        a = lax.dot_general(q, _mx(kf), NT_DIMS, preferred_element_type=F32) * dec_ref[hh]
        o = (jnp.dot(_mx(a), v, preferred_element_type=F32)
             + jnp.dot(q, _mx(s_old), preferred_element_type=F32) * xi)
        state[hh] = s_old * g_c + lax.dot_general(_mx(kf * zeta), v, TN_DIMS, preferred_element_type=F32)
        y_ref[:, vcols] = _group_norm_gate(o, gate_ref[:, vcols], gn_ref[:, vcols]).astype(y_ref.dtype)

    @pl.when(c == n_chunks - 1)
    def _():
        s_ref[...] = state[...]


def retention_prompt(h, nb, t, gn_g, o_idx):
    ch = RET_CHUNK
    n_chunks = t // ch
    dec, xi, zeta, g_c = _ret_tables(ch)
    coef = jnp.stack([xi, zeta, jnp.broadcast_to(g_c[:, None], xi.shape)], axis=-1)
    coef = jnp.pad(coef, ((0, 0), (0, 0), (0, 128 - 3)))
    cos, sin = _ret_rope_tables(jnp.arange(t))
    hs = RET_HEADS_PER_STEP
    groups = RET_HEADS // hs
    kw, vw = hs * RET_DK, hs * RET_DV
    kb, vb, gb = O_RK // kw, O_RV // vw, O_RG // vw
    return pl.pallas_call(
        functools.partial(_retention_kernel, n_chunks=n_chunks, heads=hs),
        grid=(nb, groups, n_chunks),
        in_specs=[
            pl.BlockSpec((ch, kw), lambda b, hg, c: (b * n_chunks + c, hg)),
            pl.BlockSpec((ch, kw), lambda b, hg, c: (b * n_chunks + c, kb + hg)),
            pl.BlockSpec((ch, vw), lambda b, hg, c: (b * n_chunks + c, vb + hg)),
            pl.BlockSpec((ch, vw), lambda b, hg, c: (b * n_chunks + c, gb + hg)),
            pl.BlockSpec((ch, RET_DK // 2), lambda b, hg, c: (c, 0)),
            pl.BlockSpec((ch, RET_DK // 2), lambda b, hg, c: (c, 0)),
            pl.BlockSpec((hs, ch, ch), lambda b, hg, c: (hg, 0, 0)),
            pl.BlockSpec((hs, ch, 128), lambda b, hg, c: (hg, 0, 0)),
            pl.BlockSpec((None, 1, vw), lambda b, hg, c: (o_idx, 0, hg)),
        ],
        out_specs=[
            pl.BlockSpec((ch, vw), lambda b, hg, c: (b * n_chunks + c, hg)),
            pl.BlockSpec((None, hs, RET_DK, RET_DV), lambda b, hg, c: (b, hg, 0, 0)),
        ],
        out_shape=[
            jax.ShapeDtypeStruct((nb * t, RET_HEADS * RET_DV), MXU_DTYPE),
            jax.ShapeDtypeStruct((nb, RET_HEADS, RET_DK, RET_DV), F32),
        ],
        scratch_shapes=[pltpu.VMEM((hs, RET_DK, RET_DV), F32)],
        compiler_params=_params("arbitrary", "arbitrary", "arbitrary"),
        name="retention_prompt",
    )(h, h, h, h, cos, sin, dec, coef, gn_g[:, None, :])


def _retention_step_kernel(q_ref, k_ref, v_ref, gate_ref, cos_ref, sin_ref, coef_ref, gn_ref, s0_ref,
                           y_ref, s_ref):
    half = RET_DK // 2
    cos, sin = cos_ref[...], sin_ref[...]

    def rot(x):
        x1, x2 = x[:half, :], x[half:, :]
        return jnp.concatenate([x1 * cos - x2 * sin, x2 * cos + x1 * sin], axis=0)

    for hh in range(RET_HEADS):
        coef = coef_ref[hh:hh + 1, :]
        dec, xi, zeta, g_c = coef[:, 0:1], coef[:, 1:2], coef[:, 2:3], coef[:, 3:4]
        q = rot(q_ref[hh])
        k = rot(k_ref[hh]) * (RET_DK ** -0.5)
        v = v_ref[hh]
        s_old = s0_ref[hh]
        a = jnp.sum(q * k, axis=0, keepdims=True) * dec
        o = a * v + jnp.sum(q * s_old, axis=0, keepdims=True) * xi
        s_ref[hh] = s_old * g_c + (k * zeta) * v
        y_ref[hh] = _group_norm_gate(o, gate_ref[hh], gn_ref[:, hh * RET_DV:(hh + 1) * RET_DV])


def retention_step(q_col, k_col, v_row, gate_row, pos, s0, o_idx, gn_g):
    nb = q_col.shape[0]
    dec, xi, zeta, g_c = _ret_tables(1)
    coef = jnp.stack([dec[:, 0, 0], xi[:, 0], zeta[:, 0], g_c], axis=-1)
    coef = jnp.pad(coef, ((0, 0), (0, 128 - 4)))
    cos, sin = _ret_rope_tables(pos)
    col = pl.BlockSpec((None, RET_HEADS, RET_DK, 1), lambda b: (b, 0, 0, 0))
    row = pl.BlockSpec((None, RET_HEADS, 1, RET_DV), lambda b: (b, 0, 0, 0))
    tab = pl.BlockSpec((RET_DK // 2, 1), lambda b: (0, 0))
    return pl.pallas_call(
        _retention_step_kernel,
        grid=(nb,),
        in_specs=[
            col, col, row, row, tab, tab,
            pl.BlockSpec((RET_HEADS, 128), lambda b: (0, 0)),
            pl.BlockSpec((None, 1, RET_HEADS * RET_DV), lambda b: (o_idx, 0, 0)),
            pl.BlockSpec((None, None, RET_HEADS, RET_DK, RET_DV), lambda b: (o_idx, b, 0, 0, 0)),
        ],
        out_specs=[row, pl.BlockSpec((None, RET_HEADS, RET_DK, RET_DV), lambda b: (b, 0, 0, 0))],
        out_shape=[
            jax.ShapeDtypeStruct((nb, RET_HEADS, 1, RET_DV), F32),
            jax.ShapeDtypeStruct((nb, RET_HEADS, RET_DK, RET_DV), F32),
        ],
        compiler_params=_params("arbitrary"),
        name="retention_step",
    )(q_col, k_col, v_row, gate_row, cos.reshape(-1, 1), sin.reshape(-1, 1), coef, gn_g[:, None, :], s0)


def _pad_rows(x, rows):
    return jnp.pad(x, ((0, rows - x.shape[0]),) + ((0, 0),) * (x.ndim - 1))


def _repack_even(w):
    small = jnp.concatenate([w[:, O_NG:O_FQ], w[:, O_FF:E_EVEN]], axis=1)
    small = jnp.pad(small, ((0, 0), (0, 128 - small.shape[1])))
    return jnp.concatenate([w[:, O_NQ:O_NG], w[:, O_FQ:O_FF], small], axis=1)


def _even_layer(e, xb_p, xb_s, nb, t, ns, past, caches, page_table, wts):
    cache_cmp, cache_slc, cache_win, cache_fox, logf_pool_t = caches
    w_in, fox_f_bias, cmp_pos, cmp_w1, cmp_w2 = wts
    w_rep = _repack_even(w_in[e])
    fb_row = jnp.zeros((1, 128), F32).at[0, N_GATE:N_GATE + FOX_HEADS].set(fox_f_bias[e])
    kvw = 2 * NSA_KV * HEAD_DIM
    nw = NSA_HEADS * HEAD_DIM

    h_p, h_s = matmul(xb_p, xb_s, w_rep, (), (1152, 640, 384, 128))
    tm_post = _pick(nb * t, (512, 256, 128))
    q_rot, slc_p, win_p, gates_p, logf_p = even_post(
        h_p, _rope_tables(jnp.arange(t)), fb_row, t // tm_post)
    c_p = cumsum_rows(logf_p, nb, t)
    c_heads = jnp.swapaxes(c_p[:, N_GATE:N_GATE + FOX_HEADS].reshape(nb, t, FOX_HEADS), 1, 2)
    n16 = t // CMP_STRIDE
    kvc_p = nsa_compress(
        h_p, pl.BlockSpec((t, HEAD_DIM), lambda b, c: (b, P_NKV // HEAD_DIM + c)), nb, t, e, cmp_w1, cmp_pos, cmp_w2)
    a_p = _sel_matrix(n16, n16 - 1, t // SEL_LEN, 128)
    o_c, sel = cmp_select(h_p, kvc_p, a_p, nb, t)
    o_s = flash_attention("slc", nb, t, q_rot, lambda g: g, slc_p, lambda g: g, slc_p, lambda g: NSA_KV + g, (sel,))
    o_w = flash_attention("win", nb, t, q_rot, lambda g: g, win_p, lambda g: g, win_p, lambda g: NSA_KV + g)
    fq0 = P_FQ // (NSA_GROUP * HEAD_DIM)
    fgroups = FOX_HEADS // NSA_GROUP
    o_f = flash_attention("fox", nb, t, h_p, lambda g: fq0 + g, h_p, lambda g: fq0 + fgroups + g,
                          h_p, lambda g: fq0 + 2 * fgroups + g, (c_heads[..., None], c_heads[:, :, None, :]))
    xcat_p = combine_heads(o_c, o_s, o_w, o_f, gates_p)
    st_p = (
        h_p[:, P_NKV:P_NKV + kvw].reshape(nb, t, 2, NSA_KV, HEAD_DIM),
        slc_p.reshape(nb, t, 2, NSA_KV, HEAD_DIM),
        win_p.reshape(nb, t, 2, NSA_KV, HEAD_DIM)[:, t - min(WINDOW, t):],
        h_p[:, P_FQ + nw:P_FQ + 3 * nw].reshape(nb, t, 2, FOX_HEADS, HEAD_DIM),
        logf_p[:, N_GATE:N_GATE + FOX_HEADS].reshape(nb, t, FOX_HEADS),
    )

    rows = xb_s.shape[0]
    q_rot_s, slc_s, win_s, gates_s, logf_s = even_post(
        h_s, _rope_tables(jnp.full((rows,), past)), fb_row, 1)
    new_win = jnp.concatenate(
        [cache_win[e].reshape(ns, -1, kvw), win_s[:ns, None, :]], axis=1)[:, 1:]
    cmp_rows = gather_cmp_pages(cache_cmp, e, page_table)
    l_tot = past + 1
    n16_s = l_tot // CMP_STRIDE
    assert n16_s * CMP_STRIDE == past
    kvc_s = nsa_compress(
        cmp_rows, pl.BlockSpec((None, None, past, HEAD_DIM), lambda b, c: (b, c, 0, 0)), ns, past, e,
        cmp_w1, cmp_pos, cmp_w2)
    n_sel_s = -(-l_tot // SEL_LEN)
    nsp = -(-n_sel_s // 128) * 128
    a_s = _sel_matrix(n16_s, n16_s - 1, n_sel_s, nsp)

    def q_rows(x):
        x = x[:ns].reshape(ns, NSA_KV, NSA_GROUP, HEAD_DIM)
        return jnp.pad(x, ((0, 0), (0, 0), (0, 8 - NSA_GROUP), (0, 0)))

    qraw_g, qrot_g = q_rows(h_s[:, :nw]), q_rows(q_rot_s)
    o_c_s, o_w_s, idx = sample_cmp_win(qraw_g, qrot_g, kvc_s, a_s, new_win, n16_s - 1, n_sel_s, past)
    idx2 = idx[:, :, 0, :min(SEL_TOP, n_sel_s)].reshape(ns * NSA_KV, -1)
    o_s_s = sample_slc(qrot_g, cache_slc, e, page_table, idx2, slc_s[:ns, None, :], past // SEL_LEN)
    fq = h_s[:ns, P_FQ:P_FQ + nw].reshape(ns, FOX_HEADS, HEAD_DIM)
    fk = h_s[:ns, P_FQ + nw:P_FQ + 2 * nw]
    fv = h_s[:ns, P_FQ + 2 * nw:P_FQ + 3 * nw]
    logf_new = logf_s[:ns, N_GATE:N_GATE + FOX_HEADS]
    o_f_s = sample_fox(fq, cache_fox, logf_pool_t, e, page_table, fk.reshape(ns, FOX_HEADS, HEAD_DIM),
                       fv.reshape(ns, FOX_HEADS, HEAD_DIM), logf_new[:, :, None])

    def heads_flat(x):
        return _pad_rows(x[:, :, :NSA_GROUP].reshape(ns, nw), rows)

    xcat_s = combine_heads(heads_flat(o_c_s), heads_flat(o_s_s), heads_flat(o_w_s),
                           _pad_rows(o_f_s.reshape(ns, nw), rows), gates_s)
    st_s = (
        h_s[:ns, P_NKV:P_NKV + kvw].reshape(ns, 1, 2, NSA_KV, HEAD_DIM),
        slc_s[:ns].reshape(ns, 1, 2, NSA_KV, HEAD_DIM),
        new_win.reshape(ns, -1, 2, NSA_KV, HEAD_DIM),
        h_s[:ns, P_FQ + nw:P_FQ + 3 * nw].reshape(ns, 1, 2, FOX_HEADS, HEAD_DIM),
        logf_new.reshape(ns, 1, FOX_HEADS),
    )
    return xcat_p, xcat_s, st_p, st_s


def _odd_layer(o, xb_p, xb_s, nb, t, ns, past, state_ret, w_in_odd, ret_gn_g):
    h_p, h_s = matmul(xb_p, xb_s, w_in_odd, (o,), (1024, 512, 256, 128))
    y_p, s_p = retention_prompt(h_p, nb, t, ret_gn_g, o)
    rows = xb_s.shape[0]
    h_s = h_s[:ns]
    q_col = h_s[:, :O_RK].reshape(ns, RET_HEADS, RET_DK, 1)
    k_col = h_s[:, O_RK:O_RV].reshape(ns, RET_HEADS, RET_DK, 1)
    v_row = h_s[:, O_RV:O_RG].reshape(ns, RET_HEADS, 1, RET_DV)
    g_row = h_s[:, O_RG:E_ODD].reshape(ns, RET_HEADS, 1, RET_DV)
    y_s, s_s = retention_step(q_col, k_col, v_row, g_row, jnp.full((1,), past), state_ret, o, ret_gn_g)
    y_s = _pad_rows(y_s.reshape(ns, RET_HEADS * RET_DV), rows).astype(MXU_DTYPE)
    return y_p, y_s, s_p, s_s


def kernel(x_prompt, x_sample, cache_nsa_cmp, cache_nsa_slc, cache_nsa_win, cache_fox_kv, cache_fox_logf,
           state_ret, page_table, w_ffn_in, w_ffn_out, ln_g, ln_b, w_in_even, w_out_even, fox_f_bias,
           nsa_cmp_pos, nsa_cmp_w1, nsa_cmp_w2, w_in_odd, ret_gn_g, w_out_odd):
    nb, t, d = x_prompt.shape
    ns, ts, _ = x_sample.shape
    depth = w_ffn_in.shape[0]
    n_pages = page_table.shape[1]
    page = cache_nsa_cmp.shape[2]
    past = n_pages * page
    assert ts == 1 and past >= WINDOW and past % SEL_LEN == 0 and t % RET_CHUNK == 0
    alpha = (2.0 * depth) ** 0.25
    rows = max(16, -(-ns // 16) * 16)

    xp = x_prompt.reshape(nb * t, d)
    xs = _pad_rows(x_sample.reshape(ns * ts, d), rows)
    xp_b, xs_b = xp.astype(MXU_DTYPE), xs.astype(MXU_DTYPE)

    caches = (cache_nsa_cmp, cache_nsa_slc, cache_nsa_win, cache_fox_kv, jnp.swapaxes(cache_fox_logf, 2, 3))
    w_ffn_out, w_out_even, w_out_odd = _mx(w_ffn_out), _mx(w_out_even), _mx(w_out_odd)

    def ffn(xp, xp_b, xs, xs_b, l, s):
        hid_p, hid_s = swiglu_hidden(xp_b, xs_b, w_ffn_in, l, s)
        return (matmul_postnorm(hid_p, w_ffn_out, (l, s), xp, ln_g, ln_b, (l, 2 * s), alpha, 0.5)
                + matmul_postnorm(hid_s, w_ffn_out, (l, s), xs, ln_g, ln_b, (l, 2 * s), alpha, 0.5))

    new_p = [[] for _ in range(6)]
    new_s = [[] for _ in range(6)]
    for l in range(depth):
        xp, xp_b, xs, xs_b = ffn(xp, xp_b, xs, xs_b, l, 0)
        if l % 2 == 0:
            e = l // 2
            mp, ms, st_p, st_s = _even_layer(
                e, xp_b, xs_b, nb, t, ns, past, caches, page_table,
                (w_in_even, fox_f_bias, nsa_cmp_pos, nsa_cmp_w1, nsa_cmp_w2))
            w_out, widx = w_out_even, (e,)
            slots = (0, 1, 2, 3, 4)
        else:
            o = l // 2
            mp, ms, sp, ss = _odd_layer(o, xp_b, xs_b, nb, t, ns, past, state_ret, w_in_odd, ret_gn_g)
            st_p, st_s = (sp,), (ss,)
            w_out, widx = w_out_odd, (o,)
            slots = (5,)
        for i, a_p, a_s in zip(slots, st_p, st_s):
            new_p[i].append(a_p)
            new_s[i].append(a_s)
        xp, xp_b = matmul_postnorm(mp, w_out, widx, xp, ln_g, ln_b, (l, 1), alpha, 1.0)
        xs, xs_b = matmul_postnorm(ms, w_out, widx, xs, ln_g, ln_b, (l, 1), alpha, 1.0)
        xp, xp_b, xs, xs_b = ffn(xp, xp_b, xs, xs_b, l, 1)

    outs = [xp.reshape(nb, t, d), xs[:ns].reshape(ns, ts, d)]
    for i in range(6):
        outs.append(jnp.stack(new_p[i]))
        outs.append(jnp.stack(new_s[i]))
    return tuple(outs)
```

```python
import functools
import math

import numpy as np
import jax
import jax.numpy as jnp
from jax import lax
from jax.experimental import pallas as pl
from jax.experimental.pallas import tpu as pltpu

F32 = jnp.float32
MXU_DTYPE = jnp.bfloat16

HEAD_DIM = 128
NSA_HEADS = 8
NSA_KV = 2
NSA_GROUP = NSA_HEADS // NSA_KV
CMP_LEN = 32
CMP_STRIDE = 16
CMP_HIDDEN = 256
SEL_LEN = 64
SEL_TOP = 16
WINDOW = 512
FOX_HEADS = 8
ROPE_THETA = 500000.0
ROPE_DIMS = HEAD_DIM // 4
RET_HEADS = 8
RET_DK = 256
RET_DV = 512
RET_CHUNK = 128
RET_THETA = 10000.0
RET_HEADS_PER_STEP = 8
LN_EPS = 1e-5
GN_EPS = 1e-6

O_NQ = 0
O_NKV = O_NQ + NSA_HEADS * HEAD_DIM
O_NG = O_NKV + 6 * NSA_KV * HEAD_DIM
O_FQ = O_NG + 3 * NSA_HEADS
O_FK = O_FQ + FOX_HEADS * HEAD_DIM
O_FV = O_FK + FOX_HEADS * HEAD_DIM
O_FF = O_FV + FOX_HEADS * HEAD_DIM
E_EVEN = O_FF + FOX_HEADS
P_NKV = NSA_HEADS * HEAD_DIM
P_FQ = P_NKV + 6 * NSA_KV * HEAD_DIM
P_SMALL = P_FQ + 3 * FOX_HEADS * HEAD_DIM
P_EVEN = P_SMALL + 128
N_GATE = 3 * NSA_HEADS
O_RK = RET_HEADS * RET_DK
O_RV = 2 * RET_HEADS * RET_DK
O_RG = O_RV + RET_HEADS * RET_DV
E_ODD = O_RG + RET_HEADS * RET_DV

V7X_VMEM_LIMIT = 56 * 1024 * 1024
LN_ROWS = 128
LOG2E = 1.4426950408889634
NEG_BIG = -1e30
NT_DIMS = (((1,), (1,)), ((), ()))
TN_DIMS = (((0,), (0,)), ((), ()))


def _params(*sem):
    return pltpu.CompilerParams(dimension_semantics=sem, vmem_limit_bytes=V7X_VMEM_LIMIT)


def _mx(x):
    return x.astype(MXU_DTYPE)


def _pick(n, prefs):
    for p in prefs:
        if n % p == 0:
            return p
    return n


def _swiglu_kernel(x_ref, xs_ref, wa_ref, wb_ref, o_ref, os_ref, wa_s, wb_s):
    def hidden(x):
        a = jnp.dot(x, wa_s[...], preferred_element_type=F32)
        b = jnp.dot(x, wb_s[...], preferred_element_type=F32)
        return (a * jax.nn.sigmoid(a) * b).astype(o_ref.dtype)

    @pl.when(pl.program_id(1) == 0)
    def _():
        wa_s[...] = _mx(wa_ref[...])
        wb_s[...] = _mx(wb_ref[...])
        os_ref[...] = hidden(xs_ref[...])

    o_ref[...] = hidden(x_ref[...])


def swiglu_hidden(xb, xsb, w_in, l, s):
    m, d = xb.shape
    ms = xsb.shape[0]
    f = w_in.shape[-1] // 2
    tm = _pick(m, (1024, 512, 256, 128))
    tn = _pick(f, (512, 256, 128))
    nj = f // tn
    return pl.pallas_call(
        _swiglu_kernel,
        grid=(nj, m // tm),
        in_specs=[
            pl.BlockSpec((tm, d), lambda j, i: (i, 0)),
            pl.BlockSpec((ms, d), lambda j, i: (0, 0)),
            pl.BlockSpec((None, None, d, tn), lambda j, i: (l, s, 0, j)),
            pl.BlockSpec((None, None, d, tn), lambda j, i: (l, s, 0, j + nj)),
        ],
        out_specs=[pl.BlockSpec((tm, tn), lambda j, i: (i, j)), pl.BlockSpec((ms, tn), lambda j, i: (0, j))],
        out_shape=[jax.ShapeDtypeStruct((m, f), MXU_DTYPE), jax.ShapeDtypeStruct((ms, f), MXU_DTYPE)],
        scratch_shapes=[pltpu.VMEM((d, tn), MXU_DTYPE), pltpu.VMEM((d, tn), MXU_DTYPE)],
        compiler_params=_params("arbitrary", "arbitrary"),
        name="swiglu_hidden",
    )(xb, xsb, w_in, w_in)


def _mm_ln_kernel(x_ref, w_ref, r_ref, g_ref, b_ref, o_ref, ob_ref, *, alpha, scale, nk):
    k = pl.program_id(1)
    def part():
        return jnp.dot(x_ref[...], _mx(w_ref[...]), preferred_element_type=F32)

    @pl.when(k == 0)
    def _():
        o_ref[...] = part()

    @pl.when(k > 0)
    def _():
        o_ref[...] += part()

    @pl.when(k == nk - 1)
    def _():
        rows = min(LN_ROWS, o_ref.shape[0])

        def norm_rows(c, carry):
            sl = pl.ds(pl.multiple_of(c * rows, rows), rows)
            z = alpha * r_ref[sl, :] + scale * o_ref[sl, :]
            mu = jnp.mean(z, axis=-1, keepdims=True)
            dz = z - mu
            var = jnp.mean(dz * dz, axis=-1, keepdims=True)
            y = dz * lax.rsqrt(var + LN_EPS) * g_ref[...] + b_ref[...]
            o_ref[sl, :] = y
            ob_ref[sl, :] = y.astype(ob_ref.dtype)
            return carry

        lax.fori_loop(0, o_ref.shape[0] // rows, norm_rows, 0)


def matmul_postnorm(xb, w, widx, res, g, b, gidx, alpha, scale):
    m, kdim = xb.shape
    n = w.shape[-1]
    tm = _pick(m, (512, 256, 128))
    tk = _pick(kdim, (2816, 2048, 1024, 512, 256, 128))
    nk = kdim // tk
    nlead = len(widx)
    w_spec = pl.BlockSpec((None,) * nlead + (tk, n), lambda i, k: tuple(widx) + (k, 0))
    g_spec = pl.BlockSpec((None, None, 1, n), lambda i, k: tuple(gidx) + (0, 0))
    return pl.pallas_call(
        functools.partial(_mm_ln_kernel, alpha=alpha, scale=scale, nk=nk),
        grid=(m // tm, nk),
        in_specs=[
            pl.BlockSpec((tm, tk), lambda i, k: (i, k)),
            w_spec,
            pl.BlockSpec((tm, n), lambda i, k: (i, 0)),
            g_spec, g_spec,
        ],
        out_specs=[pl.BlockSpec((tm, n), lambda i, k: (i, 0)), pl.BlockSpec((tm, n), lambda i, k: (i, 0))],
        out_shape=[jax.ShapeDtypeStruct((m, n), F32), jax.ShapeDtypeStruct((m, n), MXU_DTYPE)],
        compiler_params=_params("arbitrary", "arbitrary"),
        name="matmul_postnorm",
    )(xb, w, res, g[:, :, None, :], b[:, :, None, :])


def _mm_kernel(x_ref, xs_ref, w_ref, o_ref, os_ref, w_s):
    @pl.when(pl.program_id(1) == 0)
    def _():
        w_s[...] = _mx(w_ref[...])
        os_ref[...] = jnp.dot(xs_ref[...], w_s[...], preferred_element_type=F32)
    o_ref[...] = jnp.dot(x_ref[...], w_s[...], preferred_element_type=F32)


def matmul(xb, xsb, w, widx, tn_prefs):
    m, kdim = xb.shape
    ms = xsb.shape[0]
    n = w.shape[-1]
    tm = _pick(m, (1024, 512, 256, 128))
    tn = _pick(n, tn_prefs)
    nlead = len(widx)
    return pl.pallas_call(
        _mm_kernel,
        grid=(n // tn, m // tm),
        in_specs=[
            pl.BlockSpec((tm, kdim), lambda j, i: (i, 0)),
            pl.BlockSpec((ms, kdim), lambda j, i: (0, 0)),
            pl.BlockSpec((None,) * nlead + (kdim, tn), lambda j, i: tuple(widx) + (0, j)),
        ],
        out_specs=[pl.BlockSpec((tm, tn), lambda j, i: (i, j)), pl.BlockSpec((ms, tn), lambda j, i: (0, j))],
        out_shape=[jax.ShapeDtypeStruct((m, n), F32), jax.ShapeDtypeStruct((ms, n), F32)],
        scratch_shapes=[pltpu.VMEM((kdim, tn), MXU_DTYPE)],
        compiler_params=_params("arbitrary", "arbitrary"),
        name="matmul",
    )(xb, xsb, w)


def _rope_tables(pos):
    half = ROPE_DIMS // 2
    inv = 1.0 / (ROPE_THETA ** (jnp.arange(half, dtype=F32) / half))
    ang = pos.astype(F32)[:, None] * inv[None, :]
    cos, sin = jnp.cos(ang), jnp.sin(ang)
    n = pos.shape[0]
    ones = jnp.ones((n, HEAD_DIM - ROPE_DIMS), F32)
    zeros = jnp.zeros((n, HEAD_DIM - ROPE_DIMS), F32)
    zh = jnp.zeros((n, half), F32)
    c = jnp.concatenate([cos, cos, ones], -1)
    a = jnp.concatenate([-sin, zh, zeros], -1)
    b = jnp.concatenate([zh, sin, zeros], -1)
    return c, a, b


def _rope(x, c, a, b):
    half = ROPE_DIMS // 2
    return x * c + pltpu.roll(x, HEAD_DIM - half, 1) * a + pltpu.roll(x, half, 1) * b


def _even_post_kernel(q_ref, slc_ref, win_ref, sm_ref, c_ref, a_ref, b_ref, fb_ref,
                      qr_ref, slco_ref, wino_ref, gate_ref, logf_ref):
    c, a, b = c_ref[...], a_ref[...], b_ref[...]
    for h in range(NSA_HEADS):
        sl = slice(h * HEAD_DIM, (h + 1) * HEAD_DIM)
        qr_ref[:, sl] = _rope(q_ref[:, sl], c, a, b)
    for src, dst in ((slc_ref, slco_ref), (win_ref, wino_ref)):
        for g in range(NSA_KV):
            sl = slice(g * HEAD_DIM, (g + 1) * HEAD_DIM)
            dst[:, sl] = _rope(src[:, sl], c, a, b)
        vs = slice(NSA_KV * HEAD_DIM, 2 * NSA_KV * HEAD_DIM)
        dst[:, vs] = src[:, vs]
    sm = sm_ref[...]
    gate_ref[...] = jax.nn.sigmoid(sm)
    z = sm + fb_ref[...]
    logf_ref[...] = jnp.minimum(z, 0.0) - jnp.log1p(jnp.exp(-jnp.abs(z)))


def even_post(h, tabs, fb_row, npos_blocks):
    m = h.shape[0]
    tm = _pick(m, (512, 256, 128))
    kvw = 2 * NSA_KV * HEAD_DIM
    tab_spec = pl.BlockSpec((tm, HEAD_DIM), lambda i: (i % npos_blocks, 0))
    return pl.pallas_call(
        _even_post_kernel,
        grid=(m // tm,),
        in_specs=[
            pl.BlockSpec((tm, P_NKV), lambda i: (i, 0)),
            pl.BlockSpec((tm, kvw), lambda i: (i, (P_NKV + kvw) // kvw)),
            pl.BlockSpec((tm, kvw), lambda i: (i, (P_NKV + 2 * kvw) // kvw)),
            pl.BlockSpec((tm, 128), lambda i: (i, P_SMALL // 128)),
            tab_spec, tab_spec, tab_spec,
            pl.BlockSpec((1, 128), lambda i: (0, 0)),
        ],
        out_specs=[
            pl.BlockSpec((tm, P_NKV), lambda i: (i, 0)),
            pl.BlockSpec((tm, kvw), lambda i: (i, 0)),
            pl.BlockSpec((tm, kvw), lambda i: (i, 0)),
            pl.BlockSpec((tm, 128), lambda i: (i, 0)),
            pl.BlockSpec((tm, 128), lambda i: (i, 0)),
        ],
        out_shape=[
            jax.ShapeDtypeStruct((m, P_NKV), F32),
            jax.ShapeDtypeStruct((m, kvw), F32),
            jax.ShapeDtypeStruct((m, kvw), F32),
            jax.ShapeDtypeStruct((m, 128), F32),
            jax.ShapeDtypeStruct((m, 128), F32),
        ],
        compiler_params=_params("arbitrary"),
        name="even_post",
    )(h, h, h, h, *tabs, fb_row)


def _cumsum_kernel(x_ref, tri_ref, o_ref, carry):
    @pl.when(pl.program_id(1) == 0)
    def _():
        carry[...] = jnp.zeros_like(carry)
    c = jnp.dot(tri_ref[...], x_ref[...], preferred_element_type=F32,
                precision=lax.Precision.HIGHEST) + carry[...]
    o_ref[...] = c
    carry[...] = c[-1:, :]


def cumsum_rows(x, nb, t):
    tc = _pick(t, (512, 256, 128))
    tri = jnp.tril(jnp.ones((tc, tc), F32))
    nt = t // tc
    return pl.pallas_call(
        _cumsum_kernel,
        grid=(nb, nt),
        in_specs=[pl.BlockSpec((tc, 128), lambda b, i: (b * nt + i, 0)),
                  pl.BlockSpec((tc, tc), lambda b, i: (0, 0))],
        out_specs=pl.BlockSpec((tc, 128), lambda b, i: (b * nt + i, 0)),
        out_shape=jax.ShapeDtypeStruct(x.shape, F32),
        scratch_shapes=[pltpu.VMEM((1, 128), F32)],
        compiler_params=_params("arbitrary", "arbitrary"),
        name="cumsum_rows",
    )(x, tri)


def _compress_kernel(x_ref, w1_ref, pe_ref, w2_ref, o_ref, u0, u1, *, n16):
    u0[...] = jnp.zeros_like(u0)
    u1[...] = jnp.zeros_like(u1)
    for r in range(0, CMP_STRIDE, 2):
        xa = x_ref[pl.ds(r, n16, stride=CMP_STRIDE), :]
        xb = x_ref[pl.ds(r + 1, n16, stride=CMP_STRIDE), :]
        for u, j in ((u0, r), (u1, CMP_STRIDE + r)):
            lhs = jnp.concatenate([_mx(xa + pe_ref[j:j + 1, :]), _mx(xb + pe_ref[j + 1:j + 2, :])], axis=1)
            w = _mx(w1_ref[j:j + 2]).reshape(2 * HEAD_DIM, CMP_HIDDEN)
            u[...] += jnp.dot(lhs, w, preferred_element_type=F32)
    pre = u0[...] + pltpu.roll(u1[...], n16 - 1, 0)
    out = jnp.dot(_mx(jax.nn.gelu(pre)), _mx(w2_ref[...]), preferred_element_type=F32)
    row = lax.broadcasted_iota(jnp.int32, out.shape, 0)
    o_ref[...] = jnp.where(row < n16 - 1, out, 0.0)


def nsa_compress(x, x_spec, nb, length, e, w1, pe, w2):
    n16 = length // CMP_STRIDE
    w1r = w1.reshape(w1.shape[0], 2, CMP_LEN, HEAD_DIM, CMP_HIDDEN)
    return pl.pallas_call(
        functools.partial(_compress_kernel, n16=n16),
        grid=(nb, 2 * NSA_KV),
        in_specs=[
            x_spec,
            pl.BlockSpec((None, None, CMP_LEN, HEAD_DIM, CMP_HIDDEN), lambda b, c: (e, c // NSA_KV, 0, 0, 0)),
            pl.BlockSpec((None, None, CMP_LEN, HEAD_DIM), lambda b, c: (e, c // NSA_KV, 0, 0)),
            pl.BlockSpec((None, None, CMP_HIDDEN, HEAD_DIM), lambda b, c: (e, c // NSA_KV, 0, 0)),
        ],
        out_specs=pl.BlockSpec((None, None, n16, HEAD_DIM), lambda b, c: (b, c, 0, 0)),
        out_shape=jax.ShapeDtypeStruct((nb, 2 * NSA_KV, n16, HEAD_DIM), F32),
        scratch_shapes=[pltpu.VMEM((n16, CMP_HIDDEN), F32), pltpu.VMEM((n16, CMP_HIDDEN), F32)],
        compiler_params=_params("arbitrary", "arbitrary"),
        name="nsa_compress",
    )(x, w1r, pe, w2)


def _sel_matrix(n_c_pad, n_c, n_sel, width):
    ratio = SEL_LEN // CMP_STRIDE
    i = np.arange(n_c_pad)[:, None]
    j = np.arange(width)[None, :]
    a = (i >= ratio * j - 1) & (i <= ratio * j + ratio - 1) & (i < n_c) & (j < n_sel)
    return jnp.asarray(a.astype(np.float32))


def _masked_softmax(lg, mask):
    lg = jnp.where(mask, lg, -jnp.inf)
    m = jnp.max(lg, axis=-1, keepdims=True)
    m = jnp.where(m == -jnp.inf, 0.0, m)
    p = jnp.exp(lg - m)
    s = jnp.sum(p, axis=-1, keepdims=True)
    return p / jnp.where(s > 0, s, 1.0)


def _cmp_select_kernel(q_ref, kc_ref, vc_ref, a_ref, oc_ref, sel_ref, *, tq, n_sel, n_top):
    i = pl.program_id(2)
    ncp = kc_ref.shape[0]
    qpos = i * tq + lax.broadcasted_iota(jnp.int32, (tq, 1), 0)
    cend = lax.broadcasted_iota(jnp.int32, (1, ncp), 1) * CMP_STRIDE + CMP_LEN
    cmask = cend <= qpos + 1
    kc = _mx(kc_ref[...])
    vc = _mx(vc_ref[...])
    scale = HEAD_DIM ** -0.5
    imp = jnp.zeros((tq, ncp), F32)
    for m in range(NSA_GROUP):
        sl = slice(m * HEAD_DIM, (m + 1) * HEAD_DIM)
        lg = lax.dot_general(_mx(q_ref[:, sl]), kc, NT_DIMS, preferred_element_type=F32) * scale
        p = _masked_softmax(lg, cmask)
        oc_ref[:, sl] = jnp.dot(_mx(p), vc, preferred_element_type=F32)
        imp = imp + p
    s_sel = jnp.dot(imp, a_ref[...], preferred_element_type=F32, precision=lax.Precision.HIGHEST)
    st = s_sel.T[:n_sel, :]
    blk = lax.broadcasted_iota(jnp.int32, (n_sel, tq), 0)
    cur = (i * tq + lax.broadcasted_iota(jnp.int32, (n_sel, tq), 1)) // SEL_LEN
    forced = (blk == 0) | (blk == cur) | (blk == cur - 1)
    allowed = blk <= cur
    v = jnp.where(allowed, jnp.where(forced, jnp.inf, st), -jnp.inf)
    rank = jnp.zeros((n_sel, tq), jnp.int32)
    for r in range(n_sel):
        vr = v[r:r + 1, :]
        before = (vr > v) | ((vr == v) & (blk > r))
        rank = rank + before.astype(jnp.int32)
    sel = ((rank < n_top) & allowed).astype(F32)
    if n_sel < 128:
        sel = jnp.concatenate([sel, jnp.zeros((128 - n_sel, tq), F32)], axis=0)
    sel_ref[...] = sel.T.astype(sel_ref.dtype)


def cmp_select(h, kvc, a_mat, nb, t):
    tq = _pick(t, (256, 128))
    nt = t // tq
    n_sel = t // SEL_LEN
    assert n_sel <= 128 and n_sel % 8 == 0
    ncp = kvc.shape[2]
    gw = NSA_GROUP * HEAD_DIM
    return pl.pallas_call(
        functools.partial(_cmp_select_kernel, tq=tq, n_sel=n_sel, n_top=min(SEL_TOP, n_sel)),
        grid=(nb, NSA_KV, nt),
        in_specs=[
            pl.BlockSpec((tq, gw), lambda b, g, i: (b * nt + i, g)),
            pl.BlockSpec((None, None, ncp, HEAD_DIM), lambda b, g, i: (b, g, 0, 0)),
            pl.BlockSpec((None, None, ncp, HEAD_DIM), lambda b, g, i: (b, NSA_KV + g, 0, 0)),
            pl.BlockSpec((ncp, 128), lambda b, g, i: (0, 0)),
        ],
        out_specs=[
            pl.BlockSpec((tq, gw), lambda b, g, i: (b * nt + i, g)),
            pl.BlockSpec((None, None, tq, 128), lambda b, g, i: (b, g, i, 0)),
        ],
        out_shape=[
            jax.ShapeDtypeStruct((nb * t, NSA_HEADS * HEAD_DIM), F32),
            jax.ShapeDtypeStruct((nb, NSA_KV, t, 128), MXU_DTYPE),
        ],
        compiler_params=_params("arbitrary", "arbitrary", "arbitrary"),
        name="cmp_select",
    )(h, kvc, kvc, a_mat)


def _flash_steps(mode, t, tq, tk):
    rows = []
    for qi in range(t // tq):
        q_lo, q_hi = qi * tq, qi * tq + tq - 1
        k_hi = q_hi // tk
        k_lo = k_hi if mode == "win" else 0
        for kt in range(k_lo, k_hi + 1):
            every_key_visible = kt * tk + tk - 1 <= q_lo
            rows.append((qi, kt, int(kt == k_lo), int(kt == k_hi), int(not every_key_visible)))
    return jnp.asarray(np.array(rows, np.int32).T)


def _flash_kernel(*refs, mode, tq, tk, heads):
    if mode == "fox":
        tab, q_ref, k_ref, v_ref, cq_ref, ck_ref, o_ref, m_s, acc = refs
    elif mode == "slc":
        tab, q_ref, k_ref, v_ref, sel_ref, o_ref, m_s, acc = refs
    else:
        tab, q_ref, k_ref, v_ref, kp_ref, vp_ref, o_ref, m_s, acc = refs
    step = pl.program_id(2)
    qi, kt = tab[0, step], tab[1, step]

    @pl.when(tab[2, step] == 1)
    def _():
        m_s[...] = jnp.full_like(m_s, NEG_BIG)
        acc[...] = jnp.zeros_like(acc)

    def update(causal):
        nkeys = 2 * tk if mode == "win" else tk
        ones = jnp.ones((nkeys, HEAD_DIM), MXU_DTYPE)
        if mode == "slc":
            kb = _mx(k_ref[...])
            v_aug = jnp.concatenate([_mx(v_ref[...]), ones], axis=1)
        elif mode == "win":
            kb = jnp.concatenate([_mx(kp_ref[...]), _mx(k_ref[...])], axis=0)
            v_aug = jnp.concatenate([jnp.concatenate([_mx(vp_ref[...]), _mx(v_ref[...])], axis=0), ones], axis=1)
        valid = None
        if causal or mode == "win":
            qpos = qi * tq + lax.broadcasted_iota(jnp.int32, (tq, nkeys), 0)
            kpos = (kt * tk - (nkeys - tk)) + lax.broadcasted_iota(jnp.int32, (tq, nkeys), 1)
            valid = kpos <= qpos
            if mode == "win":
                valid = valid & (kpos > qpos - WINDOW) & (kpos >= 0)
        if mode == "slc":
            blk = lax.broadcasted_iota(jnp.int32, (128, tk), 0)
            kblk = (kt * tk + lax.broadcasted_iota(jnp.int32, (128, tk), 1)) // SEL_LEN
            chosen = jnp.dot(sel_ref[...], (blk == kblk).astype(MXU_DTYPE), preferred_element_type=F32) > 0.5
            valid = chosen if valid is None else valid & chosen
        for hh in range(heads):
            cols = slice(hh * HEAD_DIM, (hh + 1) * HEAD_DIM)
            if mode == "fox":
                kb = _mx(k_ref[:, cols])
                v_aug = jnp.concatenate([_mx(v_ref[:, cols]), ones], axis=1)
                ck2 = ck_ref[hh] * LOG2E
                cq2 = cq_ref[hh] * LOG2E
            s = lax.dot_general(_mx(q_ref[:, cols]), kb, NT_DIMS, preferred_element_type=F32)
            x = s * (HEAD_DIM ** -0.5 * LOG2E)
            if mode == "fox":
                x = x - ck2
            if valid is not None:
                x = jnp.where(valid, x, NEG_BIG)
            top = jnp.max(x, axis=-1, keepdims=True)
            if mode == "fox":
                top = top + cq2
            m_old = m_s[hh]
            m_new = jnp.maximum(m_old, top)
            p = jnp.exp2(x - (m_new - cq2 if mode == "fox" else m_new))
            acc[hh] = jnp.exp2(m_old - m_new) * acc[hh] + jnp.dot(_mx(p), v_aug, preferred_element_type=F32)
            m_s[hh] = m_new

    if mode == "win":
        update(True)
    else:
        pl.when(tab[4, step] == 1)(functools.partial(update, True))
        pl.when(tab[4, step] == 0)(functools.partial(update, False))

    @pl.when(tab[3, step] == 1)
    def _():
        for hh in range(heads):
            l = acc[hh, :, HEAD_DIM:]
            o_ref[:, hh * HEAD_DIM:(hh + 1) * HEAD_DIM] = acc[hh, :, :HEAD_DIM] / jnp.where(l > 0, l, 1.0)


def flash_attention(mode, nb, t, q, qcol, k, kcol, v, vcol, extra=()):
    heads = NSA_GROUP
    tk = _pick(t, (512,) if mode == "win" else (1024, 512, 256, 128))
    tq = _pick(t, (512, 256, 128))
    ntq, ntk = t // tq, t // tk
    qw = heads * HEAD_DIM
    kw = qw if mode == "fox" else HEAD_DIM
    steps = _flash_steps(mode, t, tq, tk)
    in_specs = [
        pl.BlockSpec((tq, qw), lambda b, h, s, tab: (b * ntq + tab[0, s], qcol(h))),
        pl.BlockSpec((tk, kw), lambda b, h, s, tab: (b * ntk + tab[1, s], kcol(h))),
        pl.BlockSpec((tk, kw), lambda b, h, s, tab: (b * ntk + tab[1, s], vcol(h))),
    ]
    scratch = [pltpu.VMEM((heads, tq, 1), F32), pltpu.VMEM((heads, tq, 2 * HEAD_DIM), F32)]
    if mode == "fox":
        in_specs += [
            pl.BlockSpec((None, heads, tq, 1), lambda b, h, s, tab: (b, h, tab[0, s], 0)),
            pl.BlockSpec((None, heads, 1, tk), lambda b, h, s, tab: (b, h, 0, tab[1, s])),
        ]
    elif mode == "slc":
        in_specs += [pl.BlockSpec((None, None, tq, 128), lambda b, h, s, tab: (b, h, tab[0, s], 0))]
    else:
        assert tq == tk == WINDOW
        in_specs += [
            pl.BlockSpec((tk, kw), lambda b, h, s, tab: (b * ntk + jnp.maximum(tab[1, s] - 1, 0), kcol(h))),
            pl.BlockSpec((tk, kw), lambda b, h, s, tab: (b * ntk + jnp.maximum(tab[1, s] - 1, 0), vcol(h))),
        ]
        extra = (k, v)
    return pl.pallas_call(
        functools.partial(_flash_kernel, mode=mode, tq=tq, tk=tk, heads=heads),
        grid_spec=pltpu.PrefetchScalarGridSpec(
            num_scalar_prefetch=1,
            grid=(nb, NSA_HEADS // heads, steps.shape[1]),
            in_specs=in_specs,
            out_specs=pl.BlockSpec((tq, qw), lambda b, h, s, tab: (b * ntq + tab[0, s], h)),
            scratch_shapes=scratch,
        ),
        out_shape=jax.ShapeDtypeStruct((nb * t, NSA_HEADS * HEAD_DIM), F32),
        compiler_params=_params("arbitrary", "arbitrary", "arbitrary"),
        name="flash_" + mode,
    )(steps, q, k, v, *extra)


def _combine_kernel(oc_ref, os_ref, ow_ref, of_ref, g_ref, o_ref):
    gates = g_ref[...]
    nw = NSA_HEADS * HEAD_DIM
    for h in range(NSA_HEADS):
        sl = slice(h * HEAD_DIM, (h + 1) * HEAD_DIM)
        o = (gates[:, 3 * h:3 * h + 1] * oc_ref[:, sl] + gates[:, 3 * h + 1:3 * h + 2] * os_ref[:, sl]
             + gates[:, 3 * h + 2:3 * h + 3] * ow_ref[:, sl])
        o_ref[:, sl] = o.astype(o_ref.dtype)
    o_ref[:, nw:] = of_ref[...].astype(o_ref.dtype)


def combine_heads(o_c, o_s, o_w, o_f, gates):
    m, nw = o_c.shape
    tm = _pick(m, (512, 256, 128))
    spec = pl.BlockSpec((tm, nw), lambda i: (i, 0))
    return pl.pallas_call(
        _combine_kernel,
        grid=(m // tm,),
        in_specs=[spec, spec, spec, spec, pl.BlockSpec((tm, 128), lambda i: (i, 0))],
        out_specs=pl.BlockSpec((tm, 2 * nw), lambda i: (i, 0)),
        out_shape=jax.ShapeDtypeStruct((m, 2 * nw), MXU_DTYPE),
        compiler_params=_params("arbitrary"),
        name="combine_heads",
    )(o_c, o_s, o_w, o_f, gates)


def _gather_pages_kernel(pt_ref, *refs, page):
    *x_refs, o_ref = refs
    for j, x_ref in enumerate(x_refs):
        for c in range(2 * NSA_KV):
            o_ref[c, j * page:(j + 1) * page, :] = x_ref[:, c // NSA_KV, c % NSA_KV, :]


def gather_cmp_pages(pool, e, page_table):
    nb, n_pages = page_table.shape
    page = pool.shape[2]
    per_step = _pick(n_pages, (16, 8, 4, 2, 1))

    def page_spec(j):
        return pl.BlockSpec((None, None, page, 2, NSA_KV, HEAD_DIM),
                            lambda b, p, pt: (e, pt[b, p * per_step + j], 0, 0, 0, 0))

    return pl.pallas_call(
        functools.partial(_gather_pages_kernel, page=page),
        grid_spec=pltpu.PrefetchScalarGridSpec(
            num_scalar_prefetch=1,
            grid=(nb, n_pages // per_step),
            in_specs=[page_spec(j) for j in range(per_step)],
            out_specs=pl.BlockSpec((None, 2 * NSA_KV, per_step * page, HEAD_DIM), lambda b, p, pt: (b, 0, p, 0)),
        ),
        out_shape=jax.ShapeDtypeStruct((nb, 2 * NSA_KV, n_pages * page, HEAD_DIM), F32),
        compiler_params=_params("arbitrary", "arbitrary"),
        name="gather_cmp_pages",
    )(page_table, *([pool] * per_step))


def _sample_cmp_win_kernel(qraw_ref, qrot_ref, kc_ref, vc_ref, a_ref, kw_ref, vw_ref,
                           oc_ref, ow_ref, idx_ref, *, n_c, n_sel, n_top, q_pos):
    scale = HEAD_DIM ** -0.5
    ncp = kc_ref.shape[0]
    rows = qraw_ref.shape[0]
    cend = lax.broadcasted_iota(jnp.int32, (1, ncp), 1) * CMP_STRIDE + CMP_LEN
    cidx = lax.broadcasted_iota(jnp.int32, (1, ncp), 1)
    cmask = (cend <= q_pos + 1) & (cidx < n_c)
    lg = lax.dot_general(_mx(qraw_ref[...]), _mx(kc_ref[...]), NT_DIMS, preferred_element_type=F32) * scale
    p = _masked_softmax(lg, cmask)
    oc_ref[...] = jnp.dot(_mx(p), _mx(vc_ref[...]), preferred_element_type=F32)
    head = lax.broadcasted_iota(jnp.int32, p.shape, 0)
    imp = jnp.sum(jnp.where(head < NSA_GROUP, p, 0.0), axis=0, keepdims=True)
    imp = jnp.broadcast_to(imp, (rows, ncp))
    s_sel = jnp.dot(imp, a_ref[...], preferred_element_type=F32, precision=lax.Precision.HIGHEST)
    nsp = s_sel.shape[1]
    lane = lax.broadcasted_iota(jnp.int32, (rows, nsp), 1).astype(F32)
    cur = q_pos // SEL_LEN
    forced = (lane == 0.0) | (lane == float(cur)) | (lane == float(cur - 1))
    v = jnp.where(lane <= float(cur), jnp.where(forced, jnp.inf, s_sel), -jnp.inf)
    out_lane = lax.broadcasted_iota(jnp.int32, (rows, 128), 1)
    picked = jnp.full((rows, 128), -1.0, F32)
    for r in range(n_top):
        mx = jnp.max(v, axis=1, keepdims=True)
        ix = jnp.min(jnp.where(v == mx, lane, float(nsp)), axis=1, keepdims=True)
        ix = jnp.where(mx > -jnp.inf, ix, -1.0)
        picked = jnp.where(out_lane == r, ix, picked)
        v = jnp.where(lane == ix, -jnp.inf, v)
    idx_ref[...] = picked.astype(jnp.int32)
    lw = lax.dot_general(_mx(qrot_ref[...]), _mx(kw_ref[...]), NT_DIMS, preferred_element_type=F32) * scale
    pw = _masked_softmax(lw, jnp.full(lw.shape, True))
    ow_ref[...] = jnp.dot(_mx(pw), _mx(vw_ref[...]), preferred_element_type=F32)


def sample_cmp_win(qraw, qrot, kvc, a_mat, win, n_c, n_sel, q_pos):
    nb, _, rows, _ = qraw.shape
    ncp = kvc.shape[2]
    nsp = a_mat.shape[1]
    wlen = win.shape[1]
    qspec = pl.BlockSpec((None, None, rows, HEAD_DIM), lambda b, g: (b, g, 0, 0))
    return pl.pallas_call(
        functools.partial(_sample_cmp_win_kernel, n_c=n_c, n_sel=n_sel, n_top=min(SEL_TOP, n_sel), q_pos=q_pos),
        grid=(nb, NSA_KV),
        in_specs=[
            qspec, qspec,
            pl.BlockSpec((None, None, ncp, HEAD_DIM), lambda b, g: (b, g, 0, 0)),
            pl.BlockSpec((None, None, ncp, HEAD_DIM), lambda b, g: (b, NSA_KV + g, 0, 0)),
            pl.BlockSpec((ncp, nsp), lambda b, g: (0, 0)),
            pl.BlockSpec((None, wlen, HEAD_DIM), lambda b, g: (b, 0, g)),
            pl.BlockSpec((None, wlen, HEAD_DIM), lambda b, g: (b, 0, NSA_KV + g)),
        ],
        out_specs=[qspec, qspec, pl.BlockSpec((None, None, rows, 128), lambda b, g: (b, g, 0, 0))],
        out_shape=[
            jax.ShapeDtypeStruct(qraw.shape, F32),
            jax.ShapeDtypeStruct(qraw.shape, F32),
            jax.ShapeDtypeStruct((nb, NSA_KV, rows, 128), jnp.int32),
        ],
        compiler_params=_params("arbitrary", "arbitrary"),
        name="sample_cmp_win",
    )(qraw, qrot, kvc, kvc, a_mat, win, win)


def _sample_slc_kernel(pt_ref, idx_ref, q_ref, *refs, n_top, n_past_blk):
    kv_refs = refs[:n_top]
    kn_ref, vn_ref, o_ref = refs[n_top:]
    b, g = pl.program_id(0), pl.program_id(1)
    row = b * NSA_KV + g
    scale = HEAD_DIM ** -0.5

    def attend(gg):
        q = q_ref[...]
        qb = _mx(q)
        picked = [idx_ref[row, j] for j in range(n_top)]
        in_past = [(blk >= 0) & (blk < n_past_blk) for blk in picked]
        has_new = picked[0] == n_past_blk
        for blk in picked[1:]:
            has_new = has_new | (blk == n_past_blk)
        s_new = jnp.where(has_new, jnp.sum(q * kn_ref[...], axis=-1, keepdims=True) * scale, NEG_BIG)
        logits = []
        for j in range(n_top):
            s = lax.dot_general(qb, _mx(kv_refs[j][:, 0, gg, :]), NT_DIMS, preferred_element_type=F32) * scale
            logits.append(jnp.where(in_past[j], s, NEG_BIG))
        m = s_new
        for s in logits:
            m = jnp.maximum(m, jnp.max(s, axis=-1, keepdims=True))
        p_new = jnp.where(has_new, jnp.exp(s_new - m), 0.0)
        l = p_new
        o = p_new * vn_ref[...]
        for j in range(n_top):
            p = jnp.where(in_past[j], jnp.exp(logits[j] - m), 0.0)
            l = l + jnp.sum(p, axis=-1, keepdims=True)
            o = o + jnp.dot(_mx(p), _mx(kv_refs[j][:, 1, gg, :]), preferred_element_type=F32)
        o_ref[...] = o / jnp.where(l > 0, l, 1.0)

    for gg in range(NSA_KV):
        pl.when(g == gg)(functools.partial(attend, gg))


def sample_slc(qrot, pool, e, page_table, idx, slc_new, n_past_blk):
    nb, _, rows, _ = qrot.shape
    n_top = idx.shape[1]
    page = pool.shape[2]
    per_page = page // SEL_LEN

    def kv_spec(j):
        def index(b, g, pt, ix):
            blk = jnp.clip(ix[b * NSA_KV + g, j], 0, n_past_blk - 1)
            return (e, pt[b, blk // per_page], blk % per_page, 0, 0, 0)
        return pl.BlockSpec((None, None, SEL_LEN, 2, NSA_KV, HEAD_DIM), index)

    return pl.pallas_call(
        functools.partial(_sample_slc_kernel, n_top=n_top, n_past_blk=n_past_blk),
        grid_spec=pltpu.PrefetchScalarGridSpec(
            num_scalar_prefetch=2,
            grid=(nb, NSA_KV),
            in_specs=[
                pl.BlockSpec((None, None, rows, HEAD_DIM), lambda b, g, pt, ix: (b, g, 0, 0)),
                *[kv_spec(j) for j in range(n_top)],
                pl.BlockSpec((None, 1, HEAD_DIM), lambda b, g, pt, ix: (b, 0, g)),
                pl.BlockSpec((None, 1, HEAD_DIM), lambda b, g, pt, ix: (b, 0, NSA_KV + g)),
            ],
            out_specs=pl.BlockSpec((None, None, rows, HEAD_DIM), lambda b, g, pt, ix: (b, g, 0, 0)),
        ),
        out_shape=jax.ShapeDtypeStruct(qrot.shape, F32),
        compiler_params=_params("arbitrary", "arbitrary"),
        name="sample_slc",
    )(page_table, idx, qrot, *([pool] * n_top), slc_new, slc_new)


def _sample_fox_kernel(pt_ref, q_ref, *refs, n_steps, per_step):
    page_refs = refs[:3 * per_step]
    tri_ref, kn_ref, vn_ref, lfn_ref, o_ref, m_s, l_s, acc, crun = refs[3 * per_step:]
    step = pl.program_id(1)
    scale = HEAD_DIM ** -0.5
    nh = FOX_HEADS
    page = page_refs[0].shape[0]
    cols = page * nh

    @pl.when(step == 0)
    def _():
        m_s[...] = jnp.full_like(m_s, NEG_BIG)
        l_s[...] = jnp.zeros_like(l_s)
        acc[...] = jnp.zeros_like(acc)
        crun[...] = jnp.zeros_like(crun)

    qb = _mx(q_ref[...])
    tri = _mx(tri_ref[...])
    lane = lax.broadcasted_iota(jnp.int32, (nh, HEAD_DIM), 1) // nh
    own = (lax.broadcasted_iota(jnp.int32, (nh, cols), 1) % nh) == lax.broadcasted_iota(jnp.int32, (nh, cols), 0)
    c_off = crun[...]
    logits, values = [], []
    for j in range(per_step):
        k_ref, v_ref, lf_ref = page_refs[3 * j:3 * j + 3]
        lf = lf_ref[...]
        hi = _mx(lf).astype(F32)
        mid = _mx(lf - hi).astype(F32)
        lo = lf - hi - mid
        cs = jnp.dot(_mx(jnp.concatenate([hi, mid, lo], axis=0)), tri, preferred_element_type=F32)
        ck = cs[:nh] + cs[nh:2 * nh] + cs[2 * nh:] + c_off
        c_off = ck[:, -1:]
        ck_x = jnp.concatenate(
            [jnp.take_along_axis(ck, lane + i * (HEAD_DIM // nh), axis=1) for i in range(cols // HEAD_DIM)], axis=1)
        k2 = k_ref[...].reshape(cols, HEAD_DIM)
        s = lax.dot_general(qb, _mx(k2), NT_DIMS, preferred_element_type=F32) * scale - ck_x
        logits.append(jnp.where(own, s, NEG_BIG))
        values.append(_mx(v_ref[...].reshape(cols, HEAD_DIM)))
    crun[...] = c_off
    m_old = m_s[...]
    m_new = m_old
    for s in logits:
        m_new = jnp.maximum(m_new, jnp.max(s, axis=-1, keepdims=True))
    alpha = jnp.exp(m_old - m_new)
    l_new = alpha * l_s[...]
    o_new = alpha * acc[...]
    for s, v2 in zip(logits, values):
        p = jnp.where(own, jnp.exp(s - m_new), 0.0)
        l_new = l_new + jnp.sum(p, axis=-1, keepdims=True)
        o_new = o_new + jnp.dot(_mx(p), v2, preferred_element_type=F32)
    l_s[...] = l_new
    acc[...] = o_new
    m_s[...] = m_new

    @pl.when(step == n_steps - 1)
    def _():
        cq = c_off + lfn_ref[...]
        s_new = jnp.sum(q_ref[...] * kn_ref[...], axis=-1, keepdims=True) * scale - cq
        m_fin = jnp.maximum(m_new, s_new)
        p_new = jnp.exp(s_new - m_fin)
        a_fin = jnp.exp(m_new - m_fin)
        o_ref[...] = (a_fin * o_new + p_new * vn_ref[...]) / (a_fin * l_new + p_new)


def sample_fox(fq, pool, logf_t, e, page_table, k_new, v_new, logf_new):
    nb, n_pages = page_table.shape
    page = pool.shape[2]
    per_step = _pick(n_pages, (8, 4, 2, 1))
    tri = jnp.triu(jnp.ones((page, page), F32))

    def page_specs(j):
        def kv(which):
            return pl.BlockSpec((None, None, page, None, FOX_HEADS, HEAD_DIM),
                                lambda b, p, pt: (e, pt[b, p * per_step + j], 0, which, 0, 0))
        return [kv(0), kv(1),
                pl.BlockSpec((None, None, FOX_HEADS, page), lambda b, p, pt: (e, pt[b, p * per_step + j], 0, 0))]

    page_ops = []
    for j in range(per_step):
        page_ops += [pool, pool, logf_t]
    return pl.pallas_call(
        functools.partial(_sample_fox_kernel, n_steps=n_pages // per_step, per_step=per_step),
        grid_spec=pltpu.PrefetchScalarGridSpec(
            num_scalar_prefetch=1,
            grid=(nb, n_pages // per_step),
            in_specs=[
                pl.BlockSpec((None, FOX_HEADS, HEAD_DIM), lambda b, p, pt: (b, 0, 0)),
                *[spec for j in range(per_step) for spec in page_specs(j)],
                pl.BlockSpec((page, page), lambda b, p, pt: (0, 0)),
                pl.BlockSpec((None, FOX_HEADS, HEAD_DIM), lambda b, p, pt: (b, 0, 0)),
                pl.BlockSpec((None, FOX_HEADS, HEAD_DIM), lambda b, p, pt: (b, 0, 0)),
                pl.BlockSpec((None, FOX_HEADS, 1), lambda b, p, pt: (b, 0, 0)),
            ],
            out_specs=pl.BlockSpec((None, FOX_HEADS, HEAD_DIM), lambda b, p, pt: (b, 0, 0)),
            scratch_shapes=[
                pltpu.VMEM((FOX_HEADS, 1), F32), pltpu.VMEM((FOX_HEADS, 1), F32),
                pltpu.VMEM((FOX_HEADS, HEAD_DIM), F32), pltpu.VMEM((FOX_HEADS, 1), F32),
            ],
        ),
        out_shape=jax.ShapeDtypeStruct((nb, FOX_HEADS, HEAD_DIM), F32),
        compiler_params=_params("arbitrary", "arbitrary"),
        name="sample_fox",
    )(page_table, fq, *page_ops, tri, k_new, v_new, logf_new)


def _ret_tables(chunk):
    lg = jnp.log1p(-(2.0 ** (-5.0 - jnp.arange(RET_HEADS, dtype=F32))))
    i = jnp.arange(chunk, dtype=F32)
    diff = i[:, None] - i[None, :]
    dec = jnp.where(diff >= 0, jnp.exp(jnp.maximum(diff, 0.0)[None] * lg[:, None, None]), 0.0)
    xi = jnp.exp((i[None, :] + 1.0) * lg[:, None])
    zeta = jnp.exp((chunk - 1.0 - i)[None, :] * lg[:, None])
    g_c = jnp.exp(chunk * lg)
    return dec, xi, zeta, g_c


def _ret_rope_tables(pos):
    half = RET_DK // 2
    inv = 1.0 / (RET_THETA ** (jnp.arange(half, dtype=F32) / half))
    ang = pos.astype(F32)[:, None] * inv[None, :]
    return jnp.cos(ang), jnp.sin(ang)


def _group_norm_gate(o, gate, gn):
    mu = jnp.mean(o, axis=-1, keepdims=True)
    d = o - mu
    var = jnp.mean(d * d, axis=-1, keepdims=True)
    y = d * lax.rsqrt(var + GN_EPS) * gn
    return gate * jax.nn.sigmoid(gate) * y


def _retention_kernel(q_ref, k_ref, v_ref, gate_ref, cos_ref, sin_ref, dec_ref, coef_ref, gn_ref,
                      y_ref, s_ref, state, *, n_chunks, heads):
    c = pl.program_id(2)
    half = RET_DK // 2

    @pl.when(c == 0)
    def _():
        state[...] = jnp.zeros_like(state)

    cos, sin = cos_ref[...], sin_ref[...]

    def rot(x_ref, hh):
        x1 = x_ref[:, hh * RET_DK:hh * RET_DK + half]
        x2 = x_ref[:, hh * RET_DK + half:(hh + 1) * RET_DK]
        return jnp.concatenate([x1 * cos - x2 * sin, x2 * cos + x1 * sin], axis=-1)

    for hh in range(heads):
        vcols = slice(hh * RET_DV, (hh + 1) * RET_DV)
        coef = coef_ref[hh]
        xi, zeta, g_c = coef[:, 0:1], coef[:, 1:2], coef[0:1, 2:3]
        q = _mx(rot(q_ref, hh))
        kf = rot(k_ref, hh) * (RET_DK ** -0.5)
        v = _mx(v_ref[:, vcols])
        s_old = state[hh]
        a = lax.dot_general(q, _mx(kf), NT_DIMS, preferred_element_type=F32) * dec_ref[hh]
        o = (jnp.dot(_mx(a), v, preferred_element_type=F32)
             + jnp.dot(q, _mx(s_old), preferred_element_type=F32) * xi)
        state[hh] = s_old * g_c + lax.dot_general(_mx(kf * zeta), v, TN_DIMS, preferred_element_type=F32)
        y_ref[:, vcols] = _group_norm_gate(o, gate_ref[:, vcols], gn_ref[:, vcols]).astype(y_ref.dtype)

    @pl.when(c == n_chunks - 1)
    def _():
        s_ref[...] = state[...]


def retention_prompt(h, nb, t, gn_g, o_idx):
    ch = RET_CHUNK
    n_chunks = t // ch
    dec, xi, zeta, g_c = _ret_tables(ch)
    coef = jnp.stack([xi, zeta, jnp.broadcast_to(g_c[:, None], xi.shape)], axis=-1)
    coef = jnp.pad(coef, ((0, 0), (0, 0), (0, 128 - 3)))
    cos, sin = _ret_rope_tables(jnp.arange(t))
    hs = RET_HEADS_PER_STEP
    groups = RET_HEADS // hs
    kw, vw = hs * RET_DK, hs * RET_DV
    kb, vb, gb = O_RK // kw, O_RV // vw, O_RG // vw
    return pl.pallas_call(
        functools.partial(_retention_kernel, n_chunks=n_chunks, heads=hs),
        grid=(nb, groups, n_chunks),
        in_specs=[
            pl.BlockSpec((ch, kw), lambda b, hg, c: (b * n_chunks + c, hg)),
            pl.BlockSpec((ch, kw), lambda b, hg, c: (b * n_chunks + c, kb + hg)),
            pl.BlockSpec((ch, vw), lambda b, hg, c: (b * n_chunks + c, vb + hg)),
            pl.BlockSpec((ch, vw), lambda b, hg, c: (b * n_chunks + c, gb + hg)),
            pl.BlockSpec((ch, RET_DK // 2), lambda b, hg, c: (c, 0)),
            pl.BlockSpec((ch, RET_DK // 2), lambda b, hg, c: (c, 0)),
            pl.BlockSpec((hs, ch, ch), lambda b, hg, c: (hg, 0, 0)),
            pl.BlockSpec((hs, ch, 128), lambda b, hg, c: (hg, 0, 0)),
            pl.BlockSpec((None, 1, vw), lambda b, hg, c: (o_idx, 0, hg)),
        ],
        out_specs=[
            pl.BlockSpec((ch, vw), lambda b, hg, c: (b * n_chunks + c, hg)),
            pl.BlockSpec((None, hs, RET_DK, RET_DV), lambda b, hg, c: (b, hg, 0, 0)),
        ],
        out_shape=[
            jax.ShapeDtypeStruct((nb * t, RET_HEADS * RET_DV), MXU_DTYPE),
            jax.ShapeDtypeStruct((nb, RET_HEADS, RET_DK, RET_DV), F32),
        ],
        scratch_shapes=[pltpu.VMEM((hs, RET_DK, RET_DV), F32)],
        compiler_params=_params("arbitrary", "arbitrary", "arbitrary"),
        name="retention_prompt",
    )(h, h, h, h, cos, sin, dec, coef, gn_g[:, None, :])


def _retention_step_kernel(q_ref, k_ref, v_ref, gate_ref, cos_ref, sin_ref, coef_ref, gn_ref, s0_ref,
                           y_ref, s_ref):
    half = RET_DK // 2
    cos, sin = cos_ref[...], sin_ref[...]

    def rot(x):
        x1, x2 = x[:half, :], x[half:, :]
        return jnp.concatenate([x1 * cos - x2 * sin, x2 * cos + x1 * sin], axis=0)

    for hh in range(RET_HEADS):
        coef = coef_ref[hh:hh + 1, :]
        dec, xi, zeta, g_c = coef[:, 0:1], coef[:, 1:2], coef[:, 2:3], coef[:, 3:4]
        q = rot(q_ref[hh])
        k = rot(k_ref[hh]) * (RET_DK ** -0.5)
        v = v_ref[hh]
        s_old = s0_ref[hh]
        a = jnp.sum(q * k, axis=0, keepdims=True) * dec
        o = a * v + jnp.sum(q * s_old, axis=0, keepdims=True) * xi
        s_ref[hh] = s_old * g_c + (k * zeta) * v
        y_ref[hh] = _group_norm_gate(o, gate_ref[hh], gn_ref[:, hh * RET_DV:(hh + 1) * RET_DV])


def retention_step(q_col, k_col, v_row, gate_row, pos, s0, o_idx, gn_g):
    nb = q_col.shape[0]
    dec, xi, zeta, g_c = _ret_tables(1)
    coef = jnp.stack([dec[:, 0, 0], xi[:, 0], zeta[:, 0], g_c], axis=-1)
    coef = jnp.pad(coef, ((0, 0), (0, 128 - 4)))
    cos, sin = _ret_rope_tables(pos)
    col = pl.BlockSpec((None, RET_HEADS, RET_DK, 1), lambda b: (b, 0, 0, 0))
    row = pl.BlockSpec((None, RET_HEADS, 1, RET_DV), lambda b: (b, 0, 0, 0))
    tab = pl.BlockSpec((RET_DK // 2, 1), lambda b: (0, 0))
    return pl.pallas_call(
        _retention_step_kernel,
        grid=(nb,),
        in_specs=[
            col, col, row, row, tab, tab,
            pl.BlockSpec((RET_HEADS, 128), lambda b: (0, 0)),
            pl.BlockSpec((None, 1, RET_HEADS * RET_DV), lambda b: (o_idx, 0, 0)),
            pl.BlockSpec((None, None, RET_HEADS, RET_DK, RET_DV), lambda b: (o_idx, b, 0, 0, 0)),
        ],
        out_specs=[row, pl.BlockSpec((None, RET_HEADS, RET_DK, RET_DV), lambda b: (b, 0, 0, 0))],
        out_shape=[
            jax.ShapeDtypeStruct((nb, RET_HEADS, 1, RET_DV), F32),
            jax.ShapeDtypeStruct((nb, RET_HEADS, RET_DK, RET_DV), F32),
        ],
        compiler_params=_params("arbitrary"),
        name="retention_step",
    )(q_col, k_col, v_row, gate_row, cos.reshape(-1, 1), sin.reshape(-1, 1), coef, gn_g[:, None, :], s0)


def _pad_rows(x, rows):
    return jnp.pad(x, ((0, rows - x.shape[0]),) + ((0, 0),) * (x.ndim - 1))


def _repack_even(w):
    small = jnp.concatenate([w[:, O_NG:O_FQ], w[:, O_FF:E_EVEN]], axis=1)
    small = jnp.pad(small, ((0, 0), (0, 128 - small.shape[1])))
    return jnp.concatenate([w[:, O_NQ:O_NG], w[:, O_FQ:O_FF], small], axis=1)


def _even_layer(e, xb_p, xb_s, nb, t, ns, past, caches, page_table, wts):
    cache_cmp, cache_slc, cache_win, cache_fox, logf_pool_t = caches
    w_in, fox_f_bias, cmp_pos, cmp_w1, cmp_w2 = wts
    w_rep = _repack_even(w_in[e])
    fb_row = jnp.zeros((1, 128), F32).at[0, N_GATE:N_GATE + FOX_HEADS].set(fox_f_bias[e])
    kvw = 2 * NSA_KV * HEAD_DIM
    nw = NSA_HEADS * HEAD_DIM

    h_p, h_s = matmul(xb_p, xb_s, w_rep, (), (1152, 640, 384, 128))
    tm_post = _pick(nb * t, (512, 256, 128))
    q_rot, slc_p, win_p, gates_p, logf_p = even_post(
        h_p, _rope_tables(jnp.arange(t)), fb_row, t // tm_post)
    c_p = cumsum_rows(logf_p, nb, t)
    c_heads = jnp.swapaxes(c_p[:, N_GATE:N_GATE + FOX_HEADS].reshape(nb, t, FOX_HEADS), 1, 2)
    n16 = t // CMP_STRIDE
    kvc_p = nsa_compress(
        h_p, pl.BlockSpec((t, HEAD_DIM), lambda b, c: (b, P_NKV // HEAD_DIM + c)), nb, t, e, cmp_w1, cmp_pos, cmp_w2)
    a_p = _sel_matrix(n16, n16 - 1, t // SEL_LEN, 128)
    o_c, sel = cmp_select(h_p, kvc_p, a_p, nb, t)
    o_s = flash_attention("slc", nb, t, q_rot, lambda g: g, slc_p, lambda g: g, slc_p, lambda g: NSA_KV + g, (sel,))
    o_w = flash_attention("win", nb, t, q_rot, lambda g: g, win_p, lambda g: g, win_p, lambda g: NSA_KV + g)
    fq0 = P_FQ // (NSA_GROUP * HEAD_DIM)
    fgroups = FOX_HEADS // NSA_GROUP
    o_f = flash_attention("fox", nb, t, h_p, lambda g: fq0 + g, h_p, lambda g: fq0 + fgroups + g,
                          h_p, lambda g: fq0 + 2 * fgroups + g, (c_heads[..., None], c_heads[:, :, None, :]))
    xcat_p = combine_heads(o_c, o_s, o_w, o_f, gates_p)
    st_p = (
        h_p[:, P_NKV:P_NKV + kvw].reshape(nb, t, 2, NSA_KV, HEAD_DIM),
        slc_p.reshape(nb, t, 2, NSA_KV, HEAD_DIM),
        win_p.reshape(nb, t, 2, NSA_KV, HEAD_DIM)[:, t - min(WINDOW, t):],
        h_p[:, P_FQ + nw:P_FQ + 3 * nw].reshape(nb, t, 2, FOX_HEADS, HEAD_DIM),
        logf_p[:, N_GATE:N_GATE + FOX_HEADS].reshape(nb, t, FOX_HEADS),
    )

    rows = xb_s.shape[0]
    q_rot_s, slc_s, win_s, gates_s, logf_s = even_post(
        h_s, _rope_tables(jnp.full((rows,), past)), fb_row, 1)
    new_win = jnp.concatenate(
        [cache_win[e].reshape(ns, -1, kvw), win_s[:ns, None, :]], axis=1)[:, 1:]
    cmp_rows = gather_cmp_pages(cache_cmp, e, page_table)
    l_tot = past + 1
    n16_s = l_tot // CMP_STRIDE
    assert n16_s * CMP_STRIDE == past
    kvc_s = nsa_compress(
        cmp_rows, pl.BlockSpec((None, None, past, HEAD_DIM), lambda b, c: (b, c, 0, 0)), ns, past, e,
        cmp_w1, cmp_pos, cmp_w2)
    n_sel_s = -(-l_tot // SEL_LEN)
    nsp = -(-n_sel_s // 128) * 128
    a_s = _sel_matrix(n16_s, n16_s - 1, n_sel_s, nsp)

    def q_rows(x):
        x = x[:ns].reshape(ns, NSA_KV, NSA_GROUP, HEAD_DIM)
        return jnp.pad(x, ((0, 0), (0, 0), (0, 8 - NSA_GROUP), (0, 0)))

    qraw_g, qrot_g = q_rows(h_s[:, :nw]), q_rows(q_rot_s)
    o_c_s, o_w_s, idx = sample_cmp_win(qraw_g, qrot_g, kvc_s, a_s, new_win, n16_s - 1, n_sel_s, past)
    idx2 = idx[:, :, 0, :min(SEL_TOP, n_sel_s)].reshape(ns * NSA_KV, -1)
    o_s_s = sample_slc(qrot_g, cache_slc, e, page_table, idx2, slc_s[:ns, None, :], past // SEL_LEN)
    fq = h_s[:ns, P_FQ:P_FQ + nw].reshape(ns, FOX_HEADS, HEAD_DIM)
    fk = h_s[:ns, P_FQ + nw:P_FQ + 2 * nw]
    fv = h_s[:ns, P_FQ + 2 * nw:P_FQ + 3 * nw]
    logf_new = logf_s[:ns, N_GATE:N_GATE + FOX_HEADS]
    o_f_s = sample_fox(fq, cache_fox, logf_pool_t, e, page_table, fk.reshape(ns, FOX_HEADS, HEAD_DIM),
                       fv.reshape(ns, FOX_HEADS, HEAD_DIM), logf_new[:, :, None])

    def heads_flat(x):
        return _pad_rows(x[:, :, :NSA_GROUP].reshape(ns, nw), rows)

    xcat_s = combine_heads(heads_flat(o_c_s), heads_flat(o_s_s), heads_flat(o_w_s),
                           _pad_rows(o_f_s.reshape(ns, nw), rows), gates_s)
    st_s = (
        h_s[:ns, P_NKV:P_NKV + kvw].reshape(ns, 1, 2, NSA_KV, HEAD_DIM),
        slc_s[:ns].reshape(ns, 1, 2, NSA_KV, HEAD_DIM),
        new_win.reshape(ns, -1, 2, NSA_KV, HEAD_DIM),
        h_s[:ns, P_FQ + nw:P_FQ + 3 * nw].reshape(ns, 1, 2, FOX_HEADS, HEAD_DIM),
        logf_new.reshape(ns, 1, FOX_HEADS),
    )
    return xcat_p, xcat_s, st_p, st_s


def _odd_layer(o, xb_p, xb_s, nb, t, ns, past, state_ret, w_in_odd, ret_gn_g):
    h_p, h_s = matmul(xb_p, xb_s, w_in_odd, (o,), (1024, 512, 256, 128))
    y_p, s_p = retention_prompt(h_p, nb, t, ret_gn_g, o)
    rows = xb_s.shape[0]
    h_s = h_s[:ns]
    q_col = h_s[:, :O_RK].reshape(ns, RET_HEADS, RET_DK, 1)
    k_col = h_s[:, O_RK:O_RV].reshape(ns, RET_HEADS, RET_DK, 1)
    v_row = h_s[:, O_RV:O_RG].reshape(ns, RET_HEADS, 1, RET_DV)
    g_row = h_s[:, O_RG:E_ODD].reshape(ns, RET_HEADS, 1, RET_DV)
    y_s, s_s = retention_step(q_col, k_col, v_row, g_row, jnp.full((1,), past), state_ret, o, ret_gn_g)
    y_s = _pad_rows(y_s.reshape(ns, RET_HEADS * RET_DV), rows).astype(MXU_DTYPE)
    return y_p, y_s, s_p, s_s


def kernel(x_prompt, x_sample, cache_nsa_cmp, cache_nsa_slc, cache_nsa_win, cache_fox_kv, cache_fox_logf,
           state_ret, page_table, w_ffn_in, w_ffn_out, ln_g, ln_b, w_in_even, w_out_even, fox_f_bias,
           nsa_cmp_pos, nsa_cmp_w1, nsa_cmp_w2, w_in_odd, ret_gn_g, w_out_odd):
    nb, t, d = x_prompt.shape
    ns, ts, _ = x_sample.shape
    depth = w_ffn_in.shape[0]
    n_pages = page_table.shape[1]
    page = cache_nsa_cmp.shape[2]
    past = n_pages * page
    assert ts == 1 and past >= WINDOW and past % SEL_LEN == 0 and t % RET_CHUNK == 0
    alpha = (2.0 * depth) ** 0.25
    rows = max(16, -(-ns // 16) * 16)

    xp = x_prompt.reshape(nb * t, d)
    xs = _pad_rows(x_sample.reshape(ns * ts, d), rows)
    xp_b, xs_b = xp.astype(MXU_DTYPE), xs.astype(MXU_DTYPE)

    caches = (cache_nsa_cmp, cache_nsa_slc, cache_nsa_win, cache_fox_kv, jnp.swapaxes(cache_fox_logf, 2, 3))
    w_ffn_out, w_out_even, w_out_odd = _mx(w_ffn_out), _mx(w_out_even), _mx(w_out_odd)

    def ffn(xp, xp_b, xs, xs_b, l, s):
        hid_p, hid_s = swiglu_hidden(xp_b, xs_b, w_ffn_in, l, s)
        return (matmul_postnorm(hid_p, w_ffn_out, (l, s), xp, ln_g, ln_b, (l, 2 * s), alpha, 0.5)
                + matmul_postnorm(hid_s, w_ffn_out, (l, s), xs, ln_g, ln_b, (l, 2 * s), alpha, 0.5))

    new_p = [[] for _ in range(6)]
    new_s = [[] for _ in range(6)]
    for l in range(depth):
        xp, xp_b, xs, xs_b = ffn(xp, xp_b, xs, xs_b, l, 0)
        if l % 2 == 0:
            e = l // 2
            mp, ms, st_p, st_s = _even_layer(
                e, xp_b, xs_b, nb, t, ns, past, caches, page_table,
                (w_in_even, fox_f_bias, nsa_cmp_pos, nsa_cmp_w1, nsa_cmp_w2))
            w_out, widx = w_out_even, (e,)
            slots = (0, 1, 2, 3, 4)
        else:
            o = l // 2
            mp, ms, sp, ss = _odd_layer(o, xp_b, xs_b, nb, t, ns, past, state_ret, w_in_odd, ret_gn_g)
            st_p, st_s = (sp,), (ss,)
            w_out, widx = w_out_odd, (o,)
            slots = (5,)
        for i, a_p, a_s in zip(slots, st_p, st_s):
            new_p[i].append(a_p)
            new_s[i].append(a_s)
        xp, xp_b = matmul_postnorm(mp, w_out, widx, xp, ln_g, ln_b, (l, 1), alpha, 1.0)
        xs, xs_b = matmul_postnorm(ms, w_out, widx, xs, ln_g, ln_b, (l, 1), alpha, 1.0)
        xp, xp_b, xs, xs_b = ffn(xp, xp_b, xs, xs_b, l, 1)

    outs = [xp.reshape(nb, t, d), xs[:ns].reshape(ns, ts, d)]
    for i in range(6):
        outs.append(jnp.stack(new_p[i]))
        outs.append(jnp.stack(new_s[i]))
    return tuple(outs)
```
